```python
import math, functools
import jax, jax.numpy as jnp
from jax import lax
import numpy as np

D_MODEL = 1024
BATCH = 8
SEQ = 2048
DEPTH = 2
DEC_BATCH = 128
DEC_SEQ = 1
PAST_LEN = 16384
PAGE_SIZE = 128

D_MIX = D_MODEL
D_A = D_MIX // 4
A_BLOCKS = 4
A_BLK = D_A // A_BLOCKS
CONV_W = 4
RGLRU_C = 8.0
B_HEADS = 4
B_DK = D_MIX // 16
B_DV = B_DK
D_B = B_HEADS * B_DK
C_HEADS = 4
C_DK = D_MIX // 8
C_DV = C_DK
D_C = C_HEADS * C_DK
PROJ_WIDTHS = (D_A, D_A, D_B, D_B, D_B, D_B, D_C, D_C, D_C, D_C)
D_IN = D_A * 2 + D_B * 4 + D_C * 4
B_CHUNK = 64
C_CHUNK = 128
ROPE_BASE = 10000.0
N_EXPERTS = 16
N_GROUPS = 4
EXP_PER_GROUP = N_EXPERTS // N_GROUPS
TOP_K = 2
D_EXPERT = D_MODEL // 2
LN_EPS = 1e-5
RMS_EPS = 1e-6
GN_EPS = 1e-6
F_TINY = 1e-30
ALPHA = (2 * DEPTH) ** 0.25
BETA = (8 * DEPTH) ** -0.25

kernel_name = "hybrid_rglru_hgrn2_retention_moe_step"


def _split_points():
    pts, acc = [], 0
    for w in PROJ_WIDTHS[:-1]:
        acc += w
        pts.append(acc)
    return pts


def _layernorm(x, g, b):
    xf = x.astype(jnp.float32)
    mu = xf.mean(-1, keepdims=True)
    var = jnp.square(xf - mu).mean(-1, keepdims=True)
    return ((xf - mu) * lax.rsqrt(var + LN_EPS) * g + b).astype(x.dtype)


def _linear_combine(c1, c2):
    a1, b1 = c1
    a2, b2 = c2
    return a1 * a2, a2 * b1 + b2


def _chunked_scan(step, S0, xs, chunk):
    Bn, T = xs[0].shape[:2]
    ch = chunk if T % chunk == 0 else T
    n = T // ch
    xs_c = tuple(jnp.moveaxis(a.reshape((Bn, n, ch) + a.shape[2:]), 1, 0) for a in xs)
    S, out = lax.scan(lambda S, c: step(S, *c), S0, xs_c)
    out = jnp.moveaxis(out, 0, 1).reshape((Bn, T) + out.shape[3:])
    return out, S


def _rotary(x, pos):
    half = x.shape[-1] // 2
    inv = ROPE_BASE ** (-jnp.arange(half, dtype=jnp.float32) / half)
    ang = pos.astype(jnp.float32)[:, None] * inv[None]
    cos = jnp.cos(ang)[None, :, None, :]
    sin = jnp.sin(ang)[None, :, None, :]
    x1, x2 = x[..., :half], x[..., half:]
    return jnp.concatenate([x1 * cos - x2 * sin, x2 * cos + x1 * sin], axis=-1)


def _rglru_mixer(xa, ga, conv0, h0, conv_w, conv_b, w_r, b_r, w_i, b_i, lam):
    f32 = jnp.float32
    Bn, T, _ = xa.shape
    xp = jnp.concatenate([conv0.astype(xa.dtype), xa], axis=1)
    u = conv_b + xp[:, 0:T] * conv_w[0]
    for j in range(1, CONV_W):
        u = u + xp[:, j:j + T] * conv_w[j]
    conv_new = xp[:, T:]
    ub = u.reshape(Bn, T, A_BLOCKS, A_BLK)
    r = jax.nn.sigmoid(jnp.einsum('btgi,gij->btgj', ub, w_r) + b_r).reshape(Bn, T, D_A).astype(f32)
    i = jax.nn.sigmoid(jnp.einsum('btgi,gij->btgj', ub, w_i) + b_i).reshape(Bn, T, D_A).astype(f32)
    log_a = RGLRU_C * r * jax.nn.log_sigmoid(lam.astype(f32))
    a = jnp.exp(log_a)
    bterm = jnp.sqrt(jnp.maximum(-jnp.expm1(2.0 * log_a), 0.0)) * (i * u.astype(f32))
    bterm = bterm.at[:, 0].add(a[:, 0] * h0.astype(f32))
    _, h = lax.associative_scan(_linear_combine, (a, bterm), axis=1)
    ya = (h * jax.nn.gelu(ga.astype(f32))).astype(xa.dtype)
    return ya, h[:, -1], conv_new


def _hgrn2_chunk(S, q, logf, k, v):
    L = q.shape[1]
    b = jnp.cumsum(logf, axis=1)
    causal = jnp.tril(jnp.ones((L, L), dtype=bool))[None, :, :, None, None]
    diff = b[:, :, None] - b[:, None, :]
    decay = jnp.where(causal, jnp.exp(jnp.where(causal, diff, 0.0)), 0.0)
    scores = jnp.sum(q[:, :, None] * k[:, None] * decay, axis=-1)
    o = jnp.einsum('btsh,bshv->bthv', scores, v) + jnp.einsum('bthk,bhkv->bthv', q * jnp.exp(b), S)
    bL = b[:, -1]
    S_new = jnp.exp(bL)[..., None] * S + jnp.einsum('bshk,bshv->bhkv', k * jnp.exp(bL[:, None] - b), v)
    return S_new, o


def _hgrn2_mixer(qb, fb, vb, gb, S0, lb, norm_g):
    f32 = jnp.float32
    Bn, T, _ = qb.shape
    shp = (Bn, T, B_HEADS, B_DK)
    fl = fb.astype(f32)
    q = jax.nn.silu(qb.astype(f32)).reshape(shp)
    f = lb + (1.0 - lb) * jax.nn.sigmoid(fl)
    logf = jnp.log(jnp.maximum(f, F_TINY)).reshape(shp)
    k = ((1.0 - lb) * jax.nn.sigmoid(-fl)).reshape(shp)
    v = vb.astype(f32).reshape(Bn, T, B_HEADS, B_DV)
    o, S = _chunked_scan(_hgrn2_chunk, S0.astype(f32), (q, logf, k, v), B_CHUNK)
    o = o * lax.rsqrt(jnp.mean(o * o, axis=-1, keepdims=True) + RMS_EPS) * norm_g
    y = o.reshape(Bn, T, D_B) * jax.nn.silu(gb.astype(f32))
    return y.astype(qb.dtype), S


def _retention_chunk(S, q, k, v, log_gamma):
    L = q.shape[1]
    idx = jnp.arange(L, dtype=jnp.float32)
    rel = idx[:, None] - idx[None, :]
    mask = (rel >= 0)[None]
    D = jnp.where(mask, jnp.exp(jnp.where(mask, rel[None], 0.0) * log_gamma[:, None, None]), 0.0)
    scores = jnp.einsum('bthk,bshk->bhts', q, k) * D[None]
    q_dec = jnp.exp((idx + 1.0)[:, None] * log_gamma[None])
    k_dec = jnp.exp((L - 1.0 - idx)[:, None] * log_gamma[None])
    o = jnp.einsum('bhts,bshv->bthv', scores, v) + jnp.einsum('bthk,bhkv->bthv', q * q_dec[None, :, :, None], S)
    S_new = jnp.exp(L * log_gamma)[None, :, None, None] * S + jnp.einsum('bshk,bshv->bhkv', k * k_dec[None, :, :, None], v)
    return S_new, o


def _retention_mixer(qc, kc, vc, gc, pos, S0):
    f32 = jnp.float32
    Bn, T, _ = qc.shape
    shp = (Bn, T, C_HEADS, C_DK)
    q = _rotary(qc.astype(f32).reshape(shp), pos)
    k = _rotary(kc.astype(f32).reshape(shp), pos) * (C_DK ** -0.5)
    v = vc.astype(f32).reshape(Bn, T, C_HEADS, C_DV)
    log_gamma = jnp.log1p(-jnp.exp2(-5.0 - jnp.arange(C_HEADS, dtype=f32)))
    step = functools.partial(_retention_chunk, log_gamma=log_gamma)
    o, S = _chunked_scan(step, S0.astype(f32), (q, k, v), C_CHUNK)
    mu = o.mean(-1, keepdims=True)
    var = jnp.square(o - mu).mean(-1, keepdims=True)
    o = (o - mu) * lax.rsqrt(var + GN_EPS)
    y = o.reshape(Bn, T, D_C) * jax.nn.silu(gc.astype(f32))
    return y.astype(qc.dtype), S


def _moe(x, router_w, router_b, wg, wu, wd):
    Bn, T, D = x.shape
    xt = x.reshape(Bn * T, D)
    logits = (xt @ router_w + router_b).astype(jnp.float32)
    probs = jax.nn.softmax(logits, axis=-1)
    pg = probs.reshape(-1, N_GROUPS, EXP_PER_GROUP)
    group_score = lax.top_k(pg, TOP_K)[0].sum(-1)
    g_sel = jnp.argmax(group_score, axis=-1)
    in_group = jnp.take_along_axis(pg, g_sel[:, None, None], axis=1)[:, 0]
    vals, idx = lax.top_k(in_group, TOP_K)
    gates = vals / vals.sum(-1, keepdims=True)
    expert_idx = g_sel[:, None] * EXP_PER_GROUP + idx
    combine = jnp.sum(jax.nn.one_hot(expert_idx, N_EXPERTS, dtype=jnp.float32) * gates[..., None], axis=1)
    h = jax.nn.silu(jnp.einsum('nd,edf->nef', xt, wg)) * jnp.einsum('nd,edf->nef', xt, wu)
    y = jnp.einsum('nef,efd->nd', h * combine[:, :, None].astype(h.dtype), wd)
    return y.reshape(Bn, T, D).astype(x.dtype)


def _trunk(x, pos, h0s, conv0s, sb0s, sc0s, params):
    (w_in, conv_w, conv_b, w_rgate, b_rgate, w_igate, b_igate, rglru_lambda,
     hgrn_lb_logits, hgrn_norm_g, w_out, ln1_g, ln1_b, router_w, router_b,
     exp_w_gate, exp_w_up, exp_w_down, ln2_g, ln2_b) = params
    s = jax.nn.softmax(hgrn_lb_logits.astype(jnp.float32), axis=0)
    lbs = jnp.cumsum(s, axis=0) - s[0]
    pts = _split_points()
    hs, convs, sbs, scs = [], [], [], []
    for l in range(DEPTH):
        proj = jnp.einsum('btd,de->bte', x, w_in[l])
        xa, ga, qb, fb, vb, gb, qc, kc, vc, gc = jnp.split(proj, pts, axis=-1)
        ya, h_new, conv_new = _rglru_mixer(xa, ga, conv0s[l], h0s[l], conv_w[l], conv_b[l],
                                           w_rgate[l], b_rgate[l], w_igate[l], b_igate[l], rglru_lambda[l])
        yb, sb_new = _hgrn2_mixer(qb, fb, vb, gb, sb0s[l], lbs[l], hgrn_norm_g[l])
        yc, sc_new = _retention_mixer(qc, kc, vc, gc, pos, sc0s[l])
        y_mix = jnp.einsum('bte,ed->btd', jnp.concatenate([ya, yb, yc], axis=-1), w_out[l])
        x = _layernorm(ALPHA * x + y_mix, ln1_g[l], ln1_b[l])
        x = _layernorm(ALPHA * x + _moe(x, router_w, router_b, exp_w_gate[l], exp_w_up[l], exp_w_down[l]),
                       ln2_g[l], ln2_b[l])
        hs.append(h_new)
        convs.append(conv_new)
        sbs.append(sb_new)
        scs.append(sc_new)
    return x, jnp.stack(hs), jnp.stack(convs), jnp.stack(sbs), jnp.stack(scs)


def setup_inputs(seed: int = 0) -> dict:
    key = jax.random.key(seed)
    ks = jax.random.split(key, 32)
    nrm = jax.random.normal
    f32 = jnp.float32
    u = jax.random.uniform(ks[13], (DEPTH, D_A), f32, 0.9, 0.999)
    a = u ** (1.0 / RGLRU_C)
    return {
        "x_prompt": nrm(ks[0], (BATCH, SEQ, D_MODEL), f32),
        "x_sample": nrm(ks[1], (DEC_BATCH, DEC_SEQ, D_MODEL), f32),
        "state_rglru_h": 0.5 * nrm(ks[2], (DEPTH, DEC_BATCH, D_A), f32),
        "state_conv": nrm(ks[3], (DEPTH, DEC_BATCH, CONV_W - 1, D_A), f32),
        "state_hgrn": 0.5 * nrm(ks[4], (DEPTH, DEC_BATCH, B_HEADS, B_DK, B_DV), f32),
        "state_ret": nrm(ks[5], (DEPTH, DEC_BATCH, C_HEADS, C_DK, C_DV), f32),
        "w_in": nrm(ks[6], (DEPTH, D_MODEL, D_IN), f32) * D_MODEL ** -0.5,
        "conv_w": nrm(ks[7], (DEPTH, CONV_W, D_A), f32) * CONV_W ** -0.5,
        "conv_b": 0.01 * nrm(ks[8], (DEPTH, D_A), f32),
        "w_rgate": nrm(ks[9], (DEPTH, A_BLOCKS, A_BLK, A_BLK), f32) * A_BLK ** -0.5,
        "b_rgate": 0.01 * nrm(ks[10], (DEPTH, A_BLOCKS, A_BLK), f32),
        "w_igate": nrm(ks[11], (DEPTH, A_BLOCKS, A_BLK, A_BLK), f32) * A_BLK ** -0.5,
        "b_igate": 0.01 * nrm(ks[12], (DEPTH, A_BLOCKS, A_BLK), f32),
        "rglru_lambda": jnp.log(a) - jnp.log1p(-a),
        "hgrn_lb_logits": 0.5 * nrm(ks[14], (DEPTH, D_B), f32),
        "hgrn_norm_g": 1.0 + 0.01 * nrm(ks[15], (DEPTH, B_DV), f32),
        "w_out": nrm(ks[16], (DEPTH, D_MIX, D_MODEL), f32) * (D_MIX ** -0.5 * BETA),
        "ln1_g": 1.0 + 0.01 * nrm(ks[17], (DEPTH, D_MODEL), f32),
        "ln1_b": 0.01 * nrm(ks[18], (DEPTH, D_MODEL), f32),
        "router_w": nrm(ks[19], (D_MODEL, N_EXPERTS), f32) * D_MODEL ** -0.5,
        "router_b": 0.01 * nrm(ks[20], (N_EXPERTS,), f32),
        "exp_w_gate": nrm(ks[21], (DEPTH, N_EXPERTS, D_MODEL, D_EXPERT), f32) * D_MODEL ** -0.5,
        "exp_w_up": nrm(ks[22], (DEPTH, N_EXPERTS, D_MODEL, D_EXPERT), f32) * D_MODEL ** -0.5,
        "exp_w_down": nrm(ks[23], (DEPTH, N_EXPERTS, D_EXPERT, D_MODEL), f32) * (D_EXPERT ** -0.5 * BETA),
        "ln2_g": 1.0 + 0.01 * nrm(ks[24], (DEPTH, D_MODEL), f32),
        "ln2_b": 0.01 * nrm(ks[25], (DEPTH, D_MODEL), f32),
    }


def reference(x_prompt, x_sample, state_rglru_h, state_conv, state_hgrn, state_ret,
              w_in, conv_w, conv_b, w_rgate, b_rgate, w_igate, b_igate, rglru_lambda,
              hgrn_lb_logits, hgrn_norm_g, w_out, ln1_g, ln1_b, router_w, router_b,
              exp_w_gate, exp_w_up, exp_w_down, ln2_g, ln2_b):
    params = (w_in, conv_w, conv_b, w_rgate, b_rgate, w_igate, b_igate, rglru_lambda,
              hgrn_lb_logits, hgrn_norm_g, w_out, ln1_g, ln1_b, router_w, router_b,
              exp_w_gate, exp_w_up, exp_w_down, ln2_g, ln2_b)
    f32 = jnp.float32
    Bp, Tp = x_prompt.shape[0], x_prompt.shape[1]
    z_h = jnp.zeros((DEPTH, Bp, D_A), f32)
    z_conv = jnp.zeros((DEPTH, Bp, CONV_W - 1, D_A), x_prompt.dtype)
    z_hgrn = jnp.zeros((DEPTH, Bp, B_HEADS, B_DK, B_DV), f32)
    z_ret = jnp.zeros((DEPTH, Bp, C_HEADS, C_DK, C_DV), f32)
    pos_p = jnp.arange(Tp, dtype=jnp.int32)
    pos_s = PAST_LEN + jnp.arange(x_sample.shape[1], dtype=jnp.int32)
    y_prompt, h_p, conv_p, hg_p, rt_p = _trunk(x_prompt, pos_p, z_h, z_conv, z_hgrn, z_ret, params)
    y_sample, h_s, conv_s, hg_s, rt_s = _trunk(x_sample, pos_s, state_rglru_h, state_conv,
                                               state_hgrn, state_ret, params)
    return (y_prompt, y_sample, h_p, h_s, conv_p, conv_s, hg_p, hg_s, rt_p, rt_s)
```

```python
import functools

import numpy as np
import jax
import jax.numpy as jnp
from jax import lax
from jax.experimental import pallas as pl
from jax.experimental.pallas import tpu as pltpu

D_MODEL = 1024
DEPTH = 2
PAST_LEN = 16384
D_A = 256
A_BLOCKS = 4
A_BLK = 64
CONV_W = 4
RGLRU_C = 8.0
B_HEADS = 4
B_DK = 64
D_B = 256
C_HEADS = 4
C_DK = 128
D_C = 512
D_IN = 3584
B_CHUNK = 64
C_CHUNK = 128
ROPE_BASE = 10000.0
N_EXPERTS = 16
N_GROUPS = 4
EXP_PER_GROUP = 4
D_EXPERT = 512
LN_EPS = 1e-5
RMS_EPS = 1e-6
GN_EPS = 1e-6
F_TINY = 1e-30
ALPHA = (2 * DEPTH) ** 0.25

O_XA, O_GA, O_QB, O_FB, O_VB, O_GB, O_QC, O_KC, O_VC, O_GC = (
    0, 256, 512, 768, 1024, 1280, 1536, 2048, 2560, 3072)

V7X_VMEM_LIMIT_BYTES = 56 * 1024 * 1024
SUBLANES = 8
LANES = 128
HGRN_SAFE_MIN_LOGDECAY = -60.0

BF16 = jnp.bfloat16
F32 = jnp.float32
_NT = (((1,), (1,)), ((), ()))
_TN = (((0,), (0,)), ((), ()))


def _cparams(n_axes):
    return pltpu.CompilerParams(
        dimension_semantics=("arbitrary",) * n_axes,
        vmem_limit_bytes=V7X_VMEM_LIMIT_BYTES)


def _dot(a, b):
    return jnp.dot(a.astype(BF16), b.astype(BF16), preferred_element_type=F32)


def _dot_g(a, b, dims):
    return lax.dot_general(a.astype(BF16), b.astype(BF16), dims, preferred_element_type=F32)


def _sigmoid(x):
    return 1.0 / (1.0 + jnp.exp(-x))


def _silu(x):
    return x * _sigmoid(x)


def _gelu_tanh(x):
    c = np.float32(np.sqrt(2.0 / np.pi))
    return 0.5 * x * (1.0 + jnp.tanh(c * (x + np.float32(0.044715) * (x * x * x))))


def _log_sigmoid(x):
    return -(jnp.maximum(-x, 0.0) + jnp.log(1.0 + jnp.exp(-jnp.abs(x))))


def _layernorm(z, g, b):
    mu = jnp.mean(z, axis=-1, keepdims=True)
    zc = z - mu
    var = jnp.mean(zc * zc, axis=-1, keepdims=True)
    return zc * lax.rsqrt(var + LN_EPS) * g + b


def _hgrn_lower_bound(lbl, layer):
    rows = [lbl[j:j + 1, :] for j in range(DEPTH)]
    m = rows[0]
    for r in rows[1:]:
        m = jnp.maximum(m, r)
    ex = [jnp.exp(r - m) for r in rows]
    tot = ex[0]
    for e in ex[1:]:
        tot = tot + e
    lb = jnp.zeros_like(m)
    for j in range(1, layer + 1):
        lb = lb + ex[j] / tot
    return lb


def _rglru_gates(u, wri, bri, lam):
    gates = _dot(u, wri) + bri
    r = _sigmoid(gates[:, :D_A])
    i = _sigmoid(gates[:, D_A:])
    log_a = RGLRU_C * r * _log_sigmoid(lam)
    a = jnp.exp(log_a)
    bterm = jnp.sqrt(jnp.maximum(1.0 - jnp.exp(2.0 * log_a), 0.0)) * (i * u)
    return a, bterm


def _proj_kernel(x_ref, w_ref, o_ref):
    xb = x_ref[...].astype(BF16)
    for j in range(0, D_IN, 512):
        o_ref[:, j:j + 512] = jnp.dot(xb, w_ref[:, j:j + 512], preferred_element_type=F32)


def _proj(x, w_in_bf, layer, tm):
    n = x.shape[0]
    return pl.pallas_call(
        _proj_kernel,
        grid=(n // tm,),
        in_specs=[pl.BlockSpec((tm, D_MODEL), lambda i: (i, 0)),
                  pl.BlockSpec((None, D_MODEL, D_IN), lambda i: (layer, 0, 0))],
        out_specs=pl.BlockSpec((tm, D_IN), lambda i: (i, 0)),
        out_shape=jax.ShapeDtypeStruct((n, D_IN), F32),
        compiler_params=_cparams(1),
        name="proj",
    )(x, w_in_bf)


def _retention_consts(chunk):
    lg = np.log1p(-np.exp2(-5.0 - np.arange(C_HEADS, dtype=np.float64)))
    idx = np.arange(chunk, dtype=np.float64)
    rel = idx[:, None] - idx[None, :]
    mask = rel >= 0
    dmat = np.where(mask[None], np.exp(np.where(mask, rel, 0.0)[None] * lg[:, None, None]), 0.0)
    qdec = np.exp((idx + 1.0)[None, :] * lg[:, None])
    kdec = np.exp((chunk - 1.0 - idx)[None, :] * lg[:, None])
    sdec = np.exp(chunk * lg)
    qdec_b = np.broadcast_to(qdec[:, :, None], (C_HEADS, chunk, C_DK))
    kdec_b = np.broadcast_to(kdec[:, :, None], (C_HEADS, chunk, C_DK))
    return (dmat.astype(np.float32), np.ascontiguousarray(qdec_b).astype(np.float32),
            np.ascontiguousarray(kdec_b).astype(np.float32), [float(v) for v in sdec])


def _rope_tables(positions):
    half = C_DK // 2
    inv = ROPE_BASE ** (-np.arange(half, dtype=np.float64) / half)
    ang = np.asarray(positions, dtype=np.float64)[:, None] * inv[None]
    cos = np.concatenate([np.cos(ang), np.cos(ang)], axis=-1)
    sin = np.concatenate([-np.sin(ang), np.sin(ang)], axis=-1)
    return cos.astype(np.float32), sin.astype(np.float32)


def _rope(x, cos, sin_signed):
    return x * cos + pltpu.roll(x, C_DK // 2, 1) * sin_signed


def _mix_prompt_kernel(p_ref, cos_ref, sin_ref, cw_ref, cb_ref, wri_ref, bri_ref, lam_ref,
                       lbl_ref, ng_ref, dmat_ref, qdec_ref, kdec_ref,
                       y_ref, h_ref, conv_ref, hg_ref, rt_ref,
                       prev_sc, hprev_sc, st_sc, sret_sc, kb_sc, bb_sc, vb_sc, oi_sc,
                       *, layer, tt, sdec):
    t = pl.program_id(1)
    nt = pl.num_programs(1)

    @pl.when(t == 0)
    def _():
        prev_sc[...] = jnp.zeros_like(prev_sc)
        hprev_sc[...] = jnp.zeros_like(hprev_sc)
        st_sc[...] = jnp.zeros_like(st_sc)
        sret_sc[...] = jnp.zeros_like(sret_sc)

    xa = p_ref[:, O_XA:O_XA + D_A]
    ga = p_ref[:, O_GA:O_GA + D_A]
    row = lax.broadcasted_iota(jnp.int32, (tt, D_A), 0)
    row8 = lax.broadcasted_iota(jnp.int32, (SUBLANES, D_A), 0)
    prev = prev_sc[...]

    def shifted(j):
        r = pltpu.roll(xa, j, 0)
        top = jnp.where(row8 < j, pltpu.roll(prev, j, 0), r[0:SUBLANES])
        return jnp.concatenate([top, r[SUBLANES:]], axis=0)

    u = cb_ref[...] + shifted(3) * cw_ref[0:1, :]
    u = u + shifted(2) * cw_ref[1:2, :]
    u = u + shifted(1) * cw_ref[2:3, :]
    u = u + xa * cw_ref[3:4, :]
    last8 = xa[tt - SUBLANES:tt]
    prev_sc[...] = last8

    a, bterm = _rglru_gates(u, wri_ref[...], bri_ref[...], lam_ref[...])
    s = 1
    while s < tt:
        keep = row >= s
        a_sh = jnp.where(keep, pltpu.roll(a, s, 0), 1.0)
        b_sh = jnp.where(keep, pltpu.roll(bterm, s, 0), 0.0)
        bterm = a * b_sh + bterm
        a = a * a_sh
        s *= 2
    h = a * hprev_sc[SUBLANES - 1:SUBLANES, :] + bterm
    hlast8 = h[tt - SUBLANES:tt]
    hprev_sc[...] = hlast8
    y_ref[:, 0:D_A] = h * _gelu_tanh(ga)

    @pl.when(t == nt - 1)
    def _():
        h_ref[...] = pltpu.roll(hlast8, 1, 0)[0:1]
        conv_ref[...] = pltpu.roll(last8, CONV_W - 1, 0)[0:CONV_W - 1]

    lb = _hgrn_lower_bound(lbl_ref[...], layer)
    ng = ng_ref[...]
    cl = B_CHUNK
    crow = lax.broadcasted_iota(jnp.int32, (cl, D_B), 0)
    ccol = lax.broadcasted_iota(jnp.int32, (cl, D_B), 1)
    causal = (ccol % B_DK) <= crow
    br = lax.broadcasted_iota(jnp.int32, (D_B, D_B), 0)
    bc = lax.broadcasted_iota(jnp.int32, (D_B, D_B), 1)
    head_mask = (br // B_DK) == (bc // B_DK)
    seg = jnp.where(head_mask, 1.0, 0.0).astype(BF16)

    def seg_mean(x):
        hi = x.astype(BF16)
        lo = (x - hi.astype(F32)).astype(BF16)
        tot = (jnp.dot(hi, seg, preferred_element_type=F32)
               + jnp.dot(lo, seg, preferred_element_type=F32))
        return tot * (1.0 / B_DK)

    def hgrn_chunk(c, carry):
        r0 = pl.multiple_of(c * cl, cl)
        rows = pl.ds(r0, cl)
        q = _silu(p_ref[rows, O_QB:O_QB + D_B])
        fl = p_ref[rows, O_FB:O_FB + D_B]
        v = p_ref[rows, O_VB:O_VB + D_B]
        g = p_ref[rows, O_GB:O_GB + D_B]
        f = lb + (1.0 - lb) * _sigmoid(fl)
        b = jnp.log(jnp.maximum(f, F_TINY))
        k = (1.0 - lb) * _sigmoid(-fl)
        sh = 1
        while sh < cl:
            b = b + jnp.where(crow >= sh, pltpu.roll(b, sh, 0), 0.0)
            sh *= 2
        b_last = b[cl - SUBLANES:cl][SUBLANES - 1:SUBLANES]
        qd = q * jnp.exp(b)
        kl = k * jnp.exp(b_last - b)
        st = st_sc[...]
        o_inter = _dot_g(qd, st, _NT)
        safe = jnp.min(b_last) >= HGRN_SAFE_MIN_LOGDECAY

        @pl.when(safe)
        def _():
            kinv = k * jnp.exp(-b)
            k4 = jnp.where(head_mask, jnp.concatenate([kinv] * B_HEADS, axis=0), 0.0)
            sc = jnp.where(causal, _dot_g(qd, k4, _NT), 0.0)
            v4 = jnp.where(head_mask, jnp.concatenate([v] * B_HEADS, axis=0), 0.0)
            oi_sc[...] = _dot(sc, v4)

        @pl.when(jnp.logical_not(safe))
        def _():
            kb_sc[...] = k
            bb_sc[...] = b
            vb_sc[...] = v

            def pair(sidx, acc):
                ks = kb_sc[pl.ds(sidx, 1), :]
                bs = bb_sc[pl.ds(sidx, 1), :]
                vs = vb_sc[pl.ds(sidx, 1), :]
                e = jnp.exp(jnp.minimum(b - bs, 0.0)) * (q * ks)
                scr = jnp.dot(e.astype(BF16), seg, preferred_element_type=F32)
                return acc + jnp.where(crow >= sidx, scr, 0.0) * vs

            oi_sc[...] = lax.fori_loop(0, cl, pair, jnp.zeros((cl, D_B), F32))

        o = oi_sc[...] + o_inter
        st_sc[...] = st * jnp.exp(b_last) + jnp.where(head_mask, _dot_g(v, kl, _TN), 0.0)
        o = o * lax.rsqrt(seg_mean(o * o) + RMS_EPS) * ng
        y_ref[rows, D_A:D_A + D_B] = o * _silu(g)
        return carry

    lax.fori_loop(0, tt // cl, hgrn_chunk, 0)

    @pl.when(t == nt - 1)
    def _():
        s_bd = st_sc[...].T
        for hh in range(B_HEADS):
            hg_ref[hh] = s_bd[hh * B_DK:(hh + 1) * B_DK, hh * B_DK:(hh + 1) * B_DK]

    rl = C_CHUNK

    def ret_chunk(c, carry):
        r0 = pl.multiple_of(c * rl, rl)
        rows = pl.ds(r0, rl)
        cos = cos_ref[rows, :]
        sin = sin_ref[rows, :]
        for hh in range(C_HEADS):
            lo = hh * C_DK
            q = _rope(p_ref[rows, O_QC + lo:O_QC + lo + C_DK], cos, sin)
            k = _rope(p_ref[rows, O_KC + lo:O_KC + lo + C_DK], cos, sin) * (C_DK ** -0.5)
            v = p_ref[rows, O_VC + lo:O_VC + lo + C_DK]
            g = p_ref[rows, O_GC + lo:O_GC + lo + C_DK]
            sc = _dot_g(q, k, _NT) * dmat_ref[hh]
            s_old = sret_sc[hh]
            o = _dot(sc, v) + _dot(q * qdec_ref[hh], s_old)
            sret_sc[hh] = sdec[hh] * s_old + _dot_g(k * kdec_ref[hh], v, _TN)
            mu = jnp.mean(o, axis=-1, keepdims=True)
            oc = o - mu
            var = jnp.mean(oc * oc, axis=-1, keepdims=True)
            o = oc * lax.rsqrt(var + GN_EPS)
            y_ref[rows, D_A + D_B + lo:D_A + D_B + lo + C_DK] = o * _silu(g)
        return carry

    lax.fori_loop(0, tt // rl, ret_chunk, 0)

    @pl.when(t == nt - 1)
    def _():
        rt_ref[...] = sret_sc[...]


def _mix_prompt(proj, batch, seq, layer, tt, cw, cb, wri, bri, lam, lbl, ng):
    nt = seq // tt
    dmat, qdec, kdec, sdec = _retention_consts(C_CHUNK)
    cos, sin = _rope_tables(np.arange(seq))
    full = lambda shape: pl.BlockSpec(shape, lambda b, t: (0,) * len(shape))
    lay = lambda shape: pl.BlockSpec((None,) + shape, lambda b, t: (layer,) + (0,) * len(shape))
    kern = functools.partial(_mix_prompt_kernel, layer=layer, tt=tt, sdec=sdec)
    return pl.pallas_call(
        kern,
        grid=(batch, nt),
        in_specs=[
            pl.BlockSpec((tt, D_IN), lambda b, t: (b * nt + t, 0)),
            pl.BlockSpec((tt, C_DK), lambda b, t: (t, 0)),
            pl.BlockSpec((tt, C_DK), lambda b, t: (t, 0)),
            lay((CONV_W, D_A)), lay((1, D_A)), lay((D_A, 2 * D_A)), lay((1, 2 * D_A)),
            lay((1, D_A)), full((DEPTH, D_B)), lay((1, D_B)),
            full((C_HEADS, C_CHUNK, C_CHUNK)), full((C_HEADS, C_CHUNK, C_DK)),
            full((C_HEADS, C_CHUNK, C_DK)),
        ],
        out_specs=[
            pl.BlockSpec((tt, D_MODEL), lambda b, t: (b * nt + t, 0)),
            pl.BlockSpec((None, 1, D_A), lambda b, t: (b, 0, 0)),
            pl.BlockSpec((None, CONV_W - 1, D_A), lambda b, t: (b, 0, 0)),
            pl.BlockSpec((None, B_HEADS, B_DK, B_DK), lambda b, t: (b, 0, 0, 0)),
            pl.BlockSpec((None, C_HEADS, C_DK, C_DK), lambda b, t: (b, 0, 0, 0)),
        ],
        out_shape=[
            jax.ShapeDtypeStruct((batch * seq, D_MODEL), F32),
            jax.ShapeDtypeStruct((batch, 1, D_A), F32),
            jax.ShapeDtypeStruct((batch, CONV_W - 1, D_A), F32),
            jax.ShapeDtypeStruct((batch, B_HEADS, B_DK, B_DK), F32),
            jax.ShapeDtypeStruct((batch, C_HEADS, C_DK, C_DK), F32),
        ],
        scratch_shapes=[
            pltpu.VMEM((SUBLANES, D_A), F32), pltpu.VMEM((SUBLANES, D_A), F32),
            pltpu.VMEM((D_B, D_B), F32), pltpu.VMEM((C_HEADS, C_DK, C_DK), F32),
            pltpu.VMEM((B_CHUNK, D_B), F32), pltpu.VMEM((B_CHUNK, D_B), F32),
            pltpu.VMEM((B_CHUNK, D_B), F32), pltpu.VMEM((B_CHUNK, D_B), F32),
        ],
        compiler_params=_cparams(2),
        name="mix_prompt",
    )(proj, cos, sin, cw, cb, wri, bri, lam, lbl, ng, dmat, qdec, kdec)


def _column_matrix(x):
    pad = jnp.zeros((LANES - SUBLANES, LANES), F32)
    return jnp.concatenate([x, pad], axis=0).T


def _mix_sample_kernel(p_ref, h0_ref, conv0_ref, hg0_ref, rt0_ref, cos_ref, sin_ref,
                       cw_ref, cb_ref, wri_ref, bri_ref, lam_ref, lbl_ref, ng_ref,
                       y_ref, h_ref, conv_ref, hg_ref, rt_ref, o_sc, *, layer, gammas):
    tb = SUBLANES
    xa = p_ref[:, O_XA:O_XA + D_A]
    ga = p_ref[:, O_GA:O_GA + D_A]
    c0, c1, c2 = conv0_ref[0], conv0_ref[1], conv0_ref[2]
    u = cb_ref[...] + c0 * cw_ref[0:1, :]
    u = u + c1 * cw_ref[1:2, :]
    u = u + c2 * cw_ref[2:3, :]
    u = u + xa * cw_ref[3:4, :]
    conv_ref[0] = c1
    conv_ref[1] = c2
    conv_ref[2] = xa
    a, bterm = _rglru_gates(u, wri_ref[...], bri_ref[...], lam_ref[...])
    h = a * h0_ref[...] + bterm
    h_ref[...] = h
    y_ref[:, 0:D_A] = h * _gelu_tanh(ga)

    lb = _hgrn_lower_bound(lbl_ref[...], layer)
    for j in range(B_HEADS // 2):
        lo = j * LANES
        q = _silu(p_ref[:, O_QB + lo:O_QB + lo + LANES])
        fl = p_ref[:, O_FB + lo:O_FB + lo + LANES]
        v = p_ref[:, O_VB + lo:O_VB + lo + LANES]
        lbj = lb[:, lo:lo + LANES]
        f = lbj + (1.0 - lbj) * _sigmoid(fl)
        ef = jnp.exp(jnp.log(jnp.maximum(f, F_TINY)))
        k = (1.0 - lbj) * _sigmoid(-fl)
        qk = q * k
        qf_cols = _column_matrix(q * ef)
        k_cols = _column_matrix(k)
        f_cols = _column_matrix(ef)
        for hh in range(2):
            head = 2 * j + hh
            sl = slice(hh * B_DK, (hh + 1) * B_DK)
            dots = jnp.sum(qk[:, sl], axis=-1, keepdims=True)
            for b in range(tb):
                s_old = hg0_ref[b, head]
                v_row = v[b:b + 1, sl]
                hg_ref[b, head] = f_cols[sl, b:b + 1] * s_old + k_cols[sl, b:b + 1] * v_row
                o_row = (jnp.sum(qf_cols[sl, b:b + 1] * s_old, axis=0, keepdims=True)
                         + dots[b:b + 1, :] * v_row)
                o_sc[b:b + 1, head * B_DK:(head + 1) * B_DK] = o_row
    ng = ng_ref[...]
    for head in range(B_HEADS):
        sl = slice(head * B_DK, (head + 1) * B_DK)
        o = o_sc[:, sl]
        o = o * lax.rsqrt(jnp.mean(o * o, axis=-1, keepdims=True) + RMS_EPS) * ng[:, sl]
        g = p_ref[:, O_GB + head * B_DK:O_GB + (head + 1) * B_DK]
        y_ref[:, D_A + head * B_DK:D_A + (head + 1) * B_DK] = o * _silu(g)

    cos = cos_ref[...]
    sin = sin_ref[...]
    for head in range(C_HEADS):
        lo = head * C_DK
        gamma = gammas[head]
        q = _rope(p_ref[:, O_QC + lo:O_QC + lo + C_DK], cos, sin)
        k = _rope(p_ref[:, O_KC + lo:O_KC + lo + C_DK], cos, sin) * (C_DK ** -0.5)
        v = p_ref[:, O_VC + lo:O_VC + lo + C_DK]
        g = p_ref[:, O_GC + lo:O_GC + lo + C_DK]
        dots = jnp.sum(q * k, axis=-1, keepdims=True)
        q_cols = _column_matrix(q * gamma)
        k_cols = _column_matrix(k)
        for b in range(tb):
            s_old = rt0_ref[b, head]
            v_row = v[b:b + 1, :]
            rt_ref[b, head] = gamma * s_old + k_cols[:, b:b + 1] * v_row
            o_row = (jnp.sum(q_cols[:, b:b + 1] * s_old, axis=0, keepdims=True)
                     + dots[b:b + 1, :] * v_row)
            o_sc[b:b + 1, 0:C_DK] = o_row
        o = o_sc[:, 0:C_DK]
        mu = jnp.mean(o, axis=-1, keepdims=True)
        oc = o - mu
        var = jnp.mean(oc * oc, axis=-1, keepdims=True)
        y_ref[:, D_A + D_B + lo:D_A + D_B + lo + C_DK] = oc * lax.rsqrt(var + GN_EPS) * _silu(g)


def _mix_sample(proj, h0, conv0_t, hg0, rt0, layer, cw, cb, wri, bri, lam, lbl, ng):
    nb = proj.shape[0]
    tb = SUBLANES
    lg = np.log1p(-np.exp2(-5.0 - np.arange(C_HEADS, dtype=np.float64)))
    gammas = [float(np.exp(v)) for v in lg]
    cos, sin = _rope_tables([PAST_LEN])
    full = lambda shape: pl.BlockSpec(shape, lambda i: (0,) * len(shape))
    lay = lambda shape: pl.BlockSpec((None,) + shape, lambda i: (layer,) + (0,) * len(shape))
    kern = functools.partial(_mix_sample_kernel, layer=layer, gammas=gammas)
    return pl.pallas_call(
        kern,
        grid=(nb // tb,),
        in_specs=[
            pl.BlockSpec((tb, D_IN), lambda i: (i, 0)),
            pl.BlockSpec((None, tb, D_A), lambda i: (layer, i, 0)),
            pl.BlockSpec((None, CONV_W - 1, tb, D_A), lambda i: (layer, 0, i, 0)),
            pl.BlockSpec((None, tb, B_HEADS, B_DK, B_DK), lambda i: (layer, i, 0, 0, 0)),
            pl.BlockSpec((None, tb, C_HEADS, C_DK, C_DK), lambda i: (layer, i, 0, 0, 0)),
            full((1, C_DK)), full((1, C_DK)),
            lay((CONV_W, D_A)), lay((1, D_A)), lay((D_A, 2 * D_A)), lay((1, 2 * D_A)),
            lay((1, D_A)), full((DEPTH, D_B)), lay((1, D_B)),
        ],
        out_specs=[
            pl.BlockSpec((tb, D_MODEL), lambda i: (i, 0)),
            pl.BlockSpec((tb, D_A), lambda i: (i, 0)),
            pl.BlockSpec((CONV_W - 1, tb, D_A), lambda i: (0, i, 0)),
            pl.BlockSpec((tb, B_HEADS, B_DK, B_DK), lambda i: (i, 0, 0, 0)),
            pl.BlockSpec((tb, C_HEADS, C_DK, C_DK), lambda i: (i, 0, 0, 0)),
        ],
        out_shape=[
            jax.ShapeDtypeStruct((nb, D_MODEL), F32),
            jax.ShapeDtypeStruct((nb, D_A), F32),
            jax.ShapeDtypeStruct((CONV_W - 1, nb, D_A), F32),
            jax.ShapeDtypeStruct((nb, B_HEADS, B_DK, B_DK), F32),
            jax.ShapeDtypeStruct((nb, C_HEADS, C_DK, C_DK), F32),
        ],
        scratch_shapes=[pltpu.VMEM((tb, D_B), F32)],
        compiler_params=_cparams(1),
        name="mix_sample",
    )(proj, h0, conv0_t, hg0, rt0, cos, sin, cw, cb, wri, bri, lam, lbl, ng)


def _route_rows(l):
    m = l[0]
    for x in l[1:]:
        m = jnp.maximum(m, x)
    ex = [jnp.exp(x - m) for x in l]
    tot = ex[0]
    for x in ex[1:]:
        tot = tot + x
    p = [x / tot for x in ex]
    scores = []
    for gi in range(N_GROUPS):
        a, b, c, d = p[4 * gi:4 * gi + 4]
        hi1, lo1 = jnp.maximum(a, b), jnp.minimum(a, b)
        hi2, lo2 = jnp.maximum(c, d), jnp.minimum(c, d)
        top1 = jnp.maximum(hi1, hi2)
        top2 = jnp.maximum(jnp.minimum(hi1, hi2), jnp.maximum(lo1, lo2))
        scores.append(top1 + top2)
    best = scores[0]
    gsel = jnp.zeros_like(best, dtype=jnp.int32)
    for gi in range(1, N_GROUPS):
        upd = scores[gi] > best
        gsel = jnp.where(upd, gi, gsel)
        best = jnp.where(upd, scores[gi], best)
    vals = []
    for j in range(EXP_PER_GROUP):
        v = p[j]
        for gi in range(1, N_GROUPS):
            v = jnp.where(gsel == gi, p[4 * gi + j], v)
        vals.append(v)
    sel = []
    for j in range(EXP_PER_GROUP):
        rank = jnp.zeros_like(gsel)
        for i in range(EXP_PER_GROUP):
            if i == j:
                continue
            ahead = (vals[i] > vals[j]) | ((vals[i] == vals[j]) & (i < j))
            rank = rank + jnp.where(ahead, 1, 0)
        sel.append(rank < 2)
    denom = jnp.zeros_like(best)
    for j in range(EXP_PER_GROUP):
        denom = denom + jnp.where(sel[j], vals[j], 0.0)
    gates = [jnp.where(sel[j], vals[j] / denom, 0.0) for j in range(EXP_PER_GROUP)]
    return [jnp.where(gsel == (e // EXP_PER_GROUP), gates[e % EXP_PER_GROUP], 0.0)
            for e in range(N_EXPERTS)]


def _outproj_kernel(y_ref, x_ref, w_ref, g_ref, b_ref, rwt_ref, rb_ref, x1_ref, route_ref, rt_sc):
    z = ALPHA * x_ref[...] + jnp.dot(y_ref[...].astype(BF16), w_ref[...], preferred_element_type=F32)
    x1 = _layernorm(z, g_ref[...], b_ref[...])
    x1_ref[...] = x1
    hi = x1.astype(BF16)
    lo = (x1 - hi.astype(F32)).astype(BF16)
    rw = rwt_ref[...]
    rhi = rw.astype(BF16)
    rlo = (rw - rhi.astype(F32)).astype(BF16)
    lg = (lax.dot_general(rhi, hi, _NT, preferred_element_type=F32)
          + lax.dot_general(rhi, lo, _NT, preferred_element_type=F32)
          + lax.dot_general(rlo, hi, _NT, preferred_element_type=F32)) + rb_ref[...]
    comb = _route_rows([lg[e:e + 1, :] for e in range(N_EXPERTS)])
    rt_sc[...] = jnp.zeros_like(rt_sc)
    for e in range(N_EXPERTS):
        rt_sc[e:e + 1, :] = comb[e]
    route_ref[...] = rt_sc[...].T


def _outproj(y, x, w_out_bf, layer, tm, g, b, rwt, rb):
    n = x.shape[0]
    full = lambda shape: pl.BlockSpec(shape, lambda i: (0,) * len(shape))
    lay = lambda shape: pl.BlockSpec((None,) + shape, lambda i: (layer,) + (0,) * len(shape))
    return pl.pallas_call(
        _outproj_kernel,
        grid=(n // tm,),
        in_specs=[pl.BlockSpec((tm, D_MODEL), lambda i: (i, 0)),
                  pl.BlockSpec((tm, D_MODEL), lambda i: (i, 0)),
                  lay((D_MODEL, D_MODEL)), lay((1, D_MODEL)), lay((1, D_MODEL)),
                  full((N_EXPERTS, D_MODEL)), full((N_EXPERTS, 1))],
        out_specs=[pl.BlockSpec((tm, D_MODEL), lambda i: (i, 0)),
                   pl.BlockSpec((tm, LANES), lambda i: (i, 0))],
        out_shape=[jax.ShapeDtypeStruct((n, D_MODEL), F32),
                   jax.ShapeDtypeStruct((n, LANES), F32)],
        scratch_shapes=[pltpu.VMEM((LANES, tm), F32)],
        compiler_params=_cparams(1),
        name="outproj",
    )(y, x, w_out_bf, g, b, rwt, rb)


def _moe_kernel(x_ref, r_ref, wg_ref, wu_ref, wd_ref, g_ref, b_ref, o_ref, xb_sc, acc_sc):
    e = pl.program_id(1)

    @pl.when(e == 0)
    def _():
        xb_sc[...] = x_ref[...].astype(BF16)
        acc_sc[...] = jnp.zeros_like(acc_sc)

    xb = xb_sc[...]
    hg = jnp.dot(xb, wg_ref[...], preferred_element_type=F32)
    hu = jnp.dot(xb, wu_ref[...], preferred_element_type=F32)
    r = r_ref[...]
    lane = lax.broadcasted_iota(jnp.int32, r.shape, 1)
    c = jnp.sum(jnp.where(lane == e, r, 0.0), axis=-1, keepdims=True)
    h = _silu(hg) * hu * c
    acc_sc[...] += jnp.dot(h.astype(BF16), wd_ref[...], preferred_element_type=F32)

    @pl.when(e == pl.num_programs(1) - 1)
    def _():
        o_ref[...] = _layernorm(ALPHA * x_ref[...] + acc_sc[...], g_ref[...], b_ref[...])


def _moe(x1, route, wg_bf, wu_bf, wd_bf, layer, tm, g, b):
    n = x1.shape[0]
    lay = lambda shape: pl.BlockSpec((None,) + shape, lambda i, e: (layer,) + (0,) * len(shape))
    return pl.pallas_call(
        _moe_kernel,
        grid=(n // tm, N_EXPERTS),
        in_specs=[pl.BlockSpec((tm, D_MODEL), lambda i, e: (i, 0)),
                  pl.BlockSpec((tm, LANES), lambda i, e: (i, 0)),
                  pl.BlockSpec((None, None, D_MODEL, D_EXPERT), lambda i, e: (layer, e, 0, 0)),
                  pl.BlockSpec((None, None, D_MODEL, D_EXPERT), lambda i, e: (layer, e, 0, 0)),
                  pl.BlockSpec((None, None, D_EXPERT, D_MODEL), lambda i, e: (layer, e, 0, 0)),
                  lay((1, D_MODEL)), lay((1, D_MODEL))],
        out_specs=pl.BlockSpec((tm, D_MODEL), lambda i, e: (i, 0)),
        out_shape=jax.ShapeDtypeStruct((n, D_MODEL), F32),
        scratch_shapes=[pltpu.VMEM((tm, D_MODEL), BF16), pltpu.VMEM((tm, D_MODEL), F32)],
        compiler_params=_cparams(2),
        name="moe",
    )(x1, route, wg_bf, wu_bf, wd_bf, g, b)


def _block_diag(w):
    out = jnp.zeros((DEPTH, D_A, D_A), w.dtype)
    for gi in range(A_BLOCKS):
        out = out.at[:, gi * A_BLK:(gi + 1) * A_BLK, gi * A_BLK:(gi + 1) * A_BLK].set(w[:, gi])
    return out


def kernel(x_prompt, x_sample, state_rglru_h, state_conv, state_hgrn, state_ret, w_in, conv_w, conv_b, w_rgate, b_rgate, w_igate, b_igate, rglru_lambda, hgrn_lb_logits, hgrn_norm_g, w_out, ln1_g, ln1_b, router_w, router_b, exp_w_gate, exp_w_up, exp_w_down, ln2_g, ln2_b):
    batch, seq, _ = x_prompt.shape
    nb = x_sample.shape[0]

    w_in_bf = w_in.astype(BF16)
    w_out_bf = w_out.astype(BF16)
    wg_bf = exp_w_gate.astype(BF16)
    wu_bf = exp_w_up.astype(BF16)
    wd_bf = exp_w_down.astype(BF16)
    wri = jnp.concatenate([_block_diag(w_rgate), _block_diag(w_igate)], axis=-1).astype(BF16)
    bri = jnp.concatenate([b_rgate.reshape(DEPTH, 1, D_A), b_igate.reshape(DEPTH, 1, D_A)], axis=-1)
    cb = conv_b.reshape(DEPTH, 1, D_A)
    lam = rglru_lambda.reshape(DEPTH, 1, D_A)
    ng = jnp.tile(hgrn_norm_g, (1, B_HEADS)).reshape(DEPTH, 1, D_B)
    g1, b1 = ln1_g.reshape(DEPTH, 1, D_MODEL), ln1_b.reshape(DEPTH, 1, D_MODEL)
    g2, b2 = ln2_g.reshape(DEPTH, 1, D_MODEL), ln2_b.reshape(DEPTH, 1, D_MODEL)
    rwt = router_w.T
    rb = router_b.reshape(N_EXPERTS, 1)
    conv0_t = jnp.transpose(state_conv, (0, 2, 1, 3))

    xp = x_prompt.reshape(batch * seq, D_MODEL)
    xs = x_sample.reshape(nb, D_MODEL)
    hs_p, convs_p, hgs_p, rts_p = [], [], [], []
    hs_s, convs_s, hgs_s, rts_s = [], [], [], []
    for l in range(DEPTH):
        proj_p = _proj(xp, w_in_bf, l, 512)
        y_p, h_p, conv_p, hg_p, rt_p = _mix_prompt(
            proj_p, batch, seq, l, 512, conv_w, cb, wri, bri, lam, hgrn_lb_logits, ng)
        x1_p, route_p = _outproj(y_p, xp, w_out_bf, l, 512, g1, b1, rwt, rb)
        xp = _moe(x1_p, route_p, wg_bf, wu_bf, wd_bf, l, 1024, g2, b2)
        hs_p.append(h_p.reshape(batch, D_A))
        convs_p.append(conv_p)
        hgs_p.append(hg_p)
        rts_p.append(rt_p)

        proj_s = _proj(xs, w_in_bf, l, nb)
        y_s, h_s, conv_s, hg_s, rt_s = _mix_sample(
            proj_s, state_rglru_h, conv0_t, state_hgrn, state_ret, l,
            conv_w, cb, wri, bri, lam, hgrn_lb_logits, ng)
        x1_s, route_s = _outproj(y_s, xs, w_out_bf, l, nb, g1, b1, rwt, rb)
        xs = _moe(x1_s, route_s, wg_bf, wu_bf, wd_bf, l, nb, g2, b2)
        hs_s.append(h_s)
        convs_s.append(jnp.transpose(conv_s, (1, 0, 2)))
        hgs_s.append(hg_s)
        rts_s.append(rt_s)

    return (xp.reshape(batch, seq, D_MODEL), xs.reshape(nb, 1, D_MODEL),
            jnp.stack(hs_p), jnp.stack(hs_s), jnp.stack(convs_p), jnp.stack(convs_s),
            jnp.stack(hgs_p), jnp.stack(hgs_s), jnp.stack(rts_p), jnp.stack(rts_s))
```

```python
import functools

import numpy as np
import jax
import jax.numpy as jnp
from jax import lax
from jax.experimental import pallas as pl
from jax.experimental.pallas import tpu as pltpu

D_MODEL = 1024
DEPTH = 2
PAST_LEN = 16384
D_A = 256
A_BLOCKS = 4
A_BLK = 64
CONV_W = 4
RGLRU_C = 8.0
B_HEADS = 4
B_DK = 64
D_B = 256
C_HEADS = 4
C_DK = 128
D_C = 512
D_IN = 3584
B_CHUNK = 64
C_CHUNK = 128
ROPE_BASE = 10000.0
N_EXPERTS = 16
N_GROUPS = 4
EXP_PER_GROUP = 4
D_EXPERT = 512
LN_EPS = 1e-5
RMS_EPS = 1e-6
GN_EPS = 1e-6
F_TINY = 1e-30
ALPHA = (2 * DEPTH) ** 0.25

O_XA, O_GA, O_QB, O_FB, O_VB, O_GB, O_QC, O_KC, O_VC, O_GC = (
    0, 256, 512, 768, 1024, 1280, 1536, 2048, 2560, 3072)

V7X_VMEM_LIMIT_BYTES = 56 * 1024 * 1024
SUBLANES = 8
LANES = 128
HGRN_SAFE_MIN_LOGDECAY = -60.0

BF16 = jnp.bfloat16
F32 = jnp.float32
_NT = (((1,), (1,)), ((), ()))
_TN = (((0,), (0,)), ((), ()))


def _cparams(n_axes):
    return pltpu.CompilerParams(
        dimension_semantics=("arbitrary",) * n_axes,
        vmem_limit_bytes=V7X_VMEM_LIMIT_BYTES)


def _dot(a, b):
    return jnp.dot(a.astype(BF16), b.astype(BF16), preferred_element_type=F32)


def _dot_g(a, b, dims):
    return lax.dot_general(a.astype(BF16), b.astype(BF16), dims, preferred_element_type=F32)


def _sigmoid(x):
    return 1.0 / (1.0 + jnp.exp(-x))


def _silu(x):
    return x * _sigmoid(x)


def _gelu_tanh(x):
    c = np.float32(np.sqrt(2.0 / np.pi))
    return 0.5 * x * (1.0 + jnp.tanh(c * (x + np.float32(0.044715) * (x * x * x))))


def _log_sigmoid(x):
    return -(jnp.maximum(-x, 0.0) + jnp.log(1.0 + jnp.exp(-jnp.abs(x))))


def _layernorm(z, g, b):
    mu = jnp.mean(z, axis=-1, keepdims=True)
    zc = z - mu
    var = jnp.mean(zc * zc, axis=-1, keepdims=True)
    return zc * lax.rsqrt(var + LN_EPS) * g + b


def _hgrn_lower_bound(lbl, layer):
    rows = [lbl[j:j + 1, :] for j in range(DEPTH)]
    m = rows[0]
    for r in rows[1:]:
        m = jnp.maximum(m, r)
    ex = [jnp.exp(r - m) for r in rows]
    tot = ex[0]
    for e in ex[1:]:
        tot = tot + e
    lb = jnp.zeros_like(m)
    for j in range(1, layer + 1):
        lb = lb + ex[j] / tot
    return lb


def _rglru_gates(u, wri, bri, lam):
    gates = _dot(u, wri) + bri
    r = _sigmoid(gates[:, :D_A])
    i = _sigmoid(gates[:, D_A:])
    log_a = RGLRU_C * r * _log_sigmoid(lam)
    a = jnp.exp(log_a)
    bterm = jnp.sqrt(jnp.maximum(1.0 - jnp.exp(2.0 * log_a), 0.0)) * (i * u)
    return a, bterm


def _proj_kernel(x_ref, w_ref, o_ref):
    xb = x_ref[...].astype(BF16)
    for j in range(0, D_IN, 512):
        o_ref[:, j:j + 512] = jnp.dot(xb, w_ref[:, j:j + 512], preferred_element_type=F32)


def _proj(x, w_in_bf, layer, tm):
    n = x.shape[0]
    return pl.pallas_call(
        _proj_kernel,
        grid=(n // tm,),
        in_specs=[pl.BlockSpec((tm, D_MODEL), lambda i: (i, 0)),
                  pl.BlockSpec((None, D_MODEL, D_IN), lambda i: (layer, 0, 0))],
        out_specs=pl.BlockSpec((tm, D_IN), lambda i: (i, 0)),
        out_shape=jax.ShapeDtypeStruct((n, D_IN), F32),
        compiler_params=_cparams(1),
        name="proj",
    )(x, w_in_bf)


def _retention_consts(chunk):
    lg = np.log1p(-np.exp2(-5.0 - np.arange(C_HEADS, dtype=np.float64)))
    idx = np.arange(chunk, dtype=np.float64)
    rel = idx[:, None] - idx[None, :]
    mask = rel >= 0
    dmat = np.where(mask[None], np.exp(np.where(mask, rel, 0.0)[None] * lg[:, None, None]), 0.0)
    qdec = np.exp((idx + 1.0)[None, :] * lg[:, None])
    kdec = np.exp((chunk - 1.0 - idx)[None, :] * lg[:, None])
    sdec = np.exp(chunk * lg)
    qdec_b = np.broadcast_to(qdec[:, :, None], (C_HEADS, chunk, C_DK))
    kdec_b = np.broadcast_to(kdec[:, :, None], (C_HEADS, chunk, C_DK))
    return (dmat.astype(np.float32), np.ascontiguousarray(qdec_b).astype(np.float32),
            np.ascontiguousarray(kdec_b).astype(np.float32), [float(v) for v in sdec])


def _rope_tables(positions):
    half = C_DK // 2
    inv = ROPE_BASE ** (-np.arange(half, dtype=np.float64) / half)
    ang = np.asarray(positions, dtype=np.float64)[:, None] * inv[None]
    cos = np.concatenate([np.cos(ang), np.cos(ang)], axis=-1)
    sin = np.concatenate([-np.sin(ang), np.sin(ang)], axis=-1)
    return cos.astype(np.float32), sin.astype(np.float32)


def _rope(x, cos, sin_signed):
    return x * cos + pltpu.roll(x, C_DK // 2, 1) * sin_signed


def _mix_prompt_kernel(p_ref, cos_ref, sin_ref, cw_ref, cb_ref, wri_ref, bri_ref, lam_ref,
                       lbl_ref, ng_ref, dmat_ref, qdec_ref, kdec_ref,
                       y_ref, h_ref, conv_ref, hg_ref, rt_ref,
                       prev_sc, hprev_sc, st_sc, sret_sc, kb_sc, bb_sc, vb_sc, oi_sc,
                       *, layer, tt, sdec):
    t = pl.program_id(1)
    nt = pl.num_programs(1)

    @pl.when(t == 0)
    def _():
        prev_sc[...] = jnp.zeros_like(prev_sc)
        hprev_sc[...] = jnp.zeros_like(hprev_sc)
        st_sc[...] = jnp.zeros_like(st_sc)
        sret_sc[...] = jnp.zeros_like(sret_sc)

    xa = p_ref[:, O_XA:O_XA + D_A]
    ga = p_ref[:, O_GA:O_GA + D_A]
    row = lax.broadcasted_iota(jnp.int32, (tt, D_A), 0)
    row8 = lax.broadcasted_iota(jnp.int32, (SUBLANES, D_A), 0)
    prev = prev_sc[...]

    def shifted(j):
        r = pltpu.roll(xa, j, 0)
        top = jnp.where(row8 < j, pltpu.roll(prev, j, 0), r[0:SUBLANES])
        return jnp.concatenate([top, r[SUBLANES:]], axis=0)

    u = cb_ref[...] + shifted(3) * cw_ref[0:1, :]
    u = u + shifted(2) * cw_ref[1:2, :]
    u = u + shifted(1) * cw_ref[2:3, :]
    u = u + xa * cw_ref[3:4, :]
    last8 = xa[tt - SUBLANES:tt]
    prev_sc[...] = last8

    a, bterm = _rglru_gates(u, wri_ref[...], bri_ref[...], lam_ref[...])
    s = 1
    while s < tt:
        keep = row >= s
        a_sh = jnp.where(keep, pltpu.roll(a, s, 0), 1.0)
        b_sh = jnp.where(keep, pltpu.roll(bterm, s, 0), 0.0)
        bterm = a * b_sh + bterm
        a = a * a_sh
        s *= 2
    h = a * hprev_sc[SUBLANES - 1:SUBLANES, :] + bterm
    hlast8 = h[tt - SUBLANES:tt]
    hprev_sc[...] = hlast8
    y_ref[:, 0:D_A] = h * _gelu_tanh(ga)

    @pl.when(t == nt - 1)
    def _():
        h_ref[...] = pltpu.roll(hlast8, 1, 0)[0:1]
        conv_ref[...] = pltpu.roll(last8, CONV_W - 1, 0)[0:CONV_W - 1]

    lb = _hgrn_lower_bound(lbl_ref[...], layer)
    ng = ng_ref[...]
    cl = B_CHUNK
    crow = lax.broadcasted_iota(jnp.int32, (cl, D_B), 0)
    ccol = lax.broadcasted_iota(jnp.int32, (cl, D_B), 1)
    causal = (ccol % B_DK) <= crow
    br = lax.broadcasted_iota(jnp.int32, (D_B, D_B), 0)
    bc = lax.broadcasted_iota(jnp.int32, (D_B, D_B), 1)
    head_mask = (br // B_DK) == (bc // B_DK)
    seg = jnp.where(head_mask, 1.0, 0.0).astype(BF16)

    def seg_mean(x):
        hi = x.astype(BF16)
        lo = (x - hi.astype(F32)).astype(BF16)
        tot = (jnp.dot(hi, seg, preferred_element_type=F32)
               + jnp.dot(lo, seg, preferred_element_type=F32))
        return tot * (1.0 / B_DK)

    def hgrn_chunk(c, carry):
        r0 = pl.multiple_of(c * cl, cl)
        rows = pl.ds(r0, cl)
        q = _silu(p_ref[rows, O_QB:O_QB + D_B])
        fl = p_ref[rows, O_FB:O_FB + D_B]
        v = p_ref[rows, O_VB:O_VB + D_B]
        g = p_ref[rows, O_GB:O_GB + D_B]
        f = lb + (1.0 - lb) * _sigmoid(fl)
        b = jnp.log(jnp.maximum(f, F_TINY))
        k = (1.0 - lb) * _sigmoid(-fl)
        sh = 1
        while sh < cl:
            b = b + jnp.where(crow >= sh, pltpu.roll(b, sh, 0), 0.0)
            sh *= 2
        b_last = b[cl - SUBLANES:cl][SUBLANES - 1:SUBLANES]
        b_mid = b[cl // 2 - SUBLANES:cl // 2][SUBLANES - 1:SUBLANES]
        qd = q * jnp.exp(b)
        kl = k * jnp.exp(b_last - b)
        st = st_sc[...]
        o_inter = _dot_g(qd, st, _NT)
        safe = jnp.min(jnp.minimum(b_mid, b_last - b_mid)) >= HGRN_SAFE_MIN_LOGDECAY

        @pl.when(safe)
        def _():
            qmid = q * jnp.exp(b - b_mid)
            kinv = k * jnp.exp(b_mid - b)
            k4 = jnp.where(head_mask, jnp.concatenate([kinv] * B_HEADS, axis=0), 0.0)
            sc = jnp.where(causal, _dot_g(qmid, k4, _NT), 0.0)
            v4 = jnp.where(head_mask, jnp.concatenate([v] * B_HEADS, axis=0), 0.0)
            oi_sc[...] = _dot(sc, v4)

        @pl.when(jnp.logical_not(safe))
        def _():
            kb_sc[...] = k
            bb_sc[...] = b
            vb_sc[...] = v

            def pair(sidx, acc):
                ks = kb_sc[pl.ds(sidx, 1), :]
                bs = bb_sc[pl.ds(sidx, 1), :]
                vs = vb_sc[pl.ds(sidx, 1), :]
                e = jnp.exp(jnp.minimum(b - bs, 0.0)) * (q * ks)
                scr = jnp.dot(e.astype(BF16), seg, preferred_element_type=F32)
                return acc + jnp.where(crow >= sidx, scr, 0.0) * vs

            oi_sc[...] = lax.fori_loop(0, cl, pair, jnp.zeros((cl, D_B), F32))

        o = oi_sc[...] + o_inter
        st_sc[...] = st * jnp.exp(b_last) + jnp.where(head_mask, _dot_g(v, kl, _TN), 0.0)
        o = o * lax.rsqrt(seg_mean(o * o) + RMS_EPS) * ng
        y_ref[rows, D_A:D_A + D_B] = o * _silu(g)
        return carry

    lax.fori_loop(0, tt // cl, hgrn_chunk, 0)

    @pl.when(t == nt - 1)
    def _():
        s_bd = st_sc[...].T
        for hh in range(B_HEADS):
            hg_ref[hh] = s_bd[hh * B_DK:(hh + 1) * B_DK, hh * B_DK:(hh + 1) * B_DK]

    rl = C_CHUNK

    def ret_chunk(c, carry):
        r0 = pl.multiple_of(c * rl, rl)
        rows = pl.ds(r0, rl)
        cos = cos_ref[rows, :]
        sin = sin_ref[rows, :]
        for hh in range(C_HEADS):
            lo = hh * C_DK
            q = _rope(p_ref[rows, O_QC + lo:O_QC + lo + C_DK], cos, sin)
            k = _rope(p_ref[rows, O_KC + lo:O_KC + lo + C_DK], cos, sin) * (C_DK ** -0.5)
            v = p_ref[rows, O_VC + lo:O_VC + lo + C_DK]
            g = p_ref[rows, O_GC + lo:O_GC + lo + C_DK]
            sc = _dot_g(q, k, _NT) * dmat_ref[hh]
            s_old = sret_sc[hh]
            o = _dot(sc, v) + _dot(q * qdec_ref[hh], s_old)
            sret_sc[hh] = sdec[hh] * s_old + _dot_g(k * kdec_ref[hh], v, _TN)
            mu = jnp.mean(o, axis=-1, keepdims=True)
            oc = o - mu
            var = jnp.mean(oc * oc, axis=-1, keepdims=True)
            o = oc * lax.rsqrt(var + GN_EPS)
            y_ref[rows, D_A + D_B + lo:D_A + D_B + lo + C_DK] = o * _silu(g)
        return carry

    lax.fori_loop(0, tt // rl, ret_chunk, 0)

    @pl.when(t == nt - 1)
    def _():
        rt_ref[...] = sret_sc[...]


def _mix_prompt(proj, batch, seq, layer, tt, cw, cb, wri, bri, lam, lbl, ng):
    nt = seq // tt
    dmat, qdec, kdec, sdec = _retention_consts(C_CHUNK)
    cos, sin = _rope_tables(np.arange(seq))
    full = lambda shape: pl.BlockSpec(shape, lambda b, t: (0,) * len(shape))
    lay = lambda shape: pl.BlockSpec((None,) + shape, lambda b, t: (layer,) + (0,) * len(shape))
    kern = functools.partial(_mix_prompt_kernel, layer=layer, tt=tt, sdec=sdec)
    return pl.pallas_call(
        kern,
        grid=(batch, nt),
        in_specs=[
            pl.BlockSpec((tt, D_IN), lambda b, t: (b * nt + t, 0)),
            pl.BlockSpec((tt, C_DK), lambda b, t: (t, 0)),
            pl.BlockSpec((tt, C_DK), lambda b, t: (t, 0)),
            lay((CONV_W, D_A)), lay((1, D_A)), lay((D_A, 2 * D_A)), lay((1, 2 * D_A)),
            lay((1, D_A)), full((DEPTH, D_B)), lay((1, D_B)),
            full((C_HEADS, C_CHUNK, C_CHUNK)), full((C_HEADS, C_CHUNK, C_DK)),
            full((C_HEADS, C_CHUNK, C_DK)),
        ],
        out_specs=[
            pl.BlockSpec((tt, D_MODEL), lambda b, t: (b * nt + t, 0)),
            pl.BlockSpec((None, 1, D_A), lambda b, t: (b, 0, 0)),
            pl.BlockSpec((None, CONV_W - 1, D_A), lambda b, t: (b, 0, 0)),
            pl.BlockSpec((None, B_HEADS, B_DK, B_DK), lambda b, t: (b, 0, 0, 0)),
            pl.BlockSpec((None, C_HEADS, C_DK, C_DK), lambda b, t: (b, 0, 0, 0)),
        ],
        out_shape=[
            jax.ShapeDtypeStruct((batch * seq, D_MODEL), F32),
            jax.ShapeDtypeStruct((batch, 1, D_A), F32),
            jax.ShapeDtypeStruct((batch, CONV_W - 1, D_A), F32),
            jax.ShapeDtypeStruct((batch, B_HEADS, B_DK, B_DK), F32),
            jax.ShapeDtypeStruct((batch, C_HEADS, C_DK, C_DK), F32),
        ],
        scratch_shapes=[
            pltpu.VMEM((SUBLANES, D_A), F32), pltpu.VMEM((SUBLANES, D_A), F32),
            pltpu.VMEM((D_B, D_B), F32), pltpu.VMEM((C_HEADS, C_DK, C_DK), F32),
            pltpu.VMEM((B_CHUNK, D_B), F32), pltpu.VMEM((B_CHUNK, D_B), F32),
            pltpu.VMEM((B_CHUNK, D_B), F32), pltpu.VMEM((B_CHUNK, D_B), F32),
        ],
        compiler_params=_cparams(2),
        name="mix_prompt",
    )(proj, cos, sin, cw, cb, wri, bri, lam, lbl, ng, dmat, qdec, kdec)


def _column_matrix(x):
    pad = jnp.zeros((LANES - SUBLANES, LANES), F32)
    return jnp.concatenate([x, pad], axis=0).T


def _mix_sample_kernel(p_ref, h0_ref, conv0_ref, hg0_ref, rt0_ref, cos_ref, sin_ref,
                       cw_ref, cb_ref, wri_ref, bri_ref, lam_ref, lbl_ref, ng_ref,
                       y_ref, h_ref, conv_ref, hg_ref, rt_ref, o_sc, *, layer, gammas):
    tb = SUBLANES
    xa = p_ref[:, O_XA:O_XA + D_A]
    ga = p_ref[:, O_GA:O_GA + D_A]
    c0, c1, c2 = conv0_ref[0], conv0_ref[1], conv0_ref[2]
    u = cb_ref[...] + c0 * cw_ref[0:1, :]
    u = u + c1 * cw_ref[1:2, :]
    u = u + c2 * cw_ref[2:3, :]
    u = u + xa * cw_ref[3:4, :]
    conv_ref[0] = c1
    conv_ref[1] = c2
    conv_ref[2] = xa
    a, bterm = _rglru_gates(u, wri_ref[...], bri_ref[...], lam_ref[...])
    h = a * h0_ref[...] + bterm
    h_ref[...] = h
    y_ref[:, 0:D_A] = h * _gelu_tanh(ga)

    lb = _hgrn_lower_bound(lbl_ref[...], layer)
    for j in range(B_HEADS // 2):
        lo = j * LANES
        q = _silu(p_ref[:, O_QB + lo:O_QB + lo + LANES])
        fl = p_ref[:, O_FB + lo:O_FB + lo + LANES]
        v = p_ref[:, O_VB + lo:O_VB + lo + LANES]
        lbj = lb[:, lo:lo + LANES]
        f = lbj + (1.0 - lbj) * _sigmoid(fl)
        ef = jnp.exp(jnp.log(jnp.maximum(f, F_TINY)))
        k = (1.0 - lbj) * _sigmoid(-fl)
        qk = q * k
        qf_cols = _column_matrix(q * ef)
        k_cols = _column_matrix(k)
        f_cols = _column_matrix(ef)
        for hh in range(2):
            head = 2 * j + hh
            sl = slice(hh * B_DK, (hh + 1) * B_DK)
            dots = jnp.sum(qk[:, sl], axis=-1, keepdims=True)
            for b in range(tb):
                s_old = hg0_ref[b, head]
                v_row = v[b:b + 1, sl]
                hg_ref[b, head] = f_cols[sl, b:b + 1] * s_old + k_cols[sl, b:b + 1] * v_row
                o_row = (jnp.sum(qf_cols[sl, b:b + 1] * s_old, axis=0, keepdims=True)
                         + dots[b:b + 1, :] * v_row)
                o_sc[b:b + 1, head * B_DK:(head + 1) * B_DK] = o_row
    ng = ng_ref[...]
    for head in range(B_HEADS):
        sl = slice(head * B_DK, (head + 1) * B_DK)
        o = o_sc[:, sl]
        o = o * lax.rsqrt(jnp.mean(o * o, axis=-1, keepdims=True) + RMS_EPS) * ng[:, sl]
        g = p_ref[:, O_GB + head * B_DK:O_GB + (head + 1) * B_DK]
        y_ref[:, D_A + head * B_DK:D_A + (head + 1) * B_DK] = o * _silu(g)

    cos = cos_ref[...]
    sin = sin_ref[...]
    for head in range(C_HEADS):
        lo = head * C_DK
        gamma = gammas[head]
        q = _rope(p_ref[:, O_QC + lo:O_QC + lo + C_DK], cos, sin)
        k = _rope(p_ref[:, O_KC + lo:O_KC + lo + C_DK], cos, sin) * (C_DK ** -0.5)
        v = p_ref[:, O_VC + lo:O_VC + lo + C_DK]
        g = p_ref[:, O_GC + lo:O_GC + lo + C_DK]
        dots = jnp.sum(q * k, axis=-1, keepdims=True)
        q_cols = _column_matrix(q * gamma)
        k_cols = _column_matrix(k)
        for b in range(tb):
            s_old = rt0_ref[b, head]
            v_row = v[b:b + 1, :]
            rt_ref[b, head] = gamma * s_old + k_cols[:, b:b + 1] * v_row
            o_row = (jnp.sum(q_cols[:, b:b + 1] * s_old, axis=0, keepdims=True)
                     + dots[b:b + 1, :] * v_row)
            o_sc[b:b + 1, 0:C_DK] = o_row
        o = o_sc[:, 0:C_DK]
        mu = jnp.mean(o, axis=-1, keepdims=True)
        oc = o - mu
        var = jnp.mean(oc * oc, axis=-1, keepdims=True)
        y_ref[:, D_A + D_B + lo:D_A + D_B + lo + C_DK] = oc * lax.rsqrt(var + GN_EPS) * _silu(g)


def _mix_sample(proj, h0, conv0_t, hg0, rt0, layer, cw, cb, wri, bri, lam, lbl, ng):
    nb = proj.shape[0]
    tb = SUBLANES
    lg = np.log1p(-np.exp2(-5.0 - np.arange(C_HEADS, dtype=np.float64)))
    gammas = [float(np.exp(v)) for v in lg]
    cos, sin = _rope_tables([PAST_LEN])
    full = lambda shape: pl.BlockSpec(shape, lambda i: (0,) * len(shape))
    lay = lambda shape: pl.BlockSpec((None,) + shape, lambda i: (layer,) + (0,) * len(shape))
    kern = functools.partial(_mix_sample_kernel, layer=layer, gammas=gammas)
    return pl.pallas_call(
        kern,
        grid=(nb // tb,),
        in_specs=[
            pl.BlockSpec((tb, D_IN), lambda i: (i, 0)),
            pl.BlockSpec((None, tb, D_A), lambda i: (layer, i, 0)),
            pl.BlockSpec((None, CONV_W - 1, tb, D_A), lambda i: (layer, 0, i, 0)),
            pl.BlockSpec((None, tb, B_HEADS, B_DK, B_DK), lambda i: (layer, i, 0, 0, 0)),
            pl.BlockSpec((None, tb, C_HEADS, C_DK, C_DK), lambda i: (layer, i, 0, 0, 0)),
            full((1, C_DK)), full((1, C_DK)),
            lay((CONV_W, D_A)), lay((1, D_A)), lay((D_A, 2 * D_A)), lay((1, 2 * D_A)),
            lay((1, D_A)), full((DEPTH, D_B)), lay((1, D_B)),
        ],
        out_specs=[
            pl.BlockSpec((tb, D_MODEL), lambda i: (i, 0)),
            pl.BlockSpec((tb, D_A), lambda i: (i, 0)),
            pl.BlockSpec((CONV_W - 1, tb, D_A), lambda i: (0, i, 0)),
            pl.BlockSpec((tb, B_HEADS, B_DK, B_DK), lambda i: (i, 0, 0, 0)),
            pl.BlockSpec((tb, C_HEADS, C_DK, C_DK), lambda i: (i, 0, 0, 0)),
        ],
        out_shape=[
            jax.ShapeDtypeStruct((nb, D_MODEL), F32),
            jax.ShapeDtypeStruct((nb, D_A), F32),
            jax.ShapeDtypeStruct((CONV_W - 1, nb, D_A), F32),
            jax.ShapeDtypeStruct((nb, B_HEADS, B_DK, B_DK), F32),
            jax.ShapeDtypeStruct((nb, C_HEADS, C_DK, C_DK), F32),
        ],
        scratch_shapes=[pltpu.VMEM((tb, D_B), F32)],
        compiler_params=_cparams(1),
        name="mix_sample",
    )(proj, h0, conv0_t, hg0, rt0, cos, sin, cw, cb, wri, bri, lam, lbl, ng)


def _route_rows(l):
    m = l[0]
    for x in l[1:]:
        m = jnp.maximum(m, x)
    ex = [jnp.exp(x - m) for x in l]
    tot = ex[0]
    for x in ex[1:]:
        tot = tot + x
    p = [x / tot for x in ex]
    scores = []
    for gi in range(N_GROUPS):
        a, b, c, d = p[4 * gi:4 * gi + 4]
        hi1, lo1 = jnp.maximum(a, b), jnp.minimum(a, b)
        hi2, lo2 = jnp.maximum(c, d), jnp.minimum(c, d)
        top1 = jnp.maximum(hi1, hi2)
        top2 = jnp.maximum(jnp.minimum(hi1, hi2), jnp.maximum(lo1, lo2))
        scores.append(top1 + top2)
    best = scores[0]
    gsel = jnp.zeros_like(best, dtype=jnp.int32)
    for gi in range(1, N_GROUPS):
        upd = scores[gi] > best
        gsel = jnp.where(upd, gi, gsel)
        best = jnp.where(upd, scores[gi], best)
    vals = []
    for j in range(EXP_PER_GROUP):
        v = p[j]
        for gi in range(1, N_GROUPS):
            v = jnp.where(gsel == gi, p[4 * gi + j], v)
        vals.append(v)
    sel = []
    for j in range(EXP_PER_GROUP):
        rank = jnp.zeros_like(gsel)
        for i in range(EXP_PER_GROUP):
            if i == j:
                continue
            ahead = (vals[i] > vals[j]) | ((vals[i] == vals[j]) & (i < j))
            rank = rank + jnp.where(ahead, 1, 0)
        sel.append(rank < 2)
    denom = jnp.zeros_like(best)
    for j in range(EXP_PER_GROUP):
        denom = denom + jnp.where(sel[j], vals[j], 0.0)
    gates = [jnp.where(sel[j], vals[j] / denom, 0.0) for j in range(EXP_PER_GROUP)]
    return [jnp.where(gsel == (e // EXP_PER_GROUP), gates[e % EXP_PER_GROUP], 0.0)
            for e in range(N_EXPERTS)]


def _outproj_kernel(y_ref, x_ref, w_ref, g_ref, b_ref, rwt_ref, rb_ref, x1_ref, route_ref, rt_sc):
    z = ALPHA * x_ref[...] + jnp.dot(y_ref[...].astype(BF16), w_ref[...], preferred_element_type=F32)
    x1 = _layernorm(z, g_ref[...], b_ref[...])
    x1_ref[...] = x1
    hi = x1.astype(BF16)
    lo = (x1 - hi.astype(F32)).astype(BF16)
    rw = rwt_ref[...]
    rhi = rw.astype(BF16)
    rlo = (rw - rhi.astype(F32)).astype(BF16)
    lg = (lax.dot_general(rhi, hi, _NT, preferred_element_type=F32)
          + lax.dot_general(rhi, lo, _NT, preferred_element_type=F32)
          + lax.dot_general(rlo, hi, _NT, preferred_element_type=F32)) + rb_ref[...]
    comb = _route_rows([lg[e:e + 1, :] for e in range(N_EXPERTS)])
    rt_sc[...] = jnp.zeros_like(rt_sc)
    for e in range(N_EXPERTS):
        rt_sc[e:e + 1, :] = comb[e]
    route_ref[...] = rt_sc[...].T


def _outproj(y, x, w_out_bf, layer, tm, g, b, rwt, rb):
    n = x.shape[0]
    full = lambda shape: pl.BlockSpec(shape, lambda i: (0,) * len(shape))
    lay = lambda shape: pl.BlockSpec((None,) + shape, lambda i: (layer,) + (0,) * len(shape))
    return pl.pallas_call(
        _outproj_kernel,
        grid=(n // tm,),
        in_specs=[pl.BlockSpec((tm, D_MODEL), lambda i: (i, 0)),
                  pl.BlockSpec((tm, D_MODEL), lambda i: (i, 0)),
                  lay((D_MODEL, D_MODEL)), lay((1, D_MODEL)), lay((1, D_MODEL)),
                  full((N_EXPERTS, D_MODEL)), full((N_EXPERTS, 1))],
        out_specs=[pl.BlockSpec((tm, D_MODEL), lambda i: (i, 0)),
                   pl.BlockSpec((tm, LANES), lambda i: (i, 0))],
        out_shape=[jax.ShapeDtypeStruct((n, D_MODEL), F32),
                   jax.ShapeDtypeStruct((n, LANES), F32)],
        scratch_shapes=[pltpu.VMEM((LANES, tm), F32)],
        compiler_params=_cparams(1),
        name="outproj",
    )(y, x, w_out_bf, g, b, rwt, rb)


def _moe_kernel(x_ref, r_ref, wg_ref, wu_ref, wd_ref, g_ref, b_ref, o_ref, xb_sc, acc_sc):
    e = pl.program_id(1)

    @pl.when(e == 0)
    def _():
        xb_sc[...] = x_ref[...].astype(BF16)
        acc_sc[...] = jnp.zeros_like(acc_sc)

    xb = xb_sc[...]
    hg = jnp.dot(xb, wg_ref[...], preferred_element_type=F32)
    hu = jnp.dot(xb, wu_ref[...], preferred_element_type=F32)
    r = r_ref[...]
    lane = lax.broadcasted_iota(jnp.int32, r.shape, 1)
    c = jnp.sum(jnp.where(lane == e, r, 0.0), axis=-1, keepdims=True)
    h = _silu(hg) * hu * c
    acc_sc[...] += jnp.dot(h.astype(BF16), wd_ref[...], preferred_element_type=F32)

    @pl.when(e == pl.num_programs(1) - 1)
    def _():
        o_ref[...] = _layernorm(ALPHA * x_ref[...] + acc_sc[...], g_ref[...], b_ref[...])


def _moe(x1, route, wg_bf, wu_bf, wd_bf, layer, tm, g, b):
    n = x1.shape[0]
    lay = lambda shape: pl.BlockSpec((None,) + shape, lambda i, e: (layer,) + (0,) * len(shape))
    return pl.pallas_call(
        _moe_kernel,
        grid=(n // tm, N_EXPERTS),
        in_specs=[pl.BlockSpec((tm, D_MODEL), lambda i, e: (i, 0)),
                  pl.BlockSpec((tm, LANES), lambda i, e: (i, 0)),
                  pl.BlockSpec((None, None, D_MODEL, D_EXPERT), lambda i, e: (layer, e, 0, 0)),
                  pl.BlockSpec((None, None, D_MODEL, D_EXPERT), lambda i, e: (layer, e, 0, 0)),
                  pl.BlockSpec((None, None, D_EXPERT, D_MODEL), lambda i, e: (layer, e, 0, 0)),
                  lay((1, D_MODEL)), lay((1, D_MODEL))],
        out_specs=pl.BlockSpec((tm, D_MODEL), lambda i, e: (i, 0)),
        out_shape=jax.ShapeDtypeStruct((n, D_MODEL), F32),
        scratch_shapes=[pltpu.VMEM((tm, D_MODEL), BF16), pltpu.VMEM((tm, D_MODEL), F32)],
        compiler_params=_cparams(2),
        name="moe",
    )(x1, route, wg_bf, wu_bf, wd_bf, g, b)


def _block_diag(w):
    out = jnp.zeros((DEPTH, D_A, D_A), w.dtype)
    for gi in range(A_BLOCKS):
        out = out.at[:, gi * A_BLK:(gi + 1) * A_BLK, gi * A_BLK:(gi + 1) * A_BLK].set(w[:, gi])
    return out


def kernel(x_prompt, x_sample, state_rglru_h, state_conv, state_hgrn, state_ret, w_in, conv_w, conv_b, w_rgate, b_rgate, w_igate, b_igate, rglru_lambda, hgrn_lb_logits, hgrn_norm_g, w_out, ln1_g, ln1_b, router_w, router_b, exp_w_gate, exp_w_up, exp_w_down, ln2_g, ln2_b):
    batch, seq, _ = x_prompt.shape
    nb = x_sample.shape[0]

    w_in_bf = w_in.astype(BF16)
    w_out_bf = w_out.astype(BF16)
    wg_bf = exp_w_gate.astype(BF16)
    wu_bf = exp_w_up.astype(BF16)
    wd_bf = exp_w_down.astype(BF16)
    wri = jnp.concatenate([_block_diag(w_rgate), _block_diag(w_igate)], axis=-1).astype(BF16)
    bri = jnp.concatenate([b_rgate.reshape(DEPTH, 1, D_A), b_igate.reshape(DEPTH, 1, D_A)], axis=-1)
    cb = conv_b.reshape(DEPTH, 1, D_A)
    lam = rglru_lambda.reshape(DEPTH, 1, D_A)
    ng = jnp.tile(hgrn_norm_g, (1, B_HEADS)).reshape(DEPTH, 1, D_B)
    g1, b1 = ln1_g.reshape(DEPTH, 1, D_MODEL), ln1_b.reshape(DEPTH, 1, D_MODEL)
    g2, b2 = ln2_g.reshape(DEPTH, 1, D_MODEL), ln2_b.reshape(DEPTH, 1, D_MODEL)
    rwt = router_w.T
    rb = router_b.reshape(N_EXPERTS, 1)
    conv0_t = jnp.transpose(state_conv, (0, 2, 1, 3))

    xp = x_prompt.reshape(batch * seq, D_MODEL)
    xs = x_sample.reshape(nb, D_MODEL)
    hs_p, convs_p, hgs_p, rts_p = [], [], [], []
    hs_s, convs_s, hgs_s, rts_s = [], [], [], []
    for l in range(DEPTH):
        proj_p = _proj(xp, w_in_bf, l, 512)
        y_p, h_p, conv_p, hg_p, rt_p = _mix_prompt(
            proj_p, batch, seq, l, 512, conv_w, cb, wri, bri, lam, hgrn_lb_logits, ng)
        x1_p, route_p = _outproj(y_p, xp, w_out_bf, l, 512, g1, b1, rwt, rb)
        xp = _moe(x1_p, route_p, wg_bf, wu_bf, wd_bf, l, 1024, g2, b2)
        hs_p.append(h_p.reshape(batch, D_A))
        convs_p.append(conv_p)
        hgs_p.append(hg_p)
        rts_p.append(rt_p)

        proj_s = _proj(xs, w_in_bf, l, nb)
        y_s, h_s, conv_s, hg_s, rt_s = _mix_sample(
            proj_s, state_rglru_h, conv0_t, state_hgrn, state_ret, l,
            conv_w, cb, wri, bri, lam, hgrn_lb_logits, ng)
        x1_s, route_s = _outproj(y_s, xs, w_out_bf, l, nb, g1, b1, rwt, rb)
        xs = _moe(x1_s, route_s, wg_bf, wu_bf, wd_bf, l, nb, g2, b2)
        hs_s.append(h_s)
        convs_s.append(jnp.transpose(conv_s, (1, 0, 2)))
        hgs_s.append(hg_s)
        rts_s.append(rt_s)

    return (xp.reshape(batch, seq, D_MODEL), xs.reshape(nb, 1, D_MODEL),
            jnp.stack(hs_p), jnp.stack(hs_s), jnp.stack(convs_p), jnp.stack(convs_s),
            jnp.stack(hgs_p), jnp.stack(hgs_s), jnp.stack(rts_p), jnp.stack(rts_s))
```

```python
import functools

import numpy as np
import jax
import jax.numpy as jnp
from jax import lax
from jax.experimental import pallas as pl
from jax.experimental.pallas import tpu as pltpu

D_MODEL = 1024
DEPTH = 2
PAST_LEN = 16384
D_A = 256
A_BLOCKS = 4
A_BLK = 64
CONV_W = 4
RGLRU_C = 8.0
B_HEADS = 4
B_DK = 64
D_B = 256
C_HEADS = 4
C_DK = 128
D_C = 512
D_IN = 3584
B_CHUNK = 64
C_CHUNK = 128
ROPE_BASE = 10000.0
N_EXPERTS = 16
N_GROUPS = 4
EXP_PER_GROUP = 4
D_EXPERT = 512
LN_EPS = 1e-5
RMS_EPS = 1e-6
GN_EPS = 1e-6
F_TINY = 1e-30
ALPHA = (2 * DEPTH) ** 0.25

O_XA, O_GA, O_QB, O_FB, O_VB, O_GB, O_QC, O_KC, O_VC, O_GC = (
    0, 256, 512, 768, 1024, 1280, 1536, 2048, 2560, 3072)

V7X_VMEM_LIMIT_BYTES = 56 * 1024 * 1024
SUBLANES = 8
LANES = 128
HGRN_SAFE_MIN_LOGDECAY = -60.0

BF16 = jnp.bfloat16
F32 = jnp.float32
_NT = (((1,), (1,)), ((), ()))
_TN = (((0,), (0,)), ((), ()))


def _cparams(n_axes):
    return pltpu.CompilerParams(
        dimension_semantics=("arbitrary",) * n_axes,
        vmem_limit_bytes=V7X_VMEM_LIMIT_BYTES)


def _dot(a, b):
    return jnp.dot(a.astype(BF16), b.astype(BF16), preferred_element_type=F32)


def _dot_g(a, b, dims):
    return lax.dot_general(a.astype(BF16), b.astype(BF16), dims, preferred_element_type=F32)


def _sigmoid(x):
    return 1.0 / (1.0 + jnp.exp(-x))


def _silu(x):
    return x * _sigmoid(x)


def _gelu_tanh(x):
    c = np.float32(np.sqrt(2.0 / np.pi))
    return 0.5 * x * (1.0 + jnp.tanh(c * (x + np.float32(0.044715) * (x * x * x))))


def _log_sigmoid(x):
    return -(jnp.maximum(-x, 0.0) + jnp.log(1.0 + jnp.exp(-jnp.abs(x))))


def _layernorm(z, g, b):
    mu = jnp.mean(z, axis=-1, keepdims=True)
    zc = z - mu
    var = jnp.mean(zc * zc, axis=-1, keepdims=True)
    return zc * lax.rsqrt(var + LN_EPS) * g + b


def _hgrn_lower_bound(lbl, layer):
    rows = [lbl[j:j + 1, :] for j in range(DEPTH)]
    m = rows[0]
    for r in rows[1:]:
        m = jnp.maximum(m, r)
    ex = [jnp.exp(r - m) for r in rows]
    tot = ex[0]
    for e in ex[1:]:
        tot = tot + e
    lb = jnp.zeros_like(m)
    for j in range(1, layer + 1):
        lb = lb + ex[j] / tot
    return lb


def _rglru_gates(u, wri, bri, lam):
    gates = _dot(u, wri) + bri
    r = _sigmoid(gates[:, :D_A])
    i = _sigmoid(gates[:, D_A:])
    log_a = RGLRU_C * r * _log_sigmoid(lam)
    a = jnp.exp(log_a)
    bterm = jnp.sqrt(jnp.maximum(1.0 - jnp.exp(2.0 * log_a), 0.0)) * (i * u)
    return a, bterm


def _split(x):
    hi = x.astype(BF16)
    return hi, (x - hi.astype(F32)).astype(BF16)


def _dot3(x, w):
    xh, xl = _split(x)
    wh, wl = _split(w)
    return (jnp.dot(xh, wh, preferred_element_type=F32)
            + jnp.dot(xl, wh, preferred_element_type=F32)
            + jnp.dot(xh, wl, preferred_element_type=F32))


def _proj_kernel(x_ref, w_ref, o_ref):
    if w_ref.dtype == BF16:
        xb = x_ref[...].astype(BF16)
        for j in range(0, D_IN, 512):
            o_ref[:, j:j + 512] = jnp.dot(xb, w_ref[:, j:j + 512], preferred_element_type=F32)
    else:
        x = x_ref[...]
        for j in range(0, D_IN, 512):
            o_ref[:, j:j + 512] = _dot3(x, w_ref[:, j:j + 512])


def _proj(x, w_in_bf, layer, tm):
    n = x.shape[0]
    return pl.pallas_call(
        _proj_kernel,
        grid=(n // tm,),
        in_specs=[pl.BlockSpec((tm, D_MODEL), lambda i: (i, 0)),
                  pl.BlockSpec((None, D_MODEL, D_IN), lambda i: (layer, 0, 0))],
        out_specs=pl.BlockSpec((tm, D_IN), lambda i: (i, 0)),
        out_shape=jax.ShapeDtypeStruct((n, D_IN), F32),
        compiler_params=_cparams(1),
        name="proj",
    )(x, w_in_bf)


def _retention_consts(chunk):
    lg = np.log1p(-np.exp2(-5.0 - np.arange(C_HEADS, dtype=np.float64)))
    idx = np.arange(chunk, dtype=np.float64)
    rel = idx[:, None] - idx[None, :]
    mask = rel >= 0
    dmat = np.where(mask[None], np.exp(np.where(mask, rel, 0.0)[None] * lg[:, None, None]), 0.0)
    qdec = np.exp((idx + 1.0)[None, :] * lg[:, None])
    kdec = np.exp((chunk - 1.0 - idx)[None, :] * lg[:, None])
    sdec = np.exp(chunk * lg)
    qdec_b = np.broadcast_to(qdec[:, :, None], (C_HEADS, chunk, C_DK))
    kdec_b = np.broadcast_to(kdec[:, :, None], (C_HEADS, chunk, C_DK))
    return (dmat.astype(np.float32), np.ascontiguousarray(qdec_b).astype(np.float32),
            np.ascontiguousarray(kdec_b).astype(np.float32), [float(v) for v in sdec])


def _rope_tables(positions):
    half = C_DK // 2
    inv = ROPE_BASE ** (-np.arange(half, dtype=np.float64) / half)
    ang = np.asarray(positions, dtype=np.float64)[:, None] * inv[None]
    cos = np.concatenate([np.cos(ang), np.cos(ang)], axis=-1)
    sin = np.concatenate([-np.sin(ang), np.sin(ang)], axis=-1)
    return cos.astype(np.float32), sin.astype(np.float32)


def _rope(x, cos, sin_signed):
    return x * cos + pltpu.roll(x, C_DK // 2, 1) * sin_signed


def _mix_prompt_kernel(p_ref, cos_ref, sin_ref, cw_ref, cb_ref, wri_ref, bri_ref, lam_ref,
                       lbl_ref, ng_ref, dmat_ref, qdec_ref, kdec_ref,
                       y_ref, h_ref, conv_ref, hg_ref, rt_ref,
                       prev_sc, hprev_sc, st_sc, sret_sc, kb_sc, bb_sc, vb_sc, oi_sc,
                       *, layer, tt, sdec):
    t = pl.program_id(1)
    nt = pl.num_programs(1)

    @pl.when(t == 0)
    def _():
        prev_sc[...] = jnp.zeros_like(prev_sc)
        hprev_sc[...] = jnp.zeros_like(hprev_sc)
        st_sc[...] = jnp.zeros_like(st_sc)
        sret_sc[...] = jnp.zeros_like(sret_sc)

    xa = p_ref[:, O_XA:O_XA + D_A]
    ga = p_ref[:, O_GA:O_GA + D_A]
    row = lax.broadcasted_iota(jnp.int32, (tt, D_A), 0)
    row8 = lax.broadcasted_iota(jnp.int32, (SUBLANES, D_A), 0)
    prev = prev_sc[...]

    def shifted(j):
        r = pltpu.roll(xa, j, 0)
        top = jnp.where(row8 < j, pltpu.roll(prev, j, 0), r[0:SUBLANES])
        return jnp.concatenate([top, r[SUBLANES:]], axis=0)

    u = cb_ref[...] + shifted(3) * cw_ref[0:1, :]
    u = u + shifted(2) * cw_ref[1:2, :]
    u = u + shifted(1) * cw_ref[2:3, :]
    u = u + xa * cw_ref[3:4, :]
    last8 = xa[tt - SUBLANES:tt]
    prev_sc[...] = last8

    a, bterm = _rglru_gates(u, wri_ref[...], bri_ref[...], lam_ref[...])
    s = 1
    while s < tt:
        keep = row >= s
        a_sh = jnp.where(keep, pltpu.roll(a, s, 0), 1.0)
        b_sh = jnp.where(keep, pltpu.roll(bterm, s, 0), 0.0)
        bterm = a * b_sh + bterm
        a = a * a_sh
        s *= 2
    h = a * hprev_sc[SUBLANES - 1:SUBLANES, :] + bterm
    hlast8 = h[tt - SUBLANES:tt]
    hprev_sc[...] = hlast8
    y_ref[:, 0:D_A] = h * _gelu_tanh(ga)

    @pl.when(t == nt - 1)
    def _():
        h_ref[...] = pltpu.roll(hlast8, 1, 0)[0:1]
        conv_ref[...] = pltpu.roll(last8, CONV_W - 1, 0)[0:CONV_W - 1]

    lb = _hgrn_lower_bound(lbl_ref[...], layer)
    ng = ng_ref[...]
    cl = B_CHUNK
    crow = lax.broadcasted_iota(jnp.int32, (cl, D_B), 0)
    ccol = lax.broadcasted_iota(jnp.int32, (cl, D_B), 1)
    causal = (ccol % B_DK) <= crow
    br = lax.broadcasted_iota(jnp.int32, (D_B, D_B), 0)
    bc = lax.broadcasted_iota(jnp.int32, (D_B, D_B), 1)
    head_mask = (br // B_DK) == (bc // B_DK)
    seg = jnp.where(head_mask, 1.0, 0.0).astype(BF16)

    def seg_mean(x):
        hi = x.astype(BF16)
        lo = (x - hi.astype(F32)).astype(BF16)
        tot = (jnp.dot(hi, seg, preferred_element_type=F32)
               + jnp.dot(lo, seg, preferred_element_type=F32))
        return tot * (1.0 / B_DK)

    def hgrn_chunk(c, carry):
        r0 = pl.multiple_of(c * cl, cl)
        rows = pl.ds(r0, cl)
        q = _silu(p_ref[rows, O_QB:O_QB + D_B])
        fl = p_ref[rows, O_FB:O_FB + D_B]
        v = p_ref[rows, O_VB:O_VB + D_B]
        g = p_ref[rows, O_GB:O_GB + D_B]
        f = lb + (1.0 - lb) * _sigmoid(fl)
        b = jnp.log(jnp.maximum(f, F_TINY))
        k = (1.0 - lb) * _sigmoid(-fl)
        sh = 1
        while sh < cl:
            b = b + jnp.where(crow >= sh, pltpu.roll(b, sh, 0), 0.0)
            sh *= 2
        b_last = b[cl - SUBLANES:cl][SUBLANES - 1:SUBLANES]
        b_mid = b[cl // 2 - SUBLANES:cl // 2][SUBLANES - 1:SUBLANES]
        qd = q * jnp.exp(b)
        kl = k * jnp.exp(b_last - b)
        st = st_sc[...]
        o_inter = _dot_g(qd, st, _NT)
        safe = jnp.min(jnp.minimum(b_mid, b_last - b_mid)) >= HGRN_SAFE_MIN_LOGDECAY

        @pl.when(safe)
        def _():
            qmid = q * jnp.exp(b - b_mid)
            kinv = k * jnp.exp(b_mid - b)
            k4 = jnp.where(head_mask, jnp.concatenate([kinv] * B_HEADS, axis=0), 0.0)
            sc = jnp.where(causal, _dot_g(qmid, k4, _NT), 0.0)
            v4 = jnp.where(head_mask, jnp.concatenate([v] * B_HEADS, axis=0), 0.0)
            oi_sc[...] = _dot(sc, v4)

        @pl.when(jnp.logical_not(safe))
        def _():
            kb_sc[...] = k
            bb_sc[...] = b
            vb_sc[...] = v

            def pair(sidx, acc):
                ks = kb_sc[pl.ds(sidx, 1), :]
                bs = bb_sc[pl.ds(sidx, 1), :]
                vs = vb_sc[pl.ds(sidx, 1), :]
                e = jnp.exp(jnp.minimum(b - bs, 0.0)) * (q * ks)
                scr = jnp.dot(e.astype(BF16), seg, preferred_element_type=F32)
                return acc + jnp.where(crow >= sidx, scr, 0.0) * vs

            oi_sc[...] = lax.fori_loop(0, cl, pair, jnp.zeros((cl, D_B), F32))

        o = oi_sc[...] + o_inter
        st_sc[...] = st * jnp.exp(b_last) + jnp.where(head_mask, _dot_g(v, kl, _TN), 0.0)
        o = o * lax.rsqrt(seg_mean(o * o) + RMS_EPS) * ng
        y_ref[rows, D_A:D_A + D_B] = o * _silu(g)
        return carry

    lax.fori_loop(0, tt // cl, hgrn_chunk, 0)

    @pl.when(t == nt - 1)
    def _():
        s_bd = st_sc[...].T
        for hh in range(B_HEADS):
            hg_ref[hh] = s_bd[hh * B_DK:(hh + 1) * B_DK, hh * B_DK:(hh + 1) * B_DK]

    rl = C_CHUNK

    def ret_chunk(c, carry):
        r0 = pl.multiple_of(c * rl, rl)
        rows = pl.ds(r0, rl)
        cos = cos_ref[rows, :]
        sin = sin_ref[rows, :]
        for hh in range(C_HEADS):
            lo = hh * C_DK
            q = _rope(p_ref[rows, O_QC + lo:O_QC + lo + C_DK], cos, sin)
            k = _rope(p_ref[rows, O_KC + lo:O_KC + lo + C_DK], cos, sin) * (C_DK ** -0.5)
            v = p_ref[rows, O_VC + lo:O_VC + lo + C_DK]
            g = p_ref[rows, O_GC + lo:O_GC + lo + C_DK]
            sc = _dot_g(q, k, _NT) * dmat_ref[hh]
            s_old = sret_sc[hh]
            o = _dot(sc, v) + _dot(q * qdec_ref[hh], s_old)
            sret_sc[hh] = sdec[hh] * s_old + _dot_g(k * kdec_ref[hh], v, _TN)
            mu = jnp.mean(o, axis=-1, keepdims=True)
            oc = o - mu
            var = jnp.mean(oc * oc, axis=-1, keepdims=True)
            o = oc * lax.rsqrt(var + GN_EPS)
            y_ref[rows, D_A + D_B + lo:D_A + D_B + lo + C_DK] = o * _silu(g)
        return carry

    lax.fori_loop(0, tt // rl, ret_chunk, 0)

    @pl.when(t == nt - 1)
    def _():
        rt_ref[...] = sret_sc[...]


def _mix_prompt(proj, batch, seq, layer, tt, cw, cb, wri, bri, lam, lbl, ng):
    nt = seq // tt
    dmat, qdec, kdec, sdec = _retention_consts(C_CHUNK)
    cos, sin = _rope_tables(np.arange(seq))
    full = lambda shape: pl.BlockSpec(shape, lambda b, t: (0,) * len(shape))
    lay = lambda shape: pl.BlockSpec((None,) + shape, lambda b, t: (layer,) + (0,) * len(shape))
    kern = functools.partial(_mix_prompt_kernel, layer=layer, tt=tt, sdec=sdec)
    return pl.pallas_call(
        kern,
        grid=(batch, nt),
        in_specs=[
            pl.BlockSpec((tt, D_IN), lambda b, t: (b * nt + t, 0)),
            pl.BlockSpec((tt, C_DK), lambda b, t: (t, 0)),
            pl.BlockSpec((tt, C_DK), lambda b, t: (t, 0)),
            lay((CONV_W, D_A)), lay((1, D_A)), lay((D_A, 2 * D_A)), lay((1, 2 * D_A)),
            lay((1, D_A)), full((DEPTH, D_B)), lay((1, D_B)),
            full((C_HEADS, C_CHUNK, C_CHUNK)), full((C_HEADS, C_CHUNK, C_DK)),
            full((C_HEADS, C_CHUNK, C_DK)),
        ],
        out_specs=[
            pl.BlockSpec((tt, D_MODEL), lambda b, t: (b * nt + t, 0)),
            pl.BlockSpec((None, 1, D_A), lambda b, t: (b, 0, 0)),
            pl.BlockSpec((None, CONV_W - 1, D_A), lambda b, t: (b, 0, 0)),
            pl.BlockSpec((None, B_HEADS, B_DK, B_DK), lambda b, t: (b, 0, 0, 0)),
            pl.BlockSpec((None, C_HEADS, C_DK, C_DK), lambda b, t: (b, 0, 0, 0)),
        ],
        out_shape=[
            jax.ShapeDtypeStruct((batch * seq, D_MODEL), F32),
            jax.ShapeDtypeStruct((batch, 1, D_A), F32),
            jax.ShapeDtypeStruct((batch, CONV_W - 1, D_A), F32),
            jax.ShapeDtypeStruct((batch, B_HEADS, B_DK, B_DK), F32),
            jax.ShapeDtypeStruct((batch, C_HEADS, C_DK, C_DK), F32),
        ],
        scratch_shapes=[
            pltpu.VMEM((SUBLANES, D_A), F32), pltpu.VMEM((SUBLANES, D_A), F32),
            pltpu.VMEM((D_B, D_B), F32), pltpu.VMEM((C_HEADS, C_DK, C_DK), F32),
            pltpu.VMEM((B_CHUNK, D_B), F32), pltpu.VMEM((B_CHUNK, D_B), F32),
            pltpu.VMEM((B_CHUNK, D_B), F32), pltpu.VMEM((B_CHUNK, D_B), F32),
        ],
        compiler_params=_cparams(2),
        name="mix_prompt",
    )(proj, cos, sin, cw, cb, wri, bri, lam, lbl, ng, dmat, qdec, kdec)


def _column_matrix(x):
    pad = jnp.zeros((LANES - SUBLANES, LANES), F32)
    return jnp.concatenate([x, pad], axis=0).T


def _mix_sample_kernel(p_ref, h0_ref, conv0_ref, hg0_ref, rt0_ref, cos_ref, sin_ref,
                       cw_ref, cb_ref, wri_ref, bri_ref, lam_ref, lbl_ref, ng_ref,
                       y_ref, h_ref, conv_ref, hg_ref, rt_ref, o_sc, *, layer, gammas):
    tb = SUBLANES
    xa = p_ref[:, O_XA:O_XA + D_A]
    ga = p_ref[:, O_GA:O_GA + D_A]
    c0, c1, c2 = conv0_ref[0], conv0_ref[1], conv0_ref[2]
    u = cb_ref[...] + c0 * cw_ref[0:1, :]
    u = u + c1 * cw_ref[1:2, :]
    u = u + c2 * cw_ref[2:3, :]
    u = u + xa * cw_ref[3:4, :]
    conv_ref[0] = c1
    conv_ref[1] = c2
    conv_ref[2] = xa
    a, bterm = _rglru_gates(u, wri_ref[...], bri_ref[...], lam_ref[...])
    h = a * h0_ref[...] + bterm
    h_ref[...] = h
    y_ref[:, 0:D_A] = h * _gelu_tanh(ga)

    lb = _hgrn_lower_bound(lbl_ref[...], layer)
    for j in range(B_HEADS // 2):
        lo = j * LANES
        q = _silu(p_ref[:, O_QB + lo:O_QB + lo + LANES])
        fl = p_ref[:, O_FB + lo:O_FB + lo + LANES]
        v = p_ref[:, O_VB + lo:O_VB + lo + LANES]
        lbj = lb[:, lo:lo + LANES]
        f = lbj + (1.0 - lbj) * _sigmoid(fl)
        ef = jnp.exp(jnp.log(jnp.maximum(f, F_TINY)))
        k = (1.0 - lbj) * _sigmoid(-fl)
        qk = q * k
        qf_cols = _column_matrix(q * ef)
        k_cols = _column_matrix(k)
        f_cols = _column_matrix(ef)
        for hh in range(2):
            head = 2 * j + hh
            sl = slice(hh * B_DK, (hh + 1) * B_DK)
            dots = jnp.sum(qk[:, sl], axis=-1, keepdims=True)
            for b in range(tb):
                s_old = hg0_ref[b, head]
                v_row = v[b:b + 1, sl]
                hg_ref[b, head] = f_cols[sl, b:b + 1] * s_old + k_cols[sl, b:b + 1] * v_row
                o_row = (jnp.sum(qf_cols[sl, b:b + 1] * s_old, axis=0, keepdims=True)
                         + dots[b:b + 1, :] * v_row)
                o_sc[b:b + 1, head * B_DK:(head + 1) * B_DK] = o_row
    ng = ng_ref[...]
    for head in range(B_HEADS):
        sl = slice(head * B_DK, (head + 1) * B_DK)
        o = o_sc[:, sl]
        o = o * lax.rsqrt(jnp.mean(o * o, axis=-1, keepdims=True) + RMS_EPS) * ng[:, sl]
        g = p_ref[:, O_GB + head * B_DK:O_GB + (head + 1) * B_DK]
        y_ref[:, D_A + head * B_DK:D_A + (head + 1) * B_DK] = o * _silu(g)

    cos = cos_ref[...]
    sin = sin_ref[...]
    for head in range(C_HEADS):
        lo = head * C_DK
        gamma = gammas[head]
        q = _rope(p_ref[:, O_QC + lo:O_QC + lo + C_DK], cos, sin)
        k = _rope(p_ref[:, O_KC + lo:O_KC + lo + C_DK], cos, sin) * (C_DK ** -0.5)
        v = p_ref[:, O_VC + lo:O_VC + lo + C_DK]
        g = p_ref[:, O_GC + lo:O_GC + lo + C_DK]
        dots = jnp.sum(q * k, axis=-1, keepdims=True)
        q_cols = _column_matrix(q * gamma)
        k_cols = _column_matrix(k)
        for b in range(tb):
            s_old = rt0_ref[b, head]
            v_row = v[b:b + 1, :]
            rt_ref[b, head] = gamma * s_old + k_cols[:, b:b + 1] * v_row
            o_row = (jnp.sum(q_cols[:, b:b + 1] * s_old, axis=0, keepdims=True)
                     + dots[b:b + 1, :] * v_row)
            o_sc[b:b + 1, 0:C_DK] = o_row
        o = o_sc[:, 0:C_DK]
        mu = jnp.mean(o, axis=-1, keepdims=True)
        oc = o - mu
        var = jnp.mean(oc * oc, axis=-1, keepdims=True)
        y_ref[:, D_A + D_B + lo:D_A + D_B + lo + C_DK] = oc * lax.rsqrt(var + GN_EPS) * _silu(g)


def _mix_sample(proj, h0, conv0_t, hg0, rt0, layer, cw, cb, wri, bri, lam, lbl, ng):
    nb = proj.shape[0]
    tb = SUBLANES
    lg = np.log1p(-np.exp2(-5.0 - np.arange(C_HEADS, dtype=np.float64)))
    gammas = [float(np.exp(v)) for v in lg]
    cos, sin = _rope_tables([PAST_LEN])
    full = lambda shape: pl.BlockSpec(shape, lambda i: (0,) * len(shape))
    lay = lambda shape: pl.BlockSpec((None,) + shape, lambda i: (layer,) + (0,) * len(shape))
    kern = functools.partial(_mix_sample_kernel, layer=layer, gammas=gammas)
    return pl.pallas_call(
        kern,
        grid=(nb // tb,),
        in_specs=[
            pl.BlockSpec((tb, D_IN), lambda i: (i, 0)),
            pl.BlockSpec((None, tb, D_A), lambda i: (layer, i, 0)),
            pl.BlockSpec((None, CONV_W - 1, tb, D_A), lambda i: (layer, 0, i, 0)),
            pl.BlockSpec((None, tb, B_HEADS, B_DK, B_DK), lambda i: (layer, i, 0, 0, 0)),
            pl.BlockSpec((None, tb, C_HEADS, C_DK, C_DK), lambda i: (layer, i, 0, 0, 0)),
            full((1, C_DK)), full((1, C_DK)),
            lay((CONV_W, D_A)), lay((1, D_A)), lay((D_A, 2 * D_A)), lay((1, 2 * D_A)),
            lay((1, D_A)), full((DEPTH, D_B)), lay((1, D_B)),
        ],
        out_specs=[
            pl.BlockSpec((tb, D_MODEL), lambda i: (i, 0)),
            pl.BlockSpec((tb, D_A), lambda i: (i, 0)),
            pl.BlockSpec((CONV_W - 1, tb, D_A), lambda i: (0, i, 0)),
            pl.BlockSpec((tb, B_HEADS, B_DK, B_DK), lambda i: (i, 0, 0, 0)),
            pl.BlockSpec((tb, C_HEADS, C_DK, C_DK), lambda i: (i, 0, 0, 0)),
        ],
        out_shape=[
            jax.ShapeDtypeStruct((nb, D_MODEL), F32),
            jax.ShapeDtypeStruct((nb, D_A), F32),
            jax.ShapeDtypeStruct((CONV_W - 1, nb, D_A), F32),
            jax.ShapeDtypeStruct((nb, B_HEADS, B_DK, B_DK), F32),
            jax.ShapeDtypeStruct((nb, C_HEADS, C_DK, C_DK), F32),
        ],
        scratch_shapes=[pltpu.VMEM((tb, D_B), F32)],
        compiler_params=_cparams(1),
        name="mix_sample",
    )(proj, h0, conv0_t, hg0, rt0, cos, sin, cw, cb, wri, bri, lam, lbl, ng)


def _route_rows(l):
    m = l[0]
    for x in l[1:]:
        m = jnp.maximum(m, x)
    ex = [jnp.exp(x - m) for x in l]
    tot = ex[0]
    for x in ex[1:]:
        tot = tot + x
    p = [x / tot for x in ex]
    scores = []
    for gi in range(N_GROUPS):
        a, b, c, d = p[4 * gi:4 * gi + 4]
        hi1, lo1 = jnp.maximum(a, b), jnp.minimum(a, b)
        hi2, lo2 = jnp.maximum(c, d), jnp.minimum(c, d)
        top1 = jnp.maximum(hi1, hi2)
        top2 = jnp.maximum(jnp.minimum(hi1, hi2), jnp.maximum(lo1, lo2))
        scores.append(top1 + top2)
    best = scores[0]
    gsel = jnp.zeros_like(best, dtype=jnp.int32)
    for gi in range(1, N_GROUPS):
        upd = scores[gi] > best
        gsel = jnp.where(upd, gi, gsel)
        best = jnp.where(upd, scores[gi], best)
    vals = []
    for j in range(EXP_PER_GROUP):
        v = p[j]
        for gi in range(1, N_GROUPS):
            v = jnp.where(gsel == gi, p[4 * gi + j], v)
        vals.append(v)
    sel = []
    for j in range(EXP_PER_GROUP):
        rank = jnp.zeros_like(gsel)
        for i in range(EXP_PER_GROUP):
            if i == j:
                continue
            ahead = (vals[i] > vals[j]) | ((vals[i] == vals[j]) & (i < j))
            rank = rank + jnp.where(ahead, 1, 0)
        sel.append(rank < 2)
    denom = jnp.zeros_like(best)
    for j in range(EXP_PER_GROUP):
        denom = denom + jnp.where(sel[j], vals[j], 0.0)
    gates = [jnp.where(sel[j], vals[j] / denom, 0.0) for j in range(EXP_PER_GROUP)]
    comb = [jnp.where(gsel == (e // EXP_PER_GROUP), gates[e % EXP_PER_GROUP], 0.0)
            for e in range(N_EXPERTS)]
    j1 = jnp.where(sel[0], 0, jnp.where(sel[1], 1, 2))
    j2 = jnp.where(sel[3], 3, jnp.where(sel[2], 2, 1))
    g1 = jnp.zeros_like(best)
    g2 = jnp.zeros_like(best)
    for j in range(EXP_PER_GROUP):
        g1 = jnp.where(j1 == j, gates[j], g1)
        g2 = jnp.where(j2 == j, gates[j], g2)
    return comb, gsel * EXP_PER_GROUP + j1, gsel * EXP_PER_GROUP + j2, g1, g2


ROUTE_G1, ROUTE_G2 = N_EXPERTS, N_EXPERTS + 1
PLAN_E1, PLAN_R1, PLAN_E2, PLAN_R2 = 0, 1, 2, 3


def _outproj_kernel(*refs, plan):
    if plan:
        (y_ref, x_ref, w_ref, g_ref, b_ref, rwt_ref, rb_ref, tri_ref,
         x1_ref, route_ref, plan_ref, cnt_ref, rt_sc, oh_sc, base_sc) = refs
    else:
        y_ref, x_ref, w_ref, g_ref, b_ref, rwt_ref, rb_ref, x1_ref, route_ref, rt_sc = refs
    if w_ref.dtype == BF16:
        y = jnp.dot(y_ref[...].astype(BF16), w_ref[...], preferred_element_type=F32)
    else:
        y = _dot3(y_ref[...], w_ref[...])
    x1 = _layernorm(ALPHA * x_ref[...] + y, g_ref[...], b_ref[...])
    x1_ref[...] = x1
    hi, lo = _split(x1)
    rhi, rlo = _split(rwt_ref[...])
    lg = (lax.dot_general(rhi, hi, _NT, preferred_element_type=F32)
          + lax.dot_general(rhi, lo, _NT, preferred_element_type=F32)
          + lax.dot_general(rlo, hi, _NT, preferred_element_type=F32)) + rb_ref[...]
    comb, e1, e2, g1, g2 = _route_rows([lg[e:e + 1, :] for e in range(N_EXPERTS)])
    rt_sc[...] = jnp.zeros_like(rt_sc)
    for e in range(N_EXPERTS):
        rt_sc[e:e + 1, :] = comb[e]
    rt_sc[ROUTE_G1:ROUTE_G1 + 1, :] = g1
    rt_sc[ROUTE_G2:ROUTE_G2 + 1, :] = g2
    route_ref[...] = rt_sc[...].T
    if not plan:
        return

    @pl.when(pl.program_id(0) == 0)
    def _():
        base_sc[...] = jnp.zeros_like(base_sc)

    for e in range(N_EXPERTS):
        oh_sc[e:e + 1, :] = jnp.where((e1 == e) | (e2 == e), 1.0, 0.0)
    oh = oh_sc[...]
    base = base_sc[...]
    cum = jnp.dot(oh.astype(BF16), tri_ref[...], preferred_element_type=F32) + base[:, 0:1]
    r1 = jnp.zeros_like(g1)
    r2 = jnp.zeros_like(g1)
    for e in range(N_EXPERTS):
        row = cum[e:e + 1, :]
        r1 = jnp.where(e1 == e, row, r1)
        r2 = jnp.where(e2 == e, row, r2)
    plan_ref[...] = jnp.zeros_like(plan_ref)
    plan_ref[PLAN_E1:PLAN_E1 + 1, :] = e1
    plan_ref[PLAN_R1:PLAN_R1 + 1, :] = (r1 - 1.0).astype(jnp.int32)
    plan_ref[PLAN_E2:PLAN_E2 + 1, :] = e2
    plan_ref[PLAN_R2:PLAN_R2 + 1, :] = (r2 - 1.0).astype(jnp.int32)
    new_base = base + jnp.sum(oh, axis=1, keepdims=True)
    base_sc[...] = new_base
    cnt_ref[...] = new_base.astype(jnp.int32)


def _outproj(y, x, w_out_bf, layer, tm, g, b, rwt, rb, plan=False):
    n = x.shape[0]
    full = lambda shape: pl.BlockSpec(shape, lambda i: (0,) * len(shape))
    lay = lambda shape: pl.BlockSpec((None,) + shape, lambda i: (layer,) + (0,) * len(shape))
    in_specs = [pl.BlockSpec((tm, D_MODEL), lambda i: (i, 0)),
                pl.BlockSpec((tm, D_MODEL), lambda i: (i, 0)),
                lay((D_MODEL, D_MODEL)), lay((1, D_MODEL)), lay((1, D_MODEL)),
                full((N_EXPERTS, D_MODEL)), full((N_EXPERTS, 1))]
    out_specs = [pl.BlockSpec((tm, D_MODEL), lambda i: (i, 0)),
                 pl.BlockSpec((tm, LANES), lambda i: (i, 0))]
    out_shape = [jax.ShapeDtypeStruct((n, D_MODEL), F32),
                 jax.ShapeDtypeStruct((n, LANES), F32)]
    scratch = [pltpu.VMEM((LANES, tm), F32)]
    args = [y, x, w_out_bf, g, b, rwt, rb]
    if plan:
        tri = np.triu(np.ones((tm, tm), np.float32)).astype(jnp.bfloat16)
        in_specs.append(full((tm, tm)))
        args.append(tri)
        out_specs += [pl.BlockSpec((None, SUBLANES, tm), lambda i: (i, 0, 0)),
                      full((N_EXPERTS, LANES))]
        out_shape += [jax.ShapeDtypeStruct((n // tm, SUBLANES, tm), jnp.int32),
                      jax.ShapeDtypeStruct((N_EXPERTS, LANES), jnp.int32)]
        scratch += [pltpu.VMEM((N_EXPERTS, tm), F32), pltpu.VMEM((N_EXPERTS, LANES), F32)]
    return pl.pallas_call(
        functools.partial(_outproj_kernel, plan=plan),
        grid=(n // tm,),
        in_specs=in_specs,
        out_specs=out_specs,
        out_shape=out_shape,
        scratch_shapes=scratch,
        compiler_params=_cparams(1),
        name="outproj_plan" if plan else "outproj",
    )(*args)


MOE_TILE = 512
TOK_ROWS = D_MODEL // LANES


def _to_token_tiles(dst_ref, x, n):
    for c in range(TOK_ROWS):
        dst_ref[pl.ds(c, n, stride=TOK_ROWS), :] = x[:, c * LANES:(c + 1) * LANES]


def _from_token_tiles(src_ref, n):
    return jnp.concatenate(
        [src_ref[pl.ds(c, n, stride=TOK_ROWS), :] for c in range(TOK_ROWS)], axis=-1)


def _token_copy(src, dst, src_tok, dst_tok, sem):
    s0 = pl.multiple_of(src_tok * TOK_ROWS, TOK_ROWS)
    d0 = pl.multiple_of(dst_tok * TOK_ROWS, TOK_ROWS)
    return pltpu.make_async_copy(src.at[pl.ds(s0, TOK_ROWS)], dst.at[pl.ds(d0, TOK_ROWS)], sem)


def _dispatch_kernel(seg_ref, cnt_ref, plan_ref, x_ref, xs_ref, x3_sc, zero_sc, sem, *, tt, total_tiles):
    @pl.when(pl.program_id(0) == 0)
    def _():
        zero_sc[...] = jnp.zeros_like(zero_sc)

        def zero_tile(i):
            d0 = pl.multiple_of(i * (MOE_TILE * TOK_ROWS), MOE_TILE * TOK_ROWS)
            return pltpu.make_async_copy(
                zero_sc, xs_ref.at[pl.ds(d0, MOE_TILE * TOK_ROWS)], sem.at[1])

        for start in (True, False):
            for e in range(N_EXPERTS):
                end = seg_ref[e] + cnt_ref[e]

                @pl.when(end % MOE_TILE != 0)
                def _():
                    cp = zero_tile(end // MOE_TILE)
                    cp.start() if start else cp.wait()

        last = N_EXPERTS - 1
        used = (seg_ref[last] + cnt_ref[last] + MOE_TILE - 1) // MOE_TILE
        lax.fori_loop(used, total_tiles, lambda i, c: (zero_tile(i).start(), c)[1], 0)
        lax.fori_loop(used, total_tiles, lambda i, c: (zero_tile(i).wait(), c)[1], 0)

    _to_token_tiles(x3_sc, x_ref[...], tt)

    def issue(r, c):
        p1 = seg_ref[plan_ref[PLAN_E1, r]] + plan_ref[PLAN_R1, r]
        p2 = seg_ref[plan_ref[PLAN_E2, r]] + plan_ref[PLAN_R2, r]
        _token_copy(x3_sc, xs_ref, r, p1, sem.at[0]).start()
        _token_copy(x3_sc, xs_ref, r, p2, sem.at[0]).start()
        return c

    lax.fori_loop(0, tt, issue, 0)

    def drain(r, c):
        _token_copy(x3_sc, xs_ref, 0, 0, sem.at[0]).wait()
        _token_copy(x3_sc, xs_ref, 0, 0, sem.at[0]).wait()
        return c

    lax.fori_loop(0, tt, drain, 0)


def _dispatch(x1, plan, seg, cnt, tt, rows):
    n = x1.shape[0]
    return pl.pallas_call(
        functools.partial(_dispatch_kernel, tt=tt, total_tiles=rows // MOE_TILE),
        grid_spec=pltpu.PrefetchScalarGridSpec(
            num_scalar_prefetch=2,
            grid=(n // tt,),
            in_specs=[pl.BlockSpec((None, SUBLANES, tt), lambda i, s, c: (i, 0, 0),
                                   memory_space=pltpu.SMEM),
                      pl.BlockSpec((tt, D_MODEL), lambda i, s, c: (i, 0))],
            out_specs=pl.BlockSpec(memory_space=pl.ANY),
            scratch_shapes=[pltpu.VMEM((tt * TOK_ROWS, LANES), F32),
                            pltpu.VMEM((MOE_TILE * TOK_ROWS, LANES), F32),
                            pltpu.SemaphoreType.DMA((2,))]),
        out_shape=jax.ShapeDtypeStruct((rows * TOK_ROWS, LANES), F32),
        compiler_params=_cparams(1),
        name="moe_dispatch",
    )(seg, cnt, plan, x1)


def _ffn_kernel(te_ref, nv_ref, xs_ref, wg_ref, wu_ref, wd_ref, ys_ref):
    used = pl.program_id(0) < nv_ref[0]

    @pl.when(used)
    def _():
        xb = _from_token_tiles(xs_ref, MOE_TILE).astype(BF16)
        hg = jnp.dot(xb, wg_ref[...], preferred_element_type=F32)
        hu = jnp.dot(xb, wu_ref[...], preferred_element_type=F32)
        h = (_silu(hg) * hu).astype(BF16)
        _to_token_tiles(ys_ref, jnp.dot(h, wd_ref[...], preferred_element_type=F32), MOE_TILE)

    @pl.when(jnp.logical_not(used))
    def _():
        ys_ref[...] = jnp.zeros_like(ys_ref)


def _ffn(xs, tile_expert, n_valid, wg_bf, wu_bf, wd_bf, layer):
    total_tiles = xs.shape[0] // (MOE_TILE * TOK_ROWS)
    w_in_spec = pl.BlockSpec((None, None, D_MODEL, D_EXPERT), lambda i, te, nv: (layer, te[i], 0, 0))
    tok_spec = pl.BlockSpec((MOE_TILE * TOK_ROWS, LANES), lambda i, te, nv: (i, 0))
    return pl.pallas_call(
        _ffn_kernel,
        grid_spec=pltpu.PrefetchScalarGridSpec(
            num_scalar_prefetch=2,
            grid=(total_tiles,),
            in_specs=[tok_spec, w_in_spec, w_in_spec,
                      pl.BlockSpec((None, None, D_EXPERT, D_MODEL),
                                   lambda i, te, nv: (layer, te[i], 0, 0))],
            out_specs=tok_spec),
        out_shape=jax.ShapeDtypeStruct(xs.shape, F32),
        compiler_params=_cparams(1),
        name="moe_ffn",
    )(tile_expert, n_valid, xs, wg_bf, wu_bf, wd_bf)


def _combine_kernel(seg_ref, plan_ref, x_ref, route_ref, g_ref, b_ref, ys_ref, o_ref,
                    a_sc, b_sc, sem, *, tt):
    def issue(r, c):
        p1 = seg_ref[plan_ref[PLAN_E1, r]] + plan_ref[PLAN_R1, r]
        p2 = seg_ref[plan_ref[PLAN_E2, r]] + plan_ref[PLAN_R2, r]
        _token_copy(ys_ref, a_sc, p1, r, sem.at[0]).start()
        _token_copy(ys_ref, b_sc, p2, r, sem.at[0]).start()
        return c

    lax.fori_loop(0, tt, issue, 0)

    def drain(r, c):
        _token_copy(ys_ref, a_sc, 0, 0, sem.at[0]).wait()
        _token_copy(ys_ref, b_sc, 0, 0, sem.at[0]).wait()
        return c

    lax.fori_loop(0, tt, drain, 0)
    route = route_ref[...]
    y = (route[:, ROUTE_G1:ROUTE_G1 + 1] * _from_token_tiles(a_sc, tt)
         + route[:, ROUTE_G2:ROUTE_G2 + 1] * _from_token_tiles(b_sc, tt))
    o_ref[...] = _layernorm(ALPHA * x_ref[...] + y, g_ref[...], b_ref[...])


def _combine(x1, route, plan, seg, ys, layer, tt, g, b):
    n = x1.shape[0]
    lay = lambda shape: pl.BlockSpec((None,) + shape, lambda i, s: (layer,) + (0,) * len(shape))
    return pl.pallas_call(
        functools.partial(_combine_kernel, tt=tt),
        grid_spec=pltpu.PrefetchScalarGridSpec(
            num_scalar_prefetch=1,
            grid=(n // tt,),
            in_specs=[pl.BlockSpec((None, SUBLANES, tt), lambda i, s: (i, 0, 0),
                                   memory_space=pltpu.SMEM),
                      pl.BlockSpec((tt, D_MODEL), lambda i, s: (i, 0)),
                      pl.BlockSpec((tt, LANES), lambda i, s: (i, 0)),
                      lay((1, D_MODEL)), lay((1, D_MODEL)),
                      pl.BlockSpec(memory_space=pl.ANY)],
            out_specs=pl.BlockSpec((tt, D_MODEL), lambda i, s: (i, 0)),
            scratch_shapes=[pltpu.VMEM((tt * TOK_ROWS, LANES), F32),
                            pltpu.VMEM((tt * TOK_ROWS, LANES), F32),
                            pltpu.SemaphoreType.DMA((1,))]),
        out_shape=jax.ShapeDtypeStruct((n, D_MODEL), F32),
        compiler_params=_cparams(1),
        name="moe_combine",
    )(seg, plan, x1, route, g, b, ys)


def _moe_sparse(x1, route, plan, counts, wg_bf, wu_bf, wd_bf, layer, tt, g, b):
    n = x1.shape[0]
    max_tiles = (2 * n) // MOE_TILE + N_EXPERTS
    rows = max_tiles * MOE_TILE
    cnt = counts[:, 0]
    tiles = (cnt + MOE_TILE - 1) // MOE_TILE
    tile_end = jnp.cumsum(tiles)
    tile_start = tile_end - tiles
    seg = (tile_start * MOE_TILE).astype(jnp.int32)
    n_valid = tile_end[-1]
    idx = jnp.minimum(jnp.arange(max_tiles, dtype=jnp.int32), n_valid - 1)
    tile_expert = jnp.sum(idx[:, None] >= tile_end[None, :], axis=1).astype(jnp.int32)
    xs = _dispatch(x1, plan, seg, cnt, tt, rows)
    ys = _ffn(xs, tile_expert, n_valid.reshape(1).astype(jnp.int32), wg_bf, wu_bf, wd_bf, layer)
    return _combine(x1, route, plan, seg, ys, layer, tt, g, b)


def _moe_kernel(x_ref, r_ref, wg_ref, wu_ref, wd_ref, g_ref, b_ref, o_ref, xb_sc, acc_sc):
    e = pl.program_id(1)

    @pl.when(e == 0)
    def _():
        xb_sc[...] = x_ref[...].astype(BF16)
        acc_sc[...] = jnp.zeros_like(acc_sc)

    xb = xb_sc[...]
    hg = jnp.dot(xb, wg_ref[...], preferred_element_type=F32)
    hu = jnp.dot(xb, wu_ref[...], preferred_element_type=F32)
    r = r_ref[...]
    lane = lax.broadcasted_iota(jnp.int32, r.shape, 1)
    c = jnp.sum(jnp.where(lane == e, r, 0.0), axis=-1, keepdims=True)
    h = _silu(hg) * hu * c
    acc_sc[...] += jnp.dot(h.astype(BF16), wd_ref[...], preferred_element_type=F32)

    @pl.when(e == pl.num_programs(1) - 1)
    def _():
        o_ref[...] = _layernorm(ALPHA * x_ref[...] + acc_sc[...], g_ref[...], b_ref[...])


def _moe(x1, route, wg_bf, wu_bf, wd_bf, layer, tm, g, b):
    n = x1.shape[0]
    lay = lambda shape: pl.BlockSpec((None,) + shape, lambda i, e: (layer,) + (0,) * len(shape))
    return pl.pallas_call(
        _moe_kernel,
        grid=(n // tm, N_EXPERTS),
        in_specs=[pl.BlockSpec((tm, D_MODEL), lambda i, e: (i, 0)),
                  pl.BlockSpec((tm, LANES), lambda i, e: (i, 0)),
                  pl.BlockSpec((None, None, D_MODEL, D_EXPERT), lambda i, e: (layer, e, 0, 0)),
                  pl.BlockSpec((None, None, D_MODEL, D_EXPERT), lambda i, e: (layer, e, 0, 0)),
                  pl.BlockSpec((None, None, D_EXPERT, D_MODEL), lambda i, e: (layer, e, 0, 0)),
                  lay((1, D_MODEL)), lay((1, D_MODEL))],
        out_specs=pl.BlockSpec((tm, D_MODEL), lambda i, e: (i, 0)),
        out_shape=jax.ShapeDtypeStruct((n, D_MODEL), F32),
        scratch_shapes=[pltpu.VMEM((tm, D_MODEL), BF16), pltpu.VMEM((tm, D_MODEL), F32)],
        compiler_params=_cparams(2),
        name="moe",
    )(x1, route, wg_bf, wu_bf, wd_bf, g, b)


def _block_diag(w):
    out = jnp.zeros((DEPTH, D_A, D_A), w.dtype)
    for gi in range(A_BLOCKS):
        out = out.at[:, gi * A_BLK:(gi + 1) * A_BLK, gi * A_BLK:(gi + 1) * A_BLK].set(w[:, gi])
    return out


def kernel(x_prompt, x_sample, state_rglru_h, state_conv, state_hgrn, state_ret, w_in, conv_w, conv_b, w_rgate, b_rgate, w_igate, b_igate, rglru_lambda, hgrn_lb_logits, hgrn_norm_g, w_out, ln1_g, ln1_b, router_w, router_b, exp_w_gate, exp_w_up, exp_w_down, ln2_g, ln2_b):
    batch, seq, _ = x_prompt.shape
    nb = x_sample.shape[0]

    w_in_bf = w_in.astype(BF16)
    w_out_bf = w_out.astype(BF16)
    wg_bf = exp_w_gate.astype(BF16)
    wu_bf = exp_w_up.astype(BF16)
    wd_bf = exp_w_down.astype(BF16)
    wri = jnp.concatenate([_block_diag(w_rgate), _block_diag(w_igate)], axis=-1).astype(BF16)
    bri = jnp.concatenate([b_rgate.reshape(DEPTH, 1, D_A), b_igate.reshape(DEPTH, 1, D_A)], axis=-1)
    cb = conv_b.reshape(DEPTH, 1, D_A)
    lam = rglru_lambda.reshape(DEPTH, 1, D_A)
    ng = jnp.tile(hgrn_norm_g, (1, B_HEADS)).reshape(DEPTH, 1, D_B)
    g1, b1 = ln1_g.reshape(DEPTH, 1, D_MODEL), ln1_b.reshape(DEPTH, 1, D_MODEL)
    g2, b2 = ln2_g.reshape(DEPTH, 1, D_MODEL), ln2_b.reshape(DEPTH, 1, D_MODEL)
    rwt = router_w.T
    rb = router_b.reshape(N_EXPERTS, 1)
    conv0_t = jnp.transpose(state_conv, (0, 2, 1, 3))

    xp = x_prompt.reshape(batch * seq, D_MODEL)
    xs = x_sample.reshape(nb, D_MODEL)
    hs_p, convs_p, hgs_p, rts_p = [], [], [], []
    hs_s, convs_s, hgs_s, rts_s = [], [], [], []
    for l in range(DEPTH):
        proj_p = _proj(xp, w_in_bf, l, 512)
        y_p, h_p, conv_p, hg_p, rt_p = _mix_prompt(
            proj_p, batch, seq, l, 512, conv_w, cb, wri, bri, lam, hgrn_lb_logits, ng)
        x1_p, route_p, plan_p, counts_p = _outproj(y_p, xp, w_out_bf, l, 512, g1, b1, rwt, rb, plan=True)
        xp = _moe_sparse(x1_p, route_p, plan_p, counts_p, wg_bf, wu_bf, wd_bf, l, 512, g2, b2)
        hs_p.append(h_p.reshape(batch, D_A))
        convs_p.append(conv_p)
        hgs_p.append(hg_p)
        rts_p.append(rt_p)

        proj_s = _proj(xs, w_in, l, nb)
        y_s, h_s, conv_s, hg_s, rt_s = _mix_sample(
            proj_s, state_rglru_h, conv0_t, state_hgrn, state_ret, l,
            conv_w, cb, wri, bri, lam, hgrn_lb_logits, ng)
        x1_s, route_s = _outproj(y_s, xs, w_out, l, nb, g1, b1, rwt, rb)
        xs = _moe(x1_s, route_s, wg_bf, wu_bf, wd_bf, l, nb, g2, b2)
        hs_s.append(h_s)
        convs_s.append(jnp.transpose(conv_s, (1, 0, 2)))
        hgs_s.append(hg_s)
        rts_s.append(rt_s)

    return (xp.reshape(batch, seq, D_MODEL), xs.reshape(nb, 1, D_MODEL),
            jnp.stack(hs_p), jnp.stack(hs_s), jnp.stack(convs_p), jnp.stack(convs_s),
            jnp.stack(hgs_p), jnp.stack(hgs_s), jnp.stack(rts_p), jnp.stack(rts_s))
```

```python
import functools

import numpy as np
import jax
import jax.numpy as jnp
from jax import lax
from jax.experimental import pallas as pl
from jax.experimental.pallas import tpu as pltpu

D_MODEL = 1024
DEPTH = 2
PAST_LEN = 16384
D_A = 256
A_BLOCKS = 4
A_BLK = 64
CONV_W = 4
RGLRU_C = 8.0
B_HEADS = 4
B_DK = 64
D_B = 256
C_HEADS = 4
C_DK = 128
D_C = 512
D_IN = 3584
B_CHUNK = 64
C_CHUNK = 128
ROPE_BASE = 10000.0
N_EXPERTS = 16
N_GROUPS = 4
EXP_PER_GROUP = 4
D_EXPERT = 512
LN_EPS = 1e-5
RMS_EPS = 1e-6
GN_EPS = 1e-6
F_TINY = 1e-30
ALPHA = (2 * DEPTH) ** 0.25

O_XA, O_GA, O_QB, O_FB, O_VB, O_GB, O_QC, O_KC, O_VC, O_GC = (
    0, 256, 512, 768, 1024, 1280, 1536, 2048, 2560, 3072)

V7X_VMEM_LIMIT_BYTES = 56 * 1024 * 1024
SUBLANES = 8
LANES = 128
HGRN_SAFE_MIN_LOGDECAY = -60.0

BF16 = jnp.bfloat16
F32 = jnp.float32
_NT = (((1,), (1,)), ((), ()))
_TN = (((0,), (0,)), ((), ()))


def _cparams(n_axes):
    return pltpu.CompilerParams(
        dimension_semantics=("arbitrary",) * n_axes,
        vmem_limit_bytes=V7X_VMEM_LIMIT_BYTES)


def _dot(a, b):
    return jnp.dot(a.astype(BF16), b.astype(BF16), preferred_element_type=F32)


def _dot_g(a, b, dims):
    return lax.dot_general(a.astype(BF16), b.astype(BF16), dims, preferred_element_type=F32)


def _sigmoid(x):
    return 1.0 / (1.0 + jnp.exp(-x))


def _silu(x):
    return x * _sigmoid(x)


def _gelu_tanh(x):
    c = np.float32(np.sqrt(2.0 / np.pi))
    return 0.5 * x * (1.0 + jnp.tanh(c * (x + np.float32(0.044715) * (x * x * x))))


def _log_sigmoid(x):
    return -(jnp.maximum(-x, 0.0) + jnp.log(1.0 + jnp.exp(-jnp.abs(x))))


def _layernorm(z, g, b):
    mu = jnp.mean(z, axis=-1, keepdims=True)
    zc = z - mu
    var = jnp.mean(zc * zc, axis=-1, keepdims=True)
    return zc * lax.rsqrt(var + LN_EPS) * g + b


def _hgrn_lower_bound(lbl, layer):
    rows = [lbl[j:j + 1, :] for j in range(DEPTH)]
    m = rows[0]
    for r in rows[1:]:
        m = jnp.maximum(m, r)
    ex = [jnp.exp(r - m) for r in rows]
    tot = ex[0]
    for e in ex[1:]:
        tot = tot + e
    lb = jnp.zeros_like(m)
    for j in range(1, layer + 1):
        lb = lb + ex[j] / tot
    return lb


def _rglru_gates(u, wri, bri, lam):
    gates = _dot(u, wri) + bri
    r = _sigmoid(gates[:, :D_A])
    i = _sigmoid(gates[:, D_A:])
    log_a = RGLRU_C * r * _log_sigmoid(lam)
    a = jnp.exp(log_a)
    bterm = jnp.sqrt(jnp.maximum(1.0 - jnp.exp(2.0 * log_a), 0.0)) * (i * u)
    return a, bterm


def _split(x):
    hi = x.astype(BF16)
    return hi, (x - hi.astype(F32)).astype(BF16)


def _dot3(x, w):
    xh, xl = _split(x)
    wh, wl = _split(w)
    return (jnp.dot(xh, wh, preferred_element_type=F32)
            + jnp.dot(xl, wh, preferred_element_type=F32)
            + jnp.dot(xh, wl, preferred_element_type=F32))


def _proj_kernel(x_ref, w_ref, o_ref):
    if w_ref.dtype == BF16:
        xb = x_ref[...].astype(BF16)
        for j in range(0, D_IN, 512):
            o_ref[:, j:j + 512] = jnp.dot(xb, w_ref[:, j:j + 512], preferred_element_type=F32)
    else:
        x = x_ref[...]
        for j in range(0, D_IN, 512):
            o_ref[:, j:j + 512] = _dot3(x, w_ref[:, j:j + 512])


def _proj(x, w_in_bf, layer, tm):
    n = x.shape[0]
    return pl.pallas_call(
        _proj_kernel,
        grid=(n // tm,),
        in_specs=[pl.BlockSpec((tm, D_MODEL), lambda i: (i, 0)),
                  pl.BlockSpec((None, D_MODEL, D_IN), lambda i: (layer, 0, 0))],
        out_specs=pl.BlockSpec((tm, D_IN), lambda i: (i, 0)),
        out_shape=jax.ShapeDtypeStruct((n, D_IN), F32),
        compiler_params=_cparams(1),
        name="proj",
    )(x, w_in_bf)


def _retention_consts(chunk):
    lg = np.log1p(-np.exp2(-5.0 - np.arange(C_HEADS, dtype=np.float64)))
    idx = np.arange(chunk, dtype=np.float64)
    rel = idx[:, None] - idx[None, :]
    mask = rel >= 0
    dmat = np.where(mask[None], np.exp(np.where(mask, rel, 0.0)[None] * lg[:, None, None]), 0.0)
    qdec = np.exp((idx + 1.0)[None, :] * lg[:, None])
    kdec = np.exp((chunk - 1.0 - idx)[None, :] * lg[:, None])
    sdec = np.exp(chunk * lg)
    qdec_b = np.broadcast_to(qdec[:, :, None], (C_HEADS, chunk, C_DK))
    kdec_b = np.broadcast_to(kdec[:, :, None], (C_HEADS, chunk, C_DK))
    return (dmat.astype(np.float32), np.ascontiguousarray(qdec_b).astype(np.float32),
            np.ascontiguousarray(kdec_b).astype(np.float32), [float(v) for v in sdec])


def _rope_tables(positions):
    half = C_DK // 2
    inv = ROPE_BASE ** (-np.arange(half, dtype=np.float64) / half)
    ang = np.asarray(positions, dtype=np.float64)[:, None] * inv[None]
    cos = np.concatenate([np.cos(ang), np.cos(ang)], axis=-1)
    sin = np.concatenate([-np.sin(ang), np.sin(ang)], axis=-1)
    return cos.astype(np.float32), sin.astype(np.float32)


def _rope(x, cos, sin_signed):
    return x * cos + pltpu.roll(x, C_DK // 2, 1) * sin_signed


def _mix_prompt_kernel(p_ref, cos_ref, sin_ref, cw_ref, cb_ref, wri_ref, bri_ref, lam_ref,
                       lbl_ref, ng_ref, dmat_ref, qdec_ref, kdec_ref,
                       y_ref, h_ref, conv_ref, hg_ref, rt_ref,
                       prev_sc, hprev_sc, st_sc, sret_sc, kb_sc, bb_sc, vb_sc, oi_sc,
                       *, layer, tt, sdec):
    t = pl.program_id(1)
    nt = pl.num_programs(1)

    @pl.when(t == 0)
    def _():
        prev_sc[...] = jnp.zeros_like(prev_sc)
        hprev_sc[...] = jnp.zeros_like(hprev_sc)
        st_sc[...] = jnp.zeros_like(st_sc)
        sret_sc[...] = jnp.zeros_like(sret_sc)

    xa = p_ref[:, O_XA:O_XA + D_A]
    ga = p_ref[:, O_GA:O_GA + D_A]
    row = lax.broadcasted_iota(jnp.int32, (tt, D_A), 0)
    row8 = lax.broadcasted_iota(jnp.int32, (SUBLANES, D_A), 0)
    prev = prev_sc[...]

    def shifted(j):
        r = pltpu.roll(xa, j, 0)
        top = jnp.where(row8 < j, pltpu.roll(prev, j, 0), r[0:SUBLANES])
        return jnp.concatenate([top, r[SUBLANES:]], axis=0)

    u = cb_ref[...] + shifted(3) * cw_ref[0:1, :]
    u = u + shifted(2) * cw_ref[1:2, :]
    u = u + shifted(1) * cw_ref[2:3, :]
    u = u + xa * cw_ref[3:4, :]
    last8 = xa[tt - SUBLANES:tt]
    prev_sc[...] = last8

    a, bterm = _rglru_gates(u, wri_ref[...], bri_ref[...], lam_ref[...])
    s = 1
    while s < tt:
        keep = row >= s
        a_sh = jnp.where(keep, pltpu.roll(a, s, 0), 1.0)
        b_sh = jnp.where(keep, pltpu.roll(bterm, s, 0), 0.0)
        bterm = a * b_sh + bterm
        a = a * a_sh
        s *= 2
    h = a * hprev_sc[SUBLANES - 1:SUBLANES, :] + bterm
    hlast8 = h[tt - SUBLANES:tt]
    hprev_sc[...] = hlast8
    y_ref[:, 0:D_A] = h * _gelu_tanh(ga)

    @pl.when(t == nt - 1)
    def _():
        h_ref[...] = pltpu.roll(hlast8, 1, 0)[0:1]
        conv_ref[...] = pltpu.roll(last8, CONV_W - 1, 0)[0:CONV_W - 1]

    lb = _hgrn_lower_bound(lbl_ref[...], layer)
    ng = ng_ref[...]
    cl = B_CHUNK
    crow = lax.broadcasted_iota(jnp.int32, (cl, D_B), 0)
    ccol = lax.broadcasted_iota(jnp.int32, (cl, D_B), 1)
    causal = (ccol % B_DK) <= crow
    br = lax.broadcasted_iota(jnp.int32, (D_B, D_B), 0)
    bc = lax.broadcasted_iota(jnp.int32, (D_B, D_B), 1)
    head_mask = (br // B_DK) == (bc // B_DK)
    seg = jnp.where(head_mask, 1.0, 0.0).astype(BF16)

    def seg_mean(x):
        hi = x.astype(BF16)
        lo = (x - hi.astype(F32)).astype(BF16)
        tot = (jnp.dot(hi, seg, preferred_element_type=F32)
               + jnp.dot(lo, seg, preferred_element_type=F32))
        return tot * (1.0 / B_DK)

    def hgrn_chunk(c, carry):
        r0 = pl.multiple_of(c * cl, cl)
        rows = pl.ds(r0, cl)
        q = _silu(p_ref[rows, O_QB:O_QB + D_B])
        fl = p_ref[rows, O_FB:O_FB + D_B]
        v = p_ref[rows, O_VB:O_VB + D_B]
        g = p_ref[rows, O_GB:O_GB + D_B]
        f = lb + (1.0 - lb) * _sigmoid(fl)
        b = jnp.log(jnp.maximum(f, F_TINY))
        k = (1.0 - lb) * _sigmoid(-fl)
        sh = 1
        while sh < cl:
            b = b + jnp.where(crow >= sh, pltpu.roll(b, sh, 0), 0.0)
            sh *= 2
        b_last = b[cl - SUBLANES:cl][SUBLANES - 1:SUBLANES]
        b_mid = b[cl // 2 - SUBLANES:cl // 2][SUBLANES - 1:SUBLANES]
        qd = q * jnp.exp(b)
        kl = k * jnp.exp(b_last - b)
        st = st_sc[...]
        o_inter = _dot_g(qd, st, _NT)
        safe = jnp.min(jnp.minimum(b_mid, b_last - b_mid)) >= HGRN_SAFE_MIN_LOGDECAY

        @pl.when(safe)
        def _():
            qmid = q * jnp.exp(b - b_mid)
            kinv = k * jnp.exp(b_mid - b)
            k4 = jnp.where(head_mask, jnp.concatenate([kinv] * B_HEADS, axis=0), 0.0)
            sc = jnp.where(causal, _dot_g(qmid, k4, _NT), 0.0)
            v4 = jnp.where(head_mask, jnp.concatenate([v] * B_HEADS, axis=0), 0.0)
            oi_sc[...] = _dot(sc, v4)

        @pl.when(jnp.logical_not(safe))
        def _():
            kb_sc[...] = k
            bb_sc[...] = b
            vb_sc[...] = v

            def pair(sidx, acc):
                ks = kb_sc[pl.ds(sidx, 1), :]
                bs = bb_sc[pl.ds(sidx, 1), :]
                vs = vb_sc[pl.ds(sidx, 1), :]
                e = jnp.exp(jnp.minimum(b - bs, 0.0)) * (q * ks)
                scr = jnp.dot(e.astype(BF16), seg, preferred_element_type=F32)
                return acc + jnp.where(crow >= sidx, scr, 0.0) * vs

            oi_sc[...] = lax.fori_loop(0, cl, pair, jnp.zeros((cl, D_B), F32))

        o = oi_sc[...] + o_inter
        st_sc[...] = st * jnp.exp(b_last) + jnp.where(head_mask, _dot_g(v, kl, _TN), 0.0)
        o = o * lax.rsqrt(seg_mean(o * o) + RMS_EPS) * ng
        y_ref[rows, D_A:D_A + D_B] = o * _silu(g)
        return carry

    lax.fori_loop(0, tt // cl, hgrn_chunk, 0)

    @pl.when(t == nt - 1)
    def _():
        s_bd = st_sc[...].T
        for hh in range(B_HEADS):
            hg_ref[hh] = s_bd[hh * B_DK:(hh + 1) * B_DK, hh * B_DK:(hh + 1) * B_DK]

    rl = C_CHUNK

    def ret_chunk(c, carry):
        r0 = pl.multiple_of(c * rl, rl)
        rows = pl.ds(r0, rl)
        cos = cos_ref[rows, :]
        sin = sin_ref[rows, :]
        for hh in range(C_HEADS):
            lo = hh * C_DK
            q = _rope(p_ref[rows, O_QC + lo:O_QC + lo + C_DK], cos, sin)
            k = _rope(p_ref[rows, O_KC + lo:O_KC + lo + C_DK], cos, sin) * (C_DK ** -0.5)
            v = p_ref[rows, O_VC + lo:O_VC + lo + C_DK]
            g = p_ref[rows, O_GC + lo:O_GC + lo + C_DK]
            sc = _dot_g(q, k, _NT) * dmat_ref[hh]
            s_old = sret_sc[hh]
            o = _dot(sc, v) + _dot(q * qdec_ref[hh], s_old)
            sret_sc[hh] = sdec[hh] * s_old + _dot_g(k * kdec_ref[hh], v, _TN)
            mu = jnp.mean(o, axis=-1, keepdims=True)
            oc = o - mu
            var = jnp.mean(oc * oc, axis=-1, keepdims=True)
            o = oc * lax.rsqrt(var + GN_EPS)
            y_ref[rows, D_A + D_B + lo:D_A + D_B + lo + C_DK] = o * _silu(g)
        return carry

    lax.fori_loop(0, tt // rl, ret_chunk, 0)

    @pl.when(t == nt - 1)
    def _():
        rt_ref[...] = sret_sc[...]


def _mix_prompt(proj, batch, seq, layer, tt, cw, cb, wri, bri, lam, lbl, ng):
    nt = seq // tt
    dmat, qdec, kdec, sdec = _retention_consts(C_CHUNK)
    cos, sin = _rope_tables(np.arange(seq))
    full = lambda shape: pl.BlockSpec(shape, lambda b, t: (0,) * len(shape))
    lay = lambda shape: pl.BlockSpec((None,) + shape, lambda b, t: (layer,) + (0,) * len(shape))
    kern = functools.partial(_mix_prompt_kernel, layer=layer, tt=tt, sdec=sdec)
    return pl.pallas_call(
        kern,
        grid=(batch, nt),
        in_specs=[
            pl.BlockSpec((tt, D_IN), lambda b, t: (b * nt + t, 0)),
            pl.BlockSpec((tt, C_DK), lambda b, t: (t, 0)),
            pl.BlockSpec((tt, C_DK), lambda b, t: (t, 0)),
            lay((CONV_W, D_A)), lay((1, D_A)), lay((D_A, 2 * D_A)), lay((1, 2 * D_A)),
            lay((1, D_A)), full((DEPTH, D_B)), lay((1, D_B)),
            full((C_HEADS, C_CHUNK, C_CHUNK)), full((C_HEADS, C_CHUNK, C_DK)),
            full((C_HEADS, C_CHUNK, C_DK)),
        ],
        out_specs=[
            pl.BlockSpec((tt, D_MODEL), lambda b, t: (b * nt + t, 0)),
            pl.BlockSpec((None, 1, D_A), lambda b, t: (b, 0, 0)),
            pl.BlockSpec((None, CONV_W - 1, D_A), lambda b, t: (b, 0, 0)),
            pl.BlockSpec((None, B_HEADS, B_DK, B_DK), lambda b, t: (b, 0, 0, 0)),
            pl.BlockSpec((None, C_HEADS, C_DK, C_DK), lambda b, t: (b, 0, 0, 0)),
        ],
        out_shape=[
            jax.ShapeDtypeStruct((batch * seq, D_MODEL), F32),
            jax.ShapeDtypeStruct((batch, 1, D_A), F32),
            jax.ShapeDtypeStruct((batch, CONV_W - 1, D_A), F32),
            jax.ShapeDtypeStruct((batch, B_HEADS, B_DK, B_DK), F32),
            jax.ShapeDtypeStruct((batch, C_HEADS, C_DK, C_DK), F32),
        ],
        scratch_shapes=[
            pltpu.VMEM((SUBLANES, D_A), F32), pltpu.VMEM((SUBLANES, D_A), F32),
            pltpu.VMEM((D_B, D_B), F32), pltpu.VMEM((C_HEADS, C_DK, C_DK), F32),
            pltpu.VMEM((B_CHUNK, D_B), F32), pltpu.VMEM((B_CHUNK, D_B), F32),
            pltpu.VMEM((B_CHUNK, D_B), F32), pltpu.VMEM((B_CHUNK, D_B), F32),
        ],
        compiler_params=_cparams(2),
        name="mix_prompt",
    )(proj, cos, sin, cw, cb, wri, bri, lam, lbl, ng, dmat, qdec, kdec)


def _column_matrix(x):
    pad = jnp.zeros((LANES - SUBLANES, LANES), F32)
    return jnp.concatenate([x, pad], axis=0).T


def _mix_sample_kernel(p_ref, h0_ref, conv0_ref, hg0_ref, rt0_ref, cos_ref, sin_ref,
                       cw_ref, cb_ref, wri_ref, bri_ref, lam_ref, lbl_ref, ng_ref,
                       y_ref, h_ref, conv_ref, hg_ref, rt_ref, o_sc, *, layer, gammas):
    tb = SUBLANES
    xa = p_ref[:, O_XA:O_XA + D_A]
    ga = p_ref[:, O_GA:O_GA + D_A]
    c0, c1, c2 = conv0_ref[0], conv0_ref[1], conv0_ref[2]
    u = cb_ref[...] + c0 * cw_ref[0:1, :]
    u = u + c1 * cw_ref[1:2, :]
    u = u + c2 * cw_ref[2:3, :]
    u = u + xa * cw_ref[3:4, :]
    conv_ref[0] = c1
    conv_ref[1] = c2
    conv_ref[2] = xa
    a, bterm = _rglru_gates(u, wri_ref[...], bri_ref[...], lam_ref[...])
    h = a * h0_ref[...] + bterm
    h_ref[...] = h
    y_ref[:, 0:D_A] = h * _gelu_tanh(ga)

    lb = _hgrn_lower_bound(lbl_ref[...], layer)
    for j in range(B_HEADS // 2):
        lo = j * LANES
        q = _silu(p_ref[:, O_QB + lo:O_QB + lo + LANES])
        fl = p_ref[:, O_FB + lo:O_FB + lo + LANES]
        v = p_ref[:, O_VB + lo:O_VB + lo + LANES]
        lbj = lb[:, lo:lo + LANES]
        f = lbj + (1.0 - lbj) * _sigmoid(fl)
        ef = jnp.exp(jnp.log(jnp.maximum(f, F_TINY)))
        k = (1.0 - lbj) * _sigmoid(-fl)
        qk = q * k
        qf_cols = _column_matrix(q * ef)
        k_cols = _column_matrix(k)
        f_cols = _column_matrix(ef)
        for hh in range(2):
            head = 2 * j + hh
            sl = slice(hh * B_DK, (hh + 1) * B_DK)
            dots = jnp.sum(qk[:, sl], axis=-1, keepdims=True)
            for b in range(tb):
                s_old = hg0_ref[b, head]
                v_row = v[b:b + 1, sl]
                hg_ref[b, head] = f_cols[sl, b:b + 1] * s_old + k_cols[sl, b:b + 1] * v_row
                o_row = (jnp.sum(qf_cols[sl, b:b + 1] * s_old, axis=0, keepdims=True)
                         + dots[b:b + 1, :] * v_row)
                o_sc[b:b + 1, head * B_DK:(head + 1) * B_DK] = o_row
    ng = ng_ref[...]
    for head in range(B_HEADS):
        sl = slice(head * B_DK, (head + 1) * B_DK)
        o = o_sc[:, sl]
        o = o * lax.rsqrt(jnp.mean(o * o, axis=-1, keepdims=True) + RMS_EPS) * ng[:, sl]
        g = p_ref[:, O_GB + head * B_DK:O_GB + (head + 1) * B_DK]
        y_ref[:, D_A + head * B_DK:D_A + (head + 1) * B_DK] = o * _silu(g)

    cos = cos_ref[...]
    sin = sin_ref[...]
    for head in range(C_HEADS):
        lo = head * C_DK
        gamma = gammas[head]
        q = _rope(p_ref[:, O_QC + lo:O_QC + lo + C_DK], cos, sin)
        k = _rope(p_ref[:, O_KC + lo:O_KC + lo + C_DK], cos, sin) * (C_DK ** -0.5)
        v = p_ref[:, O_VC + lo:O_VC + lo + C_DK]
        g = p_ref[:, O_GC + lo:O_GC + lo + C_DK]
        dots = jnp.sum(q * k, axis=-1, keepdims=True)
        q_cols = _column_matrix(q * gamma)
        k_cols = _column_matrix(k)
        for b in range(tb):
            s_old = rt0_ref[b, head]
            v_row = v[b:b + 1, :]
            rt_ref[b, head] = gamma * s_old + k_cols[:, b:b + 1] * v_row
            o_row = (jnp.sum(q_cols[:, b:b + 1] * s_old, axis=0, keepdims=True)
                     + dots[b:b + 1, :] * v_row)
            o_sc[b:b + 1, 0:C_DK] = o_row
        o = o_sc[:, 0:C_DK]
        mu = jnp.mean(o, axis=-1, keepdims=True)
        oc = o - mu
        var = jnp.mean(oc * oc, axis=-1, keepdims=True)
        y_ref[:, D_A + D_B + lo:D_A + D_B + lo + C_DK] = oc * lax.rsqrt(var + GN_EPS) * _silu(g)


def _mix_sample(proj, h0, conv0_t, hg0, rt0, layer, cw, cb, wri, bri, lam, lbl, ng):
    nb = proj.shape[0]
    tb = SUBLANES
    lg = np.log1p(-np.exp2(-5.0 - np.arange(C_HEADS, dtype=np.float64)))
    gammas = [float(np.exp(v)) for v in lg]
    cos, sin = _rope_tables([PAST_LEN])
    full = lambda shape: pl.BlockSpec(shape, lambda i: (0,) * len(shape))
    lay = lambda shape: pl.BlockSpec((None,) + shape, lambda i: (layer,) + (0,) * len(shape))
    kern = functools.partial(_mix_sample_kernel, layer=layer, gammas=gammas)
    return pl.pallas_call(
        kern,
        grid=(nb // tb,),
        in_specs=[
            pl.BlockSpec((tb, D_IN), lambda i: (i, 0)),
            pl.BlockSpec((None, tb, D_A), lambda i: (layer, i, 0)),
            pl.BlockSpec((None, CONV_W - 1, tb, D_A), lambda i: (layer, 0, i, 0)),
            pl.BlockSpec((None, tb, B_HEADS, B_DK, B_DK), lambda i: (layer, i, 0, 0, 0)),
            pl.BlockSpec((None, tb, C_HEADS, C_DK, C_DK), lambda i: (layer, i, 0, 0, 0)),
            full((1, C_DK)), full((1, C_DK)),
            lay((CONV_W, D_A)), lay((1, D_A)), lay((D_A, 2 * D_A)), lay((1, 2 * D_A)),
            lay((1, D_A)), full((DEPTH, D_B)), lay((1, D_B)),
        ],
        out_specs=[
            pl.BlockSpec((tb, D_MODEL), lambda i: (i, 0)),
            pl.BlockSpec((tb, D_A), lambda i: (i, 0)),
            pl.BlockSpec((CONV_W - 1, tb, D_A), lambda i: (0, i, 0)),
            pl.BlockSpec((tb, B_HEADS, B_DK, B_DK), lambda i: (i, 0, 0, 0)),
            pl.BlockSpec((tb, C_HEADS, C_DK, C_DK), lambda i: (i, 0, 0, 0)),
        ],
        out_shape=[
            jax.ShapeDtypeStruct((nb, D_MODEL), F32),
            jax.ShapeDtypeStruct((nb, D_A), F32),
            jax.ShapeDtypeStruct((CONV_W - 1, nb, D_A), F32),
            jax.ShapeDtypeStruct((nb, B_HEADS, B_DK, B_DK), F32),
            jax.ShapeDtypeStruct((nb, C_HEADS, C_DK, C_DK), F32),
        ],
        scratch_shapes=[pltpu.VMEM((tb, D_B), F32)],
        compiler_params=_cparams(1),
        name="mix_sample",
    )(proj, h0, conv0_t, hg0, rt0, cos, sin, cw, cb, wri, bri, lam, lbl, ng)


def _route_rows(l):
    m = l[0]
    for x in l[1:]:
        m = jnp.maximum(m, x)
    ex = [jnp.exp(x - m) for x in l]
    tot = ex[0]
    for x in ex[1:]:
        tot = tot + x
    p = [x / tot for x in ex]
    scores = []
    for gi in range(N_GROUPS):
        a, b, c, d = p[4 * gi:4 * gi + 4]
        hi1, lo1 = jnp.maximum(a, b), jnp.minimum(a, b)
        hi2, lo2 = jnp.maximum(c, d), jnp.minimum(c, d)
        top1 = jnp.maximum(hi1, hi2)
        top2 = jnp.maximum(jnp.minimum(hi1, hi2), jnp.maximum(lo1, lo2))
        scores.append(top1 + top2)
    best = scores[0]
    gsel = jnp.zeros_like(best, dtype=jnp.int32)
    for gi in range(1, N_GROUPS):
        upd = scores[gi] > best
        gsel = jnp.where(upd, gi, gsel)
        best = jnp.where(upd, scores[gi], best)
    vals = []
    for j in range(EXP_PER_GROUP):
        v = p[j]
        for gi in range(1, N_GROUPS):
            v = jnp.where(gsel == gi, p[4 * gi + j], v)
        vals.append(v)
    sel = []
    for j in range(EXP_PER_GROUP):
        rank = jnp.zeros_like(gsel)
        for i in range(EXP_PER_GROUP):
            if i == j:
                continue
            ahead = (vals[i] > vals[j]) | ((vals[i] == vals[j]) & (i < j))
            rank = rank + jnp.where(ahead, 1, 0)
        sel.append(rank < 2)
    denom = jnp.zeros_like(best)
    for j in range(EXP_PER_GROUP):
        denom = denom + jnp.where(sel[j], vals[j], 0.0)
    gates = [jnp.where(sel[j], vals[j] / denom, 0.0) for j in range(EXP_PER_GROUP)]
    comb = [jnp.where(gsel == (e // EXP_PER_GROUP), gates[e % EXP_PER_GROUP], 0.0)
            for e in range(N_EXPERTS)]
    j1 = jnp.where(sel[0], 0, jnp.where(sel[1], 1, 2))
    j2 = jnp.where(sel[3], 3, jnp.where(sel[2], 2, 1))
    g1 = jnp.zeros_like(best)
    g2 = jnp.zeros_like(best)
    for j in range(EXP_PER_GROUP):
        g1 = jnp.where(j1 == j, gates[j], g1)
        g2 = jnp.where(j2 == j, gates[j], g2)
    return comb, gsel * EXP_PER_GROUP + j1, gsel * EXP_PER_GROUP + j2, g1, g2


MOE_TILE = 512
GRANULE = 16
GRANULES_PER_TILE = MOE_TILE // GRANULE
ROUTE_SLOT1, ROUTE_SLOT2 = N_EXPERTS, N_EXPERTS + 1
PLAN_SLOT1, PLAN_SLOT2, PLAN_G1, PLAN_G2 = 0, 1, 2, 3


def _local_slots(tt):
    worst = 2 * tt + (N_EXPERTS - 1) * GRANULE
    return -(-worst // LANES) * LANES


def _outproj_kernel(*refs, plan):
    if plan:
        (y_ref, x_ref, w_ref, g_ref, b_ref, rwt_ref, rb_ref, tri_ref, ltri_ref,
         x1_ref, route_ref, plan_ref, gran_ref, loff_ref, rt_sc, oh_sc) = refs
    else:
        y_ref, x_ref, w_ref, g_ref, b_ref, rwt_ref, rb_ref, x1_ref, route_ref, rt_sc = refs
    if w_ref.dtype == BF16:
        y = jnp.dot(y_ref[...].astype(BF16), w_ref[...], preferred_element_type=F32)
    else:
        y = _dot3(y_ref[...], w_ref[...])
    x1 = _layernorm(ALPHA * x_ref[...] + y, g_ref[...], b_ref[...])
    x1_ref[...] = x1
    hi, lo = _split(x1)
    rhi, rlo = _split(rwt_ref[...])
    lg = (lax.dot_general(rhi, hi, _NT, preferred_element_type=F32)
          + lax.dot_general(rhi, lo, _NT, preferred_element_type=F32)
          + lax.dot_general(rlo, hi, _NT, preferred_element_type=F32)) + rb_ref[...]
    comb, e1, e2, g1, g2 = _route_rows([lg[e:e + 1, :] for e in range(N_EXPERTS)])
    rt_sc[...] = jnp.zeros_like(rt_sc)
    for e in range(N_EXPERTS):
        rt_sc[e:e + 1, :] = comb[e]
    if plan:
        for e in range(N_EXPERTS):
            oh_sc[e:e + 1, :] = jnp.where((e1 == e) | (e2 == e), 1.0, 0.0)
        oh = oh_sc[...]
        cum = jnp.dot(oh.astype(BF16), tri_ref[...], preferred_element_type=F32)
        count = jnp.sum(oh, axis=1, keepdims=True)
        gran = jnp.floor((count + (GRANULE - 1)) * (1.0 / GRANULE))
        gran_b = jnp.broadcast_to(gran, (N_EXPERTS, LANES))
        loff_b = jnp.dot(ltri_ref[...], gran_b.astype(BF16), preferred_element_type=F32)
        slot_base = loff_b[:, 0:1] * GRANULE - 1.0 + cum
        s1 = jnp.zeros_like(g1)
        s2 = jnp.zeros_like(g1)
        for e in range(N_EXPERTS):
            row = slot_base[e:e + 1, :]
            s1 = jnp.where(e1 == e, row, s1)
            s2 = jnp.where(e2 == e, row, s2)
        plan_ref[...] = jnp.zeros_like(plan_ref)
        plan_ref[PLAN_SLOT1:PLAN_SLOT1 + 1, :] = s1
        plan_ref[PLAN_SLOT2:PLAN_SLOT2 + 1, :] = s2
        plan_ref[PLAN_G1:PLAN_G1 + 1, :] = g1
        plan_ref[PLAN_G2:PLAN_G2 + 1, :] = g2
        gran_ref[...] = gran_b.astype(jnp.int32)
        loff_ref[...] = loff_b.astype(jnp.int32)
        rt_sc[ROUTE_SLOT1:ROUTE_SLOT1 + 1, :] = s1
        rt_sc[ROUTE_SLOT2:ROUTE_SLOT2 + 1, :] = s2
    route_ref[...] = rt_sc[...].T


def _outproj(y, x, w_out_bf, layer, tm, g, b, rwt, rb, plan=False):
    n = x.shape[0]
    full = lambda shape: pl.BlockSpec(shape, lambda i: (0,) * len(shape))
    lay = lambda shape: pl.BlockSpec((None,) + shape, lambda i: (layer,) + (0,) * len(shape))
    in_specs = [pl.BlockSpec((tm, D_MODEL), lambda i: (i, 0)),
                pl.BlockSpec((tm, D_MODEL), lambda i: (i, 0)),
                lay((D_MODEL, D_MODEL)), lay((1, D_MODEL)), lay((1, D_MODEL)),
                full((N_EXPERTS, D_MODEL)), full((N_EXPERTS, 1))]
    out_specs = [pl.BlockSpec((tm, D_MODEL), lambda i: (i, 0)),
                 pl.BlockSpec((tm, LANES), lambda i: (i, 0))]
    out_shape = [jax.ShapeDtypeStruct((n, D_MODEL), F32),
                 jax.ShapeDtypeStruct((n, LANES), F32)]
    scratch = [pltpu.VMEM((LANES, tm), F32)]
    args = [y, x, w_out_bf, g, b, rwt, rb]
    if plan:
        tri = np.triu(np.ones((tm, tm), np.float32)).astype(jnp.bfloat16)
        ltri = np.tril(np.ones((N_EXPERTS, N_EXPERTS), np.float32), -1).astype(jnp.bfloat16)
        in_specs += [full((tm, tm)), full((N_EXPERTS, N_EXPERTS))]
        args += [tri, ltri]
        per_tile = pl.BlockSpec((None, N_EXPERTS, LANES), lambda i: (i, 0, 0))
        out_specs += [pl.BlockSpec((None, SUBLANES, tm), lambda i: (i, 0, 0)), per_tile, per_tile]
        out_shape += [jax.ShapeDtypeStruct((n // tm, SUBLANES, tm), F32),
                      jax.ShapeDtypeStruct((n // tm, N_EXPERTS, LANES), jnp.int32),
                      jax.ShapeDtypeStruct((n // tm, N_EXPERTS, LANES), jnp.int32)]
        scratch += [pltpu.VMEM((N_EXPERTS, tm), F32)]
    return pl.pallas_call(
        functools.partial(_outproj_kernel, plan=plan),
        grid=(n // tm,),
        in_specs=in_specs,
        out_specs=out_specs,
        out_shape=out_shape,
        scratch_shapes=scratch,
        compiler_params=_cparams(1),
        name="outproj_plan" if plan else "outproj",
    )(*args)


def _granule_copies(src, dst, src_g, dst_g, n, sem, start):
    size = GRANULES_PER_TILE
    while size >= 1:
        bit = n & size

        @pl.when(bit != 0)
        def _(size=size, src_g=src_g, dst_g=dst_g):
            s0 = pl.multiple_of(src_g * GRANULE, GRANULE)
            d0 = pl.multiple_of(dst_g * GRANULE, GRANULE)
            cp = pltpu.make_async_copy(src.at[pl.ds(s0, size * GRANULE)],
                                       dst.at[pl.ds(d0, size * GRANULE)], sem)
            cp.start() if start else cp.wait()

        src_g = src_g + bit
        dst_g = dst_g + bit
        size //= 2


def _dispatch_kernel(segg_ref, totg_ref, ctl_ref, plan_ref, x_ref, xs_ref, gs_ref,
                     xl_sc, gl_sc, zx_sc, zg_sc, sem, *, tt, total_tiles):
    @pl.when(pl.program_id(0) == 0)
    def _():
        zx_sc[...] = jnp.zeros_like(zx_sc)
        zg_sc[...] = jnp.zeros_like(zg_sc)

        def zero_tile(i):
            r0 = pl.multiple_of(i * MOE_TILE, MOE_TILE)
            return (pltpu.make_async_copy(zx_sc, xs_ref.at[pl.ds(r0, MOE_TILE)], sem.at[2]),
                    pltpu.make_async_copy(zg_sc, gs_ref.at[pl.ds(r0, MOE_TILE)], sem.at[2]))

        def run(i, start):
            for cp in zero_tile(i):
                cp.start() if start else cp.wait()

        for start in (True, False):
            for e in range(N_EXPERTS):
                end = segg_ref[e] + totg_ref[e]

                @pl.when(end % GRANULES_PER_TILE != 0)
                def _():
                    run(end // GRANULES_PER_TILE, start)

        last = N_EXPERTS - 1
        used = (segg_ref[last] + totg_ref[last] + GRANULES_PER_TILE - 1) // GRANULES_PER_TILE
        lax.fori_loop(used, total_tiles, lambda i, c: (run(i, True), c)[1], 0)
        lax.fori_loop(used, total_tiles, lambda i, c: (run(i, False), c)[1], 0)

    n_slots = xl_sc.shape[0]
    slot = lax.broadcasted_iota(jnp.int32, (n_slots, tt), 0).astype(F32)
    m1 = slot == plan_ref[PLAN_SLOT1:PLAN_SLOT1 + 1, :]
    m2 = slot == plan_ref[PLAN_SLOT2:PLAN_SLOT2 + 1, :]
    perm = jnp.where(m1 | m2, 1.0, 0.0).astype(BF16)
    xl_sc[...] = jnp.dot(perm, x_ref[...].astype(BF16), preferred_element_type=F32).astype(BF16)
    gate = (jnp.where(m1, plan_ref[PLAN_G1:PLAN_G1 + 1, :], 0.0)
            + jnp.where(m2, plan_ref[PLAN_G2:PLAN_G2 + 1, :], 0.0))
    gl_sc[...] = jnp.broadcast_to(jnp.sum(gate, axis=1, keepdims=True), gl_sc.shape)

    for start in (True, False):
        for e in range(N_EXPERTS):
            n, src_g, dst_g = ctl_ref[0, e], ctl_ref[1, e], ctl_ref[2, e]
            _granule_copies(xl_sc, xs_ref, src_g, dst_g, n, sem.at[0], start)
            _granule_copies(gl_sc, gs_ref, src_g, dst_g, n, sem.at[1], start)


def _dispatch(x1, plan, ctl, seg_g, tot_g, tt, rows):
    n = x1.shape[0]
    n_slots = _local_slots(tt)
    return pl.pallas_call(
        functools.partial(_dispatch_kernel, tt=tt, total_tiles=rows // MOE_TILE),
        grid_spec=pltpu.PrefetchScalarGridSpec(
            num_scalar_prefetch=2,
            grid=(n // tt,),
            in_specs=[pl.BlockSpec((None, SUBLANES, LANES), lambda i, s, c: (i, 0, 0),
                                   memory_space=pltpu.SMEM),
                      pl.BlockSpec((None, SUBLANES, tt), lambda i, s, c: (i, 0, 0)),
                      pl.BlockSpec((tt, D_MODEL), lambda i, s, c: (i, 0))],
            out_specs=[pl.BlockSpec(memory_space=pl.ANY), pl.BlockSpec(memory_space=pl.ANY)],
            scratch_shapes=[pltpu.VMEM((n_slots, D_MODEL), BF16), pltpu.VMEM((n_slots, LANES), F32),
                            pltpu.VMEM((MOE_TILE, D_MODEL), BF16), pltpu.VMEM((MOE_TILE, LANES), F32),
                            pltpu.SemaphoreType.DMA((3,))]),
        out_shape=[jax.ShapeDtypeStruct((rows, D_MODEL), BF16),
                   jax.ShapeDtypeStruct((rows, LANES), F32)],
        compiler_params=_cparams(1),
        name="moe_dispatch",
    )(seg_g, tot_g, ctl, plan, x1)


def _ffn_kernel(te_ref, nv_ref, xs_ref, gs_ref, wg_ref, wu_ref, wd_ref, ys_ref):
    used = pl.program_id(0) < nv_ref[0]

    @pl.when(used)
    def _():
        xb = xs_ref[...]
        hg = jnp.dot(xb, wg_ref[...], preferred_element_type=F32)
        hu = jnp.dot(xb, wu_ref[...], preferred_element_type=F32)
        h = (_silu(hg) * hu * gs_ref[:, 0:1]).astype(BF16)
        ys_ref[...] = jnp.dot(h, wd_ref[...], preferred_element_type=F32).astype(BF16)

    @pl.when(jnp.logical_not(used))
    def _():
        ys_ref[...] = jnp.zeros_like(ys_ref)


def _ffn(xs, gs, tile_expert, n_valid, wg_bf, wu_bf, wd_bf, layer):
    total_tiles = xs.shape[0] // MOE_TILE
    w_in_spec = pl.BlockSpec((None, None, D_MODEL, D_EXPERT), lambda i, te, nv: (layer, te[i], 0, 0))
    return pl.pallas_call(
        _ffn_kernel,
        grid_spec=pltpu.PrefetchScalarGridSpec(
            num_scalar_prefetch=2,
            grid=(total_tiles,),
            in_specs=[pl.BlockSpec((MOE_TILE, D_MODEL), lambda i, te, nv: (i, 0)),
                      pl.BlockSpec((MOE_TILE, LANES), lambda i, te, nv: (i, 0)),
                      w_in_spec, w_in_spec,
                      pl.BlockSpec((None, None, D_EXPERT, D_MODEL),
                                   lambda i, te, nv: (layer, te[i], 0, 0))],
            out_specs=pl.BlockSpec((MOE_TILE, D_MODEL), lambda i, te, nv: (i, 0))),
        out_shape=jax.ShapeDtypeStruct(xs.shape, BF16),
        compiler_params=_cparams(1),
        name="moe_ffn",
    )(tile_expert, n_valid, xs, gs, wg_bf, wu_bf, wd_bf)


def _combine_kernel(ctl_ref, x_ref, route_ref, g_ref, b_ref, ys_ref, o_ref, yl_sc, sem, *, tt):
    @pl.when(pl.program_id(0) == 0)
    def _():
        yl_sc[...] = jnp.zeros_like(yl_sc)

    for start in (True, False):
        for e in range(N_EXPERTS):
            n, loc_g, buf_g = ctl_ref[0, e], ctl_ref[1, e], ctl_ref[2, e]
            _granule_copies(ys_ref, yl_sc, buf_g, loc_g, n, sem.at[0], start)

    n_slots = yl_sc.shape[0]
    route = route_ref[...]
    slot = lax.broadcasted_iota(jnp.int32, (tt, n_slots), 1).astype(F32)
    pick = (slot == route[:, ROUTE_SLOT1:ROUTE_SLOT1 + 1]) | (slot == route[:, ROUTE_SLOT2:ROUTE_SLOT2 + 1])
    y = jnp.dot(jnp.where(pick, 1.0, 0.0).astype(BF16), yl_sc[...], preferred_element_type=F32)
    o_ref[...] = _layernorm(ALPHA * x_ref[...] + y, g_ref[...], b_ref[...])


def _combine(x1, route, ctl, ys, layer, tt, g, b):
    n = x1.shape[0]
    lay = lambda shape: pl.BlockSpec((None,) + shape, lambda i: (layer,) + (0,) * len(shape))
    return pl.pallas_call(
        functools.partial(_combine_kernel, tt=tt),
        grid=(n // tt,),
        in_specs=[pl.BlockSpec((None, SUBLANES, LANES), lambda i: (i, 0, 0), memory_space=pltpu.SMEM),
                  pl.BlockSpec((tt, D_MODEL), lambda i: (i, 0)),
                  pl.BlockSpec((tt, LANES), lambda i: (i, 0)),
                  lay((1, D_MODEL)), lay((1, D_MODEL)),
                  pl.BlockSpec(memory_space=pl.ANY)],
        out_specs=pl.BlockSpec((tt, D_MODEL), lambda i: (i, 0)),
        scratch_shapes=[pltpu.VMEM((_local_slots(tt), D_MODEL), BF16), pltpu.SemaphoreType.DMA((1,))],
        out_shape=jax.ShapeDtypeStruct((n, D_MODEL), F32),
        compiler_params=_cparams(1),
        name="moe_combine",
    )(ctl, x1, route, g, b, ys)


def _moe_sparse(x1, route, plan, gran, loff, wg_bf, wu_bf, wd_bf, layer, tt, g, b):
    n = x1.shape[0]
    n_tiles = n // tt
    gran = gran[:, :, 0]
    loff = loff[:, :, 0]
    max_tiles = -(-(2 * n + (GRANULE - 1) * N_EXPERTS * n_tiles) // MOE_TILE) + N_EXPERTS
    rows = max_tiles * MOE_TILE
    tot_g = jnp.sum(gran, axis=0)
    tiles = (tot_g + GRANULES_PER_TILE - 1) // GRANULES_PER_TILE
    tile_end = jnp.cumsum(tiles)
    seg_g = ((tile_end - tiles) * GRANULES_PER_TILE).astype(jnp.int32)
    n_valid = tile_end[-1]
    idx = jnp.minimum(jnp.arange(max_tiles, dtype=jnp.int32), n_valid - 1)
    tile_expert = jnp.sum(idx[:, None] >= tile_end[None, :], axis=1).astype(jnp.int32)
    buf_g = seg_g[None, :] + jnp.cumsum(gran, axis=0) - gran
    ctl = jnp.zeros((n_tiles, SUBLANES, LANES), jnp.int32)
    ctl = ctl.at[:, 0, :N_EXPERTS].set(gran).at[:, 1, :N_EXPERTS].set(loff).at[:, 2, :N_EXPERTS].set(buf_g)
    xs, gs = _dispatch(x1, plan, ctl, seg_g, tot_g.astype(jnp.int32), tt, rows)
    ys = _ffn(xs, gs, tile_expert, n_valid.reshape(1).astype(jnp.int32), wg_bf, wu_bf, wd_bf, layer)
    return _combine(x1, route, ctl, ys, layer, tt, g, b)


def _moe_kernel(x_ref, r_ref, wg_ref, wu_ref, wd_ref, g_ref, b_ref, o_ref, xb_sc, acc_sc):
    e = pl.program_id(1)

    @pl.when(e == 0)
    def _():
        xb_sc[...] = x_ref[...].astype(BF16)
        acc_sc[...] = jnp.zeros_like(acc_sc)

    xb = xb_sc[...]
    hg = jnp.dot(xb, wg_ref[...], preferred_element_type=F32)
    hu = jnp.dot(xb, wu_ref[...], preferred_element_type=F32)
    r = r_ref[...]
    lane = lax.broadcasted_iota(jnp.int32, r.shape, 1)
    c = jnp.sum(jnp.where(lane == e, r, 0.0), axis=-1, keepdims=True)
    h = _silu(hg) * hu * c
    acc_sc[...] += jnp.dot(h.astype(BF16), wd_ref[...], preferred_element_type=F32)

    @pl.when(e == pl.num_programs(1) - 1)
    def _():
        o_ref[...] = _layernorm(ALPHA * x_ref[...] + acc_sc[...], g_ref[...], b_ref[...])


def _moe(x1, route, wg_bf, wu_bf, wd_bf, layer, tm, g, b):
    n = x1.shape[0]
    lay = lambda shape: pl.BlockSpec((None,) + shape, lambda i, e: (layer,) + (0,) * len(shape))
    return pl.pallas_call(
        _moe_kernel,
        grid=(n // tm, N_EXPERTS),
        in_specs=[pl.BlockSpec((tm, D_MODEL), lambda i, e: (i, 0)),
                  pl.BlockSpec((tm, LANES), lambda i, e: (i, 0)),
                  pl.BlockSpec((None, None, D_MODEL, D_EXPERT), lambda i, e: (layer, e, 0, 0)),
                  pl.BlockSpec((None, None, D_MODEL, D_EXPERT), lambda i, e: (layer, e, 0, 0)),
                  pl.BlockSpec((None, None, D_EXPERT, D_MODEL), lambda i, e: (layer, e, 0, 0)),
                  lay((1, D_MODEL)), lay((1, D_MODEL))],
        out_specs=pl.BlockSpec((tm, D_MODEL), lambda i, e: (i, 0)),
        out_shape=jax.ShapeDtypeStruct((n, D_MODEL), F32),
        scratch_shapes=[pltpu.VMEM((tm, D_MODEL), BF16), pltpu.VMEM((tm, D_MODEL), F32)],
        compiler_params=_cparams(2),
        name="moe",
    )(x1, route, wg_bf, wu_bf, wd_bf, g, b)


def _block_diag(w):
    out = jnp.zeros((DEPTH, D_A, D_A), w.dtype)
    for gi in range(A_BLOCKS):
        out = out.at[:, gi * A_BLK:(gi + 1) * A_BLK, gi * A_BLK:(gi + 1) * A_BLK].set(w[:, gi])
    return out


def kernel(x_prompt, x_sample, state_rglru_h, state_conv, state_hgrn, state_ret, w_in, conv_w, conv_b, w_rgate, b_rgate, w_igate, b_igate, rglru_lambda, hgrn_lb_logits, hgrn_norm_g, w_out, ln1_g, ln1_b, router_w, router_b, exp_w_gate, exp_w_up, exp_w_down, ln2_g, ln2_b):
    batch, seq, _ = x_prompt.shape
    nb = x_sample.shape[0]

    w_in_bf = w_in.astype(BF16)
    w_out_bf = w_out.astype(BF16)
    wg_bf = exp_w_gate.astype(BF16)
    wu_bf = exp_w_up.astype(BF16)
    wd_bf = exp_w_down.astype(BF16)
    wri = jnp.concatenate([_block_diag(w_rgate), _block_diag(w_igate)], axis=-1).astype(BF16)
    bri = jnp.concatenate([b_rgate.reshape(DEPTH, 1, D_A), b_igate.reshape(DEPTH, 1, D_A)], axis=-1)
    cb = conv_b.reshape(DEPTH, 1, D_A)
    lam = rglru_lambda.reshape(DEPTH, 1, D_A)
    ng = jnp.tile(hgrn_norm_g, (1, B_HEADS)).reshape(DEPTH, 1, D_B)
    g1, b1 = ln1_g.reshape(DEPTH, 1, D_MODEL), ln1_b.reshape(DEPTH, 1, D_MODEL)
    g2, b2 = ln2_g.reshape(DEPTH, 1, D_MODEL), ln2_b.reshape(DEPTH, 1, D_MODEL)
    rwt = router_w.T
    rb = router_b.reshape(N_EXPERTS, 1)
    conv0_t = jnp.transpose(state_conv, (0, 2, 1, 3))

    xp = x_prompt.reshape(batch * seq, D_MODEL)
    xs = x_sample.reshape(nb, D_MODEL)
    hs_p, convs_p, hgs_p, rts_p = [], [], [], []
    hs_s, convs_s, hgs_s, rts_s = [], [], [], []
    for l in range(DEPTH):
        proj_p = _proj(xp, w_in_bf, l, 512)
        y_p, h_p, conv_p, hg_p, rt_p = _mix_prompt(
            proj_p, batch, seq, l, 512, conv_w, cb, wri, bri, lam, hgrn_lb_logits, ng)
        x1_p, route_p, plan_p, gran_p, loff_p = _outproj(
            y_p, xp, w_out_bf, l, 512, g1, b1, rwt, rb, plan=True)
        xp = _moe_sparse(x1_p, route_p, plan_p, gran_p, loff_p, wg_bf, wu_bf, wd_bf, l, 512, g2, b2)
        hs_p.append(h_p.reshape(batch, D_A))
        convs_p.append(conv_p)
        hgs_p.append(hg_p)
        rts_p.append(rt_p)

        proj_s = _proj(xs, w_in, l, nb)
        y_s, h_s, conv_s, hg_s, rt_s = _mix_sample(
            proj_s, state_rglru_h, conv0_t, state_hgrn, state_ret, l,
            conv_w, cb, wri, bri, lam, hgrn_lb_logits, ng)
        x1_s, route_s = _outproj(y_s, xs, w_out, l, nb, g1, b1, rwt, rb)
        xs = _moe(x1_s, route_s, wg_bf, wu_bf, wd_bf, l, nb, g2, b2)
        hs_s.append(h_s)
        convs_s.append(jnp.transpose(conv_s, (1, 0, 2)))
        hgs_s.append(hg_s)
        rts_s.append(rt_s)

    return (xp.reshape(batch, seq, D_MODEL), xs.reshape(nb, 1, D_MODEL),
            jnp.stack(hs_p), jnp.stack(hs_s), jnp.stack(convs_p), jnp.stack(convs_s),
            jnp.stack(hgs_p), jnp.stack(hgs_s), jnp.stack(rts_p), jnp.stack(rts_s))
```

```python
import functools

import numpy as np
import jax
import jax.numpy as jnp
from jax import lax
from jax.experimental import pallas as pl
from jax.experimental.pallas import tpu as pltpu

D_MODEL = 1024
DEPTH = 2
PAST_LEN = 16384
D_A = 256
A_BLOCKS = 4
A_BLK = 64
CONV_W = 4
RGLRU_C = 8.0
B_HEADS = 4
B_DK = 64
D_B = 256
C_HEADS = 4
C_DK = 128
D_C = 512
D_IN = 3584
B_CHUNK = 64
C_CHUNK = 128
ROPE_BASE = 10000.0
N_EXPERTS = 16
N_GROUPS = 4
EXP_PER_GROUP = 4
D_EXPERT = 512
LN_EPS = 1e-5
RMS_EPS = 1e-6
GN_EPS = 1e-6
F_TINY = 1e-30
ALPHA = (2 * DEPTH) ** 0.25

O_XA, O_GA, O_QB, O_FB, O_VB, O_GB, O_QC, O_KC, O_VC, O_GC = (
    0, 256, 512, 768, 1024, 1280, 1536, 2048, 2560, 3072)

V7X_VMEM_LIMIT_BYTES = 56 * 1024 * 1024
SUBLANES = 8
LANES = 128
HGRN_SAFE_MIN_LOGDECAY = -60.0

BF16 = jnp.bfloat16
F32 = jnp.float32
_NT = (((1,), (1,)), ((), ()))
_TN = (((0,), (0,)), ((), ()))


def _cparams(n_axes):
    return pltpu.CompilerParams(
        dimension_semantics=("arbitrary",) * n_axes,
        vmem_limit_bytes=V7X_VMEM_LIMIT_BYTES)


def _dot(a, b):
    return jnp.dot(a.astype(BF16), b.astype(BF16), preferred_element_type=F32)


def _dot_g(a, b, dims):
    return lax.dot_general(a.astype(BF16), b.astype(BF16), dims, preferred_element_type=F32)


def _sigmoid(x):
    return 0.5 * jnp.tanh(0.5 * x) + 0.5


def _sqrt_nonneg(x):
    return jnp.where(x > 0.0, x * lax.rsqrt(x), 0.0)


def _silu(x):
    return x * _sigmoid(x)


def _gelu_tanh(x):
    c = np.float32(np.sqrt(2.0 / np.pi))
    return 0.5 * x * (1.0 + jnp.tanh(c * (x + np.float32(0.044715) * (x * x * x))))


def _log_sigmoid(x):
    return -(jnp.maximum(-x, 0.0) + jnp.log(1.0 + jnp.exp(-jnp.abs(x))))


def _layernorm(z, g, b):
    mu = jnp.mean(z, axis=-1, keepdims=True)
    zc = z - mu
    var = jnp.mean(zc * zc, axis=-1, keepdims=True)
    return zc * lax.rsqrt(var + LN_EPS) * g + b


def _hgrn_lower_bound(lbl, layer):
    rows = [lbl[j:j + 1, :] for j in range(DEPTH)]
    m = rows[0]
    for r in rows[1:]:
        m = jnp.maximum(m, r)
    ex = [jnp.exp(r - m) for r in rows]
    tot = ex[0]
    for e in ex[1:]:
        tot = tot + e
    lb = jnp.zeros_like(m)
    for j in range(1, layer + 1):
        lb = lb + ex[j] / tot
    return lb


def _rglru_gates(u, wri, bri, lam):
    gates = _dot(u, wri) + bri
    r = _sigmoid(gates[:, :D_A])
    i = _sigmoid(gates[:, D_A:])
    log_a = RGLRU_C * r * _log_sigmoid(lam)
    a = jnp.exp(log_a)
    bterm = _sqrt_nonneg(jnp.maximum(1.0 - a * a, 0.0)) * (i * u)
    return a, bterm


def _split(x):
    hi = x.astype(BF16)
    return hi, (x - hi.astype(F32)).astype(BF16)


def _dot3(x, w):
    xh, xl = _split(x)
    wh, wl = _split(w)
    return (jnp.dot(xh, wh, preferred_element_type=F32)
            + jnp.dot(xl, wh, preferred_element_type=F32)
            + jnp.dot(xh, wl, preferred_element_type=F32))


def _proj_kernel(x_ref, w_ref, o_ref):
    if w_ref.dtype == BF16:
        xb = x_ref[...].astype(BF16)
        for j in range(0, D_IN, 512):
            o_ref[:, j:j + 512] = jnp.dot(xb, w_ref[:, j:j + 512], preferred_element_type=F32)
    else:
        x = x_ref[...]
        for j in range(0, D_IN, 512):
            o_ref[:, j:j + 512] = _dot3(x, w_ref[:, j:j + 512])


def _proj(x, w_in_bf, layer, tm):
    n = x.shape[0]
    return pl.pallas_call(
        _proj_kernel,
        grid=(n // tm,),
        in_specs=[pl.BlockSpec((tm, D_MODEL), lambda i: (i, 0)),
                  pl.BlockSpec((None, D_MODEL, D_IN), lambda i: (layer, 0, 0))],
        out_specs=pl.BlockSpec((tm, D_IN), lambda i: (i, 0)),
        out_shape=jax.ShapeDtypeStruct((n, D_IN), F32),
        compiler_params=_cparams(1),
        name="proj",
    )(x, w_in_bf)


def _retention_consts(chunk):
    lg = np.log1p(-np.exp2(-5.0 - np.arange(C_HEADS, dtype=np.float64)))
    idx = np.arange(chunk, dtype=np.float64)
    rel = idx[:, None] - idx[None, :]
    mask = rel >= 0
    dmat = np.where(mask[None], np.exp(np.where(mask, rel, 0.0)[None] * lg[:, None, None]), 0.0)
    qdec = np.exp((idx + 1.0)[None, :] * lg[:, None])
    kdec = np.exp((chunk - 1.0 - idx)[None, :] * lg[:, None])
    sdec = np.exp(chunk * lg)
    qdec_b = np.broadcast_to(qdec[:, :, None], (C_HEADS, chunk, C_DK))
    kdec_b = np.broadcast_to(kdec[:, :, None], (C_HEADS, chunk, C_DK))
    return (dmat.astype(np.float32), np.ascontiguousarray(qdec_b).astype(np.float32),
            np.ascontiguousarray(kdec_b).astype(np.float32), [float(v) for v in sdec])


def _rope_tables(positions):
    half = C_DK // 2
    inv = ROPE_BASE ** (-np.arange(half, dtype=np.float64) / half)
    ang = np.asarray(positions, dtype=np.float64)[:, None] * inv[None]
    cos = np.concatenate([np.cos(ang), np.cos(ang)], axis=-1)
    sin = np.concatenate([-np.sin(ang), np.sin(ang)], axis=-1)
    return cos.astype(np.float32), sin.astype(np.float32)


def _rope(x, cos, sin_signed):
    return x * cos + pltpu.roll(x, C_DK // 2, 1) * sin_signed


def _mix_prompt_kernel(p_ref, cos_ref, sin_ref, cw_ref, cb_ref, wri_ref, bri_ref, lam_ref,
                       lbl_ref, ng_ref, dmat_ref, qdec_ref, kdec_ref,
                       y_ref, h_ref, conv_ref, hg_ref, rt_ref,
                       prev_sc, hprev_sc, st_sc, sret_sc, kb_sc, bb_sc, vb_sc, oi_sc,
                       *, layer, tt, sdec):
    t = pl.program_id(1)
    nt = pl.num_programs(1)

    @pl.when(t == 0)
    def _():
        prev_sc[...] = jnp.zeros_like(prev_sc)
        hprev_sc[...] = jnp.zeros_like(hprev_sc)
        st_sc[...] = jnp.zeros_like(st_sc)
        sret_sc[...] = jnp.zeros_like(sret_sc)

    xa = p_ref[:, O_XA:O_XA + D_A]
    ga = p_ref[:, O_GA:O_GA + D_A]
    row = lax.broadcasted_iota(jnp.int32, (tt, D_A), 0)
    row8 = lax.broadcasted_iota(jnp.int32, (SUBLANES, D_A), 0)
    prev = prev_sc[...]

    def shifted(j):
        r = pltpu.roll(xa, j, 0)
        top = jnp.where(row8 < j, pltpu.roll(prev, j, 0), r[0:SUBLANES])
        return jnp.concatenate([top, r[SUBLANES:]], axis=0)

    u = cb_ref[...] + shifted(3) * cw_ref[0:1, :]
    u = u + shifted(2) * cw_ref[1:2, :]
    u = u + shifted(1) * cw_ref[2:3, :]
    u = u + xa * cw_ref[3:4, :]
    last8 = xa[tt - SUBLANES:tt]
    prev_sc[...] = last8

    a, bterm = _rglru_gates(u, wri_ref[...], bri_ref[...], lam_ref[...])
    s = 1
    while s < SUBLANES:
        keep = (row % SUBLANES) >= s
        a_sh = jnp.where(keep, pltpu.roll(a, s, 0), 1.0)
        b_sh = jnp.where(keep, pltpu.roll(bterm, s, 0), 0.0)
        bterm = a * b_sh + bterm
        a = a * a_sh
        s *= 2
    carry = hprev_sc[...]
    groups = []
    for gi in range(tt // SUBLANES):
        rows = slice(gi * SUBLANES, (gi + 1) * SUBLANES)
        h_in = jnp.broadcast_to(carry[SUBLANES - 1:SUBLANES, :], (SUBLANES, D_A))
        carry = a[rows] * h_in + bterm[rows]
        groups.append(carry)
    h = jnp.concatenate(groups, axis=0)
    hlast8 = carry
    hprev_sc[...] = hlast8
    y_ref[:, 0:D_A] = h * _gelu_tanh(ga)

    @pl.when(t == nt - 1)
    def _():
        h_ref[...] = pltpu.roll(hlast8, 1, 0)[0:1]
        conv_ref[...] = pltpu.roll(last8, CONV_W - 1, 0)[0:CONV_W - 1]

    lb = _hgrn_lower_bound(lbl_ref[...], layer)
    ng = ng_ref[...]
    cl = B_CHUNK
    crow = lax.broadcasted_iota(jnp.int32, (cl, D_B), 0)
    ccol = lax.broadcasted_iota(jnp.int32, (cl, D_B), 1)
    causal = (ccol % B_DK) <= crow
    br = lax.broadcasted_iota(jnp.int32, (D_B, D_B), 0)
    bc = lax.broadcasted_iota(jnp.int32, (D_B, D_B), 1)
    head_mask = (br // B_DK) == (bc // B_DK)
    seg = jnp.where(head_mask, 1.0, 0.0).astype(BF16)

    def seg_mean(x):
        hi = x.astype(BF16)
        lo = (x - hi.astype(F32)).astype(BF16)
        tot = (jnp.dot(hi, seg, preferred_element_type=F32)
               + jnp.dot(lo, seg, preferred_element_type=F32))
        return tot * (1.0 / B_DK)

    nck = tt // cl
    q = _silu(p_ref[:, O_QB:O_QB + D_B])
    fl = p_ref[:, O_FB:O_FB + D_B]
    v = p_ref[:, O_VB:O_VB + D_B]
    half_th = 0.5 * jnp.tanh(0.5 * fl)
    f = lb + (1.0 - lb) * (0.5 + half_th)
    b = jnp.log(jnp.maximum(f, F_TINY))
    k = (1.0 - lb) * (0.5 - half_th)
    trow = lax.broadcasted_iota(jnp.int32, (tt, D_B), 0) % cl
    sh = 1
    while sh < cl:
        b = b + jnp.where(trow >= sh, pltpu.roll(b, sh, 0), 0.0)
        sh *= 2

    def chunk_row(x, c, r):
        top = c * cl + r + 1
        return x[top - SUBLANES:top][SUBLANES - 1:SUBLANES]

    def spread(rows):
        return jnp.concatenate([jnp.broadcast_to(r, (cl, D_B)) for r in rows], axis=0)

    last_rows = [chunk_row(b, c, cl - 1) for c in range(nck)]
    mid_rows = [chunk_row(b, c, cl // 2 - 1) for c in range(nck)]
    b_last = spread(last_rows)
    b_mid = spread(mid_rows)
    qd = q * jnp.exp(b)
    kl = k * jnp.exp(b_last - b)
    lasts = jnp.concatenate(last_rows, axis=0)
    mids = jnp.concatenate(mid_rows, axis=0)
    safe = jnp.min(jnp.minimum(mids, lasts - mids)) >= HGRN_SAFE_MIN_LOGDECAY

    def next_state(st, c, sl):
        kv = jnp.where(head_mask, _dot_g(v[sl], kl[sl], _TN), 0.0)
        return st * jnp.exp(last_rows[c]) + kv

    @pl.when(safe)
    def _():
        qmid = q * jnp.exp(b - b_mid)
        kinv = k * jnp.exp(b_mid - b)
        st = st_sc[...]
        for c in range(nck):
            sl = slice(c * cl, (c + 1) * cl)
            k4 = jnp.where(head_mask, jnp.concatenate([kinv[sl]] * B_HEADS, axis=0), 0.0)
            sc = jnp.where(causal, _dot_g(qmid[sl], k4, _NT), 0.0)
            v4 = jnp.where(head_mask, jnp.concatenate([v[sl]] * B_HEADS, axis=0), 0.0)
            oi_sc[sl, :] = _dot(sc, v4) + _dot_g(qd[sl], st, _NT)
            st = next_state(st, c, sl)
        st_sc[...] = st

    @pl.when(jnp.logical_not(safe))
    def _():
        kb_sc[...] = k
        bb_sc[...] = b
        vb_sc[...] = v
        st = st_sc[...]
        for c in range(nck):
            sl = slice(c * cl, (c + 1) * cl)
            qc, bc_ = q[sl], b[sl]

            def pair(sidx, acc, c=c, qc=qc, bc_=bc_):
                src = pl.ds(c * cl + sidx, 1)
                e = jnp.exp(jnp.minimum(bc_ - bb_sc[src, :], 0.0)) * (qc * kb_sc[src, :])
                scr = jnp.dot(e.astype(BF16), seg, preferred_element_type=F32)
                return acc + jnp.where(crow >= sidx, scr, 0.0) * vb_sc[src, :]

            oi = lax.fori_loop(0, cl, pair, jnp.zeros((cl, D_B), F32))
            oi_sc[sl, :] = oi + _dot_g(qd[sl], st, _NT)
            st = next_state(st, c, sl)
        st_sc[...] = st

    o = oi_sc[...]
    o = o * lax.rsqrt(seg_mean(o * o) + RMS_EPS) * ng
    y_ref[:, D_A:D_A + D_B] = o * _silu(p_ref[:, O_GB:O_GB + D_B])

    @pl.when(t == nt - 1)
    def _():
        s_bd = st_sc[...].T
        for hh in range(B_HEADS):
            hg_ref[hh] = s_bd[hh * B_DK:(hh + 1) * B_DK, hh * B_DK:(hh + 1) * B_DK]

    rl = C_CHUNK
    cos = cos_ref[...]
    sin = sin_ref[...]
    for hh in range(C_HEADS):
        lo = hh * C_DK
        q = _rope(p_ref[:, O_QC + lo:O_QC + lo + C_DK], cos, sin)
        k = _rope(p_ref[:, O_KC + lo:O_KC + lo + C_DK], cos, sin) * (C_DK ** -0.5)
        v = p_ref[:, O_VC + lo:O_VC + lo + C_DK]
        s = sret_sc[hh]
        parts = []
        for c in range(tt // rl):
            sl = slice(c * rl, (c + 1) * rl)
            sc = _dot_g(q[sl], k[sl], _NT) * dmat_ref[hh]
            parts.append(_dot(sc, v[sl]) + _dot(q[sl] * qdec_ref[hh], s))
            s = sdec[hh] * s + _dot_g(k[sl] * kdec_ref[hh], v[sl], _TN)
        sret_sc[hh] = s
        o = jnp.concatenate(parts, axis=0)
        mu = jnp.mean(o, axis=-1, keepdims=True)
        oc = o - mu
        var = jnp.mean(oc * oc, axis=-1, keepdims=True)
        g = p_ref[:, O_GC + lo:O_GC + lo + C_DK]
        y_ref[:, D_A + D_B + lo:D_A + D_B + lo + C_DK] = oc * lax.rsqrt(var + GN_EPS) * _silu(g)

    @pl.when(t == nt - 1)
    def _():
        rt_ref[...] = sret_sc[...]


def _mix_prompt(proj, batch, seq, layer, tt, cw, cb, wri, bri, lam, lbl, ng):
    nt = seq // tt
    dmat, qdec, kdec, sdec = _retention_consts(C_CHUNK)
    cos, sin = _rope_tables(np.arange(seq))
    full = lambda shape: pl.BlockSpec(shape, lambda b, t: (0,) * len(shape))
    lay = lambda shape: pl.BlockSpec((None,) + shape, lambda b, t: (layer,) + (0,) * len(shape))
    kern = functools.partial(_mix_prompt_kernel, layer=layer, tt=tt, sdec=sdec)
    return pl.pallas_call(
        kern,
        grid=(batch, nt),
        in_specs=[
            pl.BlockSpec((tt, D_IN), lambda b, t: (b * nt + t, 0)),
            pl.BlockSpec((tt, C_DK), lambda b, t: (t, 0)),
            pl.BlockSpec((tt, C_DK), lambda b, t: (t, 0)),
            lay((CONV_W, D_A)), lay((1, D_A)), lay((D_A, 2 * D_A)), lay((1, 2 * D_A)),
            lay((1, D_A)), full((DEPTH, D_B)), lay((1, D_B)),
            full((C_HEADS, C_CHUNK, C_CHUNK)), full((C_HEADS, C_CHUNK, C_DK)),
            full((C_HEADS, C_CHUNK, C_DK)),
        ],
        out_specs=[
            pl.BlockSpec((tt, D_MODEL), lambda b, t: (b * nt + t, 0)),
            pl.BlockSpec((None, 1, D_A), lambda b, t: (b, 0, 0)),
            pl.BlockSpec((None, CONV_W - 1, D_A), lambda b, t: (b, 0, 0)),
            pl.BlockSpec((None, B_HEADS, B_DK, B_DK), lambda b, t: (b, 0, 0, 0)),
            pl.BlockSpec((None, C_HEADS, C_DK, C_DK), lambda b, t: (b, 0, 0, 0)),
        ],
        out_shape=[
            jax.ShapeDtypeStruct((batch * seq, D_MODEL), F32),
            jax.ShapeDtypeStruct((batch, 1, D_A), F32),
            jax.ShapeDtypeStruct((batch, CONV_W - 1, D_A), F32),
            jax.ShapeDtypeStruct((batch, B_HEADS, B_DK, B_DK), F32),
            jax.ShapeDtypeStruct((batch, C_HEADS, C_DK, C_DK), F32),
        ],
        scratch_shapes=[
            pltpu.VMEM((SUBLANES, D_A), F32), pltpu.VMEM((SUBLANES, D_A), F32),
            pltpu.VMEM((D_B, D_B), F32), pltpu.VMEM((C_HEADS, C_DK, C_DK), F32),
            pltpu.VMEM((tt, D_B), F32), pltpu.VMEM((tt, D_B), F32),
            pltpu.VMEM((tt, D_B), F32), pltpu.VMEM((tt, D_B), F32),
        ],
        compiler_params=_cparams(2),
        name="mix_prompt",
    )(proj, cos, sin, cw, cb, wri, bri, lam, lbl, ng, dmat, qdec, kdec)


def _column_matrix(x):
    pad = jnp.zeros((LANES - SUBLANES, LANES), F32)
    return jnp.concatenate([x, pad], axis=0).T


def _mix_sample_kernel(*refs, layer, gammas, n_prev):
    (p_ref, h0_ref, conv0_ref, hg0_ref, rt0_ref, cos_ref, sin_ref,
     cw_ref, cb_ref, wri_ref, bri_ref, lam_ref, lbl_ref, ng_ref) = refs[:14]
    prev = refs[14:14 + 4 * n_prev]
    y_ref, h_ref, conv_ref, hg_ref, rt_ref, o_sc = refs[14 + 4 * n_prev:]
    if n_prev:
        for j in range(n_prev):
            for dst, src in zip((h_ref, conv_ref, hg_ref, rt_ref), prev[4 * j:4 * j + 4]):
                dst[j] = src[...]
        h_ref, conv_ref, hg_ref, rt_ref = (r.at[layer] for r in (h_ref, conv_ref, hg_ref, rt_ref))
    tb = SUBLANES
    xa = p_ref[:, O_XA:O_XA + D_A]
    ga = p_ref[:, O_GA:O_GA + D_A]
    c0, c1, c2 = conv0_ref[0], conv0_ref[1], conv0_ref[2]
    u = cb_ref[...] + c0 * cw_ref[0:1, :]
    u = u + c1 * cw_ref[1:2, :]
    u = u + c2 * cw_ref[2:3, :]
    u = u + xa * cw_ref[3:4, :]
    conv_ref[0] = c1
    conv_ref[1] = c2
    conv_ref[2] = xa
    a, bterm = _rglru_gates(u, wri_ref[...], bri_ref[...], lam_ref[...])
    h = a * h0_ref[...] + bterm
    h_ref[...] = h
    y_ref[:, 0:D_A] = h * _gelu_tanh(ga)

    lb = _hgrn_lower_bound(lbl_ref[...], layer)
    for j in range(B_HEADS // 2):
        lo = j * LANES
        q = _silu(p_ref[:, O_QB + lo:O_QB + lo + LANES])
        fl = p_ref[:, O_FB + lo:O_FB + lo + LANES]
        v = p_ref[:, O_VB + lo:O_VB + lo + LANES]
        lbj = lb[:, lo:lo + LANES]
        f = lbj + (1.0 - lbj) * _sigmoid(fl)
        ef = jnp.exp(jnp.log(jnp.maximum(f, F_TINY)))
        k = (1.0 - lbj) * _sigmoid(-fl)
        qk = q * k
        qf_cols = _column_matrix(q * ef)
        k_cols = _column_matrix(k)
        f_cols = _column_matrix(ef)
        for hh in range(2):
            head = 2 * j + hh
            sl = slice(hh * B_DK, (hh + 1) * B_DK)
            dots = jnp.sum(qk[:, sl], axis=-1, keepdims=True)
            for b in range(tb):
                s_old = hg0_ref[b, head]
                v_row = v[b:b + 1, sl]
                hg_ref[b, head] = f_cols[sl, b:b + 1] * s_old + k_cols[sl, b:b + 1] * v_row
                o_row = (jnp.sum(qf_cols[sl, b:b + 1] * s_old, axis=0, keepdims=True)
                         + dots[b:b + 1, :] * v_row)
                o_sc[b:b + 1, head * B_DK:(head + 1) * B_DK] = o_row
    ng = ng_ref[...]
    for head in range(B_HEADS):
        sl = slice(head * B_DK, (head + 1) * B_DK)
        o = o_sc[:, sl]
        o = o * lax.rsqrt(jnp.mean(o * o, axis=-1, keepdims=True) + RMS_EPS) * ng[:, sl]
        g = p_ref[:, O_GB + head * B_DK:O_GB + (head + 1) * B_DK]
        y_ref[:, D_A + head * B_DK:D_A + (head + 1) * B_DK] = o * _silu(g)

    cos = cos_ref[...]
    sin = sin_ref[...]
    for head in range(C_HEADS):
        lo = head * C_DK
        gamma = gammas[head]
        q = _rope(p_ref[:, O_QC + lo:O_QC + lo + C_DK], cos, sin)
        k = _rope(p_ref[:, O_KC + lo:O_KC + lo + C_DK], cos, sin) * (C_DK ** -0.5)
        v = p_ref[:, O_VC + lo:O_VC + lo + C_DK]
        g = p_ref[:, O_GC + lo:O_GC + lo + C_DK]
        dots = jnp.sum(q * k, axis=-1, keepdims=True)
        q_cols = _column_matrix(q * gamma)
        k_cols = _column_matrix(k)
        for b in range(tb):
            s_old = rt0_ref[b, head]
            v_row = v[b:b + 1, :]
            rt_ref[b, head] = gamma * s_old + k_cols[:, b:b + 1] * v_row
            o_row = (jnp.sum(q_cols[:, b:b + 1] * s_old, axis=0, keepdims=True)
                     + dots[b:b + 1, :] * v_row)
            o_sc[b:b + 1, 0:C_DK] = o_row
        o = o_sc[:, 0:C_DK]
        mu = jnp.mean(o, axis=-1, keepdims=True)
        oc = o - mu
        var = jnp.mean(oc * oc, axis=-1, keepdims=True)
        y_ref[:, D_A + D_B + lo:D_A + D_B + lo + C_DK] = oc * lax.rsqrt(var + GN_EPS) * _silu(g)


def _mix_sample(proj, h0, conv0_t, hg0, rt0, layer, cw, cb, wri, bri, lam, lbl, ng, prev=()):
    nb = proj.shape[0]
    tb = SUBLANES
    lg = np.log1p(-np.exp2(-5.0 - np.arange(C_HEADS, dtype=np.float64)))
    gammas = [float(np.exp(v)) for v in lg]
    cos, sin = _rope_tables([PAST_LEN])
    full = lambda shape: pl.BlockSpec(shape, lambda i: (0,) * len(shape))
    lay = lambda shape: pl.BlockSpec((None,) + shape, lambda i: (layer,) + (0,) * len(shape))
    state_shapes = [(tb, D_A), (CONV_W - 1, tb, D_A), (tb, B_HEADS, B_DK, B_DK), (tb, C_HEADS, C_DK, C_DK)]
    state_maps = [lambda i: (i, 0), lambda i: (0, i, 0), lambda i: (i, 0, 0, 0), lambda i: (i, 0, 0, 0)]
    full_shapes = [(nb, D_A), (CONV_W - 1, nb, D_A), (nb, B_HEADS, B_DK, B_DK), (nb, C_HEADS, C_DK, C_DK)]
    state_specs = [pl.BlockSpec(s, m) for s, m in zip(state_shapes, state_maps)]
    if prev:
        stack = len(prev) + 1
        out_state_specs = [pl.BlockSpec((stack,) + s, lambda i, m=m: (0,) + m(i))
                           for s, m in zip(state_shapes, state_maps)]
        out_state_shapes = [jax.ShapeDtypeStruct((stack,) + s, F32) for s in full_shapes]
    else:
        out_state_specs = state_specs
        out_state_shapes = [jax.ShapeDtypeStruct(s, F32) for s in full_shapes]
    kern = functools.partial(_mix_sample_kernel, layer=layer, gammas=gammas, n_prev=len(prev))
    return pl.pallas_call(
        kern,
        grid=(nb // tb,),
        in_specs=[
            pl.BlockSpec((tb, D_IN), lambda i: (i, 0)),
            pl.BlockSpec((None, tb, D_A), lambda i: (layer, i, 0)),
            pl.BlockSpec((None, CONV_W - 1, tb, D_A), lambda i: (layer, 0, i, 0)),
            pl.BlockSpec((None, tb, B_HEADS, B_DK, B_DK), lambda i: (layer, i, 0, 0, 0)),
            pl.BlockSpec((None, tb, C_HEADS, C_DK, C_DK), lambda i: (layer, i, 0, 0, 0)),
            full((1, C_DK)), full((1, C_DK)),
            lay((CONV_W, D_A)), lay((1, D_A)), lay((D_A, 2 * D_A)), lay((1, 2 * D_A)),
            lay((1, D_A)), full((DEPTH, D_B)), lay((1, D_B)),
        ] + state_specs * len(prev),
        out_specs=[pl.BlockSpec((tb, D_MODEL), lambda i: (i, 0))] + out_state_specs,
        out_shape=[jax.ShapeDtypeStruct((nb, D_MODEL), F32)] + out_state_shapes,
        scratch_shapes=[pltpu.VMEM((tb, D_B), F32)],
        compiler_params=_cparams(1),
        name="mix_sample",
    )(proj, h0, conv0_t, hg0, rt0, cos, sin, cw, cb, wri, bri, lam, lbl, ng,
      *[a for states in prev for a in states])


def _route_rows(l):
    m = l[0]
    for x in l[1:]:
        m = jnp.maximum(m, x)
    ex = [jnp.exp(x - m) for x in l]
    tot = ex[0]
    for x in ex[1:]:
        tot = tot + x
    p = [x / tot for x in ex]
    scores = []
    for gi in range(N_GROUPS):
        a, b, c, d = p[4 * gi:4 * gi + 4]
        hi1, lo1 = jnp.maximum(a, b), jnp.minimum(a, b)
        hi2, lo2 = jnp.maximum(c, d), jnp.minimum(c, d)
        top1 = jnp.maximum(hi1, hi2)
        top2 = jnp.maximum(jnp.minimum(hi1, hi2), jnp.maximum(lo1, lo2))
        scores.append(top1 + top2)
    best = scores[0]
    gsel = jnp.zeros_like(best, dtype=jnp.int32)
    for gi in range(1, N_GROUPS):
        upd = scores[gi] > best
        gsel = jnp.where(upd, gi, gsel)
        best = jnp.where(upd, scores[gi], best)
    vals = []
    for j in range(EXP_PER_GROUP):
        v = p[j]
        for gi in range(1, N_GROUPS):
            v = jnp.where(gsel == gi, p[4 * gi + j], v)
        vals.append(v)
    sel = []
    for j in range(EXP_PER_GROUP):
        rank = jnp.zeros_like(gsel)
        for i in range(EXP_PER_GROUP):
            if i == j:
                continue
            ahead = (vals[i] > vals[j]) | ((vals[i] == vals[j]) & (i < j))
            rank = rank + jnp.where(ahead, 1, 0)
        sel.append(rank < 2)
    denom = jnp.zeros_like(best)
    for j in range(EXP_PER_GROUP):
        denom = denom + jnp.where(sel[j], vals[j], 0.0)
    gates = [jnp.where(sel[j], vals[j] / denom, 0.0) for j in range(EXP_PER_GROUP)]
    comb = [jnp.where(gsel == (e // EXP_PER_GROUP), gates[e % EXP_PER_GROUP], 0.0)
            for e in range(N_EXPERTS)]
    j1 = jnp.where(sel[0], 0, jnp.where(sel[1], 1, 2))
    j2 = jnp.where(sel[3], 3, jnp.where(sel[2], 2, 1))
    g1 = jnp.zeros_like(best)
    g2 = jnp.zeros_like(best)
    for j in range(EXP_PER_GROUP):
        g1 = jnp.where(j1 == j, gates[j], g1)
        g2 = jnp.where(j2 == j, gates[j], g2)
    return comb, gsel * EXP_PER_GROUP + j1, gsel * EXP_PER_GROUP + j2, g1, g2


MOE_TILE = 512
GRANULE = 16
GRANULES_PER_TILE = MOE_TILE // GRANULE
ROUTE_SLOT1, ROUTE_SLOT2 = N_EXPERTS, N_EXPERTS + 1
PLAN_SLOT1, PLAN_SLOT2, PLAN_G1, PLAN_G2 = 0, 1, 2, 3


def _local_slots(tt):
    worst = 2 * tt + (N_EXPERTS - 1) * GRANULE
    return -(-worst // LANES) * LANES


def _outproj_kernel(*refs, plan):
    if plan:
        (y_ref, x_ref, w_ref, g_ref, b_ref, rwt_ref, rb_ref, tri_ref, ltri_ref,
         x1_ref, route_ref, plan_ref, gran_ref, loff_ref, rt_sc, oh_sc) = refs
    else:
        y_ref, x_ref, w_ref, g_ref, b_ref, rwt_ref, rb_ref, x1_ref, route_ref, rt_sc = refs
    if w_ref.dtype == BF16:
        y = jnp.dot(y_ref[...].astype(BF16), w_ref[...], preferred_element_type=F32)
    else:
        y = _dot3(y_ref[...], w_ref[...])
    x1 = _layernorm(ALPHA * x_ref[...] + y, g_ref[...], b_ref[...])
    x1_ref[...] = x1
    hi, lo = _split(x1)
    rhi, rlo = _split(rwt_ref[...])
    lg = (lax.dot_general(rhi, hi, _NT, preferred_element_type=F32)
          + lax.dot_general(rhi, lo, _NT, preferred_element_type=F32)
          + lax.dot_general(rlo, hi, _NT, preferred_element_type=F32)) + rb_ref[...]
    comb, e1, e2, g1, g2 = _route_rows([lg[e:e + 1, :] for e in range(N_EXPERTS)])
    rt_sc[...] = jnp.zeros_like(rt_sc)
    if not plan:
        for e in range(N_EXPERTS):
            rt_sc[e:e + 1, :] = comb[e]
    else:
        for e in range(N_EXPERTS):
            oh_sc[e:e + 1, :] = jnp.where((e1 == e) | (e2 == e), 1.0, 0.0)
        oh = oh_sc[...]
        cum = jnp.dot(oh.astype(BF16), tri_ref[...], preferred_element_type=F32)
        count = jnp.sum(oh, axis=1, keepdims=True)
        gran = jnp.floor((count + (GRANULE - 1)) * (1.0 / GRANULE))
        gran_b = jnp.broadcast_to(gran, (N_EXPERTS, LANES))
        loff_b = jnp.dot(ltri_ref[...], gran_b.astype(BF16), preferred_element_type=F32)
        slot_base = loff_b[:, 0:1] * GRANULE - 1.0 + cum
        s1 = jnp.zeros_like(g1)
        s2 = jnp.zeros_like(g1)
        for e in range(N_EXPERTS):
            row = slot_base[e:e + 1, :]
            s1 = jnp.where(e1 == e, row, s1)
            s2 = jnp.where(e2 == e, row, s2)
        plan_ref[...] = jnp.zeros_like(plan_ref)
        plan_ref[PLAN_SLOT1:PLAN_SLOT1 + 1, :] = s1
        plan_ref[PLAN_SLOT2:PLAN_SLOT2 + 1, :] = s2
        plan_ref[PLAN_G1:PLAN_G1 + 1, :] = g1
        plan_ref[PLAN_G2:PLAN_G2 + 1, :] = g2
        gran_ref[...] = gran_b.astype(jnp.int32)
        loff_ref[...] = loff_b.astype(jnp.int32)
        rt_sc[ROUTE_SLOT1:ROUTE_SLOT1 + 1, :] = s1
        rt_sc[ROUTE_SLOT2:ROUTE_SLOT2 + 1, :] = s2
    route_ref[...] = rt_sc[...].T


def _outproj(y, x, w_out_bf, layer, tm, g, b, rwt, rb, plan=False):
    n = x.shape[0]
    full = lambda shape: pl.BlockSpec(shape, lambda i: (0,) * len(shape))
    lay = lambda shape: pl.BlockSpec((None,) + shape, lambda i: (layer,) + (0,) * len(shape))
    in_specs = [pl.BlockSpec((tm, D_MODEL), lambda i: (i, 0)),
                pl.BlockSpec((tm, D_MODEL), lambda i: (i, 0)),
                lay((D_MODEL, D_MODEL)), lay((1, D_MODEL)), lay((1, D_MODEL)),
                full((N_EXPERTS, D_MODEL)), full((N_EXPERTS, 1))]
    out_specs = [pl.BlockSpec((tm, D_MODEL), lambda i: (i, 0)),
                 pl.BlockSpec((tm, LANES), lambda i: (i, 0))]
    out_shape = [jax.ShapeDtypeStruct((n, D_MODEL), F32),
                 jax.ShapeDtypeStruct((n, LANES), F32)]
    scratch = [pltpu.VMEM((LANES, tm), F32)]
    args = [y, x, w_out_bf, g, b, rwt, rb]
    if plan:
        tri = np.triu(np.ones((tm, tm), np.float32)).astype(jnp.bfloat16)
        ltri = np.tril(np.ones((N_EXPERTS, N_EXPERTS), np.float32), -1).astype(jnp.bfloat16)
        in_specs += [full((tm, tm)), full((N_EXPERTS, N_EXPERTS))]
        args += [tri, ltri]
        per_tile = pl.BlockSpec((None, N_EXPERTS, LANES), lambda i: (i, 0, 0))
        out_specs += [pl.BlockSpec((None, SUBLANES, tm), lambda i: (i, 0, 0)), per_tile, per_tile]
        out_shape += [jax.ShapeDtypeStruct((n // tm, SUBLANES, tm), F32),
                      jax.ShapeDtypeStruct((n // tm, N_EXPERTS, LANES), jnp.int32),
                      jax.ShapeDtypeStruct((n // tm, N_EXPERTS, LANES), jnp.int32)]
        scratch += [pltpu.VMEM((N_EXPERTS, tm), F32)]
    return pl.pallas_call(
        functools.partial(_outproj_kernel, plan=plan),
        grid=(n // tm,),
        in_specs=in_specs,
        out_specs=out_specs,
        out_shape=out_shape,
        scratch_shapes=scratch,
        compiler_params=_cparams(1),
        name="outproj_plan" if plan else "outproj",
    )(*args)


def _granule_copies(src, dst, src_g, dst_g, n, sem, start):
    size = GRANULES_PER_TILE
    while size >= 1:
        bit = n & size

        @pl.when(bit != 0)
        def _(size=size, src_g=src_g, dst_g=dst_g):
            s0 = pl.multiple_of(src_g * GRANULE, GRANULE)
            d0 = pl.multiple_of(dst_g * GRANULE, GRANULE)
            cp = pltpu.make_async_copy(src.at[pl.ds(s0, size * GRANULE)],
                                       dst.at[pl.ds(d0, size * GRANULE)], sem)
            cp.start() if start else cp.wait()

        src_g = src_g + bit
        dst_g = dst_g + bit
        size //= 2


def _dispatch_kernel(segg_ref, totg_ref, ctl_ref, plan_ref, x_ref, xs_ref, gs_ref,
                     xl_sc, gl_sc, zx_sc, zg_sc, sem, *, tt, total_tiles):
    @pl.when(pl.program_id(0) == 0)
    def _():
        zx_sc[...] = jnp.zeros_like(zx_sc)
        zg_sc[...] = jnp.zeros_like(zg_sc)

        def zero_tile(i):
            r0 = pl.multiple_of(i * MOE_TILE, MOE_TILE)
            return (pltpu.make_async_copy(zx_sc, xs_ref.at[pl.ds(r0, MOE_TILE)], sem.at[2]),
                    pltpu.make_async_copy(zg_sc, gs_ref.at[pl.ds(r0, MOE_TILE)], sem.at[2]))

        def run(i, start):
            for cp in zero_tile(i):
                cp.start() if start else cp.wait()

        for start in (True, False):
            for e in range(N_EXPERTS):
                end = segg_ref[e] + totg_ref[e]

                @pl.when(end % GRANULES_PER_TILE != 0)
                def _():
                    run(end // GRANULES_PER_TILE, start)

        last = N_EXPERTS - 1
        used = (segg_ref[last] + totg_ref[last] + GRANULES_PER_TILE - 1) // GRANULES_PER_TILE
        lax.fori_loop(used, total_tiles, lambda i, c: (run(i, True), c)[1], 0)
        lax.fori_loop(used, total_tiles, lambda i, c: (run(i, False), c)[1], 0)

    n_slots = xl_sc.shape[0]
    slot = lax.broadcasted_iota(jnp.int32, (n_slots, tt), 0).astype(F32)
    m1 = slot == plan_ref[PLAN_SLOT1:PLAN_SLOT1 + 1, :]
    m2 = slot == plan_ref[PLAN_SLOT2:PLAN_SLOT2 + 1, :]
    perm = jnp.where(m1 | m2, 1.0, 0.0).astype(BF16)
    xl_sc[...] = jnp.dot(perm, x_ref[...].astype(BF16), preferred_element_type=F32).astype(BF16)
    gate = (jnp.where(m1, plan_ref[PLAN_G1:PLAN_G1 + 1, :], 0.0)
            + jnp.where(m2, plan_ref[PLAN_G2:PLAN_G2 + 1, :], 0.0))
    gl_sc[...] = jnp.broadcast_to(jnp.sum(gate, axis=1, keepdims=True), gl_sc.shape)

    for start in (True, False):
        for e in range(N_EXPERTS):
            n, src_g, dst_g = ctl_ref[0, e], ctl_ref[1, e], ctl_ref[2, e]
            _granule_copies(xl_sc, xs_ref, src_g, dst_g, n, sem.at[0], start)
            _granule_copies(gl_sc, gs_ref, src_g, dst_g, n, sem.at[1], start)


def _dispatch(x1, plan, ctl, seg_g, tot_g, tt, rows):
    n = x1.shape[0]
    n_slots = _local_slots(tt)
    return pl.pallas_call(
        functools.partial(_dispatch_kernel, tt=tt, total_tiles=rows // MOE_TILE),
        grid_spec=pltpu.PrefetchScalarGridSpec(
            num_scalar_prefetch=2,
            grid=(n // tt,),
            in_specs=[pl.BlockSpec((None, SUBLANES, LANES), lambda i, s, c: (i, 0, 0),
                                   memory_space=pltpu.SMEM),
                      pl.BlockSpec((None, SUBLANES, tt), lambda i, s, c: (i, 0, 0)),
                      pl.BlockSpec((tt, D_MODEL), lambda i, s, c: (i, 0))],
            out_specs=[pl.BlockSpec(memory_space=pl.ANY), pl.BlockSpec(memory_space=pl.ANY)],
            scratch_shapes=[pltpu.VMEM((n_slots, D_MODEL), BF16), pltpu.VMEM((n_slots, LANES), F32),
                            pltpu.VMEM((MOE_TILE, D_MODEL), BF16), pltpu.VMEM((MOE_TILE, LANES), F32),
                            pltpu.SemaphoreType.DMA((3,))]),
        out_shape=[jax.ShapeDtypeStruct((rows, D_MODEL), BF16),
                   jax.ShapeDtypeStruct((rows, LANES), F32)],
        compiler_params=_cparams(1),
        name="moe_dispatch",
    )(seg_g, tot_g, ctl, plan, x1)


def _ffn_kernel(te_ref, nv_ref, xs_ref, gs_ref, wg_ref, wu_ref, wd_ref, ys_ref):
    used = pl.program_id(0) < nv_ref[0]

    @pl.when(used)
    def _():
        xb = xs_ref[...]
        hg = jnp.dot(xb, wg_ref[...].astype(BF16), preferred_element_type=F32)
        hu = jnp.dot(xb, wu_ref[...].astype(BF16), preferred_element_type=F32)
        h = (_silu(hg) * hu * gs_ref[:, 0:1]).astype(BF16)
        ys_ref[...] = jnp.dot(h, wd_ref[...].astype(BF16), preferred_element_type=F32).astype(BF16)

    @pl.when(jnp.logical_not(used))
    def _():
        ys_ref[...] = jnp.zeros_like(ys_ref)


def _ffn(xs, gs, tile_expert, n_valid, wg, wu, wd, layer):
    total_tiles = xs.shape[0] // MOE_TILE
    w_in_spec = pl.BlockSpec((None, None, D_MODEL, D_EXPERT), lambda i, te, nv: (layer, te[i], 0, 0))
    return pl.pallas_call(
        _ffn_kernel,
        grid_spec=pltpu.PrefetchScalarGridSpec(
            num_scalar_prefetch=2,
            grid=(total_tiles,),
            in_specs=[pl.BlockSpec((MOE_TILE, D_MODEL), lambda i, te, nv: (i, 0)),
                      pl.BlockSpec((MOE_TILE, LANES), lambda i, te, nv: (i, 0)),
                      w_in_spec, w_in_spec,
                      pl.BlockSpec((None, None, D_EXPERT, D_MODEL),
                                   lambda i, te, nv: (layer, te[i], 0, 0))],
            out_specs=pl.BlockSpec((MOE_TILE, D_MODEL), lambda i, te, nv: (i, 0))),
        out_shape=jax.ShapeDtypeStruct(xs.shape, BF16),
        compiler_params=_cparams(1),
        name="moe_ffn",
    )(tile_expert, n_valid, xs, gs, wg, wu, wd)


def _combine_kernel(ctl_ref, x_ref, route_ref, g_ref, b_ref, ys_ref, o_ref, yl_sc, sem, *, tt):
    @pl.when(pl.program_id(0) == 0)
    def _():
        yl_sc[...] = jnp.zeros_like(yl_sc)

    for start in (True, False):
        for e in range(N_EXPERTS):
            n, loc_g, buf_g = ctl_ref[0, e], ctl_ref[1, e], ctl_ref[2, e]
            _granule_copies(ys_ref, yl_sc, buf_g, loc_g, n, sem.at[0], start)

    n_slots = yl_sc.shape[0]
    route = route_ref[...]
    slot = lax.broadcasted_iota(jnp.int32, (tt, n_slots), 1).astype(F32)
    pick = (slot == route[:, ROUTE_SLOT1:ROUTE_SLOT1 + 1]) | (slot == route[:, ROUTE_SLOT2:ROUTE_SLOT2 + 1])
    y = jnp.dot(jnp.where(pick, 1.0, 0.0).astype(BF16), yl_sc[...], preferred_element_type=F32)
    o_ref[...] = _layernorm(ALPHA * x_ref[...] + y, g_ref[...], b_ref[...])


def _combine(x1, route, ctl, ys, layer, tt, g, b):
    n = x1.shape[0]
    lay = lambda shape: pl.BlockSpec((None,) + shape, lambda i: (layer,) + (0,) * len(shape))
    return pl.pallas_call(
        functools.partial(_combine_kernel, tt=tt),
        grid=(n // tt,),
        in_specs=[pl.BlockSpec((None, SUBLANES, LANES), lambda i: (i, 0, 0), memory_space=pltpu.SMEM),
                  pl.BlockSpec((tt, D_MODEL), lambda i: (i, 0)),
                  pl.BlockSpec((tt, LANES), lambda i: (i, 0)),
                  lay((1, D_MODEL)), lay((1, D_MODEL)),
                  pl.BlockSpec(memory_space=pl.ANY)],
        out_specs=pl.BlockSpec((tt, D_MODEL), lambda i: (i, 0)),
        scratch_shapes=[pltpu.VMEM((_local_slots(tt), D_MODEL), BF16), pltpu.SemaphoreType.DMA((1,))],
        out_shape=jax.ShapeDtypeStruct((n, D_MODEL), F32),
        compiler_params=_cparams(1),
        name="moe_combine",
    )(ctl, x1, route, g, b, ys)


def _moe_sparse(x1, route, plan, gran, loff, wg, wu, wd, layer, tt, g, b):
    n = x1.shape[0]
    n_tiles = n // tt
    gran = gran[:, :, 0]
    loff = loff[:, :, 0]
    max_tiles = -(-(2 * n + (GRANULE - 1) * N_EXPERTS * n_tiles) // MOE_TILE) + N_EXPERTS
    rows = max_tiles * MOE_TILE
    tot_g = jnp.sum(gran, axis=0)
    tiles = (tot_g + GRANULES_PER_TILE - 1) // GRANULES_PER_TILE
    tile_end = jnp.cumsum(tiles)
    seg_g = ((tile_end - tiles) * GRANULES_PER_TILE).astype(jnp.int32)
    n_valid = tile_end[-1]
    idx = jnp.minimum(jnp.arange(max_tiles, dtype=jnp.int32), n_valid - 1)
    tile_expert = jnp.sum(idx[:, None] >= tile_end[None, :], axis=1).astype(jnp.int32)
    buf_g = seg_g[None, :] + jnp.cumsum(gran, axis=0) - gran
    ctl = jnp.zeros((n_tiles, SUBLANES, LANES), jnp.int32)
    ctl = ctl.at[:, 0, :N_EXPERTS].set(gran).at[:, 1, :N_EXPERTS].set(loff).at[:, 2, :N_EXPERTS].set(buf_g)
    xs, gs = _dispatch(x1, plan, ctl, seg_g, tot_g.astype(jnp.int32), tt, rows)
    ys = _ffn(xs, gs, tile_expert, n_valid.reshape(1).astype(jnp.int32), wg, wu, wd, layer)
    return _combine(x1, route, ctl, ys, layer, tt, g, b)


def _moe_kernel(x_ref, r_ref, wg_ref, wu_ref, wd_ref, g_ref, b_ref, o_ref, xb_sc, acc_sc):
    e = pl.program_id(1)

    @pl.when(e == 0)
    def _():
        xb_sc[...] = x_ref[...].astype(BF16)
        acc_sc[...] = jnp.zeros_like(acc_sc)

    xb = xb_sc[...]
    hg = jnp.dot(xb, wg_ref[...].astype(BF16), preferred_element_type=F32)
    hu = jnp.dot(xb, wu_ref[...].astype(BF16), preferred_element_type=F32)
    r = r_ref[...]
    lane = lax.broadcasted_iota(jnp.int32, r.shape, 1)
    c = jnp.sum(jnp.where(lane == e, r, 0.0), axis=-1, keepdims=True)
    h = _silu(hg) * hu * c
    acc_sc[...] += jnp.dot(h.astype(BF16), wd_ref[...].astype(BF16), preferred_element_type=F32)

    @pl.when(e == pl.num_programs(1) - 1)
    def _():
        o_ref[...] = _layernorm(ALPHA * x_ref[...] + acc_sc[...], g_ref[...], b_ref[...])


def _moe(x1, route, wg, wu, wd, layer, tm, g, b):
    n = x1.shape[0]
    lay = lambda shape: pl.BlockSpec((None,) + shape, lambda i, e: (layer,) + (0,) * len(shape))
    return pl.pallas_call(
        _moe_kernel,
        grid=(n // tm, N_EXPERTS),
        in_specs=[pl.BlockSpec((tm, D_MODEL), lambda i, e: (i, 0)),
                  pl.BlockSpec((tm, LANES), lambda i, e: (i, 0)),
                  pl.BlockSpec((None, None, D_MODEL, D_EXPERT), lambda i, e: (layer, e, 0, 0)),
                  pl.BlockSpec((None, None, D_MODEL, D_EXPERT), lambda i, e: (layer, e, 0, 0)),
                  pl.BlockSpec((None, None, D_EXPERT, D_MODEL), lambda i, e: (layer, e, 0, 0)),
                  lay((1, D_MODEL)), lay((1, D_MODEL))],
        out_specs=pl.BlockSpec((tm, D_MODEL), lambda i, e: (i, 0)),
        out_shape=jax.ShapeDtypeStruct((n, D_MODEL), F32),
        scratch_shapes=[pltpu.VMEM((tm, D_MODEL), BF16), pltpu.VMEM((tm, D_MODEL), F32)],
        compiler_params=_cparams(2),
        name="moe",
    )(x1, route, wg, wu, wd, g, b)


def _block_diag(w):
    out = jnp.zeros((DEPTH, D_A, D_A), w.dtype)
    for gi in range(A_BLOCKS):
        out = out.at[:, gi * A_BLK:(gi + 1) * A_BLK, gi * A_BLK:(gi + 1) * A_BLK].set(w[:, gi])
    return out


def kernel(x_prompt, x_sample, state_rglru_h, state_conv, state_hgrn, state_ret, w_in, conv_w, conv_b, w_rgate, b_rgate, w_igate, b_igate, rglru_lambda, hgrn_lb_logits, hgrn_norm_g, w_out, ln1_g, ln1_b, router_w, router_b, exp_w_gate, exp_w_up, exp_w_down, ln2_g, ln2_b):
    batch, seq, _ = x_prompt.shape
    nb = x_sample.shape[0]

    w_in_bf = w_in.astype(BF16)
    w_out_bf = w_out.astype(BF16)
    wg, wu, wd = exp_w_gate, exp_w_up, exp_w_down
    wri = jnp.concatenate([_block_diag(w_rgate), _block_diag(w_igate)], axis=-1).astype(BF16)
    bri = jnp.concatenate([b_rgate.reshape(DEPTH, 1, D_A), b_igate.reshape(DEPTH, 1, D_A)], axis=-1)
    cb = conv_b.reshape(DEPTH, 1, D_A)
    lam = rglru_lambda.reshape(DEPTH, 1, D_A)
    ng = jnp.tile(hgrn_norm_g, (1, B_HEADS)).reshape(DEPTH, 1, D_B)
    g1, b1 = ln1_g.reshape(DEPTH, 1, D_MODEL), ln1_b.reshape(DEPTH, 1, D_MODEL)
    g2, b2 = ln2_g.reshape(DEPTH, 1, D_MODEL), ln2_b.reshape(DEPTH, 1, D_MODEL)
    rwt = router_w.T
    rb = router_b.reshape(N_EXPERTS, 1)
    conv0_t = jnp.transpose(state_conv, (0, 2, 1, 3))

    xp = x_prompt.reshape(batch * seq, D_MODEL)
    xs = x_sample.reshape(nb, D_MODEL)
    hs_p, convs_p, hgs_p, rts_p = [], [], [], []
    prev_s = []
    for l in range(DEPTH):
        proj_p = _proj(xp, w_in_bf, l, 512)
        y_p, h_p, conv_p, hg_p, rt_p = _mix_prompt(
            proj_p, batch, seq, l, 512, conv_w, cb, wri, bri, lam, hgrn_lb_logits, ng)
        x1_p, route_p, plan_p, gran_p, loff_p = _outproj(
            y_p, xp, w_out_bf, l, 512, g1, b1, rwt, rb, plan=True)
        xp = _moe_sparse(x1_p, route_p, plan_p, gran_p, loff_p, wg, wu, wd, l, 512, g2, b2)
        hs_p.append(h_p.reshape(batch, D_A))
        convs_p.append(conv_p)
        hgs_p.append(hg_p)
        rts_p.append(rt_p)

        proj_s = _proj(xs, w_in, l, nb)
        y_s, *states_s = _mix_sample(
            proj_s, state_rglru_h, conv0_t, state_hgrn, state_ret, l,
            conv_w, cb, wri, bri, lam, hgrn_lb_logits, ng,
            prev=prev_s if l == DEPTH - 1 else ())
        prev_s.append(states_s)
        x1_s, route_s = _outproj(y_s, xs, w_out, l, nb, g1, b1, rwt, rb)
        xs = _moe(x1_s, route_s, wg, wu, wd, l, nb, g2, b2)

    h_s, conv_s, hg_s, rt_s = prev_s[-1]
    return (xp.reshape(batch, seq, D_MODEL), xs.reshape(nb, 1, D_MODEL),
            jnp.stack(hs_p), h_s, jnp.stack(convs_p), jnp.transpose(conv_s, (0, 2, 1, 3)),
            jnp.stack(hgs_p), hg_s, jnp.stack(rts_p), rt_s)
```

```python
import functools

import numpy as np
import jax
import jax.numpy as jnp
from jax import lax
from jax.experimental import pallas as pl
from jax.experimental.pallas import tpu as pltpu

D_MODEL = 1024
DEPTH = 2
PAST_LEN = 16384
D_A = 256
A_BLOCKS = 4
A_BLK = 64
CONV_W = 4
RGLRU_C = 8.0
B_HEADS = 4
B_DK = 64
D_B = 256
C_HEADS = 4
C_DK = 128
D_C = 512
D_IN = 3584
B_CHUNK = 64
C_CHUNK = 128
ROPE_BASE = 10000.0
N_EXPERTS = 16
N_GROUPS = 4
EXP_PER_GROUP = 4
D_EXPERT = 512
LN_EPS = 1e-5
RMS_EPS = 1e-6
GN_EPS = 1e-6
F_TINY = 1e-30
ALPHA = (2 * DEPTH) ** 0.25

O_XA, O_GA, O_QB, O_FB, O_VB, O_GB, O_QC, O_KC, O_VC, O_GC = (
    0, 256, 512, 768, 1024, 1280, 1536, 2048, 2560, 3072)

V7X_VMEM_LIMIT_BYTES = 56 * 1024 * 1024
SUBLANES = 8
LANES = 128
HGRN_SAFE_MIN_LOGDECAY = -60.0

BF16 = jnp.bfloat16
F32 = jnp.float32
_NT = (((1,), (1,)), ((), ()))
_TN = (((0,), (0,)), ((), ()))


def _cparams(n_axes):
    return pltpu.CompilerParams(
        dimension_semantics=("arbitrary",) * n_axes,
        vmem_limit_bytes=V7X_VMEM_LIMIT_BYTES)


def _dot(a, b):
    return jnp.dot(a.astype(BF16), b.astype(BF16), preferred_element_type=F32)


def _dot_g(a, b, dims):
    return lax.dot_general(a.astype(BF16), b.astype(BF16), dims, preferred_element_type=F32)


def _sigmoid(x):
    return 0.5 * jnp.tanh(0.5 * x) + 0.5


def _sqrt_nonneg(x):
    return jnp.where(x > 0.0, x * lax.rsqrt(x), 0.0)


def _silu(x):
    return x * _sigmoid(x)


def _gelu_tanh(x):
    c = np.float32(np.sqrt(2.0 / np.pi))
    return 0.5 * x * (1.0 + jnp.tanh(c * (x + np.float32(0.044715) * (x * x * x))))


def _log_sigmoid(x):
    return -(jnp.maximum(-x, 0.0) + jnp.log(1.0 + jnp.exp(-jnp.abs(x))))


def _layernorm(z, g, b):
    mu = jnp.mean(z, axis=-1, keepdims=True)
    zc = z - mu
    var = jnp.mean(zc * zc, axis=-1, keepdims=True)
    return zc * lax.rsqrt(var + LN_EPS) * g + b


def _hgrn_lower_bound(lbl, layer):
    rows = [lbl[j:j + 1, :] for j in range(DEPTH)]
    m = rows[0]
    for r in rows[1:]:
        m = jnp.maximum(m, r)
    ex = [jnp.exp(r - m) for r in rows]
    tot = ex[0]
    for e in ex[1:]:
        tot = tot + e
    lb = jnp.zeros_like(m)
    for j in range(1, layer + 1):
        lb = lb + ex[j] / tot
    return lb


def _rglru_gates(u, wri, bri, lam):
    gates = _dot(u, wri) + bri
    r = _sigmoid(gates[:, :D_A])
    i = _sigmoid(gates[:, D_A:])
    log_a = RGLRU_C * r * _log_sigmoid(lam)
    a = jnp.exp(log_a)
    bterm = _sqrt_nonneg(jnp.maximum(1.0 - a * a, 0.0)) * (i * u)
    return a, bterm


def _split(x):
    hi = x.astype(BF16)
    return hi, (x - hi.astype(F32)).astype(BF16)


def _dot3(x, w):
    xh, xl = _split(x)
    wh, wl = _split(w)
    return (jnp.dot(xh, wh, preferred_element_type=F32)
            + jnp.dot(xl, wh, preferred_element_type=F32)
            + jnp.dot(xh, wl, preferred_element_type=F32))


def _proj_kernel(x_ref, w_ref, o_ref):
    x = x_ref[...]
    for j in range(0, D_IN, 512):
        o_ref[:, j:j + 512] = _dot3(x, w_ref[:, j:j + 512])


def _proj(x, w_in_bf, layer, tm):
    n = x.shape[0]
    return pl.pallas_call(
        _proj_kernel,
        grid=(n // tm,),
        in_specs=[pl.BlockSpec((tm, D_MODEL), lambda i: (i, 0)),
                  pl.BlockSpec((None, D_MODEL, D_IN), lambda i: (layer, 0, 0))],
        out_specs=pl.BlockSpec((tm, D_IN), lambda i: (i, 0)),
        out_shape=jax.ShapeDtypeStruct((n, D_IN), F32),
        compiler_params=_cparams(1),
        name="proj",
    )(x, w_in_bf)


def _retention_consts(chunk):
    lg = np.log1p(-np.exp2(-5.0 - np.arange(C_HEADS, dtype=np.float64)))
    idx = np.arange(chunk, dtype=np.float64)
    rel = idx[:, None] - idx[None, :]
    mask = rel >= 0
    dmat = np.where(mask[None], np.exp(np.where(mask, rel, 0.0)[None] * lg[:, None, None]), 0.0)
    qdec = np.exp((idx + 1.0)[None, :] * lg[:, None])
    kdec = np.exp((chunk - 1.0 - idx)[None, :] * lg[:, None])
    sdec = np.exp(chunk * lg)
    qdec_b = np.broadcast_to(qdec[:, :, None], (C_HEADS, chunk, C_DK))
    kdec_b = np.broadcast_to(kdec[:, :, None], (C_HEADS, chunk, C_DK))
    return (dmat.astype(np.float32), np.ascontiguousarray(qdec_b).astype(np.float32),
            np.ascontiguousarray(kdec_b).astype(np.float32), [float(v) for v in sdec])


def _rope_tables(positions):
    half = C_DK // 2
    inv = ROPE_BASE ** (-np.arange(half, dtype=np.float64) / half)
    ang = np.asarray(positions, dtype=np.float64)[:, None] * inv[None]
    cos = np.concatenate([np.cos(ang), np.cos(ang)], axis=-1)
    sin = np.concatenate([-np.sin(ang), np.sin(ang)], axis=-1)
    return cos.astype(np.float32), sin.astype(np.float32)


def _rope(x, cos, sin_signed):
    return x * cos + pltpu.roll(x, C_DK // 2, 1) * sin_signed


PROJ_SPLIT = 1536


def _mix_prompt_kernel(x0_ref, xn_ref, w_ref, cos_ref, sin_ref, cw_ref, cb_ref, wri_ref, bri_ref,
                       lam_ref, lbl_ref, ng_ref, dmat_ref, qdec_ref, kdec_ref,
                       y_ref, h_ref, conv_ref, hg_ref, rt_ref,
                       prev_sc, hprev_sc, st_sc, sret_sc, kb_sc, bb_sc, vb_sc, oi_sc, p_ref, next_ref,
                       *, layer, tt, sdec):
    t = pl.program_id(1)
    nt = pl.num_programs(1)
    flat = pl.program_id(0) * nt + t

    def project(x_ref, dst, c0, c1):
        xb = x_ref[...].astype(BF16)
        for j in range(c0, c1, 256):
            dst[:, j:j + 256] = jnp.dot(xb, w_ref[:, j:j + 256], preferred_element_type=F32)

    @pl.when(flat == 0)
    def _():
        project(x0_ref, p_ref, 0, D_IN)

    @pl.when(t == 0)
    def _():
        prev_sc[...] = jnp.zeros_like(prev_sc)
        hprev_sc[...] = jnp.zeros_like(hprev_sc)
        st_sc[...] = jnp.zeros_like(st_sc)
        sret_sc[...] = jnp.zeros_like(sret_sc)

    xa = p_ref[:, O_XA:O_XA + D_A]
    ga = p_ref[:, O_GA:O_GA + D_A]
    row = lax.broadcasted_iota(jnp.int32, (tt, D_A), 0)
    row8 = lax.broadcasted_iota(jnp.int32, (SUBLANES, D_A), 0)
    prev = prev_sc[...]

    def shifted(j):
        r = pltpu.roll(xa, j, 0)
        top = jnp.where(row8 < j, pltpu.roll(prev, j, 0), r[0:SUBLANES])
        return jnp.concatenate([top, r[SUBLANES:]], axis=0)

    u = cb_ref[...] + shifted(3) * cw_ref[0:1, :]
    u = u + shifted(2) * cw_ref[1:2, :]
    u = u + shifted(1) * cw_ref[2:3, :]
    u = u + xa * cw_ref[3:4, :]
    last8 = xa[tt - SUBLANES:tt]
    prev_sc[...] = last8

    a, bterm = _rglru_gates(u, wri_ref[...], bri_ref[...], lam_ref[...])
    project(xn_ref, next_ref, 0, PROJ_SPLIT)
    s = 1
    while s < SUBLANES:
        keep = (row % SUBLANES) >= s
        a_sh = jnp.where(keep, pltpu.roll(a, s, 0), 1.0)
        b_sh = jnp.where(keep, pltpu.roll(bterm, s, 0), 0.0)
        bterm = a * b_sh + bterm
        a = a * a_sh
        s *= 2
    carry = hprev_sc[...]
    groups = []
    for gi in range(tt // SUBLANES):
        rows = slice(gi * SUBLANES, (gi + 1) * SUBLANES)
        h_in = jnp.broadcast_to(carry[SUBLANES - 1:SUBLANES, :], (SUBLANES, D_A))
        carry = a[rows] * h_in + bterm[rows]
        groups.append(carry)
    h = jnp.concatenate(groups, axis=0)
    hlast8 = carry
    hprev_sc[...] = hlast8
    y_ref[:, 0:D_A] = h * _gelu_tanh(ga)

    lb = _hgrn_lower_bound(lbl_ref[...], layer)
    ng = ng_ref[...]
    cl = B_CHUNK
    crow = lax.broadcasted_iota(jnp.int32, (cl, D_B), 0)
    ccol = lax.broadcasted_iota(jnp.int32, (cl, D_B), 1)
    causal = (ccol % B_DK) <= crow
    br = lax.broadcasted_iota(jnp.int32, (D_B, D_B), 0)
    bc = lax.broadcasted_iota(jnp.int32, (D_B, D_B), 1)
    head_mask = (br // B_DK) == (bc // B_DK)
    seg = jnp.where(head_mask, 1.0, 0.0).astype(BF16)

    def seg_mean(x):
        hi = x.astype(BF16)
        lo = (x - hi.astype(F32)).astype(BF16)
        tot = (jnp.dot(hi, seg, preferred_element_type=F32)
               + jnp.dot(lo, seg, preferred_element_type=F32))
        return tot * (1.0 / B_DK)

    nck = tt // cl
    q = _silu(p_ref[:, O_QB:O_QB + D_B])
    fl = p_ref[:, O_FB:O_FB + D_B]
    v = p_ref[:, O_VB:O_VB + D_B]
    half_th = 0.5 * jnp.tanh(0.5 * fl)
    f = lb + (1.0 - lb) * (0.5 + half_th)
    b = jnp.log(jnp.maximum(f, F_TINY))
    k = (1.0 - lb) * (0.5 - half_th)
    trow = lax.broadcasted_iota(jnp.int32, (tt, D_B), 0) % cl
    sh = 1
    while sh < cl:
        b = b + jnp.where(trow >= sh, pltpu.roll(b, sh, 0), 0.0)
        sh *= 2

    def chunk_row(x, c, r):
        top = c * cl + r + 1
        return x[top - SUBLANES:top][SUBLANES - 1:SUBLANES]

    def spread(rows):
        return jnp.concatenate([jnp.broadcast_to(r, (cl, D_B)) for r in rows], axis=0)

    last_rows = [chunk_row(b, c, cl - 1) for c in range(nck)]
    mid_rows = [chunk_row(b, c, cl // 2 - 1) for c in range(nck)]
    b_last = spread(last_rows)
    b_mid = spread(mid_rows)
    qd = q * jnp.exp(b)
    kl = k * jnp.exp(b_last - b)
    lasts = jnp.concatenate(last_rows, axis=0)
    mids = jnp.concatenate(mid_rows, axis=0)
    safe = jnp.min(jnp.minimum(mids, lasts - mids)) >= HGRN_SAFE_MIN_LOGDECAY

    def next_state(st, c, sl):
        kv = jnp.where(head_mask, _dot_g(v[sl], kl[sl], _TN), 0.0)
        return st * jnp.exp(last_rows[c]) + kv

    @pl.when(safe)
    def _():
        qmid = q * jnp.exp(b - b_mid)
        kinv = k * jnp.exp(b_mid - b)
        st = st_sc[...]
        for c in range(nck):
            sl = slice(c * cl, (c + 1) * cl)
            k4 = jnp.where(head_mask, jnp.concatenate([kinv[sl]] * B_HEADS, axis=0), 0.0)
            sc = jnp.where(causal, _dot_g(qmid[sl], k4, _NT), 0.0)
            v4 = jnp.where(head_mask, jnp.concatenate([v[sl]] * B_HEADS, axis=0), 0.0)
            oi_sc[sl, :] = _dot(sc, v4) + _dot_g(qd[sl], st, _NT)
            st = next_state(st, c, sl)
        st_sc[...] = st

    @pl.when(jnp.logical_not(safe))
    def _():
        kb_sc[...] = k
        bb_sc[...] = b
        vb_sc[...] = v
        st = st_sc[...]
        for c in range(nck):
            sl = slice(c * cl, (c + 1) * cl)
            qc, bc_ = q[sl], b[sl]

            def pair(sidx, acc, c=c, qc=qc, bc_=bc_):
                src = pl.ds(c * cl + sidx, 1)
                e = jnp.exp(jnp.minimum(bc_ - bb_sc[src, :], 0.0)) * (qc * kb_sc[src, :])
                scr = jnp.dot(e.astype(BF16), seg, preferred_element_type=F32)
                return acc + jnp.where(crow >= sidx, scr, 0.0) * vb_sc[src, :]

            oi = lax.fori_loop(0, cl, pair, jnp.zeros((cl, D_B), F32))
            oi_sc[sl, :] = oi + _dot_g(qd[sl], st, _NT)
            st = next_state(st, c, sl)
        st_sc[...] = st

    o = oi_sc[...]
    o = o * lax.rsqrt(seg_mean(o * o) + RMS_EPS) * ng
    y_ref[:, D_A:D_A + D_B] = o * _silu(p_ref[:, O_GB:O_GB + D_B])

    rl = C_CHUNK
    per_head = (D_IN - PROJ_SPLIT) // C_HEADS
    cos = cos_ref[...]
    sin = sin_ref[...]
    for hh in range(C_HEADS):
        lo = hh * C_DK
        q = _rope(p_ref[:, O_QC + lo:O_QC + lo + C_DK], cos, sin)
        k = _rope(p_ref[:, O_KC + lo:O_KC + lo + C_DK], cos, sin) * (C_DK ** -0.5)
        v = p_ref[:, O_VC + lo:O_VC + lo + C_DK]
        s = sret_sc[hh]
        parts = []
        for c in range(tt // rl):
            sl = slice(c * rl, (c + 1) * rl)
            sc = _dot_g(q[sl], k[sl], _NT) * dmat_ref[hh]
            parts.append(_dot(sc, v[sl]) + _dot(q[sl] * qdec_ref[hh], s))
            s = sdec[hh] * s + _dot_g(k[sl] * kdec_ref[hh], v[sl], _TN)
        sret_sc[hh] = s
        project(xn_ref, next_ref, PROJ_SPLIT + hh * per_head, PROJ_SPLIT + (hh + 1) * per_head)
        o = jnp.concatenate(parts, axis=0)
        mu = jnp.mean(o, axis=-1, keepdims=True)
        oc = o - mu
        var = jnp.mean(oc * oc, axis=-1, keepdims=True)
        g = p_ref[:, O_GC + lo:O_GC + lo + C_DK]
        y_ref[:, D_A + D_B + lo:D_A + D_B + lo + C_DK] = oc * lax.rsqrt(var + GN_EPS) * _silu(g)

    p_ref[...] = next_ref[...]

    @pl.when(t == nt - 1)
    def _():
        h_ref[...] = pltpu.roll(hprev_sc[...], 1, 0)[0:1]
        conv_ref[...] = pltpu.roll(prev_sc[...], CONV_W - 1, 0)[0:CONV_W - 1]
        s_bd = st_sc[...].T
        for hh in range(B_HEADS):
            hg_ref[hh] = s_bd[hh * B_DK:(hh + 1) * B_DK, hh * B_DK:(hh + 1) * B_DK]
        rt_ref[...] = sret_sc[...]


def _mix_prompt(x, w_in_bf, batch, seq, layer, tt, cw, cb, wri, bri, lam, lbl, ng):
    nt = seq // tt
    last = batch * nt - 1
    dmat, qdec, kdec, sdec = _retention_consts(C_CHUNK)
    cos, sin = _rope_tables(np.arange(seq))
    full = lambda shape: pl.BlockSpec(shape, lambda b, t: (0,) * len(shape))
    lay = lambda shape: pl.BlockSpec((None,) + shape, lambda b, t: (layer,) + (0,) * len(shape))
    kern = functools.partial(_mix_prompt_kernel, layer=layer, tt=tt, sdec=sdec)
    return pl.pallas_call(
        kern,
        grid=(batch, nt),
        in_specs=[
            pl.BlockSpec((tt, D_MODEL), lambda b, t: (b * nt + t, 0)),
            pl.BlockSpec((tt, D_MODEL), lambda b, t: (jnp.minimum(b * nt + t + 1, last), 0)),
            lay((D_MODEL, D_IN)),
            pl.BlockSpec((tt, C_DK), lambda b, t: (t, 0)),
            pl.BlockSpec((tt, C_DK), lambda b, t: (t, 0)),
            lay((CONV_W, D_A)), lay((1, D_A)), lay((D_A, 2 * D_A)), lay((1, 2 * D_A)),
            lay((1, D_A)), full((DEPTH, D_B)), lay((1, D_B)),
            full((C_HEADS, C_CHUNK, C_CHUNK)), full((C_HEADS, C_CHUNK, C_DK)),
            full((C_HEADS, C_CHUNK, C_DK)),
        ],
        out_specs=[
            pl.BlockSpec((tt, D_MODEL), lambda b, t: (b * nt + t, 0)),
            pl.BlockSpec((None, 1, D_A), lambda b, t: (b, 0, 0)),
            pl.BlockSpec((None, CONV_W - 1, D_A), lambda b, t: (b, 0, 0)),
            pl.BlockSpec((None, B_HEADS, B_DK, B_DK), lambda b, t: (b, 0, 0, 0)),
            pl.BlockSpec((None, C_HEADS, C_DK, C_DK), lambda b, t: (b, 0, 0, 0)),
        ],
        out_shape=[
            jax.ShapeDtypeStruct((batch * seq, D_MODEL), F32),
            jax.ShapeDtypeStruct((batch, 1, D_A), F32),
            jax.ShapeDtypeStruct((batch, CONV_W - 1, D_A), F32),
            jax.ShapeDtypeStruct((batch, B_HEADS, B_DK, B_DK), F32),
            jax.ShapeDtypeStruct((batch, C_HEADS, C_DK, C_DK), F32),
        ],
        scratch_shapes=[
            pltpu.VMEM((SUBLANES, D_A), F32), pltpu.VMEM((SUBLANES, D_A), F32),
            pltpu.VMEM((D_B, D_B), F32), pltpu.VMEM((C_HEADS, C_DK, C_DK), F32),
            pltpu.VMEM((tt, D_B), F32), pltpu.VMEM((tt, D_B), F32),
            pltpu.VMEM((tt, D_B), F32), pltpu.VMEM((tt, D_B), F32),
            pltpu.VMEM((tt, D_IN), F32), pltpu.VMEM((tt, D_IN), F32),
        ],
        compiler_params=_cparams(2),
        name="mix_prompt",
    )(x, x, w_in_bf, cos, sin, cw, cb, wri, bri, lam, lbl, ng, dmat, qdec, kdec)


def _column_matrix(x):
    pad = jnp.zeros((LANES - SUBLANES, LANES), F32)
    return jnp.concatenate([x, pad], axis=0).T


def _mix_sample_kernel(*refs, layer, gammas, n_prev):
    (p_ref, h0_ref, conv0_ref, hg0_ref, rt0_ref, cos_ref, sin_ref,
     cw_ref, cb_ref, wri_ref, bri_ref, lam_ref, lbl_ref, ng_ref) = refs[:14]
    prev = refs[14:14 + 4 * n_prev]
    y_ref, h_ref, conv_ref, hg_ref, rt_ref, o_sc = refs[14 + 4 * n_prev:]
    if n_prev:
        for j in range(n_prev):
            for dst, src in zip((h_ref, conv_ref, hg_ref, rt_ref), prev[4 * j:4 * j + 4]):
                dst[j] = src[...]
        h_ref, conv_ref, hg_ref, rt_ref = (r.at[layer] for r in (h_ref, conv_ref, hg_ref, rt_ref))
    tb = SUBLANES
    xa = p_ref[:, O_XA:O_XA + D_A]
    ga = p_ref[:, O_GA:O_GA + D_A]
    c0, c1, c2 = conv0_ref[0], conv0_ref[1], conv0_ref[2]
    u = cb_ref[...] + c0 * cw_ref[0:1, :]
    u = u + c1 * cw_ref[1:2, :]
    u = u + c2 * cw_ref[2:3, :]
    u = u + xa * cw_ref[3:4, :]
    conv_ref[0] = c1
    conv_ref[1] = c2
    conv_ref[2] = xa
    a, bterm = _rglru_gates(u, wri_ref[...], bri_ref[...], lam_ref[...])
    h = a * h0_ref[...] + bterm
    h_ref[...] = h
    y_ref[:, 0:D_A] = h * _gelu_tanh(ga)

    lb = _hgrn_lower_bound(lbl_ref[...], layer)
    for j in range(B_HEADS // 2):
        lo = j * LANES
        q = _silu(p_ref[:, O_QB + lo:O_QB + lo + LANES])
        fl = p_ref[:, O_FB + lo:O_FB + lo + LANES]
        v = p_ref[:, O_VB + lo:O_VB + lo + LANES]
        lbj = lb[:, lo:lo + LANES]
        f = lbj + (1.0 - lbj) * _sigmoid(fl)
        ef = jnp.exp(jnp.log(jnp.maximum(f, F_TINY)))
        k = (1.0 - lbj) * _sigmoid(-fl)
        qk = q * k
        qf_cols = _column_matrix(q * ef)
        k_cols = _column_matrix(k)
        f_cols = _column_matrix(ef)
        for hh in range(2):
            head = 2 * j + hh
            sl = slice(hh * B_DK, (hh + 1) * B_DK)
            dots = jnp.sum(qk[:, sl], axis=-1, keepdims=True)
            for b in range(tb):
                s_old = hg0_ref[b, head]
                v_row = v[b:b + 1, sl]
                hg_ref[b, head] = f_cols[sl, b:b + 1] * s_old + k_cols[sl, b:b + 1] * v_row
                o_row = (jnp.sum(qf_cols[sl, b:b + 1] * s_old, axis=0, keepdims=True)
                         + dots[b:b + 1, :] * v_row)
                o_sc[b:b + 1, head * B_DK:(head + 1) * B_DK] = o_row
    ng = ng_ref[...]
    for head in range(B_HEADS):
        sl = slice(head * B_DK, (head + 1) * B_DK)
        o = o_sc[:, sl]
        o = o * lax.rsqrt(jnp.mean(o * o, axis=-1, keepdims=True) + RMS_EPS) * ng[:, sl]
        g = p_ref[:, O_GB + head * B_DK:O_GB + (head + 1) * B_DK]
        y_ref[:, D_A + head * B_DK:D_A + (head + 1) * B_DK] = o * _silu(g)

    cos = cos_ref[...]
    sin = sin_ref[...]
    for head in range(C_HEADS):
        lo = head * C_DK
        gamma = gammas[head]
        q = _rope(p_ref[:, O_QC + lo:O_QC + lo + C_DK], cos, sin)
        k = _rope(p_ref[:, O_KC + lo:O_KC + lo + C_DK], cos, sin) * (C_DK ** -0.5)
        v = p_ref[:, O_VC + lo:O_VC + lo + C_DK]
        g = p_ref[:, O_GC + lo:O_GC + lo + C_DK]
        dots = jnp.sum(q * k, axis=-1, keepdims=True)
        q_cols = _column_matrix(q * gamma)
        k_cols = _column_matrix(k)
        for b in range(tb):
            s_old = rt0_ref[b, head]
            v_row = v[b:b + 1, :]
            rt_ref[b, head] = gamma * s_old + k_cols[:, b:b + 1] * v_row
            o_row = (jnp.sum(q_cols[:, b:b + 1] * s_old, axis=0, keepdims=True)
                     + dots[b:b + 1, :] * v_row)
            o_sc[b:b + 1, 0:C_DK] = o_row
        o = o_sc[:, 0:C_DK]
        mu = jnp.mean(o, axis=-1, keepdims=True)
        oc = o - mu
        var = jnp.mean(oc * oc, axis=-1, keepdims=True)
        y_ref[:, D_A + D_B + lo:D_A + D_B + lo + C_DK] = oc * lax.rsqrt(var + GN_EPS) * _silu(g)


def _mix_sample(proj, h0, conv0_t, hg0, rt0, layer, cw, cb, wri, bri, lam, lbl, ng, prev=()):
    nb = proj.shape[0]
    tb = SUBLANES
    lg = np.log1p(-np.exp2(-5.0 - np.arange(C_HEADS, dtype=np.float64)))
    gammas = [float(np.exp(v)) for v in lg]
    cos, sin = _rope_tables([PAST_LEN])
    full = lambda shape: pl.BlockSpec(shape, lambda i: (0,) * len(shape))
    lay = lambda shape: pl.BlockSpec((None,) + shape, lambda i: (layer,) + (0,) * len(shape))
    state_shapes = [(tb, D_A), (CONV_W - 1, tb, D_A), (tb, B_HEADS, B_DK, B_DK), (tb, C_HEADS, C_DK, C_DK)]
    state_maps = [lambda i: (i, 0), lambda i: (0, i, 0), lambda i: (i, 0, 0, 0), lambda i: (i, 0, 0, 0)]
    full_shapes = [(nb, D_A), (CONV_W - 1, nb, D_A), (nb, B_HEADS, B_DK, B_DK), (nb, C_HEADS, C_DK, C_DK)]
    state_specs = [pl.BlockSpec(s, m) for s, m in zip(state_shapes, state_maps)]
    if prev:
        stack = len(prev) + 1
        out_state_specs = [pl.BlockSpec((stack,) + s, lambda i, m=m: (0,) + m(i))
                           for s, m in zip(state_shapes, state_maps)]
        out_state_shapes = [jax.ShapeDtypeStruct((stack,) + s, F32) for s in full_shapes]
    else:
        out_state_specs = state_specs
        out_state_shapes = [jax.ShapeDtypeStruct(s, F32) for s in full_shapes]
    kern = functools.partial(_mix_sample_kernel, layer=layer, gammas=gammas, n_prev=len(prev))
    return pl.pallas_call(
        kern,
        grid=(nb // tb,),
        in_specs=[
            pl.BlockSpec((tb, D_IN), lambda i: (i, 0)),
            pl.BlockSpec((None, tb, D_A), lambda i: (layer, i, 0)),
            pl.BlockSpec((None, CONV_W - 1, tb, D_A), lambda i: (layer, 0, i, 0)),
            pl.BlockSpec((None, tb, B_HEADS, B_DK, B_DK), lambda i: (layer, i, 0, 0, 0)),
            pl.BlockSpec((None, tb, C_HEADS, C_DK, C_DK), lambda i: (layer, i, 0, 0, 0)),
            full((1, C_DK)), full((1, C_DK)),
            lay((CONV_W, D_A)), lay((1, D_A)), lay((D_A, 2 * D_A)), lay((1, 2 * D_A)),
            lay((1, D_A)), full((DEPTH, D_B)), lay((1, D_B)),
        ] + state_specs * len(prev),
        out_specs=[pl.BlockSpec((tb, D_MODEL), lambda i: (i, 0))] + out_state_specs,
        out_shape=[jax.ShapeDtypeStruct((nb, D_MODEL), F32)] + out_state_shapes,
        scratch_shapes=[pltpu.VMEM((tb, D_B), F32)],
        compiler_params=_cparams(1),
        name="mix_sample",
    )(proj, h0, conv0_t, hg0, rt0, cos, sin, cw, cb, wri, bri, lam, lbl, ng,
      *[a for states in prev for a in states])


def _route_rows(l):
    m = l[0]
    for x in l[1:]:
        m = jnp.maximum(m, x)
    ex = [jnp.exp(x - m) for x in l]
    tot = ex[0]
    for x in ex[1:]:
        tot = tot + x
    p = [x / tot for x in ex]
    scores = []
    for gi in range(N_GROUPS):
        a, b, c, d = p[4 * gi:4 * gi + 4]
        hi1, lo1 = jnp.maximum(a, b), jnp.minimum(a, b)
        hi2, lo2 = jnp.maximum(c, d), jnp.minimum(c, d)
        top1 = jnp.maximum(hi1, hi2)
        top2 = jnp.maximum(jnp.minimum(hi1, hi2), jnp.maximum(lo1, lo2))
        scores.append(top1 + top2)
    best = scores[0]
    gsel = jnp.zeros_like(best, dtype=jnp.int32)
    for gi in range(1, N_GROUPS):
        upd = scores[gi] > best
        gsel = jnp.where(upd, gi, gsel)
        best = jnp.where(upd, scores[gi], best)
    vals = []
    for j in range(EXP_PER_GROUP):
        v = p[j]
        for gi in range(1, N_GROUPS):
            v = jnp.where(gsel == gi, p[4 * gi + j], v)
        vals.append(v)
    sel = []
    for j in range(EXP_PER_GROUP):
        rank = jnp.zeros_like(gsel)
        for i in range(EXP_PER_GROUP):
            if i == j:
                continue
            ahead = (vals[i] > vals[j]) | ((vals[i] == vals[j]) & (i < j))
            rank = rank + jnp.where(ahead, 1, 0)
        sel.append(rank < 2)
    denom = jnp.zeros_like(best)
    for j in range(EXP_PER_GROUP):
        denom = denom + jnp.where(sel[j], vals[j], 0.0)
    gates = [jnp.where(sel[j], vals[j] / denom, 0.0) for j in range(EXP_PER_GROUP)]
    comb = [jnp.where(gsel == (e // EXP_PER_GROUP), gates[e % EXP_PER_GROUP], 0.0)
            for e in range(N_EXPERTS)]
    j1 = jnp.where(sel[0], 0, jnp.where(sel[1], 1, 2))
    j2 = jnp.where(sel[3], 3, jnp.where(sel[2], 2, 1))
    g1 = jnp.zeros_like(best)
    g2 = jnp.zeros_like(best)
    for j in range(EXP_PER_GROUP):
        g1 = jnp.where(j1 == j, gates[j], g1)
        g2 = jnp.where(j2 == j, gates[j], g2)
    return comb, gsel * EXP_PER_GROUP + j1, gsel * EXP_PER_GROUP + j2, g1, g2


MOE_TILE = 512
GRANULE = 16
GRANULES_PER_TILE = MOE_TILE // GRANULE
ROUTE_SLOT1, ROUTE_SLOT2 = N_EXPERTS, N_EXPERTS + 1
PLAN_SLOT1, PLAN_SLOT2, PLAN_G1, PLAN_G2 = 0, 1, 2, 3


def _local_slots(tt):
    worst = 2 * tt + (N_EXPERTS - 1) * GRANULE
    return -(-worst // LANES) * LANES


def _outproj_kernel(*refs, plan):
    if plan:
        (y_ref, x_ref, w_ref, g_ref, b_ref, rwt_ref, rb_ref, tri_ref, ltri_ref,
         x1_ref, route_ref, plan_ref, gran_ref, loff_ref, rt_sc, oh_sc) = refs
    else:
        y_ref, x_ref, w_ref, g_ref, b_ref, rwt_ref, rb_ref, x1_ref, route_ref, rt_sc = refs
    if w_ref.dtype == BF16:
        y = jnp.dot(y_ref[...].astype(BF16), w_ref[...], preferred_element_type=F32)
    else:
        y = _dot3(y_ref[...], w_ref[...])
    x1 = _layernorm(ALPHA * x_ref[...] + y, g_ref[...], b_ref[...])
    x1_ref[...] = x1
    hi, lo = _split(x1)
    rhi, rlo = _split(rwt_ref[...])
    lg = (lax.dot_general(rhi, hi, _NT, preferred_element_type=F32)
          + lax.dot_general(rhi, lo, _NT, preferred_element_type=F32)
          + lax.dot_general(rlo, hi, _NT, preferred_element_type=F32)) + rb_ref[...]
    comb, e1, e2, g1, g2 = _route_rows([lg[e:e + 1, :] for e in range(N_EXPERTS)])
    rt_sc[...] = jnp.zeros_like(rt_sc)
    if not plan:
        for e in range(N_EXPERTS):
            rt_sc[e:e + 1, :] = comb[e]
    else:
        for e in range(N_EXPERTS):
            oh_sc[e:e + 1, :] = jnp.where((e1 == e) | (e2 == e), 1.0, 0.0)
        oh = oh_sc[...]
        cum = jnp.dot(oh.astype(BF16), tri_ref[...], preferred_element_type=F32)
        count = jnp.sum(oh, axis=1, keepdims=True)
        gran = jnp.floor((count + (GRANULE - 1)) * (1.0 / GRANULE))
        gran_b = jnp.broadcast_to(gran, (N_EXPERTS, LANES))
        loff_b = jnp.dot(ltri_ref[...], gran_b.astype(BF16), preferred_element_type=F32)
        slot_base = loff_b[:, 0:1] * GRANULE - 1.0 + cum
        s1 = jnp.zeros_like(g1)
        s2 = jnp.zeros_like(g1)
        for e in range(N_EXPERTS):
            row = slot_base[e:e + 1, :]
            s1 = jnp.where(e1 == e, row, s1)
            s2 = jnp.where(e2 == e, row, s2)
        plan_ref[...] = jnp.zeros_like(plan_ref)
        plan_ref[PLAN_SLOT1:PLAN_SLOT1 + 1, :] = s1
        plan_ref[PLAN_SLOT2:PLAN_SLOT2 + 1, :] = s2
        plan_ref[PLAN_G1:PLAN_G1 + 1, :] = g1
        plan_ref[PLAN_G2:PLAN_G2 + 1, :] = g2
        gran_ref[...] = gran_b.astype(jnp.int32)
        loff_ref[...] = loff_b.astype(jnp.int32)
        rt_sc[ROUTE_SLOT1:ROUTE_SLOT1 + 1, :] = s1
        rt_sc[ROUTE_SLOT2:ROUTE_SLOT2 + 1, :] = s2
    route_ref[...] = rt_sc[...].T


def _outproj(y, x, w_out_bf, layer, tm, g, b, rwt, rb, plan=False):
    n = x.shape[0]
    full = lambda shape: pl.BlockSpec(shape, lambda i: (0,) * len(shape))
    lay = lambda shape: pl.BlockSpec((None,) + shape, lambda i: (layer,) + (0,) * len(shape))
    in_specs = [pl.BlockSpec((tm, D_MODEL), lambda i: (i, 0)),
                pl.BlockSpec((tm, D_MODEL), lambda i: (i, 0)),
                lay((D_MODEL, D_MODEL)), lay((1, D_MODEL)), lay((1, D_MODEL)),
                full((N_EXPERTS, D_MODEL)), full((N_EXPERTS, 1))]
    out_specs = [pl.BlockSpec((tm, D_MODEL), lambda i: (i, 0)),
                 pl.BlockSpec((tm, LANES), lambda i: (i, 0))]
    out_shape = [jax.ShapeDtypeStruct((n, D_MODEL), F32),
                 jax.ShapeDtypeStruct((n, LANES), F32)]
    scratch = [pltpu.VMEM((LANES, tm), F32)]
    args = [y, x, w_out_bf, g, b, rwt, rb]
    if plan:
        tri = np.triu(np.ones((tm, tm), np.float32)).astype(jnp.bfloat16)
        ltri = np.tril(np.ones((N_EXPERTS, N_EXPERTS), np.float32), -1).astype(jnp.bfloat16)
        in_specs += [full((tm, tm)), full((N_EXPERTS, N_EXPERTS))]
        args += [tri, ltri]
        per_tile = pl.BlockSpec((None, N_EXPERTS, LANES), lambda i: (i, 0, 0))
        out_specs += [pl.BlockSpec((None, SUBLANES, tm), lambda i: (i, 0, 0)), per_tile, per_tile]
        out_shape += [jax.ShapeDtypeStruct((n // tm, SUBLANES, tm), F32),
                      jax.ShapeDtypeStruct((n // tm, N_EXPERTS, LANES), jnp.int32),
                      jax.ShapeDtypeStruct((n // tm, N_EXPERTS, LANES), jnp.int32)]
        scratch += [pltpu.VMEM((N_EXPERTS, tm), F32)]
    return pl.pallas_call(
        functools.partial(_outproj_kernel, plan=plan),
        grid=(n // tm,),
        in_specs=in_specs,
        out_specs=out_specs,
        out_shape=out_shape,
        scratch_shapes=scratch,
        compiler_params=_cparams(1),
        name="outproj_plan" if plan else "outproj",
    )(*args)


def _granule_copies(src, dst, src_g, dst_g, n, sem, start):
    size = GRANULES_PER_TILE
    while size >= 1:
        bit = n & size

        @pl.when(bit != 0)
        def _(size=size, src_g=src_g, dst_g=dst_g):
            s0 = pl.multiple_of(src_g * GRANULE, GRANULE)
            d0 = pl.multiple_of(dst_g * GRANULE, GRANULE)
            cp = pltpu.make_async_copy(src.at[pl.ds(s0, size * GRANULE)],
                                       dst.at[pl.ds(d0, size * GRANULE)], sem)
            cp.start() if start else cp.wait()

        src_g = src_g + bit
        dst_g = dst_g + bit
        size //= 2


def _dispatch_kernel(segg_ref, totg_ref, ctl_ref, plan_ref, x_ref, xs_ref, gs_ref,
                     xl_sc, gl_sc, zx_sc, zg_sc, sem, *, tt, total_tiles):
    @pl.when(pl.program_id(0) == 0)
    def _():
        zx_sc[...] = jnp.zeros_like(zx_sc)
        zg_sc[...] = jnp.zeros_like(zg_sc)

        def zero_tile(i):
            r0 = pl.multiple_of(i * MOE_TILE, MOE_TILE)
            return (pltpu.make_async_copy(zx_sc, xs_ref.at[pl.ds(r0, MOE_TILE)], sem.at[2]),
                    pltpu.make_async_copy(zg_sc, gs_ref.at[pl.ds(r0, MOE_TILE)], sem.at[2]))

        def run(i, start):
            for cp in zero_tile(i):
                cp.start() if start else cp.wait()

        for start in (True, False):
            for e in range(N_EXPERTS):
                end = segg_ref[e] + totg_ref[e]

                @pl.when(end % GRANULES_PER_TILE != 0)
                def _():
                    run(end // GRANULES_PER_TILE, start)

        last = N_EXPERTS - 1
        used = (segg_ref[last] + totg_ref[last] + GRANULES_PER_TILE - 1) // GRANULES_PER_TILE
        lax.fori_loop(used, total_tiles, lambda i, c: (run(i, True), c)[1], 0)
        lax.fori_loop(used, total_tiles, lambda i, c: (run(i, False), c)[1], 0)

    n_slots = xl_sc.shape[0]
    slot = lax.broadcasted_iota(jnp.int32, (n_slots, tt), 0).astype(F32)
    m1 = slot == plan_ref[PLAN_SLOT1:PLAN_SLOT1 + 1, :]
    m2 = slot == plan_ref[PLAN_SLOT2:PLAN_SLOT2 + 1, :]
    perm = jnp.where(m1 | m2, 1.0, 0.0).astype(BF16)
    xl_sc[...] = jnp.dot(perm, x_ref[...].astype(BF16), preferred_element_type=F32).astype(BF16)
    gate = (jnp.where(m1, plan_ref[PLAN_G1:PLAN_G1 + 1, :], 0.0)
            + jnp.where(m2, plan_ref[PLAN_G2:PLAN_G2 + 1, :], 0.0))
    gl_sc[...] = jnp.broadcast_to(jnp.sum(gate, axis=1, keepdims=True), gl_sc.shape)

    for start in (True, False):
        for e in range(N_EXPERTS):
            n, src_g, dst_g = ctl_ref[0, e], ctl_ref[1, e], ctl_ref[2, e]
            _granule_copies(xl_sc, xs_ref, src_g, dst_g, n, sem.at[0], start)
            _granule_copies(gl_sc, gs_ref, src_g, dst_g, n, sem.at[1], start)


def _dispatch(x1, plan, ctl, seg_g, tot_g, tt, rows):
    n = x1.shape[0]
    n_slots = _local_slots(tt)
    return pl.pallas_call(
        functools.partial(_dispatch_kernel, tt=tt, total_tiles=rows // MOE_TILE),
        grid_spec=pltpu.PrefetchScalarGridSpec(
            num_scalar_prefetch=2,
            grid=(n // tt,),
            in_specs=[pl.BlockSpec((None, SUBLANES, LANES), lambda i, s, c: (i, 0, 0),
                                   memory_space=pltpu.SMEM),
                      pl.BlockSpec((None, SUBLANES, tt), lambda i, s, c: (i, 0, 0)),
                      pl.BlockSpec((tt, D_MODEL), lambda i, s, c: (i, 0))],
            out_specs=[pl.BlockSpec(memory_space=pl.ANY), pl.BlockSpec(memory_space=pl.ANY)],
            scratch_shapes=[pltpu.VMEM((n_slots, D_MODEL), BF16), pltpu.VMEM((n_slots, LANES), F32),
                            pltpu.VMEM((MOE_TILE, D_MODEL), BF16), pltpu.VMEM((MOE_TILE, LANES), F32),
                            pltpu.SemaphoreType.DMA((3,))]),
        out_shape=[jax.ShapeDtypeStruct((rows, D_MODEL), BF16),
                   jax.ShapeDtypeStruct((rows, LANES), F32)],
        compiler_params=_cparams(1),
        name="moe_dispatch",
    )(seg_g, tot_g, ctl, plan, x1)


def _ffn_kernel(te_ref, nv_ref, xs_ref, gs_ref, wg_ref, wu_ref, wd_ref, ys_ref):
    used = pl.program_id(0) < nv_ref[0]

    @pl.when(used)
    def _():
        xb = xs_ref[...]
        hg = jnp.dot(xb, wg_ref[...].astype(BF16), preferred_element_type=F32)
        hu = jnp.dot(xb, wu_ref[...].astype(BF16), preferred_element_type=F32)
        h = (_silu(hg) * hu * gs_ref[:, 0:1]).astype(BF16)
        ys_ref[...] = jnp.dot(h, wd_ref[...].astype(BF16), preferred_element_type=F32).astype(BF16)

    @pl.when(jnp.logical_not(used))
    def _():
        ys_ref[...] = jnp.zeros_like(ys_ref)


def _ffn(xs, gs, tile_expert, n_valid, wg, wu, wd, layer):
    total_tiles = xs.shape[0] // MOE_TILE
    w_in_spec = pl.BlockSpec((None, None, D_MODEL, D_EXPERT), lambda i, te, nv: (layer, te[i], 0, 0))
    return pl.pallas_call(
        _ffn_kernel,
        grid_spec=pltpu.PrefetchScalarGridSpec(
            num_scalar_prefetch=2,
            grid=(total_tiles,),
            in_specs=[pl.BlockSpec((MOE_TILE, D_MODEL), lambda i, te, nv: (i, 0)),
                      pl.BlockSpec((MOE_TILE, LANES), lambda i, te, nv: (i, 0)),
                      w_in_spec, w_in_spec,
                      pl.BlockSpec((None, None, D_EXPERT, D_MODEL),
                                   lambda i, te, nv: (layer, te[i], 0, 0))],
            out_specs=pl.BlockSpec((MOE_TILE, D_MODEL), lambda i, te, nv: (i, 0))),
        out_shape=jax.ShapeDtypeStruct(xs.shape, BF16),
        compiler_params=_cparams(1),
        name="moe_ffn",
    )(tile_expert, n_valid, xs, gs, wg, wu, wd)


def _combine_kernel(ctl_ref, ctl_next_ref, x_ref, route_ref, g_ref, b_ref, ys_ref, o_ref,
                    yl_sc, sem, *, tt):
    i = pl.program_id(0)
    cur = i % 2

    def copies(ctl, half, start):
        for e in range(N_EXPERTS):
            n, loc_g, buf_g = ctl[0, e], ctl[1, e], ctl[2, e]
            _granule_copies(ys_ref, yl_sc.at[half], buf_g, loc_g, n, sem.at[half], start)

    @pl.when(i == 0)
    def _():
        yl_sc[...] = jnp.zeros_like(yl_sc)
        copies(ctl_ref, 0, True)

    @pl.when(i + 1 < pl.num_programs(0))
    def _():
        copies(ctl_next_ref, 1 - cur, True)

    copies(ctl_ref, cur, False)

    n_slots = yl_sc.shape[1]
    route = route_ref[...]
    slot = lax.broadcasted_iota(jnp.int32, (tt, n_slots), 1).astype(F32)
    pick = (slot == route[:, ROUTE_SLOT1:ROUTE_SLOT1 + 1]) | (slot == route[:, ROUTE_SLOT2:ROUTE_SLOT2 + 1])
    y = jnp.dot(jnp.where(pick, 1.0, 0.0).astype(BF16), yl_sc[cur], preferred_element_type=F32)
    o_ref[...] = _layernorm(ALPHA * x_ref[...] + y, g_ref[...], b_ref[...])


def _combine(x1, route, ctl, ys, layer, tt, g, b):
    n = x1.shape[0]
    last = n // tt - 1
    lay = lambda shape: pl.BlockSpec((None,) + shape, lambda i: (layer,) + (0,) * len(shape))
    return pl.pallas_call(
        functools.partial(_combine_kernel, tt=tt),
        grid=(n // tt,),
        in_specs=[pl.BlockSpec((None, SUBLANES, LANES), lambda i: (i, 0, 0), memory_space=pltpu.SMEM),
                  pl.BlockSpec((None, SUBLANES, LANES), lambda i: (jnp.minimum(i + 1, last), 0, 0),
                               memory_space=pltpu.SMEM),
                  pl.BlockSpec((tt, D_MODEL), lambda i: (i, 0)),
                  pl.BlockSpec((tt, LANES), lambda i: (i, 0)),
                  lay((1, D_MODEL)), lay((1, D_MODEL)),
                  pl.BlockSpec(memory_space=pl.ANY)],
        out_specs=pl.BlockSpec((tt, D_MODEL), lambda i: (i, 0)),
        scratch_shapes=[pltpu.VMEM((2, _local_slots(tt), D_MODEL), BF16),
                        pltpu.SemaphoreType.DMA((2,))],
        out_shape=jax.ShapeDtypeStruct((n, D_MODEL), F32),
        compiler_params=_cparams(1),
        name="moe_combine",
    )(ctl, ctl, x1, route, g, b, ys)


def _moe_sparse(x1, route, plan, gran, loff, wg, wu, wd, layer, tt, g, b):
    n = x1.shape[0]
    n_tiles = n // tt
    gran = gran[:, :, 0]
    loff = loff[:, :, 0]
    max_tiles = -(-(2 * n + (GRANULE - 1) * N_EXPERTS * n_tiles) // MOE_TILE) + N_EXPERTS
    rows = max_tiles * MOE_TILE
    tot_g = jnp.sum(gran, axis=0)
    tiles = (tot_g + GRANULES_PER_TILE - 1) // GRANULES_PER_TILE
    tile_end = jnp.cumsum(tiles)
    seg_g = ((tile_end - tiles) * GRANULES_PER_TILE).astype(jnp.int32)
    n_valid = tile_end[-1]
    idx = jnp.minimum(jnp.arange(max_tiles, dtype=jnp.int32), n_valid - 1)
    tile_expert = jnp.sum(idx[:, None] >= tile_end[None, :], axis=1).astype(jnp.int32)
    buf_g = seg_g[None, :] + jnp.cumsum(gran, axis=0) - gran
    ctl = jnp.zeros((n_tiles, SUBLANES, LANES), jnp.int32)
    ctl = ctl.at[:, 0, :N_EXPERTS].set(gran).at[:, 1, :N_EXPERTS].set(loff).at[:, 2, :N_EXPERTS].set(buf_g)
    xs, gs = _dispatch(x1, plan, ctl, seg_g, tot_g.astype(jnp.int32), tt, rows)
    ys = _ffn(xs, gs, tile_expert, n_valid.reshape(1).astype(jnp.int32), wg, wu, wd, layer)
    return _combine(x1, route, ctl, ys, layer, tt, g, b)


def _moe_kernel(x_ref, r_ref, wg_ref, wu_ref, wd_ref, g_ref, b_ref, o_ref, xb_sc, acc_sc):
    e = pl.program_id(1)

    @pl.when(e == 0)
    def _():
        xb_sc[...] = x_ref[...].astype(BF16)
        acc_sc[...] = jnp.zeros_like(acc_sc)

    xb = xb_sc[...]
    hg = jnp.dot(xb, wg_ref[...].astype(BF16), preferred_element_type=F32)
    hu = jnp.dot(xb, wu_ref[...].astype(BF16), preferred_element_type=F32)
    r = r_ref[...]
    lane = lax.broadcasted_iota(jnp.int32, r.shape, 1)
    c = jnp.sum(jnp.where(lane == e, r, 0.0), axis=-1, keepdims=True)
    h = _silu(hg) * hu * c
    acc_sc[...] += jnp.dot(h.astype(BF16), wd_ref[...].astype(BF16), preferred_element_type=F32)

    @pl.when(e == pl.num_programs(1) - 1)
    def _():
        o_ref[...] = _layernorm(ALPHA * x_ref[...] + acc_sc[...], g_ref[...], b_ref[...])


def _moe(x1, route, wg, wu, wd, layer, tm, g, b):
    n = x1.shape[0]
    lay = lambda shape: pl.BlockSpec((None,) + shape, lambda i, e: (layer,) + (0,) * len(shape))
    return pl.pallas_call(
        _moe_kernel,
        grid=(n // tm, N_EXPERTS),
        in_specs=[pl.BlockSpec((tm, D_MODEL), lambda i, e: (i, 0)),
                  pl.BlockSpec((tm, LANES), lambda i, e: (i, 0)),
                  pl.BlockSpec((None, None, D_MODEL, D_EXPERT), lambda i, e: (layer, e, 0, 0)),
                  pl.BlockSpec((None, None, D_MODEL, D_EXPERT), lambda i, e: (layer, e, 0, 0)),
                  pl.BlockSpec((None, None, D_EXPERT, D_MODEL), lambda i, e: (layer, e, 0, 0)),
                  lay((1, D_MODEL)), lay((1, D_MODEL))],
        out_specs=pl.BlockSpec((tm, D_MODEL), lambda i, e: (i, 0)),
        out_shape=jax.ShapeDtypeStruct((n, D_MODEL), F32),
        scratch_shapes=[pltpu.VMEM((tm, D_MODEL), BF16), pltpu.VMEM((tm, D_MODEL), F32)],
        compiler_params=_cparams(2),
        name="moe",
    )(x1, route, wg, wu, wd, g, b)


def _block_diag(w):
    out = jnp.zeros((DEPTH, D_A, D_A), w.dtype)
    for gi in range(A_BLOCKS):
        out = out.at[:, gi * A_BLK:(gi + 1) * A_BLK, gi * A_BLK:(gi + 1) * A_BLK].set(w[:, gi])
    return out


def kernel(x_prompt, x_sample, state_rglru_h, state_conv, state_hgrn, state_ret, w_in, conv_w, conv_b, w_rgate, b_rgate, w_igate, b_igate, rglru_lambda, hgrn_lb_logits, hgrn_norm_g, w_out, ln1_g, ln1_b, router_w, router_b, exp_w_gate, exp_w_up, exp_w_down, ln2_g, ln2_b):
    batch, seq, _ = x_prompt.shape
    nb = x_sample.shape[0]

    w_in_bf = w_in.astype(BF16)
    w_out_bf = w_out.astype(BF16)
    wg, wu, wd = exp_w_gate, exp_w_up, exp_w_down
    wri = jnp.concatenate([_block_diag(w_rgate), _block_diag(w_igate)], axis=-1).astype(BF16)
    bri = jnp.concatenate([b_rgate.reshape(DEPTH, 1, D_A), b_igate.reshape(DEPTH, 1, D_A)], axis=-1)
    cb = conv_b.reshape(DEPTH, 1, D_A)
    lam = rglru_lambda.reshape(DEPTH, 1, D_A)
    ng = jnp.tile(hgrn_norm_g, (1, B_HEADS)).reshape(DEPTH, 1, D_B)
    g1, b1 = ln1_g.reshape(DEPTH, 1, D_MODEL), ln1_b.reshape(DEPTH, 1, D_MODEL)
    g2, b2 = ln2_g.reshape(DEPTH, 1, D_MODEL), ln2_b.reshape(DEPTH, 1, D_MODEL)
    rwt = router_w.T
    rb = router_b.reshape(N_EXPERTS, 1)
    conv0_t = jnp.transpose(state_conv, (0, 2, 1, 3))

    xp = x_prompt.reshape(batch * seq, D_MODEL)
    xs = x_sample.reshape(nb, D_MODEL)
    hs_p, convs_p, hgs_p, rts_p = [], [], [], []
    prev_s = []
    for l in range(DEPTH):
        y_p, h_p, conv_p, hg_p, rt_p = _mix_prompt(
            xp, w_in_bf, batch, seq, l, 512, conv_w, cb, wri, bri, lam, hgrn_lb_logits, ng)
        x1_p, route_p, plan_p, gran_p, loff_p = _outproj(
            y_p, xp, w_out_bf, l, 512, g1, b1, rwt, rb, plan=True)
        xp = _moe_sparse(x1_p, route_p, plan_p, gran_p, loff_p, wg, wu, wd, l, 512, g2, b2)
        hs_p.append(h_p.reshape(batch, D_A))
        convs_p.append(conv_p)
        hgs_p.append(hg_p)
        rts_p.append(rt_p)

        proj_s = _proj(xs, w_in, l, nb)
        y_s, *states_s = _mix_sample(
            proj_s, state_rglru_h, conv0_t, state_hgrn, state_ret, l,
            conv_w, cb, wri, bri, lam, hgrn_lb_logits, ng,
            prev=prev_s if l == DEPTH - 1 else ())
        prev_s.append(states_s)
        x1_s, route_s = _outproj(y_s, xs, w_out, l, nb, g1, b1, rwt, rb)
        xs = _moe(x1_s, route_s, wg, wu, wd, l, nb, g2, b2)

    h_s, conv_s, hg_s, rt_s = prev_s[-1]
    return (xp.reshape(batch, seq, D_MODEL), xs.reshape(nb, 1, D_MODEL),
            jnp.stack(hs_p), h_s, jnp.stack(convs_p), jnp.transpose(conv_s, (0, 2, 1, 3)),
            jnp.stack(hgs_p), hg_s, jnp.stack(rts_p), rt_s)
```

```python
import functools

import numpy as np
import jax
import jax.numpy as jnp
from jax import lax
from jax.experimental import pallas as pl
from jax.experimental.pallas import tpu as pltpu

D_MODEL = 1024
DEPTH = 2
PAST_LEN = 16384
D_A = 256
A_BLOCKS = 4
A_BLK = 64
CONV_W = 4
RGLRU_C = 8.0
B_HEADS = 4
B_DK = 64
D_B = 256
C_HEADS = 4
C_DK = 128
D_C = 512
D_IN = 3584
B_CHUNK = 64
C_CHUNK = 128
ROPE_BASE = 10000.0
N_EXPERTS = 16
N_GROUPS = 4
EXP_PER_GROUP = 4
D_EXPERT = 512
LN_EPS = 1e-5
RMS_EPS = 1e-6
GN_EPS = 1e-6
F_TINY = 1e-30
ALPHA = (2 * DEPTH) ** 0.25

O_XA, O_GA, O_QB, O_FB, O_VB, O_GB, O_QC, O_KC, O_VC, O_GC = (
    0, 256, 512, 768, 1024, 1280, 1536, 2048, 2560, 3072)

V7X_VMEM_LIMIT_BYTES = 56 * 1024 * 1024
SUBLANES = 8
LANES = 128
HGRN_SAFE_MIN_LOGDECAY = -60.0

BF16 = jnp.bfloat16
F32 = jnp.float32
_NT = (((1,), (1,)), ((), ()))
_TN = (((0,), (0,)), ((), ()))


def _cparams(n_axes):
    return pltpu.CompilerParams(
        dimension_semantics=("arbitrary",) * n_axes,
        vmem_limit_bytes=V7X_VMEM_LIMIT_BYTES)


def _dot(a, b):
    return jnp.dot(a.astype(BF16), b.astype(BF16), preferred_element_type=F32)


def _dot_g(a, b, dims):
    return lax.dot_general(a.astype(BF16), b.astype(BF16), dims, preferred_element_type=F32)


def _sigmoid(x):
    return 0.5 * jnp.tanh(0.5 * x) + 0.5


def _sqrt_nonneg(x):
    return jnp.where(x > 0.0, x * lax.rsqrt(x), 0.0)


def _silu(x):
    return x * _sigmoid(x)


def _gelu_tanh(x):
    c = np.float32(np.sqrt(2.0 / np.pi))
    return 0.5 * x * (1.0 + jnp.tanh(c * (x + np.float32(0.044715) * (x * x * x))))


def _log_sigmoid(x):
    return -(jnp.maximum(-x, 0.0) + jnp.log(1.0 + jnp.exp(-jnp.abs(x))))


def _layernorm(z, g, b):
    mu = jnp.mean(z, axis=-1, keepdims=True)
    zc = z - mu
    var = jnp.mean(zc * zc, axis=-1, keepdims=True)
    return zc * lax.rsqrt(var + LN_EPS) * g + b


def _hgrn_lower_bound(lbl, layer):
    rows = [lbl[j:j + 1, :] for j in range(DEPTH)]
    m = rows[0]
    for r in rows[1:]:
        m = jnp.maximum(m, r)
    ex = [jnp.exp(r - m) for r in rows]
    tot = ex[0]
    for e in ex[1:]:
        tot = tot + e
    lb = jnp.zeros_like(m)
    for j in range(1, layer + 1):
        lb = lb + ex[j] / tot
    return lb


def _rglru_gates(u, wri, bri, lam):
    gates = _dot(u, wri) + bri
    r = _sigmoid(gates[:, :D_A])
    i = _sigmoid(gates[:, D_A:])
    log_a = RGLRU_C * r * _log_sigmoid(lam)
    a = jnp.exp(log_a)
    bterm = _sqrt_nonneg(jnp.maximum(1.0 - a * a, 0.0)) * (i * u)
    return a, bterm


def _split(x):
    hi = x.astype(BF16)
    return hi, (x - hi.astype(F32)).astype(BF16)


def _dot3(x, w):
    xh, xl = _split(x)
    wh, wl = _split(w)
    return (jnp.dot(xh, wh, preferred_element_type=F32)
            + jnp.dot(xl, wh, preferred_element_type=F32)
            + jnp.dot(xh, wl, preferred_element_type=F32))


def _proj_kernel(x_ref, w_ref, o_ref):
    x = x_ref[...]
    for j in range(0, D_IN, 512):
        o_ref[:, j:j + 512] = _dot3(x, w_ref[:, j:j + 512])


def _proj(x, w_in_bf, layer, tm):
    n = x.shape[0]
    return pl.pallas_call(
        _proj_kernel,
        grid=(n // tm,),
        in_specs=[pl.BlockSpec((tm, D_MODEL), lambda i: (i, 0)),
                  pl.BlockSpec((None, D_MODEL, D_IN), lambda i: (layer, 0, 0))],
        out_specs=pl.BlockSpec((tm, D_IN), lambda i: (i, 0)),
        out_shape=jax.ShapeDtypeStruct((n, D_IN), F32),
        compiler_params=_cparams(1),
        name="proj",
    )(x, w_in_bf)


def _retention_consts(chunk):
    lg = np.log1p(-np.exp2(-5.0 - np.arange(C_HEADS, dtype=np.float64)))
    idx = np.arange(chunk, dtype=np.float64)
    rel = idx[:, None] - idx[None, :]
    mask = rel >= 0
    dmat = np.where(mask[None], np.exp(np.where(mask, rel, 0.0)[None] * lg[:, None, None]), 0.0)
    qdec = np.exp((idx + 1.0)[None, :] * lg[:, None])
    kdec = np.exp((chunk - 1.0 - idx)[None, :] * lg[:, None])
    sdec = np.exp(chunk * lg)
    qdec_b = np.broadcast_to(qdec[:, :, None], (C_HEADS, chunk, C_DK))
    kdec_b = np.broadcast_to(kdec[:, :, None], (C_HEADS, chunk, C_DK))
    return (dmat.astype(np.float32), np.ascontiguousarray(qdec_b).astype(np.float32),
            np.ascontiguousarray(kdec_b).astype(np.float32), [float(v) for v in sdec])


def _rope_tables(positions):
    half = C_DK // 2
    inv = ROPE_BASE ** (-np.arange(half, dtype=np.float64) / half)
    ang = np.asarray(positions, dtype=np.float64)[:, None] * inv[None]
    cos = np.concatenate([np.cos(ang), np.cos(ang)], axis=-1)
    sin = np.concatenate([-np.sin(ang), np.sin(ang)], axis=-1)
    return cos.astype(np.float32), sin.astype(np.float32)


def _rope(x, cos, sin_signed):
    return x * cos + pltpu.roll(x, C_DK // 2, 1) * sin_signed


PROJ_SPLIT = 1536


def _mix_prompt_kernel(x0_ref, xn_ref, w_ref, cos_ref, sin_ref, cw_ref, cb_ref, wri_ref, bri_ref,
                       lam_ref, lbl_ref, ng_ref, dmat_ref, qdec_ref, kdec_ref,
                       y_ref, h_ref, conv_ref, hg_ref, rt_ref,
                       prev_sc, hprev_sc, st_sc, sret_sc, kb_sc, bb_sc, vb_sc, oi_sc, p_ref, next_ref, xb_sc, x0b_sc,
                       *, layer, tt, sdec):
    t = pl.program_id(1)
    nt = pl.num_programs(1)
    flat = pl.program_id(0) * nt + t

    def project(xb_ref, dst, c0, c1):
        xb = xb_ref[...]
        for j in range(c0, c1, 256):
            dst[:, j:j + 256] = jnp.dot(xb, w_ref[:, j:j + 256], preferred_element_type=F32)

    xb_sc[...] = xn_ref[...].astype(BF16)

    @pl.when(flat == 0)
    def _():
        x0b_sc[...] = x0_ref[...].astype(BF16)
        project(x0b_sc, p_ref, 0, D_IN)

    @pl.when(t == 0)
    def _():
        prev_sc[...] = jnp.zeros_like(prev_sc)
        hprev_sc[...] = jnp.zeros_like(hprev_sc)
        st_sc[...] = jnp.zeros_like(st_sc)
        sret_sc[...] = jnp.zeros_like(sret_sc)

    xa = p_ref[:, O_XA:O_XA + D_A]
    ga = p_ref[:, O_GA:O_GA + D_A]
    row = lax.broadcasted_iota(jnp.int32, (tt, D_A), 0)
    row8 = lax.broadcasted_iota(jnp.int32, (SUBLANES, D_A), 0)
    prev = prev_sc[...]

    def shifted(j):
        r = pltpu.roll(xa, j, 0)
        top = jnp.where(row8 < j, pltpu.roll(prev, j, 0), r[0:SUBLANES])
        return jnp.concatenate([top, r[SUBLANES:]], axis=0)

    u = cb_ref[...] + shifted(3) * cw_ref[0:1, :]
    u = u + shifted(2) * cw_ref[1:2, :]
    u = u + shifted(1) * cw_ref[2:3, :]
    u = u + xa * cw_ref[3:4, :]
    last8 = xa[tt - SUBLANES:tt]
    prev_sc[...] = last8

    a, bterm = _rglru_gates(u, wri_ref[...], bri_ref[...], lam_ref[...])
    project(xb_sc, next_ref, 0, PROJ_SPLIT)
    s = 1
    while s < SUBLANES:
        keep = (row % SUBLANES) >= s
        a_sh = jnp.where(keep, pltpu.roll(a, s, 0), 1.0)
        b_sh = jnp.where(keep, pltpu.roll(bterm, s, 0), 0.0)
        bterm = a * b_sh + bterm
        a = a * a_sh
        s *= 2
    carry = hprev_sc[...]
    groups = []
    for gi in range(tt // SUBLANES):
        rows = slice(gi * SUBLANES, (gi + 1) * SUBLANES)
        h_in = jnp.broadcast_to(carry[SUBLANES - 1:SUBLANES, :], (SUBLANES, D_A))
        carry = a[rows] * h_in + bterm[rows]
        groups.append(carry)
    h = jnp.concatenate(groups, axis=0)
    hlast8 = carry
    hprev_sc[...] = hlast8
    y_ref[:, 0:D_A] = h * _gelu_tanh(ga)

    lb = _hgrn_lower_bound(lbl_ref[...], layer)
    ng = ng_ref[...]
    cl = B_CHUNK
    crow = lax.broadcasted_iota(jnp.int32, (cl, D_B), 0)
    ccol = lax.broadcasted_iota(jnp.int32, (cl, D_B), 1)
    causal = (ccol % B_DK) <= crow
    br = lax.broadcasted_iota(jnp.int32, (D_B, D_B), 0)
    bc = lax.broadcasted_iota(jnp.int32, (D_B, D_B), 1)
    head_mask = (br // B_DK) == (bc // B_DK)
    seg = jnp.where(head_mask, 1.0, 0.0).astype(BF16)

    def seg_mean(x):
        hi = x.astype(BF16)
        lo = (x - hi.astype(F32)).astype(BF16)
        tot = (jnp.dot(hi, seg, preferred_element_type=F32)
               + jnp.dot(lo, seg, preferred_element_type=F32))
        return tot * (1.0 / B_DK)

    nck = tt // cl
    q = _silu(p_ref[:, O_QB:O_QB + D_B])
    fl = p_ref[:, O_FB:O_FB + D_B]
    v = p_ref[:, O_VB:O_VB + D_B]
    half_th = 0.5 * jnp.tanh(0.5 * fl)
    f = lb + (1.0 - lb) * (0.5 + half_th)
    b = jnp.log(jnp.maximum(f, F_TINY))
    k = (1.0 - lb) * (0.5 - half_th)
    trow = lax.broadcasted_iota(jnp.int32, (tt, D_B), 0) % cl
    sh = 1
    while sh < cl:
        b = b + jnp.where(trow >= sh, pltpu.roll(b, sh, 0), 0.0)
        sh *= 2

    def chunk_row(x, c, r):
        top = c * cl + r + 1
        return x[top - SUBLANES:top][SUBLANES - 1:SUBLANES]

    def spread(rows):
        return jnp.concatenate([jnp.broadcast_to(r, (cl, D_B)) for r in rows], axis=0)

    last_rows = [chunk_row(b, c, cl - 1) for c in range(nck)]
    mid_rows = [chunk_row(b, c, cl // 2 - 1) for c in range(nck)]
    b_last = spread(last_rows)
    b_mid = spread(mid_rows)
    qd = q * jnp.exp(b)
    kl = k * jnp.exp(b_last - b)
    lasts = jnp.concatenate(last_rows, axis=0)
    mids = jnp.concatenate(mid_rows, axis=0)
    safe = jnp.min(jnp.minimum(mids, lasts - mids)) >= HGRN_SAFE_MIN_LOGDECAY

    def next_state(st, c, sl):
        kv = jnp.where(head_mask, _dot_g(v[sl], kl[sl], _TN), 0.0)
        return st * jnp.exp(last_rows[c]) + kv

    @pl.when(safe)
    def _():
        qmid = q * jnp.exp(b - b_mid)
        kinv = k * jnp.exp(b_mid - b)
        st = st_sc[...]
        for c in range(nck):
            sl = slice(c * cl, (c + 1) * cl)
            k4 = jnp.where(head_mask, jnp.concatenate([kinv[sl]] * B_HEADS, axis=0), 0.0)
            sc = jnp.where(causal, _dot_g(qmid[sl], k4, _NT), 0.0)
            v4 = jnp.where(head_mask, jnp.concatenate([v[sl]] * B_HEADS, axis=0), 0.0)
            oi_sc[sl, :] = _dot(sc, v4) + _dot_g(qd[sl], st, _NT)
            st = next_state(st, c, sl)
        st_sc[...] = st

    @pl.when(jnp.logical_not(safe))
    def _():
        kb_sc[...] = k
        bb_sc[...] = b
        vb_sc[...] = v
        st = st_sc[...]
        for c in range(nck):
            sl = slice(c * cl, (c + 1) * cl)
            qc, bc_ = q[sl], b[sl]

            def pair(sidx, acc, c=c, qc=qc, bc_=bc_):
                src = pl.ds(c * cl + sidx, 1)
                e = jnp.exp(jnp.minimum(bc_ - bb_sc[src, :], 0.0)) * (qc * kb_sc[src, :])
                scr = jnp.dot(e.astype(BF16), seg, preferred_element_type=F32)
                return acc + jnp.where(crow >= sidx, scr, 0.0) * vb_sc[src, :]

            oi = lax.fori_loop(0, cl, pair, jnp.zeros((cl, D_B), F32))
            oi_sc[sl, :] = oi + _dot_g(qd[sl], st, _NT)
            st = next_state(st, c, sl)
        st_sc[...] = st

    o = oi_sc[...]
    o = o * lax.rsqrt(seg_mean(o * o) + RMS_EPS) * ng
    y_ref[:, D_A:D_A + D_B] = o * _silu(p_ref[:, O_GB:O_GB + D_B])

    rl = C_CHUNK
    per_head = (D_IN - PROJ_SPLIT) // C_HEADS
    cos = cos_ref[...]
    sin = sin_ref[...]
    for hh in range(C_HEADS):
        lo = hh * C_DK
        q = _rope(p_ref[:, O_QC + lo:O_QC + lo + C_DK], cos, sin)
        k = _rope(p_ref[:, O_KC + lo:O_KC + lo + C_DK], cos, sin) * (C_DK ** -0.5)
        v = p_ref[:, O_VC + lo:O_VC + lo + C_DK]
        s = sret_sc[hh]
        parts = []
        for c in range(tt // rl):
            sl = slice(c * rl, (c + 1) * rl)
            sc = _dot_g(q[sl], k[sl], _NT) * dmat_ref[hh]
            parts.append(_dot(sc, v[sl]) + _dot(q[sl] * qdec_ref[hh], s))
            s = sdec[hh] * s + _dot_g(k[sl] * kdec_ref[hh], v[sl], _TN)
        sret_sc[hh] = s
        project(xb_sc, next_ref, PROJ_SPLIT + hh * per_head, PROJ_SPLIT + (hh + 1) * per_head)
        o = jnp.concatenate(parts, axis=0)
        mu = jnp.mean(o, axis=-1, keepdims=True)
        oc = o - mu
        var = jnp.mean(oc * oc, axis=-1, keepdims=True)
        g = p_ref[:, O_GC + lo:O_GC + lo + C_DK]
        y_ref[:, D_A + D_B + lo:D_A + D_B + lo + C_DK] = oc * lax.rsqrt(var + GN_EPS) * _silu(g)

    p_ref[...] = next_ref[...]

    @pl.when(t == nt - 1)
    def _():
        h_ref[...] = pltpu.roll(hprev_sc[...], 1, 0)[0:1]
        conv_ref[...] = pltpu.roll(prev_sc[...], CONV_W - 1, 0)[0:CONV_W - 1]
        s_bd = st_sc[...].T
        for hh in range(B_HEADS):
            hg_ref[hh] = s_bd[hh * B_DK:(hh + 1) * B_DK, hh * B_DK:(hh + 1) * B_DK]
        rt_ref[...] = sret_sc[...]


def _mix_prompt(x, w_in_bf, batch, seq, layer, tt, cw, cb, wri, bri, lam, lbl, ng):
    nt = seq // tt
    last = batch * nt - 1
    dmat, qdec, kdec, sdec = _retention_consts(C_CHUNK)
    cos, sin = _rope_tables(np.arange(seq))
    full = lambda shape: pl.BlockSpec(shape, lambda b, t: (0,) * len(shape))
    lay = lambda shape: pl.BlockSpec((None,) + shape, lambda b, t: (layer,) + (0,) * len(shape))
    kern = functools.partial(_mix_prompt_kernel, layer=layer, tt=tt, sdec=sdec)
    return pl.pallas_call(
        kern,
        grid=(batch, nt),
        in_specs=[
            pl.BlockSpec((tt, D_MODEL), lambda b, t: (b * nt + t, 0)),
            pl.BlockSpec((tt, D_MODEL), lambda b, t: (jnp.minimum(b * nt + t + 1, last), 0)),
            lay((D_MODEL, D_IN)),
            pl.BlockSpec((tt, C_DK), lambda b, t: (t, 0)),
            pl.BlockSpec((tt, C_DK), lambda b, t: (t, 0)),
            lay((CONV_W, D_A)), lay((1, D_A)), lay((D_A, 2 * D_A)), lay((1, 2 * D_A)),
            lay((1, D_A)), full((DEPTH, D_B)), lay((1, D_B)),
            full((C_HEADS, C_CHUNK, C_CHUNK)), full((C_HEADS, C_CHUNK, C_DK)),
            full((C_HEADS, C_CHUNK, C_DK)),
        ],
        out_specs=[
            pl.BlockSpec((tt, D_MODEL), lambda b, t: (b * nt + t, 0)),
            pl.BlockSpec((None, 1, D_A), lambda b, t: (b, 0, 0)),
            pl.BlockSpec((None, CONV_W - 1, D_A), lambda b, t: (b, 0, 0)),
            pl.BlockSpec((None, B_HEADS, B_DK, B_DK), lambda b, t: (b, 0, 0, 0)),
            pl.BlockSpec((None, C_HEADS, C_DK, C_DK), lambda b, t: (b, 0, 0, 0)),
        ],
        out_shape=[
            jax.ShapeDtypeStruct((batch * seq, D_MODEL), F32),
            jax.ShapeDtypeStruct((batch, 1, D_A), F32),
            jax.ShapeDtypeStruct((batch, CONV_W - 1, D_A), F32),
            jax.ShapeDtypeStruct((batch, B_HEADS, B_DK, B_DK), F32),
            jax.ShapeDtypeStruct((batch, C_HEADS, C_DK, C_DK), F32),
        ],
        scratch_shapes=[
            pltpu.VMEM((SUBLANES, D_A), F32), pltpu.VMEM((SUBLANES, D_A), F32),
            pltpu.VMEM((D_B, D_B), F32), pltpu.VMEM((C_HEADS, C_DK, C_DK), F32),
            pltpu.VMEM((tt, D_B), F32), pltpu.VMEM((tt, D_B), F32),
            pltpu.VMEM((tt, D_B), F32), pltpu.VMEM((tt, D_B), F32),
            pltpu.VMEM((tt, D_IN), F32), pltpu.VMEM((tt, D_IN), F32),
            pltpu.VMEM((tt, D_MODEL), BF16), pltpu.VMEM((tt, D_MODEL), BF16),
        ],
        compiler_params=_cparams(2),
        name="mix_prompt",
    )(x, x, w_in_bf, cos, sin, cw, cb, wri, bri, lam, lbl, ng, dmat, qdec, kdec)


def _column_matrix(x):
    pad = jnp.zeros((LANES - SUBLANES, LANES), F32)
    return jnp.concatenate([x, pad], axis=0).T


def _mix_sample_kernel(*refs, layer, gammas, n_prev):
    (p_ref, h0_ref, conv0_ref, rt0_ref, cos_ref, sin_ref,
     cw_ref, cb_ref, wri_ref, bri_ref, lam_ref) = refs[:11]
    prev = refs[11:11 + 3 * n_prev]
    y_ref, h_ref, conv_ref, rt_ref, o_sc = refs[11 + 3 * n_prev:]
    if n_prev:
        for j in range(n_prev):
            for dst, src in zip((h_ref, conv_ref, rt_ref), prev[3 * j:3 * j + 3]):
                dst[j] = src[...]
        h_ref, conv_ref, rt_ref = (r.at[layer] for r in (h_ref, conv_ref, rt_ref))
    tb = SUBLANES
    xa = p_ref[:, O_XA:O_XA + D_A]
    ga = p_ref[:, O_GA:O_GA + D_A]
    c0, c1, c2 = conv0_ref[0], conv0_ref[1], conv0_ref[2]
    u = cb_ref[...] + c0 * cw_ref[0:1, :]
    u = u + c1 * cw_ref[1:2, :]
    u = u + c2 * cw_ref[2:3, :]
    u = u + xa * cw_ref[3:4, :]
    conv_ref[0] = c1
    conv_ref[1] = c2
    conv_ref[2] = xa
    a, bterm = _rglru_gates(u, wri_ref[...], bri_ref[...], lam_ref[...])
    h = a * h0_ref[...] + bterm
    h_ref[...] = h
    y_ref[:, 0:D_A] = h * _gelu_tanh(ga)

    y_ref[:, D_A:D_A + D_B] = jnp.zeros((tb, D_B), F32)

    cos = cos_ref[...]
    sin = sin_ref[...]
    for head in range(C_HEADS):
        lo = head * C_DK
        gamma = gammas[head]
        q = _rope(p_ref[:, O_QC + lo:O_QC + lo + C_DK], cos, sin)
        k = _rope(p_ref[:, O_KC + lo:O_KC + lo + C_DK], cos, sin) * (C_DK ** -0.5)
        v = p_ref[:, O_VC + lo:O_VC + lo + C_DK]
        g = p_ref[:, O_GC + lo:O_GC + lo + C_DK]
        dots = jnp.sum(q * k, axis=-1, keepdims=True)
        q_cols = _column_matrix(q * gamma)
        k_cols = _column_matrix(k)
        for b in range(tb):
            s_old = rt0_ref[b, head]
            v_row = v[b:b + 1, :]
            rt_ref[b, head] = gamma * s_old + k_cols[:, b:b + 1] * v_row
            o_row = (jnp.sum(q_cols[:, b:b + 1] * s_old, axis=0, keepdims=True)
                     + dots[b:b + 1, :] * v_row)
            o_sc[b:b + 1, 0:C_DK] = o_row
        o = o_sc[:, 0:C_DK]
        mu = jnp.mean(o, axis=-1, keepdims=True)
        oc = o - mu
        var = jnp.mean(oc * oc, axis=-1, keepdims=True)
        y_ref[:, D_A + D_B + lo:D_A + D_B + lo + C_DK] = oc * lax.rsqrt(var + GN_EPS) * _silu(g)


def _mix_sample(proj, h0, conv0_t, rt0, layer, cw, cb, wri, bri, lam, prev=()):
    nb = proj.shape[0]
    tb = SUBLANES
    lg = np.log1p(-np.exp2(-5.0 - np.arange(C_HEADS, dtype=np.float64)))
    gammas = [float(np.exp(v)) for v in lg]
    cos, sin = _rope_tables([PAST_LEN])
    full = lambda shape: pl.BlockSpec(shape, lambda i: (0,) * len(shape))
    lay = lambda shape: pl.BlockSpec((None,) + shape, lambda i: (layer,) + (0,) * len(shape))
    state_shapes = [(tb, D_A), (CONV_W - 1, tb, D_A), (tb, C_HEADS, C_DK, C_DK)]
    state_maps = [lambda i: (i, 0), lambda i: (0, i, 0), lambda i: (i, 0, 0, 0)]
    full_shapes = [(nb, D_A), (CONV_W - 1, nb, D_A), (nb, C_HEADS, C_DK, C_DK)]
    state_specs = [pl.BlockSpec(s, m) for s, m in zip(state_shapes, state_maps)]
    if prev:
        stack = len(prev) + 1
        out_state_specs = [pl.BlockSpec((stack,) + s, lambda i, m=m: (0,) + m(i))
                           for s, m in zip(state_shapes, state_maps)]
        out_state_shapes = [jax.ShapeDtypeStruct((stack,) + s, F32) for s in full_shapes]
    else:
        out_state_specs = state_specs
        out_state_shapes = [jax.ShapeDtypeStruct(s, F32) for s in full_shapes]
    kern = functools.partial(_mix_sample_kernel, layer=layer, gammas=gammas, n_prev=len(prev))
    return pl.pallas_call(
        kern,
        grid=(nb // tb,),
        in_specs=[
            pl.BlockSpec((tb, D_IN), lambda i: (i, 0)),
            pl.BlockSpec((None, tb, D_A), lambda i: (layer, i, 0)),
            pl.BlockSpec((None, CONV_W - 1, tb, D_A), lambda i: (layer, 0, i, 0)),
            pl.BlockSpec((None, tb, C_HEADS, C_DK, C_DK), lambda i: (layer, i, 0, 0, 0)),
            full((1, C_DK)), full((1, C_DK)),
            lay((CONV_W, D_A)), lay((1, D_A)), lay((D_A, 2 * D_A)), lay((1, 2 * D_A)),
            lay((1, D_A)),
        ] + state_specs * len(prev),
        out_specs=[pl.BlockSpec((tb, D_MODEL), lambda i: (i, 0))] + out_state_specs,
        out_shape=[jax.ShapeDtypeStruct((nb, D_MODEL), F32)] + out_state_shapes,
        scratch_shapes=[pltpu.VMEM((tb, D_B), F32)],
        compiler_params=_cparams(1),
        name="mix_sample",
    )(proj, h0, conv0_t, rt0, cos, sin, cw, cb, wri, bri, lam,
      *[a for states in prev for a in states])


def _hgrn_sample_kernel(*refs, layer, n_prev):
    qb_ref, fb_ref, vb_ref, gb_ref, lbl_ref, ng_ref, s0_ref, y_in_ref = refs[:8]
    prev = refs[8:8 + n_prev]
    y_ref, s_ref = refs[8 + n_prev:]
    del y_in_ref
    for j in range(n_prev):
        s_ref[j] = prev[j][...]
    if n_prev:
        s_ref = s_ref.at[layer]
    lb = _hgrn_lower_bound(lbl_ref[...], layer)
    q = _silu(qb_ref[...])
    fl = fb_ref[...]
    v = vb_ref[...]
    half_th = 0.5 * jnp.tanh(0.5 * fl)
    f = lb + (1.0 - lb) * (0.5 + half_th)
    ef = jnp.exp(jnp.log(jnp.maximum(f, F_TINY)))
    k = (1.0 - lb) * (0.5 - half_th)
    qf_t, k_t, ef_t, v_t, qk_t = (x.T for x in (q * ef, k, ef, v, q * k))
    o_parts = []
    for hh in range(2):
        base = hh * B_DK
        v_h = v_t[base:base + B_DK]
        acc = jnp.zeros((B_DK, LANES), F32)
        for kk in range(B_DK):
            r = base + kk
            s_old = s0_ref[hh, kk]
            s_ref[hh, kk] = ef_t[r:r + 1] * s_old + k_t[r:r + 1] * v_h
            acc = acc + qf_t[r:r + 1] * s_old
        dots = jnp.sum(qk_t[base:base + B_DK], axis=0, keepdims=True)
        o = acc + dots * v_h
        o_parts.append(o * lax.rsqrt(jnp.mean(o * o, axis=0, keepdims=True) + RMS_EPS))
    o = jnp.concatenate(o_parts, axis=0).T
    y_ref[...] = o * ng_ref[...] * _silu(gb_ref[...])


def _hgrn_sample(proj, y, s0_t, layer, lbl, ng_pair, prev=()):
    nb = proj.shape[0]
    col = lambda off: pl.BlockSpec((nb, LANES), lambda j: (0, off // LANES + j))
    state_block = (2, B_DK, B_DK, nb)
    state_spec = pl.BlockSpec(state_block, lambda j: (j, 0, 0, 0))
    stack = len(prev) + 1
    if prev:
        out_state_spec = pl.BlockSpec((stack,) + state_block, lambda j: (0, j, 0, 0, 0))
        out_state_shape = jax.ShapeDtypeStruct((stack, B_HEADS, B_DK, B_DK, nb), F32)
    else:
        out_state_spec = state_spec
        out_state_shape = jax.ShapeDtypeStruct((B_HEADS, B_DK, B_DK, nb), F32)
    return pl.pallas_call(
        functools.partial(_hgrn_sample_kernel, layer=layer, n_prev=len(prev)),
        grid=(B_HEADS // 2,),
        in_specs=[col(O_QB), col(O_FB), col(O_VB), col(O_GB),
                  pl.BlockSpec((DEPTH, LANES), lambda j: (0, j)),
                  pl.BlockSpec((None, 1, LANES), lambda j: (layer, 0, 0)),
                  pl.BlockSpec((None,) + state_block, lambda j: (layer, j, 0, 0, 0)),
                  pl.BlockSpec(memory_space=pl.ANY)] + [state_spec] * len(prev),
        out_specs=[col(D_A), out_state_spec],
        out_shape=[jax.ShapeDtypeStruct(y.shape, F32), out_state_shape],
        input_output_aliases={7: 0},
        compiler_params=_cparams(1),
        name="hgrn_sample",
    )(proj, proj, proj, proj, lbl, ng_pair, s0_t, y, *prev)


def _route_rows(l):
    m = l[0]
    for x in l[1:]:
        m = jnp.maximum(m, x)
    ex = [jnp.exp(x - m) for x in l]
    tot = ex[0]
    for x in ex[1:]:
        tot = tot + x
    p = [x / tot for x in ex]
    scores = []
    for gi in range(N_GROUPS):
        a, b, c, d = p[4 * gi:4 * gi + 4]
        hi1, lo1 = jnp.maximum(a, b), jnp.minimum(a, b)
        hi2, lo2 = jnp.maximum(c, d), jnp.minimum(c, d)
        top1 = jnp.maximum(hi1, hi2)
        top2 = jnp.maximum(jnp.minimum(hi1, hi2), jnp.maximum(lo1, lo2))
        scores.append(top1 + top2)
    best = scores[0]
    gsel = jnp.zeros_like(best, dtype=jnp.int32)
    for gi in range(1, N_GROUPS):
        upd = scores[gi] > best
        gsel = jnp.where(upd, gi, gsel)
        best = jnp.where(upd, scores[gi], best)
    vals = []
    for j in range(EXP_PER_GROUP):
        v = p[j]
        for gi in range(1, N_GROUPS):
            v = jnp.where(gsel == gi, p[4 * gi + j], v)
        vals.append(v)
    sel = []
    for j in range(EXP_PER_GROUP):
        rank = jnp.zeros_like(gsel)
        for i in range(EXP_PER_GROUP):
            if i == j:
                continue
            ahead = (vals[i] > vals[j]) | ((vals[i] == vals[j]) & (i < j))
            rank = rank + jnp.where(ahead, 1, 0)
        sel.append(rank < 2)
    denom = jnp.zeros_like(best)
    for j in range(EXP_PER_GROUP):
        denom = denom + jnp.where(sel[j], vals[j], 0.0)
    gates = [jnp.where(sel[j], vals[j] / denom, 0.0) for j in range(EXP_PER_GROUP)]
    comb = [jnp.where(gsel == (e // EXP_PER_GROUP), gates[e % EXP_PER_GROUP], 0.0)
            for e in range(N_EXPERTS)]
    j1 = jnp.where(sel[0], 0, jnp.where(sel[1], 1, 2))
    j2 = jnp.where(sel[3], 3, jnp.where(sel[2], 2, 1))
    g1 = jnp.zeros_like(best)
    g2 = jnp.zeros_like(best)
    for j in range(EXP_PER_GROUP):
        g1 = jnp.where(j1 == j, gates[j], g1)
        g2 = jnp.where(j2 == j, gates[j], g2)
    return comb, gsel * EXP_PER_GROUP + j1, gsel * EXP_PER_GROUP + j2, g1, g2


MOE_TILE = 512
GRANULE = 16
GRANULES_PER_TILE = MOE_TILE // GRANULE
ROUTE_SLOT1, ROUTE_SLOT2 = N_EXPERTS, N_EXPERTS + 1
PLAN_SLOT1, PLAN_SLOT2, PLAN_G1, PLAN_G2 = 0, 1, 2, 3


def _local_slots(tt):
    worst = 2 * tt + (N_EXPERTS - 1) * GRANULE
    return -(-worst // LANES) * LANES


def _outproj_kernel(*refs, plan):
    if plan:
        (y_ref, x_ref, w_ref, g_ref, b_ref, rwt_ref, rb_ref, tri_ref, ltri_ref,
         x1_ref, route_ref, plan_ref, gran_ref, loff_ref, rt_sc, oh_sc) = refs
    else:
        y_ref, x_ref, w_ref, g_ref, b_ref, rwt_ref, rb_ref, x1_ref, route_ref, rt_sc = refs
    if w_ref.dtype == BF16:
        y = jnp.dot(y_ref[...].astype(BF16), w_ref[...], preferred_element_type=F32)
    else:
        y = _dot3(y_ref[...], w_ref[...])
    x1 = _layernorm(ALPHA * x_ref[...] + y, g_ref[...], b_ref[...])
    x1_ref[...] = x1
    hi, lo = _split(x1)
    rhi, rlo = _split(rwt_ref[...])
    lg = (lax.dot_general(rhi, hi, _NT, preferred_element_type=F32)
          + lax.dot_general(rhi, lo, _NT, preferred_element_type=F32)
          + lax.dot_general(rlo, hi, _NT, preferred_element_type=F32)) + rb_ref[...]
    comb, e1, e2, g1, g2 = _route_rows([lg[e:e + 1, :] for e in range(N_EXPERTS)])
    rt_sc[...] = jnp.zeros_like(rt_sc)
    if not plan:
        for e in range(N_EXPERTS):
            rt_sc[e:e + 1, :] = comb[e]
    else:
        for e in range(N_EXPERTS):
            oh_sc[e:e + 1, :] = jnp.where((e1 == e) | (e2 == e), 1.0, 0.0)
        oh = oh_sc[...]
        cum = jnp.dot(oh.astype(BF16), tri_ref[...], preferred_element_type=F32)
        count = jnp.sum(oh, axis=1, keepdims=True)
        gran = jnp.floor((count + (GRANULE - 1)) * (1.0 / GRANULE))
        gran_b = jnp.broadcast_to(gran, (N_EXPERTS, LANES))
        loff_b = jnp.dot(ltri_ref[...], gran_b.astype(BF16), preferred_element_type=F32)
        slot_base = loff_b[:, 0:1] * GRANULE - 1.0 + cum
        s1 = jnp.zeros_like(g1)
        s2 = jnp.zeros_like(g1)
        for e in range(N_EXPERTS):
            row = slot_base[e:e + 1, :]
            s1 = jnp.where(e1 == e, row, s1)
            s2 = jnp.where(e2 == e, row, s2)
        plan_ref[...] = jnp.zeros_like(plan_ref)
        plan_ref[PLAN_SLOT1:PLAN_SLOT1 + 1, :] = s1
        plan_ref[PLAN_SLOT2:PLAN_SLOT2 + 1, :] = s2
        plan_ref[PLAN_G1:PLAN_G1 + 1, :] = g1
        plan_ref[PLAN_G2:PLAN_G2 + 1, :] = g2
        gran_ref[...] = gran_b.astype(jnp.int32)
        loff_ref[...] = loff_b.astype(jnp.int32)
        rt_sc[ROUTE_SLOT1:ROUTE_SLOT1 + 1, :] = s1
        rt_sc[ROUTE_SLOT2:ROUTE_SLOT2 + 1, :] = s2
    route_ref[...] = rt_sc[...].T


def _outproj(y, x, w_out_bf, layer, tm, g, b, rwt, rb, plan=False):
    n = x.shape[0]
    full = lambda shape: pl.BlockSpec(shape, lambda i: (0,) * len(shape))
    lay = lambda shape: pl.BlockSpec((None,) + shape, lambda i: (layer,) + (0,) * len(shape))
    in_specs = [pl.BlockSpec((tm, D_MODEL), lambda i: (i, 0)),
                pl.BlockSpec((tm, D_MODEL), lambda i: (i, 0)),
                lay((D_MODEL, D_MODEL)), lay((1, D_MODEL)), lay((1, D_MODEL)),
                full((N_EXPERTS, D_MODEL)), full((N_EXPERTS, 1))]
    out_specs = [pl.BlockSpec((tm, D_MODEL), lambda i: (i, 0)),
                 pl.BlockSpec((tm, LANES), lambda i: (i, 0))]
    out_shape = [jax.ShapeDtypeStruct((n, D_MODEL), F32),
                 jax.ShapeDtypeStruct((n, LANES), F32)]
    scratch = [pltpu.VMEM((LANES, tm), F32)]
    args = [y, x, w_out_bf, g, b, rwt, rb]
    if plan:
        tri = np.triu(np.ones((tm, tm), np.float32)).astype(jnp.bfloat16)
        ltri = np.tril(np.ones((N_EXPERTS, N_EXPERTS), np.float32), -1).astype(jnp.bfloat16)
        in_specs += [full((tm, tm)), full((N_EXPERTS, N_EXPERTS))]
        args += [tri, ltri]
        per_tile = pl.BlockSpec((None, N_EXPERTS, LANES), lambda i: (i, 0, 0))
        out_specs += [pl.BlockSpec((None, SUBLANES, tm), lambda i: (i, 0, 0)), per_tile, per_tile]
        out_shape += [jax.ShapeDtypeStruct((n // tm, SUBLANES, tm), F32),
                      jax.ShapeDtypeStruct((n // tm, N_EXPERTS, LANES), jnp.int32),
                      jax.ShapeDtypeStruct((n // tm, N_EXPERTS, LANES), jnp.int32)]
        scratch += [pltpu.VMEM((N_EXPERTS, tm), F32)]
    return pl.pallas_call(
        functools.partial(_outproj_kernel, plan=plan),
        grid=(n // tm,),
        in_specs=in_specs,
        out_specs=out_specs,
        out_shape=out_shape,
        scratch_shapes=scratch,
        compiler_params=_cparams(1),
        name="outproj_plan" if plan else "outproj",
    )(*args)


def _granule_copies(src, dst, src_g, dst_g, n, sem, start):
    size = GRANULES_PER_TILE
    while size >= 1:
        bit = n & size

        @pl.when(bit != 0)
        def _(size=size, src_g=src_g, dst_g=dst_g):
            s0 = pl.multiple_of(src_g * GRANULE, GRANULE)
            d0 = pl.multiple_of(dst_g * GRANULE, GRANULE)
            cp = pltpu.make_async_copy(src.at[pl.ds(s0, size * GRANULE)],
                                       dst.at[pl.ds(d0, size * GRANULE)], sem)
            cp.start() if start else cp.wait()

        src_g = src_g + bit
        dst_g = dst_g + bit
        size //= 2


def _dispatch_kernel(segg_ref, totg_ref, ctl_ref, plan_ref, x_ref, xs_ref, gs_ref,
                     xl_sc, gl_sc, zx_sc, zg_sc, sem, *, tt, total_tiles):
    @pl.when(pl.program_id(0) == 0)
    def _():
        zx_sc[...] = jnp.zeros_like(zx_sc)
        zg_sc[...] = jnp.zeros_like(zg_sc)

        def zero_tile(i):
            r0 = pl.multiple_of(i * MOE_TILE, MOE_TILE)
            return (pltpu.make_async_copy(zx_sc, xs_ref.at[pl.ds(r0, MOE_TILE)], sem.at[2]),
                    pltpu.make_async_copy(zg_sc, gs_ref.at[pl.ds(r0, MOE_TILE)], sem.at[2]))

        def run(i, start):
            for cp in zero_tile(i):
                cp.start() if start else cp.wait()

        for start in (True, False):
            for e in range(N_EXPERTS):
                end = segg_ref[e] + totg_ref[e]

                @pl.when(end % GRANULES_PER_TILE != 0)
                def _():
                    run(end // GRANULES_PER_TILE, start)

        last = N_EXPERTS - 1
        used = (segg_ref[last] + totg_ref[last] + GRANULES_PER_TILE - 1) // GRANULES_PER_TILE
        lax.fori_loop(used, total_tiles, lambda i, c: (run(i, True), c)[1], 0)
        lax.fori_loop(used, total_tiles, lambda i, c: (run(i, False), c)[1], 0)

    n_slots = xl_sc.shape[0]
    slot = lax.broadcasted_iota(jnp.int32, (n_slots, tt), 0).astype(F32)
    m1 = slot == plan_ref[PLAN_SLOT1:PLAN_SLOT1 + 1, :]
    m2 = slot == plan_ref[PLAN_SLOT2:PLAN_SLOT2 + 1, :]
    perm = jnp.where(m1 | m2, 1.0, 0.0).astype(BF16)
    xl_sc[...] = jnp.dot(perm, x_ref[...].astype(BF16), preferred_element_type=F32).astype(BF16)
    gate = (jnp.where(m1, plan_ref[PLAN_G1:PLAN_G1 + 1, :], 0.0)
            + jnp.where(m2, plan_ref[PLAN_G2:PLAN_G2 + 1, :], 0.0))
    gl_sc[...] = jnp.broadcast_to(jnp.sum(gate, axis=1, keepdims=True), gl_sc.shape)

    for start in (True, False):
        for e in range(N_EXPERTS):
            n, src_g, dst_g = ctl_ref[0, e], ctl_ref[1, e], ctl_ref[2, e]
            _granule_copies(xl_sc, xs_ref, src_g, dst_g, n, sem.at[0], start)
            _granule_copies(gl_sc, gs_ref, src_g, dst_g, n, sem.at[1], start)


def _dispatch(x1, plan, ctl, seg_g, tot_g, tt, rows):
    n = x1.shape[0]
    n_slots = _local_slots(tt)
    return pl.pallas_call(
        functools.partial(_dispatch_kernel, tt=tt, total_tiles=rows // MOE_TILE),
        grid_spec=pltpu.PrefetchScalarGridSpec(
            num_scalar_prefetch=2,
            grid=(n // tt,),
            in_specs=[pl.BlockSpec((None, SUBLANES, LANES), lambda i, s, c: (i, 0, 0),
                                   memory_space=pltpu.SMEM),
                      pl.BlockSpec((None, SUBLANES, tt), lambda i, s, c: (i, 0, 0)),
                      pl.BlockSpec((tt, D_MODEL), lambda i, s, c: (i, 0))],
            out_specs=[pl.BlockSpec(memory_space=pl.ANY), pl.BlockSpec(memory_space=pl.ANY)],
            scratch_shapes=[pltpu.VMEM((n_slots, D_MODEL), BF16), pltpu.VMEM((n_slots, LANES), F32),
                            pltpu.VMEM((MOE_TILE, D_MODEL), BF16), pltpu.VMEM((MOE_TILE, LANES), F32),
                            pltpu.SemaphoreType.DMA((3,))]),
        out_shape=[jax.ShapeDtypeStruct((rows, D_MODEL), BF16),
                   jax.ShapeDtypeStruct((rows, LANES), F32)],
        compiler_params=_cparams(1),
        name="moe_dispatch",
    )(seg_g, tot_g, ctl, plan, x1)


def _ffn_kernel(te_ref, nv_ref, xs_ref, gs_ref, wg_ref, wu_ref, wd_ref, ys_ref):
    used = pl.program_id(0) < nv_ref[0]

    @pl.when(used)
    def _():
        xb = xs_ref[...]
        hg = jnp.dot(xb, wg_ref[...].astype(BF16), preferred_element_type=F32)
        hu = jnp.dot(xb, wu_ref[...].astype(BF16), preferred_element_type=F32)
        h = (_silu(hg) * hu * gs_ref[:, 0:1]).astype(BF16)
        ys_ref[...] = jnp.dot(h, wd_ref[...].astype(BF16), preferred_element_type=F32).astype(BF16)

    @pl.when(jnp.logical_not(used))
    def _():
        ys_ref[...] = jnp.zeros_like(ys_ref)


def _ffn(xs, gs, tile_expert, n_valid, wg, wu, wd, layer):
    total_tiles = xs.shape[0] // MOE_TILE
    w_in_spec = pl.BlockSpec((None, None, D_MODEL, D_EXPERT), lambda i, te, nv: (layer, te[i], 0, 0))
    return pl.pallas_call(
        _ffn_kernel,
        grid_spec=pltpu.PrefetchScalarGridSpec(
            num_scalar_prefetch=2,
            grid=(total_tiles,),
            in_specs=[pl.BlockSpec((MOE_TILE, D_MODEL), lambda i, te, nv: (i, 0)),
                      pl.BlockSpec((MOE_TILE, LANES), lambda i, te, nv: (i, 0)),
                      w_in_spec, w_in_spec,
                      pl.BlockSpec((None, None, D_EXPERT, D_MODEL),
                                   lambda i, te, nv: (layer, te[i], 0, 0))],
            out_specs=pl.BlockSpec((MOE_TILE, D_MODEL), lambda i, te, nv: (i, 0))),
        out_shape=jax.ShapeDtypeStruct(xs.shape, BF16),
        compiler_params=_cparams(1),
        name="moe_ffn",
    )(tile_expert, n_valid, xs, gs, wg, wu, wd)


def _combine_kernel(ctl_ref, ctl_next_ref, x_ref, route_ref, g_ref, b_ref, ys_ref, o_ref,
                    yl_sc, sem, *, tt):
    i = pl.program_id(0)
    cur = i % 2

    def copies(ctl, half, start):
        for e in range(N_EXPERTS):
            n, loc_g, buf_g = ctl[0, e], ctl[1, e], ctl[2, e]
            _granule_copies(ys_ref, yl_sc.at[half], buf_g, loc_g, n, sem.at[half], start)

    @pl.when(i == 0)
    def _():
        yl_sc[...] = jnp.zeros_like(yl_sc)
        copies(ctl_ref, 0, True)

    @pl.when(i + 1 < pl.num_programs(0))
    def _():
        copies(ctl_next_ref, 1 - cur, True)

    copies(ctl_ref, cur, False)

    n_slots = yl_sc.shape[1]
    route = route_ref[...]
    slot = lax.broadcasted_iota(jnp.int32, (tt, n_slots), 1).astype(F32)
    pick = (slot == route[:, ROUTE_SLOT1:ROUTE_SLOT1 + 1]) | (slot == route[:, ROUTE_SLOT2:ROUTE_SLOT2 + 1])
    y = jnp.dot(jnp.where(pick, 1.0, 0.0).astype(BF16), yl_sc[cur], preferred_element_type=F32)
    o_ref[...] = _layernorm(ALPHA * x_ref[...] + y, g_ref[...], b_ref[...])


def _combine(x1, route, ctl, ys, layer, tt, g, b):
    n = x1.shape[0]
    last = n // tt - 1
    lay = lambda shape: pl.BlockSpec((None,) + shape, lambda i: (layer,) + (0,) * len(shape))
    return pl.pallas_call(
        functools.partial(_combine_kernel, tt=tt),
        grid=(n // tt,),
        in_specs=[pl.BlockSpec((None, SUBLANES, LANES), lambda i: (i, 0, 0), memory_space=pltpu.SMEM),
                  pl.BlockSpec((None, SUBLANES, LANES), lambda i: (jnp.minimum(i + 1, last), 0, 0),
                               memory_space=pltpu.SMEM),
                  pl.BlockSpec((tt, D_MODEL), lambda i: (i, 0)),
                  pl.BlockSpec((tt, LANES), lambda i: (i, 0)),
                  lay((1, D_MODEL)), lay((1, D_MODEL)),
                  pl.BlockSpec(memory_space=pl.ANY)],
        out_specs=pl.BlockSpec((tt, D_MODEL), lambda i: (i, 0)),
        scratch_shapes=[pltpu.VMEM((2, _local_slots(tt), D_MODEL), BF16),
                        pltpu.SemaphoreType.DMA((2,))],
        out_shape=jax.ShapeDtypeStruct((n, D_MODEL), F32),
        compiler_params=_cparams(1),
        name="moe_combine",
    )(ctl, ctl, x1, route, g, b, ys)


def _moe_sparse(x1, route, plan, gran, loff, wg, wu, wd, layer, tt, g, b):
    n = x1.shape[0]
    n_tiles = n // tt
    gran = gran[:, :, 0]
    loff = loff[:, :, 0]
    max_tiles = -(-(2 * n + (GRANULE - 1) * N_EXPERTS * n_tiles) // MOE_TILE) + N_EXPERTS
    rows = max_tiles * MOE_TILE
    tot_g = jnp.sum(gran, axis=0)
    tiles = (tot_g + GRANULES_PER_TILE - 1) // GRANULES_PER_TILE
    tile_end = jnp.cumsum(tiles)
    seg_g = ((tile_end - tiles) * GRANULES_PER_TILE).astype(jnp.int32)
    n_valid = tile_end[-1]
    idx = jnp.minimum(jnp.arange(max_tiles, dtype=jnp.int32), n_valid - 1)
    tile_expert = jnp.sum(idx[:, None] >= tile_end[None, :], axis=1).astype(jnp.int32)
    buf_g = seg_g[None, :] + jnp.cumsum(gran, axis=0) - gran
    ctl = jnp.zeros((n_tiles, SUBLANES, LANES), jnp.int32)
    ctl = ctl.at[:, 0, :N_EXPERTS].set(gran).at[:, 1, :N_EXPERTS].set(loff).at[:, 2, :N_EXPERTS].set(buf_g)
    xs, gs = _dispatch(x1, plan, ctl, seg_g, tot_g.astype(jnp.int32), tt, rows)
    ys = _ffn(xs, gs, tile_expert, n_valid.reshape(1).astype(jnp.int32), wg, wu, wd, layer)
    return _combine(x1, route, ctl, ys, layer, tt, g, b)


def _moe_kernel(x_ref, r_ref, wg_ref, wu_ref, wd_ref, g_ref, b_ref, o_ref, xb_sc, acc_sc):
    e = pl.program_id(1)

    @pl.when(e == 0)
    def _():
        xb_sc[...] = x_ref[...].astype(BF16)
        acc_sc[...] = jnp.zeros_like(acc_sc)

    xb = xb_sc[...]
    hg = jnp.dot(xb, wg_ref[...].astype(BF16), preferred_element_type=F32)
    hu = jnp.dot(xb, wu_ref[...].astype(BF16), preferred_element_type=F32)
    r = r_ref[...]
    lane = lax.broadcasted_iota(jnp.int32, r.shape, 1)
    c = jnp.sum(jnp.where(lane == e, r, 0.0), axis=-1, keepdims=True)
    h = _silu(hg) * hu * c
    acc_sc[...] += jnp.dot(h.astype(BF16), wd_ref[...].astype(BF16), preferred_element_type=F32)

    @pl.when(e == pl.num_programs(1) - 1)
    def _():
        o_ref[...] = _layernorm(ALPHA * x_ref[...] + acc_sc[...], g_ref[...], b_ref[...])


def _moe(x1, route, wg, wu, wd, layer, tm, g, b):
    n = x1.shape[0]
    lay = lambda shape: pl.BlockSpec((None,) + shape, lambda i, e: (layer,) + (0,) * len(shape))
    return pl.pallas_call(
        _moe_kernel,
        grid=(n // tm, N_EXPERTS),
        in_specs=[pl.BlockSpec((tm, D_MODEL), lambda i, e: (i, 0)),
                  pl.BlockSpec((tm, LANES), lambda i, e: (i, 0)),
                  pl.BlockSpec((None, None, D_MODEL, D_EXPERT), lambda i, e: (layer, e, 0, 0)),
                  pl.BlockSpec((None, None, D_MODEL, D_EXPERT), lambda i, e: (layer, e, 0, 0)),
                  pl.BlockSpec((None, None, D_EXPERT, D_MODEL), lambda i, e: (layer, e, 0, 0)),
                  lay((1, D_MODEL)), lay((1, D_MODEL))],
        out_specs=pl.BlockSpec((tm, D_MODEL), lambda i, e: (i, 0)),
        out_shape=jax.ShapeDtypeStruct((n, D_MODEL), F32),
        scratch_shapes=[pltpu.VMEM((tm, D_MODEL), BF16), pltpu.VMEM((tm, D_MODEL), F32)],
        compiler_params=_cparams(2),
        name="moe",
    )(x1, route, wg, wu, wd, g, b)


def _block_diag(w):
    out = jnp.zeros((DEPTH, D_A, D_A), w.dtype)
    for gi in range(A_BLOCKS):
        out = out.at[:, gi * A_BLK:(gi + 1) * A_BLK, gi * A_BLK:(gi + 1) * A_BLK].set(w[:, gi])
    return out


def kernel(x_prompt, x_sample, state_rglru_h, state_conv, state_hgrn, state_ret, w_in, conv_w, conv_b, w_rgate, b_rgate, w_igate, b_igate, rglru_lambda, hgrn_lb_logits, hgrn_norm_g, w_out, ln1_g, ln1_b, router_w, router_b, exp_w_gate, exp_w_up, exp_w_down, ln2_g, ln2_b):
    batch, seq, _ = x_prompt.shape
    nb = x_sample.shape[0]

    w_in_bf = w_in.astype(BF16)
    w_out_bf = w_out.astype(BF16)
    wg, wu, wd = exp_w_gate, exp_w_up, exp_w_down
    wri = jnp.concatenate([_block_diag(w_rgate), _block_diag(w_igate)], axis=-1).astype(BF16)
    bri = jnp.concatenate([b_rgate.reshape(DEPTH, 1, D_A), b_igate.reshape(DEPTH, 1, D_A)], axis=-1)
    cb = conv_b.reshape(DEPTH, 1, D_A)
    lam = rglru_lambda.reshape(DEPTH, 1, D_A)
    ng = jnp.tile(hgrn_norm_g, (1, B_HEADS)).reshape(DEPTH, 1, D_B)
    g1, b1 = ln1_g.reshape(DEPTH, 1, D_MODEL), ln1_b.reshape(DEPTH, 1, D_MODEL)
    g2, b2 = ln2_g.reshape(DEPTH, 1, D_MODEL), ln2_b.reshape(DEPTH, 1, D_MODEL)
    rwt = router_w.T
    rb = router_b.reshape(N_EXPERTS, 1)
    conv0_t = jnp.transpose(state_conv, (0, 2, 1, 3))
    hg0_t = jnp.transpose(state_hgrn, (0, 2, 3, 4, 1))

    xp = x_prompt.reshape(batch * seq, D_MODEL)
    xs = x_sample.reshape(nb, D_MODEL)
    hs_p, convs_p, hgs_p, rts_p = [], [], [], []
    prev_s, prev_hg = [], []
    for l in range(DEPTH):
        y_p, h_p, conv_p, hg_p, rt_p = _mix_prompt(
            xp, w_in_bf, batch, seq, l, 512, conv_w, cb, wri, bri, lam, hgrn_lb_logits, ng)
        x1_p, route_p, plan_p, gran_p, loff_p = _outproj(
            y_p, xp, w_out_bf, l, 512, g1, b1, rwt, rb, plan=True)
        xp = _moe_sparse(x1_p, route_p, plan_p, gran_p, loff_p, wg, wu, wd, l, 512, g2, b2)
        hs_p.append(h_p.reshape(batch, D_A))
        convs_p.append(conv_p)
        hgs_p.append(hg_p)
        rts_p.append(rt_p)

        proj_s = _proj(xs, w_in, l, nb)
        last = l == DEPTH - 1
        y_s, *states_s = _mix_sample(
            proj_s, state_rglru_h, conv0_t, state_ret, l, conv_w, cb, wri, bri, lam,
            prev=prev_s if last else ())
        prev_s.append(states_s)
        y_s, hg_s = _hgrn_sample(proj_s, y_s, hg0_t, l, hgrn_lb_logits, ng[:, :, :LANES],
                                 prev=prev_hg if last else ())
        prev_hg.append(hg_s)
        x1_s, route_s = _outproj(y_s, xs, w_out, l, nb, g1, b1, rwt, rb)
        xs = _moe(x1_s, route_s, wg, wu, wd, l, nb, g2, b2)

    h_s, conv_s, rt_s = prev_s[-1]
    hg_s = jnp.transpose(prev_hg[-1], (0, 4, 1, 2, 3))
    return (xp.reshape(batch, seq, D_MODEL), xs.reshape(nb, 1, D_MODEL),
            jnp.stack(hs_p), h_s, jnp.stack(convs_p), jnp.transpose(conv_s, (0, 2, 1, 3)),
            jnp.stack(hgs_p), hg_s, jnp.stack(rts_p), rt_s)
```

```python
import functools

import numpy as np
import jax
import jax.numpy as jnp
from jax import lax
from jax.experimental import pallas as pl
from jax.experimental.pallas import tpu as pltpu

D_MODEL = 1024
DEPTH = 2
PAST_LEN = 16384
D_A = 256
A_BLOCKS = 4
A_BLK = 64
CONV_W = 4
RGLRU_C = 8.0
B_HEADS = 4
B_DK = 64
D_B = 256
C_HEADS = 4
C_DK = 128
D_C = 512
D_IN = 3584
B_CHUNK = 64
C_CHUNK = 128
ROPE_BASE = 10000.0
N_EXPERTS = 16
N_GROUPS = 4
EXP_PER_GROUP = 4
D_EXPERT = 512
LN_EPS = 1e-5
RMS_EPS = 1e-6
GN_EPS = 1e-6
F_TINY = 1e-30
ALPHA = (2 * DEPTH) ** 0.25

O_XA, O_GA, O_QB, O_FB, O_VB, O_GB, O_QC, O_KC, O_VC, O_GC = (
    0, 256, 512, 768, 1024, 1280, 1536, 2048, 2560, 3072)

V7X_VMEM_LIMIT_BYTES = 56 * 1024 * 1024
SUBLANES = 8
LANES = 128
PROMPT_TILE = 512
HGRN_SAFE_MIN_LOGDECAY = -60.0

BF16 = jnp.bfloat16
F32 = jnp.float32
_NT = (((1,), (1,)), ((), ()))
_TN = (((0,), (0,)), ((), ()))


def _cparams(n_axes):
    return pltpu.CompilerParams(
        dimension_semantics=("arbitrary",) * n_axes,
        vmem_limit_bytes=V7X_VMEM_LIMIT_BYTES)


def _dot(a, b):
    return jnp.dot(a.astype(BF16), b.astype(BF16), preferred_element_type=F32)


def _dot_g(a, b, dims):
    return lax.dot_general(a.astype(BF16), b.astype(BF16), dims, preferred_element_type=F32)


def _sigmoid(x):
    return 0.5 * jnp.tanh(0.5 * x) + 0.5


def _sqrt_nonneg(x):
    return jnp.where(x > 0.0, x * lax.rsqrt(x), 0.0)


def _silu(x):
    return x * _sigmoid(x)


def _gelu_tanh(x):
    c = np.float32(np.sqrt(2.0 / np.pi))
    return 0.5 * x * (1.0 + jnp.tanh(c * (x + np.float32(0.044715) * (x * x * x))))


def _log_sigmoid(x):
    return -(jnp.maximum(-x, 0.0) + jnp.log(1.0 + jnp.exp(-jnp.abs(x))))


def _layernorm(z, g, b):
    mu = jnp.mean(z, axis=-1, keepdims=True)
    zc = z - mu
    var = jnp.mean(zc * zc, axis=-1, keepdims=True)
    return zc * lax.rsqrt(var + LN_EPS) * g + b


def _hgrn_lower_bound(lbl, layer):
    rows = [lbl[j:j + 1, :] for j in range(DEPTH)]
    m = rows[0]
    for r in rows[1:]:
        m = jnp.maximum(m, r)
    ex = [jnp.exp(r - m) for r in rows]
    tot = ex[0]
    for e in ex[1:]:
        tot = tot + e
    lb = jnp.zeros_like(m)
    for j in range(1, layer + 1):
        lb = lb + ex[j] / tot
    return lb


def _rglru_gates(u, wri, bri, lam):
    gates = _dot(u, wri) + bri
    r = _sigmoid(gates[:, :D_A])
    i = _sigmoid(gates[:, D_A:])
    log_a = RGLRU_C * r * _log_sigmoid(lam)
    a = jnp.exp(log_a)
    bterm = _sqrt_nonneg(jnp.maximum(1.0 - a * a, 0.0)) * (i * u)
    return a, bterm


def _split(x):
    hi = x.astype(BF16)
    return hi, (x - hi.astype(F32)).astype(BF16)


def _dot3(x, w):
    xh, xl = _split(x)
    wh, wl = _split(w)
    return (jnp.dot(xh, wh, preferred_element_type=F32)
            + jnp.dot(xl, wh, preferred_element_type=F32)
            + jnp.dot(xh, wl, preferred_element_type=F32))


def _proj_kernel(x_ref, w_ref, o_ref):
    x = x_ref[...]
    for j in range(0, D_IN, 512):
        o_ref[:, j:j + 512] = _dot3(x, w_ref[:, j:j + 512])


def _proj(x, w_in_bf, layer, tm):
    n = x.shape[0]
    return pl.pallas_call(
        _proj_kernel,
        grid=(n // tm,),
        in_specs=[pl.BlockSpec((tm, D_MODEL), lambda i: (i, 0)),
                  pl.BlockSpec((None, D_MODEL, D_IN), lambda i: (layer, 0, 0))],
        out_specs=pl.BlockSpec((tm, D_IN), lambda i: (i, 0)),
        out_shape=jax.ShapeDtypeStruct((n, D_IN), F32),
        compiler_params=_cparams(1),
        name="proj",
    )(x, w_in_bf)


def _retention_consts(chunk):
    lg = np.log1p(-np.exp2(-5.0 - np.arange(C_HEADS, dtype=np.float64)))
    idx = np.arange(chunk, dtype=np.float64)
    rel = idx[:, None] - idx[None, :]
    mask = rel >= 0
    dmat = np.where(mask[None], np.exp(np.where(mask, rel, 0.0)[None] * lg[:, None, None]), 0.0)
    qdec = np.exp((idx + 1.0)[None, :] * lg[:, None])
    kdec = np.exp((chunk - 1.0 - idx)[None, :] * lg[:, None])
    sdec = np.exp(chunk * lg)
    qdec_b = np.broadcast_to(qdec[:, :, None], (C_HEADS, chunk, C_DK))
    kdec_b = np.broadcast_to(kdec[:, :, None], (C_HEADS, chunk, C_DK))
    return (dmat.astype(np.float32), np.ascontiguousarray(qdec_b).astype(np.float32),
            np.ascontiguousarray(kdec_b).astype(np.float32), [float(v) for v in sdec])


def _rope_tables(positions):
    half = C_DK // 2
    inv = ROPE_BASE ** (-np.arange(half, dtype=np.float64) / half)
    ang = np.asarray(positions, dtype=np.float64)[:, None] * inv[None]
    cos = np.concatenate([np.cos(ang), np.cos(ang)], axis=-1)
    sin = np.concatenate([-np.sin(ang), np.sin(ang)], axis=-1)
    return cos.astype(np.float32), sin.astype(np.float32)


def _rope(x, cos, sin_signed):
    return x * cos + pltpu.roll(x, C_DK // 2, 1) * sin_signed


PROJ_SPLIT = 1536


def _mix_prompt_kernel(x0_ref, xn_ref, w_ref, cos_ref, sin_ref, cw_ref, cb_ref, wri_ref, bri_ref,
                       lam_ref, lbl_ref, ng_ref, dmat_ref, qdec_ref, kdec_ref,
                       y_ref, h_ref, conv_ref, hg_ref, rt_ref,
                       prev_sc, hprev_sc, st_sc, sret_sc, kb_sc, bb_sc, vb_sc, oi_sc, p_ref, next_ref, xb_sc, x0b_sc,
                       *, layer, tt, sdec):
    t = pl.program_id(1)
    nt = pl.num_programs(1)
    flat = pl.program_id(0) * nt + t

    def project(xb_ref, dst, c0, c1):
        xb = xb_ref[...]
        for j in range(c0, c1, 256):
            dst[:, j:j + 256] = jnp.dot(xb, w_ref[:, j:j + 256], preferred_element_type=F32)

    xb_sc[...] = xn_ref[...].astype(BF16)

    @pl.when(flat == 0)
    def _():
        x0b_sc[...] = x0_ref[...].astype(BF16)
        project(x0b_sc, p_ref, 0, D_IN)

    @pl.when(t == 0)
    def _():
        prev_sc[...] = jnp.zeros_like(prev_sc)
        hprev_sc[...] = jnp.zeros_like(hprev_sc)
        st_sc[...] = jnp.zeros_like(st_sc)
        sret_sc[...] = jnp.zeros_like(sret_sc)

    xa = p_ref[:, O_XA:O_XA + D_A]
    ga = p_ref[:, O_GA:O_GA + D_A]
    row = lax.broadcasted_iota(jnp.int32, (tt, D_A), 0)
    row8 = lax.broadcasted_iota(jnp.int32, (SUBLANES, D_A), 0)
    prev = prev_sc[...]

    def shifted(j):
        r = pltpu.roll(xa, j, 0)
        top = jnp.where(row8 < j, pltpu.roll(prev, j, 0), r[0:SUBLANES])
        return jnp.concatenate([top, r[SUBLANES:]], axis=0)

    u = cb_ref[...] + shifted(3) * cw_ref[0:1, :]
    u = u + shifted(2) * cw_ref[1:2, :]
    u = u + shifted(1) * cw_ref[2:3, :]
    u = u + xa * cw_ref[3:4, :]
    last8 = xa[tt - SUBLANES:tt]
    prev_sc[...] = last8

    a, bterm = _rglru_gates(u, wri_ref[...], bri_ref[...], lam_ref[...])
    project(xb_sc, next_ref, 0, PROJ_SPLIT)
    s = 1
    while s < SUBLANES:
        keep = (row % SUBLANES) >= s
        a_sh = jnp.where(keep, pltpu.roll(a, s, 0), 1.0)
        b_sh = jnp.where(keep, pltpu.roll(bterm, s, 0), 0.0)
        bterm = a * b_sh + bterm
        a = a * a_sh
        s *= 2
    carry = hprev_sc[...]
    groups = []
    for gi in range(tt // SUBLANES):
        rows = slice(gi * SUBLANES, (gi + 1) * SUBLANES)
        h_in = jnp.broadcast_to(carry[SUBLANES - 1:SUBLANES, :], (SUBLANES, D_A))
        carry = a[rows] * h_in + bterm[rows]
        groups.append(carry)
    h = jnp.concatenate(groups, axis=0)
    hlast8 = carry
    hprev_sc[...] = hlast8
    y_ref[:, 0:D_A] = h * _gelu_tanh(ga)

    lb = _hgrn_lower_bound(lbl_ref[...], layer)
    ng = ng_ref[...]
    cl = B_CHUNK
    crow = lax.broadcasted_iota(jnp.int32, (cl, D_B), 0)
    ccol = lax.broadcasted_iota(jnp.int32, (cl, D_B), 1)
    causal = (ccol % B_DK) <= crow
    br = lax.broadcasted_iota(jnp.int32, (D_B, D_B), 0)
    bc = lax.broadcasted_iota(jnp.int32, (D_B, D_B), 1)
    head_mask = (br // B_DK) == (bc // B_DK)
    seg = jnp.where(head_mask, 1.0, 0.0).astype(BF16)

    def seg_mean(x):
        hi = x.astype(BF16)
        lo = (x - hi.astype(F32)).astype(BF16)
        tot = (jnp.dot(hi, seg, preferred_element_type=F32)
               + jnp.dot(lo, seg, preferred_element_type=F32))
        return tot * (1.0 / B_DK)

    nck = tt // cl
    q = _silu(p_ref[:, O_QB:O_QB + D_B])
    fl = p_ref[:, O_FB:O_FB + D_B]
    v = p_ref[:, O_VB:O_VB + D_B]
    half_th = 0.5 * jnp.tanh(0.5 * fl)
    f = lb + (1.0 - lb) * (0.5 + half_th)
    b = jnp.log(jnp.maximum(f, F_TINY))
    k = (1.0 - lb) * (0.5 - half_th)
    trow = lax.broadcasted_iota(jnp.int32, (tt, D_B), 0) % cl
    sh = 1
    while sh < cl:
        b = b + jnp.where(trow >= sh, pltpu.roll(b, sh, 0), 0.0)
        sh *= 2

    def chunk_row(x, c, r):
        top = c * cl + r + 1
        return x[top - SUBLANES:top][SUBLANES - 1:SUBLANES]

    def spread(rows):
        return jnp.concatenate([jnp.broadcast_to(r, (cl, D_B)) for r in rows], axis=0)

    last_rows = [chunk_row(b, c, cl - 1) for c in range(nck)]
    mid_rows = [chunk_row(b, c, cl // 2 - 1) for c in range(nck)]
    b_last = spread(last_rows)
    b_mid = spread(mid_rows)
    qd = q * jnp.exp(b)
    kl = k * jnp.exp(b_last - b)
    lasts = jnp.concatenate(last_rows, axis=0)
    mids = jnp.concatenate(mid_rows, axis=0)
    safe = jnp.min(jnp.minimum(mids, lasts - mids)) >= HGRN_SAFE_MIN_LOGDECAY

    def next_state(st, c, sl):
        kv = jnp.where(head_mask, _dot_g(v[sl], kl[sl], _TN), 0.0)
        return st * jnp.exp(last_rows[c]) + kv

    @pl.when(safe)
    def _():
        qmid = q * jnp.exp(b - b_mid)
        kinv = k * jnp.exp(b_mid - b)
        st = st_sc[...]
        for c in range(nck):
            sl = slice(c * cl, (c + 1) * cl)
            k4 = jnp.where(head_mask, jnp.concatenate([kinv[sl]] * B_HEADS, axis=0), 0.0)
            sc = jnp.where(causal, _dot_g(qmid[sl], k4, _NT), 0.0)
            v4 = jnp.where(head_mask, jnp.concatenate([v[sl]] * B_HEADS, axis=0), 0.0)
            oi_sc[sl, :] = _dot(sc, v4) + _dot_g(qd[sl], st, _NT)
            st = next_state(st, c, sl)
        st_sc[...] = st

    @pl.when(jnp.logical_not(safe))
    def _():
        kb_sc[...] = k
        bb_sc[...] = b
        vb_sc[...] = v
        st = st_sc[...]
        for c in range(nck):
            sl = slice(c * cl, (c + 1) * cl)
            qc, bc_ = q[sl], b[sl]

            def pair(sidx, acc, c=c, qc=qc, bc_=bc_):
                src = pl.ds(c * cl + sidx, 1)
                e = jnp.exp(jnp.minimum(bc_ - bb_sc[src, :], 0.0)) * (qc * kb_sc[src, :])
                scr = jnp.dot(e.astype(BF16), seg, preferred_element_type=F32)
                return acc + jnp.where(crow >= sidx, scr, 0.0) * vb_sc[src, :]

            oi = lax.fori_loop(0, cl, pair, jnp.zeros((cl, D_B), F32))
            oi_sc[sl, :] = oi + _dot_g(qd[sl], st, _NT)
            st = next_state(st, c, sl)
        st_sc[...] = st

    o = oi_sc[...]
    o = o * lax.rsqrt(seg_mean(o * o) + RMS_EPS) * ng
    y_ref[:, D_A:D_A + D_B] = o * _silu(p_ref[:, O_GB:O_GB + D_B])

    rl = C_CHUNK
    per_head = (D_IN - PROJ_SPLIT) // C_HEADS
    cos = cos_ref[...]
    sin = sin_ref[...]
    for hh in range(C_HEADS):
        lo = hh * C_DK
        q = _rope(p_ref[:, O_QC + lo:O_QC + lo + C_DK], cos, sin)
        k = _rope(p_ref[:, O_KC + lo:O_KC + lo + C_DK], cos, sin) * (C_DK ** -0.5)
        v = p_ref[:, O_VC + lo:O_VC + lo + C_DK]
        s = sret_sc[hh]
        parts = []
        for c in range(tt // rl):
            sl = slice(c * rl, (c + 1) * rl)
            sc = _dot_g(q[sl], k[sl], _NT) * dmat_ref[hh]
            parts.append(_dot(sc, v[sl]) + _dot(q[sl] * qdec_ref[hh], s))
            s = sdec[hh] * s + _dot_g(k[sl] * kdec_ref[hh], v[sl], _TN)
        sret_sc[hh] = s
        project(xb_sc, next_ref, PROJ_SPLIT + hh * per_head, PROJ_SPLIT + (hh + 1) * per_head)
        o = jnp.concatenate(parts, axis=0)
        mu = jnp.mean(o, axis=-1, keepdims=True)
        oc = o - mu
        var = jnp.mean(oc * oc, axis=-1, keepdims=True)
        g = p_ref[:, O_GC + lo:O_GC + lo + C_DK]
        y_ref[:, D_A + D_B + lo:D_A + D_B + lo + C_DK] = oc * lax.rsqrt(var + GN_EPS) * _silu(g)

    p_ref[...] = next_ref[...]

    @pl.when(t == nt - 1)
    def _():
        h_ref[...] = pltpu.roll(hprev_sc[...], 1, 0)[0:1]
        conv_ref[...] = pltpu.roll(prev_sc[...], CONV_W - 1, 0)[0:CONV_W - 1]
        s_bd = st_sc[...].T
        for hh in range(B_HEADS):
            hg_ref[hh] = s_bd[hh * B_DK:(hh + 1) * B_DK, hh * B_DK:(hh + 1) * B_DK]
        rt_ref[...] = sret_sc[...]


def _mix_prompt(x, w_in_bf, batch, seq, layer, tt, cw, cb, wri, bri, lam, lbl, ng):
    nt = seq // tt
    last = batch * nt - 1
    dmat, qdec, kdec, sdec = _retention_consts(C_CHUNK)
    cos, sin = _rope_tables(np.arange(seq))
    full = lambda shape: pl.BlockSpec(shape, lambda b, t: (0,) * len(shape))
    lay = lambda shape: pl.BlockSpec((None,) + shape, lambda b, t: (layer,) + (0,) * len(shape))
    kern = functools.partial(_mix_prompt_kernel, layer=layer, tt=tt, sdec=sdec)
    return pl.pallas_call(
        kern,
        grid=(batch, nt),
        in_specs=[
            pl.BlockSpec((tt, D_MODEL), lambda b, t: (b * nt + t, 0)),
            pl.BlockSpec((tt, D_MODEL), lambda b, t: (jnp.minimum(b * nt + t + 1, last), 0)),
            lay((D_MODEL, D_IN)),
            pl.BlockSpec((tt, C_DK), lambda b, t: (t, 0)),
            pl.BlockSpec((tt, C_DK), lambda b, t: (t, 0)),
            lay((CONV_W, D_A)), lay((1, D_A)), lay((D_A, 2 * D_A)), lay((1, 2 * D_A)),
            lay((1, D_A)), full((DEPTH, D_B)), lay((1, D_B)),
            full((C_HEADS, C_CHUNK, C_CHUNK)), full((C_HEADS, C_CHUNK, C_DK)),
            full((C_HEADS, C_CHUNK, C_DK)),
        ],
        out_specs=[
            pl.BlockSpec((tt, D_MODEL), lambda b, t: (b * nt + t, 0)),
            pl.BlockSpec((None, 1, D_A), lambda b, t: (b, 0, 0)),
            pl.BlockSpec((None, CONV_W - 1, D_A), lambda b, t: (b, 0, 0)),
            pl.BlockSpec((None, B_HEADS, B_DK, B_DK), lambda b, t: (b, 0, 0, 0)),
            pl.BlockSpec((None, C_HEADS, C_DK, C_DK), lambda b, t: (b, 0, 0, 0)),
        ],
        out_shape=[
            jax.ShapeDtypeStruct((batch * seq, D_MODEL), F32),
            jax.ShapeDtypeStruct((batch, 1, D_A), F32),
            jax.ShapeDtypeStruct((batch, CONV_W - 1, D_A), F32),
            jax.ShapeDtypeStruct((batch, B_HEADS, B_DK, B_DK), F32),
            jax.ShapeDtypeStruct((batch, C_HEADS, C_DK, C_DK), F32),
        ],
        scratch_shapes=[
            pltpu.VMEM((SUBLANES, D_A), F32), pltpu.VMEM((SUBLANES, D_A), F32),
            pltpu.VMEM((D_B, D_B), F32), pltpu.VMEM((C_HEADS, C_DK, C_DK), F32),
            pltpu.VMEM((tt, D_B), F32), pltpu.VMEM((tt, D_B), F32),
            pltpu.VMEM((tt, D_B), F32), pltpu.VMEM((tt, D_B), F32),
            pltpu.VMEM((tt, D_IN), F32), pltpu.VMEM((tt, D_IN), F32),
            pltpu.VMEM((tt, D_MODEL), BF16), pltpu.VMEM((tt, D_MODEL), BF16),
        ],
        compiler_params=_cparams(2),
        name="mix_prompt",
    )(x, x, w_in_bf, cos, sin, cw, cb, wri, bri, lam, lbl, ng, dmat, qdec, kdec)


def _column_matrix(x):
    pad = jnp.zeros((LANES - SUBLANES, LANES), F32)
    return jnp.concatenate([x, pad], axis=0).T


def _mix_sample_kernel(*refs, layer, gammas, n_prev):
    (p_ref, h0_ref, conv0_ref, rt0_ref, cos_ref, sin_ref,
     cw_ref, cb_ref, wri_ref, bri_ref, lam_ref) = refs[:11]
    prev = refs[11:11 + 3 * n_prev]
    y_ref, h_ref, conv_ref, rt_ref, o_sc = refs[11 + 3 * n_prev:]
    if n_prev:
        for j in range(n_prev):
            for dst, src in zip((h_ref, conv_ref, rt_ref), prev[3 * j:3 * j + 3]):
                dst[j] = src[...]
        h_ref, conv_ref, rt_ref = (r.at[layer] for r in (h_ref, conv_ref, rt_ref))
    tb = SUBLANES
    xa = p_ref[:, O_XA:O_XA + D_A]
    ga = p_ref[:, O_GA:O_GA + D_A]
    c0, c1, c2 = conv0_ref[0], conv0_ref[1], conv0_ref[2]
    u = cb_ref[...] + c0 * cw_ref[0:1, :]
    u = u + c1 * cw_ref[1:2, :]
    u = u + c2 * cw_ref[2:3, :]
    u = u + xa * cw_ref[3:4, :]
    conv_ref[0] = c1
    conv_ref[1] = c2
    conv_ref[2] = xa
    a, bterm = _rglru_gates(u, wri_ref[...], bri_ref[...], lam_ref[...])
    h = a * h0_ref[...] + bterm
    h_ref[...] = h
    y_ref[:, 0:D_A] = h * _gelu_tanh(ga)

    y_ref[:, D_A:D_A + D_B] = jnp.zeros((tb, D_B), F32)

    cos = cos_ref[...]
    sin = sin_ref[...]
    for head in range(C_HEADS):
        lo = head * C_DK
        gamma = gammas[head]
        q = _rope(p_ref[:, O_QC + lo:O_QC + lo + C_DK], cos, sin)
        k = _rope(p_ref[:, O_KC + lo:O_KC + lo + C_DK], cos, sin) * (C_DK ** -0.5)
        v = p_ref[:, O_VC + lo:O_VC + lo + C_DK]
        g = p_ref[:, O_GC + lo:O_GC + lo + C_DK]
        dots = jnp.sum(q * k, axis=-1, keepdims=True)
        q_cols = _column_matrix(q * gamma)
        k_cols = _column_matrix(k)
        for b in range(tb):
            s_old = rt0_ref[b, head]
            v_row = v[b:b + 1, :]
            rt_ref[b, head] = gamma * s_old + k_cols[:, b:b + 1] * v_row
            o_row = (jnp.sum(q_cols[:, b:b + 1] * s_old, axis=0, keepdims=True)
                     + dots[b:b + 1, :] * v_row)
            o_sc[b:b + 1, 0:C_DK] = o_row
        o = o_sc[:, 0:C_DK]
        mu = jnp.mean(o, axis=-1, keepdims=True)
        oc = o - mu
        var = jnp.mean(oc * oc, axis=-1, keepdims=True)
        y_ref[:, D_A + D_B + lo:D_A + D_B + lo + C_DK] = oc * lax.rsqrt(var + GN_EPS) * _silu(g)


def _mix_sample(proj, h0, conv0_t, rt0, layer, cw, cb, wri, bri, lam, prev=()):
    nb = proj.shape[0]
    tb = SUBLANES
    lg = np.log1p(-np.exp2(-5.0 - np.arange(C_HEADS, dtype=np.float64)))
    gammas = [float(np.exp(v)) for v in lg]
    cos, sin = _rope_tables([PAST_LEN])
    full = lambda shape: pl.BlockSpec(shape, lambda i: (0,) * len(shape))
    lay = lambda shape: pl.BlockSpec((None,) + shape, lambda i: (layer,) + (0,) * len(shape))
    state_shapes = [(tb, D_A), (CONV_W - 1, tb, D_A), (tb, C_HEADS, C_DK, C_DK)]
    state_maps = [lambda i: (i, 0), lambda i: (0, i, 0), lambda i: (i, 0, 0, 0)]
    full_shapes = [(nb, D_A), (CONV_W - 1, nb, D_A), (nb, C_HEADS, C_DK, C_DK)]
    state_specs = [pl.BlockSpec(s, m) for s, m in zip(state_shapes, state_maps)]
    if prev:
        stack = len(prev) + 1
        out_state_specs = [pl.BlockSpec((stack,) + s, lambda i, m=m: (0,) + m(i))
                           for s, m in zip(state_shapes, state_maps)]
        out_state_shapes = [jax.ShapeDtypeStruct((stack,) + s, F32) for s in full_shapes]
    else:
        out_state_specs = state_specs
        out_state_shapes = [jax.ShapeDtypeStruct(s, F32) for s in full_shapes]
    kern = functools.partial(_mix_sample_kernel, layer=layer, gammas=gammas, n_prev=len(prev))
    return pl.pallas_call(
        kern,
        grid=(nb // tb,),
        in_specs=[
            pl.BlockSpec((tb, D_IN), lambda i: (i, 0)),
            pl.BlockSpec((None, tb, D_A), lambda i: (layer, i, 0)),
            pl.BlockSpec((None, CONV_W - 1, tb, D_A), lambda i: (layer, 0, i, 0)),
            pl.BlockSpec((None, tb, C_HEADS, C_DK, C_DK), lambda i: (layer, i, 0, 0, 0)),
            full((1, C_DK)), full((1, C_DK)),
            lay((CONV_W, D_A)), lay((1, D_A)), lay((D_A, 2 * D_A)), lay((1, 2 * D_A)),
            lay((1, D_A)),
        ] + state_specs * len(prev),
        out_specs=[pl.BlockSpec((tb, D_MODEL), lambda i: (i, 0))] + out_state_specs,
        out_shape=[jax.ShapeDtypeStruct((nb, D_MODEL), F32)] + out_state_shapes,
        scratch_shapes=[pltpu.VMEM((tb, D_B), F32)],
        compiler_params=_cparams(1),
        name="mix_sample",
    )(proj, h0, conv0_t, rt0, cos, sin, cw, cb, wri, bri, lam,
      *[a for states in prev for a in states])


def _hgrn_sample_kernel(*refs, layer, n_prev):
    qb_ref, fb_ref, vb_ref, gb_ref, lbl_ref, ng_ref, s0_ref, y_in_ref = refs[:8]
    prev = refs[8:8 + n_prev]
    y_ref, s_ref = refs[8 + n_prev:]
    del y_in_ref
    for j in range(n_prev):
        s_ref[j] = prev[j][...]
    if n_prev:
        s_ref = s_ref.at[layer]
    lb = _hgrn_lower_bound(lbl_ref[...], layer)
    q = _silu(qb_ref[...])
    fl = fb_ref[...]
    v = vb_ref[...]
    half_th = 0.5 * jnp.tanh(0.5 * fl)
    f = lb + (1.0 - lb) * (0.5 + half_th)
    ef = jnp.exp(jnp.log(jnp.maximum(f, F_TINY)))
    k = (1.0 - lb) * (0.5 - half_th)
    qf_t, k_t, ef_t, v_t, qk_t = (x.T for x in (q * ef, k, ef, v, q * k))
    o_parts = []
    for hh in range(2):
        base = hh * B_DK
        v_h = v_t[base:base + B_DK]
        acc = jnp.zeros((B_DK, LANES), F32)
        for kk in range(B_DK):
            r = base + kk
            s_old = s0_ref[hh, kk]
            s_ref[hh, kk] = ef_t[r:r + 1] * s_old + k_t[r:r + 1] * v_h
            acc = acc + qf_t[r:r + 1] * s_old
        dots = jnp.sum(qk_t[base:base + B_DK], axis=0, keepdims=True)
        o = acc + dots * v_h
        o_parts.append(o * lax.rsqrt(jnp.mean(o * o, axis=0, keepdims=True) + RMS_EPS))
    o = jnp.concatenate(o_parts, axis=0).T
    y_ref[...] = o * ng_ref[...] * _silu(gb_ref[...])


def _hgrn_sample(proj, y, s0_t, layer, lbl, ng_pair, prev=()):
    nb = proj.shape[0]
    col = lambda off: pl.BlockSpec((nb, LANES), lambda j: (0, off // LANES + j))
    state_block = (2, B_DK, B_DK, nb)
    state_spec = pl.BlockSpec(state_block, lambda j: (j, 0, 0, 0))
    stack = len(prev) + 1
    if prev:
        out_state_spec = pl.BlockSpec((stack,) + state_block, lambda j: (0, j, 0, 0, 0))
        out_state_shape = jax.ShapeDtypeStruct((stack, B_HEADS, B_DK, B_DK, nb), F32)
    else:
        out_state_spec = state_spec
        out_state_shape = jax.ShapeDtypeStruct((B_HEADS, B_DK, B_DK, nb), F32)
    return pl.pallas_call(
        functools.partial(_hgrn_sample_kernel, layer=layer, n_prev=len(prev)),
        grid=(B_HEADS // 2,),
        in_specs=[col(O_QB), col(O_FB), col(O_VB), col(O_GB),
                  pl.BlockSpec((DEPTH, LANES), lambda j: (0, j)),
                  pl.BlockSpec((None, 1, LANES), lambda j: (layer, 0, 0)),
                  pl.BlockSpec((None,) + state_block, lambda j: (layer, j, 0, 0, 0)),
                  pl.BlockSpec(memory_space=pl.ANY)] + [state_spec] * len(prev),
        out_specs=[col(D_A), out_state_spec],
        out_shape=[jax.ShapeDtypeStruct(y.shape, F32), out_state_shape],
        input_output_aliases={7: 0},
        compiler_params=_cparams(1),
        name="hgrn_sample",
    )(proj, proj, proj, proj, lbl, ng_pair, s0_t, y, *prev)


def _route_rows(l):
    m = l[0]
    for x in l[1:]:
        m = jnp.maximum(m, x)
    ex = [jnp.exp(x - m) for x in l]
    tot = ex[0]
    for x in ex[1:]:
        tot = tot + x
    p = [x / tot for x in ex]
    scores = []
    for gi in range(N_GROUPS):
        a, b, c, d = p[4 * gi:4 * gi + 4]
        hi1, lo1 = jnp.maximum(a, b), jnp.minimum(a, b)
        hi2, lo2 = jnp.maximum(c, d), jnp.minimum(c, d)
        top1 = jnp.maximum(hi1, hi2)
        top2 = jnp.maximum(jnp.minimum(hi1, hi2), jnp.maximum(lo1, lo2))
        scores.append(top1 + top2)
    best = scores[0]
    gsel = jnp.zeros_like(best, dtype=jnp.int32)
    for gi in range(1, N_GROUPS):
        upd = scores[gi] > best
        gsel = jnp.where(upd, gi, gsel)
        best = jnp.where(upd, scores[gi], best)
    vals = []
    for j in range(EXP_PER_GROUP):
        v = p[j]
        for gi in range(1, N_GROUPS):
            v = jnp.where(gsel == gi, p[4 * gi + j], v)
        vals.append(v)
    sel = []
    for j in range(EXP_PER_GROUP):
        rank = jnp.zeros_like(gsel)
        for i in range(EXP_PER_GROUP):
            if i == j:
                continue
            ahead = (vals[i] > vals[j]) | ((vals[i] == vals[j]) & (i < j))
            rank = rank + jnp.where(ahead, 1, 0)
        sel.append(rank < 2)
    denom = jnp.zeros_like(best)
    for j in range(EXP_PER_GROUP):
        denom = denom + jnp.where(sel[j], vals[j], 0.0)
    gates = [jnp.where(sel[j], vals[j] / denom, 0.0) for j in range(EXP_PER_GROUP)]
    comb = [jnp.where(gsel == (e // EXP_PER_GROUP), gates[e % EXP_PER_GROUP], 0.0)
            for e in range(N_EXPERTS)]
    j1 = jnp.where(sel[0], 0, jnp.where(sel[1], 1, 2))
    j2 = jnp.where(sel[3], 3, jnp.where(sel[2], 2, 1))
    g1 = jnp.zeros_like(best)
    g2 = jnp.zeros_like(best)
    for j in range(EXP_PER_GROUP):
        g1 = jnp.where(j1 == j, gates[j], g1)
        g2 = jnp.where(j2 == j, gates[j], g2)
    return comb, gsel * EXP_PER_GROUP + j1, gsel * EXP_PER_GROUP + j2, g1, g2


MOE_TILE = 512
GRANULE = 16
GRANULES_PER_TILE = MOE_TILE // GRANULE
XS_WIDTH = D_MODEL + LANES
LARGE_PIECE = 8
ROUTE_SLOT1, ROUTE_SLOT2 = N_EXPERTS, N_EXPERTS + 1
PLAN_SLOT1, PLAN_SLOT2, PLAN_G1, PLAN_G2 = 0, 1, 2, 3


def _local_slots(tt):
    worst = 2 * tt + (N_EXPERTS - 1) * GRANULE
    return -(-worst // LANES) * LANES


def _outproj_kernel(*refs, plan, sub, st):
    if plan:
        (y_ref, x_ref, w_ref, g_ref, b_ref, rwt_ref, rb_ref, tri_ref, ltri_ref,
         x1_ref, route_ref, plan_ref, gran_ref, loff_ref, rt_sc, oh_sc) = refs
    else:
        y_ref, x_ref, w_ref, g_ref, b_ref, rwt_ref, rb_ref, x1_ref, route_ref, rt_sc = refs
    rhi, rlo = _split(rwt_ref[...])
    for s in range(sub):
        rows = slice(s * st, (s + 1) * st)
        if w_ref.dtype == BF16:
            y = jnp.dot(y_ref[rows, :].astype(BF16), w_ref[...], preferred_element_type=F32)
        else:
            y = _dot3(y_ref[rows, :], w_ref[...])
        x1 = _layernorm(ALPHA * x_ref[rows, :] + y, g_ref[...], b_ref[...])
        x1_ref[rows, :] = x1
        hi, lo = _split(x1)
        lg = (lax.dot_general(rhi, hi, _NT, preferred_element_type=F32)
              + lax.dot_general(rhi, lo, _NT, preferred_element_type=F32)
              + lax.dot_general(rlo, hi, _NT, preferred_element_type=F32)) + rb_ref[...]
        comb, e1, e2, g1, g2 = _route_rows([lg[e:e + 1, :] for e in range(N_EXPERTS)])
        rt = rt_sc.at[s]
        rt[...] = jnp.zeros_like(rt)
        if not plan:
            for e in range(N_EXPERTS):
                rt[e:e + 1, :] = comb[e]
        else:
            oh_ref = oh_sc.at[s]
            for e in range(N_EXPERTS):
                oh_ref[e:e + 1, :] = jnp.where((e1 == e) | (e2 == e), 1.0, 0.0)
            oh = oh_ref[...]
            cum = jnp.dot(oh.astype(BF16), tri_ref[...], preferred_element_type=F32)
            count = jnp.sum(oh, axis=1, keepdims=True)
            gran = jnp.floor((count + (GRANULE - 1)) * (1.0 / GRANULE))
            gran_b = jnp.broadcast_to(gran, (N_EXPERTS, LANES))
            loff_b = jnp.dot(ltri_ref[...], gran_b.astype(BF16), preferred_element_type=F32)
            slot_base = loff_b[:, 0:1] * GRANULE - 1.0 + cum
            s1 = jnp.zeros_like(g1)
            s2 = jnp.zeros_like(g1)
            for e in range(N_EXPERTS):
                row = slot_base[e:e + 1, :]
                s1 = jnp.where(e1 == e, row, s1)
                s2 = jnp.where(e2 == e, row, s2)
            plan_ref[s] = jnp.zeros((SUBLANES, st), F32)
            plan_ref[s, PLAN_SLOT1:PLAN_SLOT1 + 1, :] = s1
            plan_ref[s, PLAN_SLOT2:PLAN_SLOT2 + 1, :] = s2
            plan_ref[s, PLAN_G1:PLAN_G1 + 1, :] = g1
            plan_ref[s, PLAN_G2:PLAN_G2 + 1, :] = g2
            gran_ref[s] = gran_b.astype(jnp.int32)
            loff_ref[s] = loff_b.astype(jnp.int32)
            rt[ROUTE_SLOT1:ROUTE_SLOT1 + 1, :] = s1
            rt[ROUTE_SLOT2:ROUTE_SLOT2 + 1, :] = s2
        route_ref[rows, :] = rt[...].T


def _outproj(y, x, w_out_bf, layer, st, g, b, rwt, rb, plan=False, sub=1):
    n = x.shape[0]
    tm = st * sub
    full = lambda shape: pl.BlockSpec(shape, lambda i: (0,) * len(shape))
    lay = lambda shape: pl.BlockSpec((None,) + shape, lambda i: (layer,) + (0,) * len(shape))
    in_specs = [pl.BlockSpec((tm, D_MODEL), lambda i: (i, 0)),
                pl.BlockSpec((tm, D_MODEL), lambda i: (i, 0)),
                lay((D_MODEL, D_MODEL)), lay((1, D_MODEL)), lay((1, D_MODEL)),
                full((N_EXPERTS, D_MODEL)), full((N_EXPERTS, 1))]
    out_specs = [pl.BlockSpec((tm, D_MODEL), lambda i: (i, 0)),
                 pl.BlockSpec((tm, LANES), lambda i: (i, 0))]
    out_shape = [jax.ShapeDtypeStruct((n, D_MODEL), F32),
                 jax.ShapeDtypeStruct((n, LANES), F32)]
    scratch = [pltpu.VMEM((sub, LANES, st), F32)]
    args = [y, x, w_out_bf, g, b, rwt, rb]
    if plan:
        tri = np.triu(np.ones((st, st), np.float32)).astype(jnp.bfloat16)
        ltri = np.tril(np.ones((N_EXPERTS, N_EXPERTS), np.float32), -1).astype(jnp.bfloat16)
        in_specs += [full((st, st)), full((N_EXPERTS, N_EXPERTS))]
        args += [tri, ltri]
        per_tile = pl.BlockSpec((sub, N_EXPERTS, LANES), lambda i: (i, 0, 0))
        out_specs += [pl.BlockSpec((sub, SUBLANES, st), lambda i: (i, 0, 0)), per_tile, per_tile]
        out_shape += [jax.ShapeDtypeStruct((n // st, SUBLANES, st), F32),
                      jax.ShapeDtypeStruct((n // st, N_EXPERTS, LANES), jnp.int32),
                      jax.ShapeDtypeStruct((n // st, N_EXPERTS, LANES), jnp.int32)]
        scratch += [pltpu.VMEM((sub, N_EXPERTS, st), F32)]
    return pl.pallas_call(
        functools.partial(_outproj_kernel, plan=plan, sub=sub, st=st),
        grid=(n // tm,),
        in_specs=in_specs,
        out_specs=out_specs,
        out_shape=out_shape,
        scratch_shapes=scratch,
        compiler_params=_cparams(1),
        name="outproj_plan" if plan else "outproj",
    )(*args)


def _granule_copies(src, dst, src_g, dst_g, n, sem, start, max_granules):
    def pieces(sizes, src_g, dst_g):
        for size in sizes:
            bit = n & size

            @pl.when(bit != 0)
            def _(size=size, src_g=src_g, dst_g=dst_g):
                s0 = pl.multiple_of(src_g * GRANULE, GRANULE)
                d0 = pl.multiple_of(dst_g * GRANULE, GRANULE)
                cp = pltpu.make_async_copy(src.at[pl.ds(s0, size * GRANULE)],
                                           dst.at[pl.ds(d0, size * GRANULE)], sem)
                cp.start() if start else cp.wait()

            src_g = src_g + bit
            dst_g = dst_g + bit

    sizes = [max_granules >> k for k in range(max_granules.bit_length())]
    large = [s for s in sizes if s >= LARGE_PIECE]
    small = [s for s in sizes if s < LARGE_PIECE]

    @pl.when(n >= LARGE_PIECE)
    def _():
        pieces(large, src_g, dst_g)

    done = n & sum(large)
    pieces(small, src_g + done, dst_g + done)


def _dispatch_kernel(segg_ref, totg_ref, ctl_ref, ctl_prev_ref, plan_ref, x_ref, xs_ref,
                     xl_sc, zx_sc, sem, *, tt, total_tiles):
    step = pl.program_id(0)
    cur = step % 2

    @pl.when(step == 0)
    def _():
        zx_sc[...] = jnp.zeros_like(zx_sc)

        def zero_tile(i):
            r0 = pl.multiple_of(i * MOE_TILE, MOE_TILE)
            return pltpu.make_async_copy(zx_sc, xs_ref.at[pl.ds(r0, MOE_TILE)], sem.at[2])

        def run(i, start):
            cp = zero_tile(i)
            cp.start() if start else cp.wait()

        for start in (True, False):
            for e in range(N_EXPERTS):
                end = segg_ref[e] + totg_ref[e]

                @pl.when(end % GRANULES_PER_TILE != 0)
                def _():
                    run(end // GRANULES_PER_TILE, start)

        last = N_EXPERTS - 1
        used = (segg_ref[last] + totg_ref[last] + GRANULES_PER_TILE - 1) // GRANULES_PER_TILE
        lax.fori_loop(used, total_tiles, lambda i, c: (run(i, True), c)[1], 0)
        lax.fori_loop(used, total_tiles, lambda i, c: (run(i, False), c)[1], 0)

    n_slots = xl_sc.shape[1]
    xl = xl_sc.at[cur]
    slot = lax.broadcasted_iota(jnp.int32, (n_slots, tt), 0).astype(F32)
    m1 = slot == plan_ref[PLAN_SLOT1:PLAN_SLOT1 + 1, :]
    m2 = slot == plan_ref[PLAN_SLOT2:PLAN_SLOT2 + 1, :]
    perm = jnp.where(m1 | m2, 1.0, 0.0).astype(BF16)
    xl[:, :D_MODEL] = jnp.dot(
        perm, x_ref[...].astype(BF16), preferred_element_type=F32).astype(BF16)
    gate = jnp.sum(jnp.where(m1, plan_ref[PLAN_G1:PLAN_G1 + 1, :], 0.0)
                   + jnp.where(m2, plan_ref[PLAN_G2:PLAN_G2 + 1, :], 0.0), axis=1, keepdims=True)
    g_hi = gate.astype(BF16).astype(F32)
    g_lo = (gate - g_hi).astype(BF16).astype(F32)
    lane = lax.broadcasted_iota(jnp.int32, (n_slots, LANES), 1)
    xl[:, D_MODEL:] = jnp.where(lane == 0, g_hi, jnp.where(lane == 1, g_lo, 0.0)).astype(BF16)

    def copies(ctl, half, start):
        for e in range(N_EXPERTS):
            n, src_g, dst_g = ctl[0, e], ctl[1, e], ctl[2, e]
            _granule_copies(xl_sc.at[half], xs_ref, src_g, dst_g, n, sem.at[half], start,
                            tt // GRANULE)

    copies(ctl_ref, cur, True)

    @pl.when(step > 0)
    def _():
        copies(ctl_prev_ref, 1 - cur, False)

    @pl.when(step == pl.num_programs(0) - 1)
    def _():
        copies(ctl_ref, cur, False)


def _dispatch(x1, plan, ctl, seg_g, tot_g, tt, rows):
    n = x1.shape[0]
    n_slots = _local_slots(tt)
    return pl.pallas_call(
        functools.partial(_dispatch_kernel, tt=tt, total_tiles=rows // MOE_TILE),
        grid_spec=pltpu.PrefetchScalarGridSpec(
            num_scalar_prefetch=2,
            grid=(n // tt,),
            in_specs=[pl.BlockSpec((None, SUBLANES, LANES), lambda i, s, c: (i, 0, 0),
                                   memory_space=pltpu.SMEM),
                      pl.BlockSpec((None, SUBLANES, LANES), lambda i, s, c: (jnp.maximum(i - 1, 0), 0, 0),
                                   memory_space=pltpu.SMEM),
                      pl.BlockSpec((None, SUBLANES, tt), lambda i, s, c: (i, 0, 0)),
                      pl.BlockSpec((tt, D_MODEL), lambda i, s, c: (i, 0))],
            out_specs=pl.BlockSpec(memory_space=pl.ANY),
            scratch_shapes=[pltpu.VMEM((2, n_slots, XS_WIDTH), BF16),
                            pltpu.VMEM((MOE_TILE, XS_WIDTH), BF16),
                            pltpu.SemaphoreType.DMA((3,))]),
        out_shape=jax.ShapeDtypeStruct((rows, XS_WIDTH), BF16),
        compiler_params=_cparams(1),
        name="moe_dispatch",
    )(seg_g, tot_g, ctl, ctl, plan, x1)


def _ffn_kernel(te_ref, nv_ref, xs_ref, wg_ref, wu_ref, wd_ref, ys_ref):
    used = pl.program_id(0) < nv_ref[0]

    @pl.when(used)
    def _():
        xb = xs_ref[:, :D_MODEL]
        gate_pair = xs_ref[:, D_MODEL:].astype(F32)
        gate = gate_pair[:, 0:1] + gate_pair[:, 1:2]
        hg = jnp.dot(xb, wg_ref[...].astype(BF16), preferred_element_type=F32)
        hu = jnp.dot(xb, wu_ref[...].astype(BF16), preferred_element_type=F32)
        h = (_silu(hg) * hu * gate).astype(BF16)
        ys_ref[...] = jnp.dot(h, wd_ref[...].astype(BF16), preferred_element_type=F32).astype(BF16)

    @pl.when(jnp.logical_not(used))
    def _():
        ys_ref[...] = jnp.zeros_like(ys_ref)


def _ffn(xs, tile_expert, n_valid, wg, wu, wd, layer):
    total_tiles = xs.shape[0] // MOE_TILE
    w_in_spec = pl.BlockSpec((None, None, D_MODEL, D_EXPERT), lambda i, te, nv: (layer, te[i], 0, 0))
    return pl.pallas_call(
        _ffn_kernel,
        grid_spec=pltpu.PrefetchScalarGridSpec(
            num_scalar_prefetch=2,
            grid=(total_tiles,),
            in_specs=[pl.BlockSpec((MOE_TILE, XS_WIDTH), lambda i, te, nv: (jnp.minimum(i, nv[0] - 1), 0)),
                      w_in_spec, w_in_spec,
                      pl.BlockSpec((None, None, D_EXPERT, D_MODEL),
                                   lambda i, te, nv: (layer, te[i], 0, 0))],
            out_specs=pl.BlockSpec((MOE_TILE, D_MODEL), lambda i, te, nv: (i, 0))),
        out_shape=jax.ShapeDtypeStruct((xs.shape[0], D_MODEL), BF16),
        compiler_params=_cparams(1),
        name="moe_ffn",
    )(tile_expert, n_valid, xs, wg, wu, wd)


def _combine_kernel(ctl_ref, ctl_next_ref, x_ref, route_ref, g_ref, b_ref, ys_ref, o_ref,
                    yl_sc, sem, *, tt):
    i = pl.program_id(0)
    cur = i % 2

    def copies(ctl, half, start):
        for e in range(N_EXPERTS):
            n, loc_g, buf_g = ctl[0, e], ctl[1, e], ctl[2, e]
            _granule_copies(ys_ref, yl_sc.at[half], buf_g, loc_g, n, sem.at[half], start,
                            tt // GRANULE)

    @pl.when(i == 0)
    def _():
        yl_sc[...] = jnp.zeros_like(yl_sc)
        copies(ctl_ref, 0, True)

    @pl.when(i + 1 < pl.num_programs(0))
    def _():
        copies(ctl_next_ref, 1 - cur, True)

    copies(ctl_ref, cur, False)

    n_slots = yl_sc.shape[1]
    route = route_ref[...]
    slot = lax.broadcasted_iota(jnp.int32, (tt, n_slots), 1).astype(F32)
    pick = (slot == route[:, ROUTE_SLOT1:ROUTE_SLOT1 + 1]) | (slot == route[:, ROUTE_SLOT2:ROUTE_SLOT2 + 1])
    y = jnp.dot(jnp.where(pick, 1.0, 0.0).astype(BF16), yl_sc[cur], preferred_element_type=F32)
    o_ref[...] = _layernorm(ALPHA * x_ref[...] + y, g_ref[...], b_ref[...])


def _combine(x1, route, ctl, ys, layer, tt, g, b):
    n = x1.shape[0]
    last = n // tt - 1
    lay = lambda shape: pl.BlockSpec((None,) + shape, lambda i: (layer,) + (0,) * len(shape))
    return pl.pallas_call(
        functools.partial(_combine_kernel, tt=tt),
        grid=(n // tt,),
        in_specs=[pl.BlockSpec((None, SUBLANES, LANES), lambda i: (i, 0, 0), memory_space=pltpu.SMEM),
                  pl.BlockSpec((None, SUBLANES, LANES), lambda i: (jnp.minimum(i + 1, last), 0, 0),
                               memory_space=pltpu.SMEM),
                  pl.BlockSpec((tt, D_MODEL), lambda i: (i, 0)),
                  pl.BlockSpec((tt, LANES), lambda i: (i, 0)),
                  lay((1, D_MODEL)), lay((1, D_MODEL)),
                  pl.BlockSpec(memory_space=pl.ANY)],
        out_specs=pl.BlockSpec((tt, D_MODEL), lambda i: (i, 0)),
        scratch_shapes=[pltpu.VMEM((2, _local_slots(tt), D_MODEL), BF16),
                        pltpu.SemaphoreType.DMA((2,))],
        out_shape=jax.ShapeDtypeStruct((n, D_MODEL), F32),
        compiler_params=_cparams(1),
        name="moe_combine",
    )(ctl, ctl, x1, route, g, b, ys)


def _moe_sparse(x1, route, plan, gran, loff, wg, wu, wd, layer, tt, g, b):
    n = x1.shape[0]
    n_tiles = n // tt
    gran = gran[:, :, 0]
    loff = loff[:, :, 0]
    max_tiles = -(-(2 * n + (GRANULE - 1) * N_EXPERTS * n_tiles) // MOE_TILE) + N_EXPERTS
    rows = max_tiles * MOE_TILE
    tot_g = jnp.sum(gran, axis=0)
    tiles = (tot_g + GRANULES_PER_TILE - 1) // GRANULES_PER_TILE
    tile_end = jnp.cumsum(tiles)
    seg_g = ((tile_end - tiles) * GRANULES_PER_TILE).astype(jnp.int32)
    n_valid = tile_end[-1]
    idx = jnp.minimum(jnp.arange(max_tiles, dtype=jnp.int32), n_valid - 1)
    tile_expert = jnp.sum(idx[:, None] >= tile_end[None, :], axis=1).astype(jnp.int32)
    buf_g = seg_g[None, :] + jnp.cumsum(gran, axis=0) - gran
    ctl = jnp.zeros((n_tiles, SUBLANES, LANES), jnp.int32)
    ctl = ctl.at[:, 0, :N_EXPERTS].set(gran).at[:, 1, :N_EXPERTS].set(loff).at[:, 2, :N_EXPERTS].set(buf_g)
    xs = _dispatch(x1, plan, ctl, seg_g, tot_g.astype(jnp.int32), tt, rows)
    ys = _ffn(xs, tile_expert, n_valid.reshape(1).astype(jnp.int32), wg, wu, wd, layer)
    return _combine(x1, route, ctl, ys, layer, tt, g, b)


def _moe_kernel(x_ref, r_ref, wg_ref, wu_ref, wd_ref, g_ref, b_ref, o_ref, xb_sc, acc_sc):
    e = pl.program_id(1)

    @pl.when(e == 0)
    def _():
        xb_sc[...] = x_ref[...].astype(BF16)
        acc_sc[...] = jnp.zeros_like(acc_sc)

    xb = xb_sc[...]
    hg = jnp.dot(xb, wg_ref[...].astype(BF16), preferred_element_type=F32)
    hu = jnp.dot(xb, wu_ref[...].astype(BF16), preferred_element_type=F32)
    r = r_ref[...]
    lane = lax.broadcasted_iota(jnp.int32, r.shape, 1)
    c = jnp.sum(jnp.where(lane == e, r, 0.0), axis=-1, keepdims=True)
    h = _silu(hg) * hu * c
    acc_sc[...] += jnp.dot(h.astype(BF16), wd_ref[...].astype(BF16), preferred_element_type=F32)

    @pl.when(e == pl.num_programs(1) - 1)
    def _():
        o_ref[...] = _layernorm(ALPHA * x_ref[...] + acc_sc[...], g_ref[...], b_ref[...])


def _moe(x1, route, wg, wu, wd, layer, tm, g, b):
    n = x1.shape[0]
    lay = lambda shape: pl.BlockSpec((None,) + shape, lambda i, e: (layer,) + (0,) * len(shape))
    return pl.pallas_call(
        _moe_kernel,
        grid=(n // tm, N_EXPERTS),
        in_specs=[pl.BlockSpec((tm, D_MODEL), lambda i, e: (i, 0)),
                  pl.BlockSpec((tm, LANES), lambda i, e: (i, 0)),
                  pl.BlockSpec((None, None, D_MODEL, D_EXPERT), lambda i, e: (layer, e, 0, 0)),
                  pl.BlockSpec((None, None, D_MODEL, D_EXPERT), lambda i, e: (layer, e, 0, 0)),
                  pl.BlockSpec((None, None, D_EXPERT, D_MODEL), lambda i, e: (layer, e, 0, 0)),
                  lay((1, D_MODEL)), lay((1, D_MODEL))],
        out_specs=pl.BlockSpec((tm, D_MODEL), lambda i, e: (i, 0)),
        out_shape=jax.ShapeDtypeStruct((n, D_MODEL), F32),
        scratch_shapes=[pltpu.VMEM((tm, D_MODEL), BF16), pltpu.VMEM((tm, D_MODEL), F32)],
        compiler_params=_cparams(2),
        name="moe",
    )(x1, route, wg, wu, wd, g, b)


def _block_diag(w):
    out = jnp.zeros((DEPTH, D_A, D_A), w.dtype)
    for gi in range(A_BLOCKS):
        out = out.at[:, gi * A_BLK:(gi + 1) * A_BLK, gi * A_BLK:(gi + 1) * A_BLK].set(w[:, gi])
    return out


def kernel(x_prompt, x_sample, state_rglru_h, state_conv, state_hgrn, state_ret, w_in, conv_w, conv_b, w_rgate, b_rgate, w_igate, b_igate, rglru_lambda, hgrn_lb_logits, hgrn_norm_g, w_out, ln1_g, ln1_b, router_w, router_b, exp_w_gate, exp_w_up, exp_w_down, ln2_g, ln2_b):
    batch, seq, _ = x_prompt.shape
    nb = x_sample.shape[0]

    w_in_bf = w_in.astype(BF16)
    w_out_bf = w_out.astype(BF16)
    wg, wu, wd = exp_w_gate, exp_w_up, exp_w_down
    wri = jnp.concatenate([_block_diag(w_rgate), _block_diag(w_igate)], axis=-1).astype(BF16)
    bri = jnp.concatenate([b_rgate.reshape(DEPTH, 1, D_A), b_igate.reshape(DEPTH, 1, D_A)], axis=-1)
    cb = conv_b.reshape(DEPTH, 1, D_A)
    lam = rglru_lambda.reshape(DEPTH, 1, D_A)
    ng = jnp.tile(hgrn_norm_g, (1, B_HEADS)).reshape(DEPTH, 1, D_B)
    g1, b1 = ln1_g.reshape(DEPTH, 1, D_MODEL), ln1_b.reshape(DEPTH, 1, D_MODEL)
    g2, b2 = ln2_g.reshape(DEPTH, 1, D_MODEL), ln2_b.reshape(DEPTH, 1, D_MODEL)
    rwt = router_w.T
    rb = router_b.reshape(N_EXPERTS, 1)
    conv0_t = jnp.transpose(state_conv, (0, 2, 1, 3))
    hg0_t = jnp.transpose(state_hgrn, (0, 2, 3, 4, 1))

    xp = x_prompt.reshape(batch * seq, D_MODEL)
    xs = x_sample.reshape(nb, D_MODEL)
    hs_p, convs_p, hgs_p, rts_p = [], [], [], []
    prev_s, prev_hg = [], []
    for l in range(DEPTH):
        y_p, h_p, conv_p, hg_p, rt_p = _mix_prompt(
            xp, w_in_bf, batch, seq, l, PROMPT_TILE, conv_w, cb, wri, bri, lam, hgrn_lb_logits, ng)
        x1_p, route_p, plan_p, gran_p, loff_p = _outproj(
            y_p, xp, w_out_bf, l, PROMPT_TILE, g1, b1, rwt, rb, plan=True, sub=2)
        xp = _moe_sparse(x1_p, route_p, plan_p, gran_p, loff_p, wg, wu, wd, l, PROMPT_TILE, g2, b2)
        hs_p.append(h_p.reshape(batch, D_A))
        convs_p.append(conv_p)
        hgs_p.append(hg_p)
        rts_p.append(rt_p)

        proj_s = _proj(xs, w_in, l, nb)
        last = l == DEPTH - 1
        y_s, *states_s = _mix_sample(
            proj_s, state_rglru_h, conv0_t, state_ret, l, conv_w, cb, wri, bri, lam,
            prev=prev_s if last else ())
        prev_s.append(states_s)
        y_s, hg_s = _hgrn_sample(proj_s, y_s, hg0_t, l, hgrn_lb_logits, ng[:, :, :LANES],
                                 prev=prev_hg if last else ())
        prev_hg.append(hg_s)
        x1_s, route_s = _outproj(y_s, xs, w_out, l, nb, g1, b1, rwt, rb)
        xs = _moe(x1_s, route_s, wg, wu, wd, l, nb, g2, b2)

    h_s, conv_s, rt_s = prev_s[-1]
    hg_s = jnp.transpose(prev_hg[-1], (0, 4, 1, 2, 3))
    return (xp.reshape(batch, seq, D_MODEL), xs.reshape(nb, 1, D_MODEL),
            jnp.stack(hs_p), h_s, jnp.stack(convs_p), jnp.transpose(conv_s, (0, 2, 1, 3)),
            jnp.stack(hgs_p), hg_s, jnp.stack(rts_p), rt_s)
```

```python
import functools

import numpy as np
import jax
import jax.numpy as jnp
from jax import lax
from jax.experimental import pallas as pl
from jax.experimental.pallas import tpu as pltpu

D_MODEL = 1024
DEPTH = 2
PAST_LEN = 16384
D_A = 256
A_BLOCKS = 4
A_BLK = 64
CONV_W = 4
RGLRU_C = 8.0
B_HEADS = 4
B_DK = 64
D_B = 256
C_HEADS = 4
C_DK = 128
D_C = 512
D_IN = 3584
B_CHUNK = 64
C_CHUNK = 128
ROPE_BASE = 10000.0
N_EXPERTS = 16
N_GROUPS = 4
EXP_PER_GROUP = 4
D_EXPERT = 512
LN_EPS = 1e-5
RMS_EPS = 1e-6
GN_EPS = 1e-6
F_TINY = 1e-30
ALPHA = (2 * DEPTH) ** 0.25

O_XA, O_GA, O_QB, O_FB, O_VB, O_GB, O_QC, O_KC, O_VC, O_GC = (
    0, 256, 512, 768, 1024, 1280, 1536, 2048, 2560, 3072)

V7X_VMEM_LIMIT_BYTES = 56 * 1024 * 1024
SUBLANES = 8
LANES = 128
PROMPT_TILE = 512
HGRN_SAFE_MIN_LOGDECAY = -60.0

BF16 = jnp.bfloat16
F32 = jnp.float32
_NT = (((1,), (1,)), ((), ()))
_TN = (((0,), (0,)), ((), ()))


def _cparams(n_axes):
    return pltpu.CompilerParams(
        dimension_semantics=("arbitrary",) * n_axes,
        vmem_limit_bytes=V7X_VMEM_LIMIT_BYTES)


def _dot(a, b):
    return jnp.dot(a.astype(BF16), b.astype(BF16), preferred_element_type=F32)


def _dot_g(a, b, dims):
    return lax.dot_general(a.astype(BF16), b.astype(BF16), dims, preferred_element_type=F32)


def _sigmoid(x):
    return 0.5 * jnp.tanh(0.5 * x) + 0.5


def _sqrt_nonneg(x):
    return jnp.where(x > 0.0, x * lax.rsqrt(x), 0.0)


def _silu(x):
    return x * _sigmoid(x)


def _gelu_tanh(x):
    c = np.float32(np.sqrt(2.0 / np.pi))
    return 0.5 * x * (1.0 + jnp.tanh(c * (x + np.float32(0.044715) * (x * x * x))))


def _log_sigmoid(x):
    return -(jnp.maximum(-x, 0.0) + jnp.log(1.0 + jnp.exp(-jnp.abs(x))))


def _layernorm(z, g, b):
    mu = jnp.mean(z, axis=-1, keepdims=True)
    zc = z - mu
    var = jnp.mean(zc * zc, axis=-1, keepdims=True)
    return zc * lax.rsqrt(var + LN_EPS) * g + b


def _hgrn_lower_bound(lbl, layer):
    rows = [lbl[j:j + 1, :] for j in range(DEPTH)]
    m = rows[0]
    for r in rows[1:]:
        m = jnp.maximum(m, r)
    ex = [jnp.exp(r - m) for r in rows]
    tot = ex[0]
    for e in ex[1:]:
        tot = tot + e
    lb = jnp.zeros_like(m)
    for j in range(1, layer + 1):
        lb = lb + ex[j] / tot
    return lb


def _rglru_gates(u, wri, bri, lam):
    gates = _dot(u, wri) + bri
    r = _sigmoid(gates[:, :D_A])
    i = _sigmoid(gates[:, D_A:])
    log_a = RGLRU_C * r * _log_sigmoid(lam)
    a = jnp.exp(log_a)
    bterm = _sqrt_nonneg(jnp.maximum(1.0 - a * a, 0.0)) * (i * u)
    return a, bterm


def _split(x):
    hi = x.astype(BF16)
    return hi, (x - hi.astype(F32)).astype(BF16)


def _dot3(x, w):
    xh, xl = _split(x)
    wh, wl = _split(w)
    return (jnp.dot(xh, wh, preferred_element_type=F32)
            + jnp.dot(xl, wh, preferred_element_type=F32)
            + jnp.dot(xh, wl, preferred_element_type=F32))


def _proj_kernel(x_ref, w_ref, o_ref):
    x = x_ref[...]
    for j in range(0, D_IN, 512):
        o_ref[:, j:j + 512] = _dot3(x, w_ref[:, j:j + 512])


def _proj(x, w_in_bf, layer, tm):
    n = x.shape[0]
    return pl.pallas_call(
        _proj_kernel,
        grid=(n // tm,),
        in_specs=[pl.BlockSpec((tm, D_MODEL), lambda i: (i, 0)),
                  pl.BlockSpec((None, D_MODEL, D_IN), lambda i: (layer, 0, 0))],
        out_specs=pl.BlockSpec((tm, D_IN), lambda i: (i, 0)),
        out_shape=jax.ShapeDtypeStruct((n, D_IN), F32),
        compiler_params=_cparams(1),
        name="proj",
    )(x, w_in_bf)


def _retention_consts(chunk):
    lg = np.log1p(-np.exp2(-5.0 - np.arange(C_HEADS, dtype=np.float64)))
    idx = np.arange(chunk, dtype=np.float64)
    rel = idx[:, None] - idx[None, :]
    mask = rel >= 0
    dmat = np.where(mask[None], np.exp(np.where(mask, rel, 0.0)[None] * lg[:, None, None]), 0.0)
    qdec = np.exp((idx + 1.0)[None, :] * lg[:, None])
    kdec = np.exp((chunk - 1.0 - idx)[None, :] * lg[:, None])
    sdec = np.exp(chunk * lg)
    qdec_b = np.broadcast_to(qdec[:, :, None], (C_HEADS, chunk, C_DK))
    kdec_b = np.broadcast_to(kdec[:, :, None], (C_HEADS, chunk, C_DK))
    return (dmat.astype(np.float32), np.ascontiguousarray(qdec_b).astype(np.float32),
            np.ascontiguousarray(kdec_b).astype(np.float32), [float(v) for v in sdec])


def _rope_tables(positions):
    half = C_DK // 2
    inv = ROPE_BASE ** (-np.arange(half, dtype=np.float64) / half)
    ang = np.asarray(positions, dtype=np.float64)[:, None] * inv[None]
    cos = np.concatenate([np.cos(ang), np.cos(ang)], axis=-1)
    sin = np.concatenate([-np.sin(ang), np.sin(ang)], axis=-1)
    return cos.astype(np.float32), sin.astype(np.float32)


def _rope(x, cos, sin_signed):
    return x * cos + pltpu.roll(x, C_DK // 2, 1) * sin_signed


PROJ_SPLIT = 1536


def _mix_prompt_kernel(x0_ref, xn_ref, w_ref, cos_ref, sin_ref, cw_ref, cb_ref, wri_ref, bri_ref,
                       lam_ref, lbl_ref, ng_ref, dmat_ref, qdec_ref, kdec_ref,
                       y_ref, h_ref, conv_ref, hg_ref, rt_ref,
                       prev_sc, hprev_sc, st_sc, sret_sc, kb_sc, bb_sc, vb_sc, oi_sc, p_ref, next_ref, xb_sc, x0b_sc,
                       *, layer, tt, sdec):
    t = pl.program_id(1)
    nt = pl.num_programs(1)
    flat = pl.program_id(0) * nt + t

    def project(xb_ref, dst, c0, c1):
        xb = xb_ref[...]
        for j in range(c0, c1, 256):
            dst[:, j:j + 256] = jnp.dot(xb, w_ref[:, j:j + 256], preferred_element_type=F32)

    xb_sc[...] = xn_ref[...].astype(BF16)

    @pl.when(flat == 0)
    def _():
        x0b_sc[...] = x0_ref[...].astype(BF16)
        project(x0b_sc, p_ref, 0, D_IN)

    @pl.when(t == 0)
    def _():
        prev_sc[...] = jnp.zeros_like(prev_sc)
        hprev_sc[...] = jnp.zeros_like(hprev_sc)
        st_sc[...] = jnp.zeros_like(st_sc)
        sret_sc[...] = jnp.zeros_like(sret_sc)

    xa = p_ref[:, O_XA:O_XA + D_A]
    ga = p_ref[:, O_GA:O_GA + D_A]
    row = lax.broadcasted_iota(jnp.int32, (tt, D_A), 0)
    row8 = lax.broadcasted_iota(jnp.int32, (SUBLANES, D_A), 0)
    prev = prev_sc[...]

    def shifted(j):
        r = pltpu.roll(xa, j, 0)
        top = jnp.where(row8 < j, pltpu.roll(prev, j, 0), r[0:SUBLANES])
        return jnp.concatenate([top, r[SUBLANES:]], axis=0)

    u = cb_ref[...] + shifted(3) * cw_ref[0:1, :]
    u = u + shifted(2) * cw_ref[1:2, :]
    u = u + shifted(1) * cw_ref[2:3, :]
    u = u + xa * cw_ref[3:4, :]
    last8 = xa[tt - SUBLANES:tt]
    prev_sc[...] = last8

    a, bterm = _rglru_gates(u, wri_ref[...], bri_ref[...], lam_ref[...])
    project(xb_sc, next_ref, 0, PROJ_SPLIT)
    s = 1
    while s < SUBLANES:
        keep = (row % SUBLANES) >= s
        a_sh = jnp.where(keep, pltpu.roll(a, s, 0), 1.0)
        b_sh = jnp.where(keep, pltpu.roll(bterm, s, 0), 0.0)
        bterm = a * b_sh + bterm
        a = a * a_sh
        s *= 2
    carry = hprev_sc[...]
    groups = []
    for gi in range(tt // SUBLANES):
        rows = slice(gi * SUBLANES, (gi + 1) * SUBLANES)
        h_in = jnp.broadcast_to(carry[SUBLANES - 1:SUBLANES, :], (SUBLANES, D_A))
        carry = a[rows] * h_in + bterm[rows]
        groups.append(carry)
    h = jnp.concatenate(groups, axis=0)
    hlast8 = carry
    hprev_sc[...] = hlast8
    y_ref[:, 0:D_A] = h * _gelu_tanh(ga)

    lb = _hgrn_lower_bound(lbl_ref[...], layer)
    ng = ng_ref[...]
    cl = B_CHUNK
    crow = lax.broadcasted_iota(jnp.int32, (cl, D_B), 0)
    ccol = lax.broadcasted_iota(jnp.int32, (cl, D_B), 1)
    causal = (ccol % B_DK) <= crow
    br = lax.broadcasted_iota(jnp.int32, (D_B, D_B), 0)
    bc = lax.broadcasted_iota(jnp.int32, (D_B, D_B), 1)
    head_mask = (br // B_DK) == (bc // B_DK)
    seg = jnp.where(head_mask, 1.0, 0.0).astype(BF16)

    def seg_mean(x):
        hi = x.astype(BF16)
        lo = (x - hi.astype(F32)).astype(BF16)
        tot = (jnp.dot(hi, seg, preferred_element_type=F32)
               + jnp.dot(lo, seg, preferred_element_type=F32))
        return tot * (1.0 / B_DK)

    nck = tt // cl
    q = _silu(p_ref[:, O_QB:O_QB + D_B])
    fl = p_ref[:, O_FB:O_FB + D_B]
    v = p_ref[:, O_VB:O_VB + D_B]
    half_th = 0.5 * jnp.tanh(0.5 * fl)
    f = lb + (1.0 - lb) * (0.5 + half_th)
    b = jnp.log(jnp.maximum(f, F_TINY))
    k = (1.0 - lb) * (0.5 - half_th)
    trow = lax.broadcasted_iota(jnp.int32, (tt, D_B), 0) % cl
    sh = 1
    while sh < cl:
        b = b + jnp.where(trow >= sh, pltpu.roll(b, sh, 0), 0.0)
        sh *= 2

    def chunk_row(x, c, r):
        top = c * cl + r + 1
        return x[top - SUBLANES:top][SUBLANES - 1:SUBLANES]

    def spread(rows):
        return jnp.concatenate([jnp.broadcast_to(r, (cl, D_B)) for r in rows], axis=0)

    last_rows = [chunk_row(b, c, cl - 1) for c in range(nck)]
    mid_rows = [chunk_row(b, c, cl // 2 - 1) for c in range(nck)]
    b_last = spread(last_rows)
    b_mid = spread(mid_rows)
    qd = q * jnp.exp(b)
    kl = k * jnp.exp(b_last - b)
    lasts = jnp.concatenate(last_rows, axis=0)
    mids = jnp.concatenate(mid_rows, axis=0)
    safe = jnp.min(jnp.minimum(mids, lasts - mids)) >= HGRN_SAFE_MIN_LOGDECAY

    def next_state(st, c, sl):
        kv = jnp.where(head_mask, _dot_g(v[sl], kl[sl], _TN), 0.0)
        return st * jnp.exp(last_rows[c]) + kv

    @pl.when(safe)
    def _():
        qmid = q * jnp.exp(b - b_mid)
        kinv = k * jnp.exp(b_mid - b)
        st = st_sc[...]
        for c in range(nck):
            sl = slice(c * cl, (c + 1) * cl)
            k4 = jnp.where(head_mask, jnp.concatenate([kinv[sl]] * B_HEADS, axis=0), 0.0)
            sc = jnp.where(causal, _dot_g(qmid[sl], k4, _NT), 0.0)
            v4 = jnp.where(head_mask, jnp.concatenate([v[sl]] * B_HEADS, axis=0), 0.0)
            oi_sc[sl, :] = _dot(sc, v4) + _dot_g(qd[sl], st, _NT)
            st = next_state(st, c, sl)
        st_sc[...] = st

    @pl.when(jnp.logical_not(safe))
    def _():
        kb_sc[...] = k
        bb_sc[...] = b
        vb_sc[...] = v
        st = st_sc[...]
        for c in range(nck):
            sl = slice(c * cl, (c + 1) * cl)
            qc, bc_ = q[sl], b[sl]

            def pair(sidx, acc, c=c, qc=qc, bc_=bc_):
                src = pl.ds(c * cl + sidx, 1)
                e = jnp.exp(jnp.minimum(bc_ - bb_sc[src, :], 0.0)) * (qc * kb_sc[src, :])
                scr = jnp.dot(e.astype(BF16), seg, preferred_element_type=F32)
                return acc + jnp.where(crow >= sidx, scr, 0.0) * vb_sc[src, :]

            oi = lax.fori_loop(0, cl, pair, jnp.zeros((cl, D_B), F32))
            oi_sc[sl, :] = oi + _dot_g(qd[sl], st, _NT)
            st = next_state(st, c, sl)
        st_sc[...] = st

    o = oi_sc[...]
    o = o * lax.rsqrt(seg_mean(o * o) + RMS_EPS) * ng
    y_ref[:, D_A:D_A + D_B] = o * _silu(p_ref[:, O_GB:O_GB + D_B])

    rl = C_CHUNK
    per_head = (D_IN - PROJ_SPLIT) // C_HEADS
    cos = cos_ref[...]
    sin = sin_ref[...]
    for hh in range(C_HEADS):
        lo = hh * C_DK
        q = _rope(p_ref[:, O_QC + lo:O_QC + lo + C_DK], cos, sin)
        k = _rope(p_ref[:, O_KC + lo:O_KC + lo + C_DK], cos, sin) * (C_DK ** -0.5)
        v = p_ref[:, O_VC + lo:O_VC + lo + C_DK]
        s = sret_sc[hh]
        parts = []
        for c in range(tt // rl):
            sl = slice(c * rl, (c + 1) * rl)
            sc = _dot_g(q[sl], k[sl], _NT) * dmat_ref[hh]
            parts.append(_dot(sc, v[sl]) + _dot(q[sl] * qdec_ref[hh], s))
            s = sdec[hh] * s + _dot_g(k[sl] * kdec_ref[hh], v[sl], _TN)
        sret_sc[hh] = s
        project(xb_sc, next_ref, PROJ_SPLIT + hh * per_head, PROJ_SPLIT + (hh + 1) * per_head)
        o = jnp.concatenate(parts, axis=0)
        mu = jnp.mean(o, axis=-1, keepdims=True)
        oc = o - mu
        var = jnp.mean(oc * oc, axis=-1, keepdims=True)
        g = p_ref[:, O_GC + lo:O_GC + lo + C_DK]
        y_ref[:, D_A + D_B + lo:D_A + D_B + lo + C_DK] = oc * lax.rsqrt(var + GN_EPS) * _silu(g)

    p_ref[...] = next_ref[...]

    @pl.when(t == nt - 1)
    def _():
        h_ref[...] = pltpu.roll(hprev_sc[...], 1, 0)[0:1]
        conv_ref[...] = pltpu.roll(prev_sc[...], CONV_W - 1, 0)[0:CONV_W - 1]
        s_bd = st_sc[...].T
        for hh in range(B_HEADS):
            hg_ref[hh] = s_bd[hh * B_DK:(hh + 1) * B_DK, hh * B_DK:(hh + 1) * B_DK]
        rt_ref[...] = sret_sc[...]


def _mix_prompt(x, w_in_bf, batch, seq, layer, tt, cw, cb, wri, bri, lam, lbl, ng):
    nt = seq // tt
    last = batch * nt - 1
    dmat, qdec, kdec, sdec = _retention_consts(C_CHUNK)
    cos, sin = _rope_tables(np.arange(seq))
    full = lambda shape: pl.BlockSpec(shape, lambda b, t: (0,) * len(shape))
    lay = lambda shape: pl.BlockSpec((None,) + shape, lambda b, t: (layer,) + (0,) * len(shape))
    kern = functools.partial(_mix_prompt_kernel, layer=layer, tt=tt, sdec=sdec)
    return pl.pallas_call(
        kern,
        grid=(batch, nt),
        in_specs=[
            pl.BlockSpec((tt, D_MODEL), lambda b, t: (b * nt + t, 0)),
            pl.BlockSpec((tt, D_MODEL), lambda b, t: (jnp.minimum(b * nt + t + 1, last), 0)),
            lay((D_MODEL, D_IN)),
            pl.BlockSpec((tt, C_DK), lambda b, t: (t, 0)),
            pl.BlockSpec((tt, C_DK), lambda b, t: (t, 0)),
            lay((CONV_W, D_A)), lay((1, D_A)), lay((D_A, 2 * D_A)), lay((1, 2 * D_A)),
            lay((1, D_A)), full((DEPTH, D_B)), lay((1, D_B)),
            full((C_HEADS, C_CHUNK, C_CHUNK)), full((C_HEADS, C_CHUNK, C_DK)),
            full((C_HEADS, C_CHUNK, C_DK)),
        ],
        out_specs=[
            pl.BlockSpec((tt, D_MODEL), lambda b, t: (b * nt + t, 0)),
            pl.BlockSpec((None, 1, D_A), lambda b, t: (b, 0, 0)),
            pl.BlockSpec((None, CONV_W - 1, D_A), lambda b, t: (b, 0, 0)),
            pl.BlockSpec((None, B_HEADS, B_DK, B_DK), lambda b, t: (b, 0, 0, 0)),
            pl.BlockSpec((None, C_HEADS, C_DK, C_DK), lambda b, t: (b, 0, 0, 0)),
        ],
        out_shape=[
            jax.ShapeDtypeStruct((batch * seq, D_MODEL), F32),
            jax.ShapeDtypeStruct((batch, 1, D_A), F32),
            jax.ShapeDtypeStruct((batch, CONV_W - 1, D_A), F32),
            jax.ShapeDtypeStruct((batch, B_HEADS, B_DK, B_DK), F32),
            jax.ShapeDtypeStruct((batch, C_HEADS, C_DK, C_DK), F32),
        ],
        scratch_shapes=[
            pltpu.VMEM((SUBLANES, D_A), F32), pltpu.VMEM((SUBLANES, D_A), F32),
            pltpu.VMEM((D_B, D_B), F32), pltpu.VMEM((C_HEADS, C_DK, C_DK), F32),
            pltpu.VMEM((tt, D_B), F32), pltpu.VMEM((tt, D_B), F32),
            pltpu.VMEM((tt, D_B), F32), pltpu.VMEM((tt, D_B), F32),
            pltpu.VMEM((tt, D_IN), F32), pltpu.VMEM((tt, D_IN), F32),
            pltpu.VMEM((tt, D_MODEL), BF16), pltpu.VMEM((tt, D_MODEL), BF16),
        ],
        compiler_params=_cparams(2),
        name="mix_prompt",
    )(x, x, w_in_bf, cos, sin, cw, cb, wri, bri, lam, lbl, ng, dmat, qdec, kdec)


def _column_matrix(x):
    pad = jnp.zeros((LANES - SUBLANES, LANES), F32)
    return jnp.concatenate([x, pad], axis=0).T


def _mix_sample_kernel(*refs, layer, gammas, n_prev):
    (p_ref, h0_ref, conv0_ref, rt0_ref, cos_ref, sin_ref,
     cw_ref, cb_ref, wri_ref, bri_ref, lam_ref) = refs[:11]
    prev = refs[11:11 + 3 * n_prev]
    y_ref, h_ref, conv_ref, rt_ref, o_sc = refs[11 + 3 * n_prev:]
    if n_prev:
        for j in range(n_prev):
            for dst, src in zip((h_ref, conv_ref, rt_ref), prev[3 * j:3 * j + 3]):
                dst[j] = src[...]
        h_ref, conv_ref, rt_ref = (r.at[layer] for r in (h_ref, conv_ref, rt_ref))
    tb = SUBLANES
    xa = p_ref[:, O_XA:O_XA + D_A]
    ga = p_ref[:, O_GA:O_GA + D_A]
    c0, c1, c2 = conv0_ref[0], conv0_ref[1], conv0_ref[2]
    u = cb_ref[...] + c0 * cw_ref[0:1, :]
    u = u + c1 * cw_ref[1:2, :]
    u = u + c2 * cw_ref[2:3, :]
    u = u + xa * cw_ref[3:4, :]
    conv_ref[0] = c1
    conv_ref[1] = c2
    conv_ref[2] = xa
    a, bterm = _rglru_gates(u, wri_ref[...], bri_ref[...], lam_ref[...])
    h = a * h0_ref[...] + bterm
    h_ref[...] = h
    y_ref[:, 0:D_A] = h * _gelu_tanh(ga)

    y_ref[:, D_A:D_A + D_B] = jnp.zeros((tb, D_B), F32)

    cos = cos_ref[...]
    sin = sin_ref[...]
    for head in range(C_HEADS):
        lo = head * C_DK
        gamma = gammas[head]
        q = _rope(p_ref[:, O_QC + lo:O_QC + lo + C_DK], cos, sin)
        k = _rope(p_ref[:, O_KC + lo:O_KC + lo + C_DK], cos, sin) * (C_DK ** -0.5)
        v = p_ref[:, O_VC + lo:O_VC + lo + C_DK]
        g = p_ref[:, O_GC + lo:O_GC + lo + C_DK]
        dots = jnp.sum(q * k, axis=-1, keepdims=True)
        q_cols = _column_matrix(q * gamma)
        k_cols = _column_matrix(k)
        for b in range(tb):
            s_old = rt0_ref[b, head]
            v_row = v[b:b + 1, :]
            rt_ref[b, head] = gamma * s_old + k_cols[:, b:b + 1] * v_row
            o_row = (jnp.sum(q_cols[:, b:b + 1] * s_old, axis=0, keepdims=True)
                     + dots[b:b + 1, :] * v_row)
            o_sc[b:b + 1, 0:C_DK] = o_row
        o = o_sc[:, 0:C_DK]
        mu = jnp.mean(o, axis=-1, keepdims=True)
        oc = o - mu
        var = jnp.mean(oc * oc, axis=-1, keepdims=True)
        y_ref[:, D_A + D_B + lo:D_A + D_B + lo + C_DK] = oc * lax.rsqrt(var + GN_EPS) * _silu(g)


def _mix_sample(proj, h0, conv0_t, rt0, layer, cw, cb, wri, bri, lam, prev=()):
    nb = proj.shape[0]
    tb = SUBLANES
    lg = np.log1p(-np.exp2(-5.0 - np.arange(C_HEADS, dtype=np.float64)))
    gammas = [float(np.exp(v)) for v in lg]
    cos, sin = _rope_tables([PAST_LEN])
    full = lambda shape: pl.BlockSpec(shape, lambda i: (0,) * len(shape))
    lay = lambda shape: pl.BlockSpec((None,) + shape, lambda i: (layer,) + (0,) * len(shape))
    state_shapes = [(tb, D_A), (CONV_W - 1, tb, D_A), (tb, C_HEADS, C_DK, C_DK)]
    state_maps = [lambda i: (i, 0), lambda i: (0, i, 0), lambda i: (i, 0, 0, 0)]
    full_shapes = [(nb, D_A), (CONV_W - 1, nb, D_A), (nb, C_HEADS, C_DK, C_DK)]
    state_specs = [pl.BlockSpec(s, m) for s, m in zip(state_shapes, state_maps)]
    if prev:
        stack = len(prev) + 1
        out_state_specs = [pl.BlockSpec((stack,) + s, lambda i, m=m: (0,) + m(i))
                           for s, m in zip(state_shapes, state_maps)]
        out_state_shapes = [jax.ShapeDtypeStruct((stack,) + s, F32) for s in full_shapes]
    else:
        out_state_specs = state_specs
        out_state_shapes = [jax.ShapeDtypeStruct(s, F32) for s in full_shapes]
    kern = functools.partial(_mix_sample_kernel, layer=layer, gammas=gammas, n_prev=len(prev))
    return pl.pallas_call(
        kern,
        grid=(nb // tb,),
        in_specs=[
            pl.BlockSpec((tb, D_IN), lambda i: (i, 0)),
            pl.BlockSpec((None, tb, D_A), lambda i: (layer, i, 0)),
            pl.BlockSpec((None, CONV_W - 1, tb, D_A), lambda i: (layer, 0, i, 0)),
            pl.BlockSpec((None, tb, C_HEADS, C_DK, C_DK), lambda i: (layer, i, 0, 0, 0)),
            full((1, C_DK)), full((1, C_DK)),
            lay((CONV_W, D_A)), lay((1, D_A)), lay((D_A, 2 * D_A)), lay((1, 2 * D_A)),
            lay((1, D_A)),
        ] + state_specs * len(prev),
        out_specs=[pl.BlockSpec((tb, D_MODEL), lambda i: (i, 0))] + out_state_specs,
        out_shape=[jax.ShapeDtypeStruct((nb, D_MODEL), F32)] + out_state_shapes,
        scratch_shapes=[pltpu.VMEM((tb, D_B), F32)],
        compiler_params=_cparams(1),
        name="mix_sample",
    )(proj, h0, conv0_t, rt0, cos, sin, cw, cb, wri, bri, lam,
      *[a for states in prev for a in states])


def _hgrn_sample_kernel(*refs, layer, n_prev):
    qb_ref, fb_ref, vb_ref, gb_ref, lbl_ref, ng_ref, s0_ref, y_in_ref = refs[:8]
    prev = refs[8:8 + n_prev]
    y_ref, s_ref = refs[8 + n_prev:]
    del y_in_ref
    for j in range(n_prev):
        s_ref[j] = prev[j][...]
    if n_prev:
        s_ref = s_ref.at[layer]
    lb = _hgrn_lower_bound(lbl_ref[...], layer)
    q = _silu(qb_ref[...])
    fl = fb_ref[...]
    v = vb_ref[...]
    half_th = 0.5 * jnp.tanh(0.5 * fl)
    f = lb + (1.0 - lb) * (0.5 + half_th)
    ef = jnp.exp(jnp.log(jnp.maximum(f, F_TINY)))
    k = (1.0 - lb) * (0.5 - half_th)
    qf_t, k_t, ef_t, v_t, qk_t = (x.T for x in (q * ef, k, ef, v, q * k))
    o_parts = []
    for hh in range(2):
        base = hh * B_DK
        v_h = v_t[base:base + B_DK]
        acc = jnp.zeros((B_DK, LANES), F32)
        for kk in range(B_DK):
            r = base + kk
            s_old = s0_ref[hh, kk]
            s_ref[hh, kk] = ef_t[r:r + 1] * s_old + k_t[r:r + 1] * v_h
            acc = acc + qf_t[r:r + 1] * s_old
        dots = jnp.sum(qk_t[base:base + B_DK], axis=0, keepdims=True)
        o = acc + dots * v_h
        o_parts.append(o * lax.rsqrt(jnp.mean(o * o, axis=0, keepdims=True) + RMS_EPS))
    o = jnp.concatenate(o_parts, axis=0).T
    y_ref[...] = o * ng_ref[...] * _silu(gb_ref[...])


def _hgrn_sample(proj, y, s0_t, layer, lbl, ng_pair, prev=()):
    nb = proj.shape[0]
    col = lambda off: pl.BlockSpec((nb, LANES), lambda j: (0, off // LANES + j))
    state_block = (2, B_DK, B_DK, nb)
    state_spec = pl.BlockSpec(state_block, lambda j: (j, 0, 0, 0))
    stack = len(prev) + 1
    if prev:
        out_state_spec = pl.BlockSpec((stack,) + state_block, lambda j: (0, j, 0, 0, 0))
        out_state_shape = jax.ShapeDtypeStruct((stack, B_HEADS, B_DK, B_DK, nb), F32)
    else:
        out_state_spec = state_spec
        out_state_shape = jax.ShapeDtypeStruct((B_HEADS, B_DK, B_DK, nb), F32)
    return pl.pallas_call(
        functools.partial(_hgrn_sample_kernel, layer=layer, n_prev=len(prev)),
        grid=(B_HEADS // 2,),
        in_specs=[col(O_QB), col(O_FB), col(O_VB), col(O_GB),
                  pl.BlockSpec((DEPTH, LANES), lambda j: (0, j)),
                  pl.BlockSpec((None, 1, LANES), lambda j: (layer, 0, 0)),
                  pl.BlockSpec((None,) + state_block, lambda j: (layer, j, 0, 0, 0)),
                  pl.BlockSpec(memory_space=pl.ANY)] + [state_spec] * len(prev),
        out_specs=[col(D_A), out_state_spec],
        out_shape=[jax.ShapeDtypeStruct(y.shape, F32), out_state_shape],
        input_output_aliases={7: 0},
        compiler_params=_cparams(1),
        name="hgrn_sample",
    )(proj, proj, proj, proj, lbl, ng_pair, s0_t, y, *prev)


def _route_rows(l):
    m = l[0]
    for x in l[1:]:
        m = jnp.maximum(m, x)
    ex = [jnp.exp(x - m) for x in l]
    tot = ex[0]
    for x in ex[1:]:
        tot = tot + x
    p = [x / tot for x in ex]
    scores = []
    for gi in range(N_GROUPS):
        a, b, c, d = p[4 * gi:4 * gi + 4]
        hi1, lo1 = jnp.maximum(a, b), jnp.minimum(a, b)
        hi2, lo2 = jnp.maximum(c, d), jnp.minimum(c, d)
        top1 = jnp.maximum(hi1, hi2)
        top2 = jnp.maximum(jnp.minimum(hi1, hi2), jnp.maximum(lo1, lo2))
        scores.append(top1 + top2)
    best = scores[0]
    gsel = jnp.zeros_like(best, dtype=jnp.int32)
    for gi in range(1, N_GROUPS):
        upd = scores[gi] > best
        gsel = jnp.where(upd, gi, gsel)
        best = jnp.where(upd, scores[gi], best)
    vals = []
    for j in range(EXP_PER_GROUP):
        v = p[j]
        for gi in range(1, N_GROUPS):
            v = jnp.where(gsel == gi, p[4 * gi + j], v)
        vals.append(v)
    sel = []
    for j in range(EXP_PER_GROUP):
        rank = jnp.zeros_like(gsel)
        for i in range(EXP_PER_GROUP):
            if i == j:
                continue
            ahead = (vals[i] > vals[j]) | ((vals[i] == vals[j]) & (i < j))
            rank = rank + jnp.where(ahead, 1, 0)
        sel.append(rank < 2)
    denom = jnp.zeros_like(best)
    for j in range(EXP_PER_GROUP):
        denom = denom + jnp.where(sel[j], vals[j], 0.0)
    gates = [jnp.where(sel[j], vals[j] / denom, 0.0) for j in range(EXP_PER_GROUP)]
    comb = [jnp.where(gsel == (e // EXP_PER_GROUP), gates[e % EXP_PER_GROUP], 0.0)
            for e in range(N_EXPERTS)]
    j1 = jnp.where(sel[0], 0, jnp.where(sel[1], 1, 2))
    j2 = jnp.where(sel[3], 3, jnp.where(sel[2], 2, 1))
    g1 = jnp.zeros_like(best)
    g2 = jnp.zeros_like(best)
    for j in range(EXP_PER_GROUP):
        g1 = jnp.where(j1 == j, gates[j], g1)
        g2 = jnp.where(j2 == j, gates[j], g2)
    return comb, gsel * EXP_PER_GROUP + j1, gsel * EXP_PER_GROUP + j2, g1, g2


MOE_TILE = 512
GRANULE = 16
GRANULES_PER_TILE = MOE_TILE // GRANULE
XS_WIDTH = D_MODEL + LANES
LARGE_PIECE = 8
ROUTE_SLOT1, ROUTE_SLOT2 = N_EXPERTS, N_EXPERTS + 1
PLAN_SLOT1, PLAN_SLOT2, PLAN_G1, PLAN_G2 = 0, 1, 2, 3


def _local_slots(tt):
    worst = 2 * tt + (N_EXPERTS - 1) * GRANULE
    return -(-worst // LANES) * LANES


def _outproj_kernel(*refs, plan, sub, st):
    if plan:
        (y_ref, x_ref, w_ref, g_ref, b_ref, rwt_ref, rb_ref, tri_ref, ltri_ref,
         x1_ref, route_ref, plan_ref, gran_ref, loff_ref, rt_sc, oh_sc) = refs
    else:
        y_ref, x_ref, w_ref, g_ref, b_ref, rwt_ref, rb_ref, x1_ref, route_ref, rt_sc = refs
    if w_ref.dtype == BF16:
        y = jnp.dot(y_ref[...].astype(BF16), w_ref[...], preferred_element_type=F32)
    else:
        y = _dot3(y_ref[...], w_ref[...])
    x1 = _layernorm(ALPHA * x_ref[...] + y, g_ref[...], b_ref[...])
    x1_ref[...] = x1
    hi, lo = _split(x1)
    rhi, rlo = _split(rwt_ref[...])
    lg = (lax.dot_general(rhi, hi, _NT, preferred_element_type=F32)
          + lax.dot_general(rhi, lo, _NT, preferred_element_type=F32)
          + lax.dot_general(rlo, hi, _NT, preferred_element_type=F32)) + rb_ref[...]
    comb, e1, e2, g1, g2 = _route_rows([lg[e:e + 1, :] for e in range(N_EXPERTS)])
    rt_sc[...] = jnp.zeros_like(rt_sc)
    if not plan:
        for e in range(N_EXPERTS):
            rt_sc[e:e + 1, :] = comb[e]
    else:
        for e in range(N_EXPERTS):
            oh_sc[e:e + 1, :] = jnp.where((e1 == e) | (e2 == e), 1.0, 0.0)
        bases = []
        for s in range(sub):
            oh = oh_sc[:, s * st:(s + 1) * st]
            cum = jnp.dot(oh.astype(BF16), tri_ref[...], preferred_element_type=F32)
            count = jnp.sum(oh, axis=1, keepdims=True)
            gran = jnp.floor((count + (GRANULE - 1)) * (1.0 / GRANULE))
            gran_b = jnp.broadcast_to(gran, (N_EXPERTS, LANES))
            loff_b = jnp.dot(ltri_ref[...], gran_b.astype(BF16), preferred_element_type=F32)
            bases.append(loff_b[:, 0:1] * GRANULE - 1.0 + cum)
            gran_ref[s] = gran_b.astype(jnp.int32)
            loff_ref[s] = loff_b.astype(jnp.int32)
        slot_base = jnp.concatenate(bases, axis=1)
        s1 = jnp.zeros_like(g1)
        s2 = jnp.zeros_like(g1)
        for e in range(N_EXPERTS):
            row = slot_base[e:e + 1, :]
            s1 = jnp.where(e1 == e, row, s1)
            s2 = jnp.where(e2 == e, row, s2)
        for s in range(sub):
            cols = slice(s * st, (s + 1) * st)
            plan_ref[s] = jnp.zeros((SUBLANES, st), F32)
            plan_ref[s, PLAN_SLOT1:PLAN_SLOT1 + 1, :] = s1[:, cols]
            plan_ref[s, PLAN_SLOT2:PLAN_SLOT2 + 1, :] = s2[:, cols]
            plan_ref[s, PLAN_G1:PLAN_G1 + 1, :] = g1[:, cols]
            plan_ref[s, PLAN_G2:PLAN_G2 + 1, :] = g2[:, cols]
        rt_sc[ROUTE_SLOT1:ROUTE_SLOT1 + 1, :] = s1
        rt_sc[ROUTE_SLOT2:ROUTE_SLOT2 + 1, :] = s2
    route_ref[...] = rt_sc[...].T


def _outproj(y, x, w_out_bf, layer, st, g, b, rwt, rb, plan=False, sub=1):
    n = x.shape[0]
    tm = st * sub
    full = lambda shape: pl.BlockSpec(shape, lambda i: (0,) * len(shape))
    lay = lambda shape: pl.BlockSpec((None,) + shape, lambda i: (layer,) + (0,) * len(shape))
    in_specs = [pl.BlockSpec((tm, D_MODEL), lambda i: (i, 0)),
                pl.BlockSpec((tm, D_MODEL), lambda i: (i, 0)),
                lay((D_MODEL, D_MODEL)), lay((1, D_MODEL)), lay((1, D_MODEL)),
                full((N_EXPERTS, D_MODEL)), full((N_EXPERTS, 1))]
    out_specs = [pl.BlockSpec((tm, D_MODEL), lambda i: (i, 0)),
                 pl.BlockSpec((tm, LANES), lambda i: (i, 0))]
    out_shape = [jax.ShapeDtypeStruct((n, D_MODEL), F32),
                 jax.ShapeDtypeStruct((n, LANES), F32)]
    scratch = [pltpu.VMEM((LANES, tm), F32)]
    args = [y, x, w_out_bf, g, b, rwt, rb]
    if plan:
        tri = np.triu(np.ones((st, st), np.float32)).astype(jnp.bfloat16)
        ltri = np.tril(np.ones((N_EXPERTS, N_EXPERTS), np.float32), -1).astype(jnp.bfloat16)
        in_specs += [full((st, st)), full((N_EXPERTS, N_EXPERTS))]
        args += [tri, ltri]
        per_tile = pl.BlockSpec((sub, N_EXPERTS, LANES), lambda i: (i, 0, 0))
        out_specs += [pl.BlockSpec((sub, SUBLANES, st), lambda i: (i, 0, 0)), per_tile, per_tile]
        out_shape += [jax.ShapeDtypeStruct((n // st, SUBLANES, st), F32),
                      jax.ShapeDtypeStruct((n // st, N_EXPERTS, LANES), jnp.int32),
                      jax.ShapeDtypeStruct((n // st, N_EXPERTS, LANES), jnp.int32)]
        scratch += [pltpu.VMEM((N_EXPERTS, tm), F32)]
    return pl.pallas_call(
        functools.partial(_outproj_kernel, plan=plan, sub=sub, st=st),
        grid=(n // tm,),
        in_specs=in_specs,
        out_specs=out_specs,
        out_shape=out_shape,
        scratch_shapes=scratch,
        compiler_params=_cparams(1),
        name="outproj_plan" if plan else "outproj",
    )(*args)


def _granule_copies(src, dst, src_g, dst_g, n, sem, start, max_granules):
    def pieces(sizes, src_g, dst_g):
        for size in sizes:
            bit = n & size

            @pl.when(bit != 0)
            def _(size=size, src_g=src_g, dst_g=dst_g):
                s0 = pl.multiple_of(src_g * GRANULE, GRANULE)
                d0 = pl.multiple_of(dst_g * GRANULE, GRANULE)
                cp = pltpu.make_async_copy(src.at[pl.ds(s0, size * GRANULE)],
                                           dst.at[pl.ds(d0, size * GRANULE)], sem)
                cp.start() if start else cp.wait()

            src_g = src_g + bit
            dst_g = dst_g + bit

    sizes = [max_granules >> k for k in range(max_granules.bit_length())]
    large = [s for s in sizes if s >= LARGE_PIECE]
    small = [s for s in sizes if s < LARGE_PIECE]

    @pl.when(n >= LARGE_PIECE)
    def _():
        pieces(large, src_g, dst_g)

    done = n & sum(large)
    pieces(small, src_g + done, dst_g + done)


def _dispatch_kernel(segg_ref, totg_ref, ctl_ref, ctl_prev_ref, plan_ref, x_ref, xs_ref,
                     xl_sc, zx_sc, sem, *, tt, total_tiles):
    step = pl.program_id(0)
    cur = step % 2

    @pl.when(step == 0)
    def _():
        zx_sc[...] = jnp.zeros_like(zx_sc)

        def zero_tile(i):
            r0 = pl.multiple_of(i * MOE_TILE, MOE_TILE)
            return pltpu.make_async_copy(zx_sc, xs_ref.at[pl.ds(r0, MOE_TILE)], sem.at[2])

        def run(i, start):
            cp = zero_tile(i)
            cp.start() if start else cp.wait()

        for start in (True, False):
            for e in range(N_EXPERTS):
                end = segg_ref[e] + totg_ref[e]

                @pl.when(end % GRANULES_PER_TILE != 0)
                def _():
                    run(end // GRANULES_PER_TILE, start)

        last = N_EXPERTS - 1
        used = (segg_ref[last] + totg_ref[last] + GRANULES_PER_TILE - 1) // GRANULES_PER_TILE
        lax.fori_loop(used, total_tiles, lambda i, c: (run(i, True), c)[1], 0)
        lax.fori_loop(used, total_tiles, lambda i, c: (run(i, False), c)[1], 0)

    n_slots = xl_sc.shape[1]
    xl = xl_sc.at[cur]
    slot = lax.broadcasted_iota(jnp.int32, (n_slots, tt), 0)
    m1 = slot == plan_ref[PLAN_SLOT1:PLAN_SLOT1 + 1, :].astype(jnp.int32)
    m2 = slot == plan_ref[PLAN_SLOT2:PLAN_SLOT2 + 1, :].astype(jnp.int32)
    perm = jnp.where(m1 | m2, 1.0, 0.0).astype(BF16)
    xl[:, :D_MODEL] = jnp.dot(
        perm, x_ref[...].astype(BF16), preferred_element_type=F32).astype(BF16)
    gate = jnp.sum(jnp.where(m1, plan_ref[PLAN_G1:PLAN_G1 + 1, :], 0.0)
                   + jnp.where(m2, plan_ref[PLAN_G2:PLAN_G2 + 1, :], 0.0), axis=1, keepdims=True)
    g_hi = gate.astype(BF16).astype(F32)
    g_lo = (gate - g_hi).astype(BF16).astype(F32)
    lane = lax.broadcasted_iota(jnp.int32, (n_slots, LANES), 1)
    xl[:, D_MODEL:] = jnp.where(lane == 0, g_hi, jnp.where(lane == 1, g_lo, 0.0)).astype(BF16)

    def copies(ctl, half, start):
        for e in range(N_EXPERTS):
            n, src_g, dst_g = ctl[0, e], ctl[1, e], ctl[2, e]
            _granule_copies(xl_sc.at[half], xs_ref, src_g, dst_g, n, sem.at[half], start,
                            tt // GRANULE)

    copies(ctl_ref, cur, True)

    @pl.when(step > 0)
    def _():
        copies(ctl_prev_ref, 1 - cur, False)

    @pl.when(step == pl.num_programs(0) - 1)
    def _():
        copies(ctl_ref, cur, False)


def _dispatch(x1, plan, ctl, seg_g, tot_g, tt, rows):
    n = x1.shape[0]
    n_slots = _local_slots(tt)
    return pl.pallas_call(
        functools.partial(_dispatch_kernel, tt=tt, total_tiles=rows // MOE_TILE),
        grid_spec=pltpu.PrefetchScalarGridSpec(
            num_scalar_prefetch=2,
            grid=(n // tt,),
            in_specs=[pl.BlockSpec((None, SUBLANES, LANES), lambda i, s, c: (i, 0, 0),
                                   memory_space=pltpu.SMEM),
                      pl.BlockSpec((None, SUBLANES, LANES), lambda i, s, c: (jnp.maximum(i - 1, 0), 0, 0),
                                   memory_space=pltpu.SMEM),
                      pl.BlockSpec((None, SUBLANES, tt), lambda i, s, c: (i, 0, 0)),
                      pl.BlockSpec((tt, D_MODEL), lambda i, s, c: (i, 0))],
            out_specs=pl.BlockSpec(memory_space=pl.ANY),
            scratch_shapes=[pltpu.VMEM((2, n_slots, XS_WIDTH), BF16),
                            pltpu.VMEM((MOE_TILE, XS_WIDTH), BF16),
                            pltpu.SemaphoreType.DMA((3,))]),
        out_shape=jax.ShapeDtypeStruct((rows, XS_WIDTH), BF16),
        compiler_params=_cparams(1),
        name="moe_dispatch",
    )(seg_g, tot_g, ctl, ctl, plan, x1)


def _ffn_kernel(te_ref, nv_ref, xs_ref, wg_ref, wu_ref, wd_ref, ys_ref):
    used = pl.program_id(0) < nv_ref[0]

    @pl.when(used)
    def _():
        xb = xs_ref[:, :D_MODEL]
        gate_pair = xs_ref[:, D_MODEL:].astype(F32)
        gate = gate_pair[:, 0:1] + gate_pair[:, 1:2]
        hg = jnp.dot(xb, wg_ref[...].astype(BF16), preferred_element_type=F32)
        hu = jnp.dot(xb, wu_ref[...].astype(BF16), preferred_element_type=F32)
        h = (_silu(hg) * hu * gate).astype(BF16)
        ys_ref[...] = jnp.dot(h, wd_ref[...].astype(BF16), preferred_element_type=F32).astype(BF16)

    @pl.when(jnp.logical_not(used))
    def _():
        ys_ref[...] = jnp.zeros_like(ys_ref)


def _ffn(xs, tile_expert, n_valid, wg, wu, wd, layer):
    total_tiles = xs.shape[0] // MOE_TILE
    w_in_spec = pl.BlockSpec((None, None, D_MODEL, D_EXPERT), lambda i, te, nv: (layer, te[i], 0, 0))
    return pl.pallas_call(
        _ffn_kernel,
        grid_spec=pltpu.PrefetchScalarGridSpec(
            num_scalar_prefetch=2,
            grid=(total_tiles,),
            in_specs=[pl.BlockSpec((MOE_TILE, XS_WIDTH), lambda i, te, nv: (jnp.minimum(i, nv[0] - 1), 0)),
                      w_in_spec, w_in_spec,
                      pl.BlockSpec((None, None, D_EXPERT, D_MODEL),
                                   lambda i, te, nv: (layer, te[i], 0, 0))],
            out_specs=pl.BlockSpec((MOE_TILE, D_MODEL), lambda i, te, nv: (i, 0))),
        out_shape=jax.ShapeDtypeStruct((xs.shape[0], D_MODEL), BF16),
        compiler_params=_cparams(1),
        name="moe_ffn",
    )(tile_expert, n_valid, xs, wg, wu, wd)


def _combine_kernel(ctl_ref, ctl_next_ref, x_ref, route_ref, g_ref, b_ref, ys_ref, o_ref,
                    yl_sc, sem, *, tt):
    i = pl.program_id(0)
    cur = i % 2

    def copies(ctl, half, start):
        for e in range(N_EXPERTS):
            n, loc_g, buf_g = ctl[0, e], ctl[1, e], ctl[2, e]
            _granule_copies(ys_ref, yl_sc.at[half], buf_g, loc_g, n, sem.at[half], start,
                            tt // GRANULE)

    @pl.when(i == 0)
    def _():
        yl_sc[...] = jnp.zeros_like(yl_sc)
        copies(ctl_ref, 0, True)

    @pl.when(i + 1 < pl.num_programs(0))
    def _():
        copies(ctl_next_ref, 1 - cur, True)

    copies(ctl_ref, cur, False)

    n_slots = yl_sc.shape[1]
    route = route_ref[...]
    slot = lax.broadcasted_iota(jnp.int32, (tt, n_slots), 1)
    pick = ((slot == route[:, ROUTE_SLOT1:ROUTE_SLOT1 + 1].astype(jnp.int32))
            | (slot == route[:, ROUTE_SLOT2:ROUTE_SLOT2 + 1].astype(jnp.int32)))
    y = jnp.dot(jnp.where(pick, 1.0, 0.0).astype(BF16), yl_sc[cur], preferred_element_type=F32)
    o_ref[...] = _layernorm(ALPHA * x_ref[...] + y, g_ref[...], b_ref[...])


def _combine(x1, route, ctl, ys, layer, tt, g, b):
    n = x1.shape[0]
    last = n // tt - 1
    lay = lambda shape: pl.BlockSpec((None,) + shape, lambda i: (layer,) + (0,) * len(shape))
    return pl.pallas_call(
        functools.partial(_combine_kernel, tt=tt),
        grid=(n // tt,),
        in_specs=[pl.BlockSpec((None, SUBLANES, LANES), lambda i: (i, 0, 0), memory_space=pltpu.SMEM),
                  pl.BlockSpec((None, SUBLANES, LANES), lambda i: (jnp.minimum(i + 1, last), 0, 0),
                               memory_space=pltpu.SMEM),
                  pl.BlockSpec((tt, D_MODEL), lambda i: (i, 0)),
                  pl.BlockSpec((tt, LANES), lambda i: (i, 0)),
                  lay((1, D_MODEL)), lay((1, D_MODEL)),
                  pl.BlockSpec(memory_space=pl.ANY)],
        out_specs=pl.BlockSpec((tt, D_MODEL), lambda i: (i, 0)),
        scratch_shapes=[pltpu.VMEM((2, _local_slots(tt), D_MODEL), BF16),
                        pltpu.SemaphoreType.DMA((2,))],
        out_shape=jax.ShapeDtypeStruct((n, D_MODEL), F32),
        compiler_params=_cparams(1),
        name="moe_combine",
    )(ctl, ctl, x1, route, g, b, ys)


def _moe_sparse(x1, route, plan, gran, loff, wg, wu, wd, layer, tt, g, b):
    n = x1.shape[0]
    n_tiles = n // tt
    gran = gran[:, :, 0]
    loff = loff[:, :, 0]
    max_tiles = -(-(2 * n + (GRANULE - 1) * N_EXPERTS * n_tiles) // MOE_TILE) + N_EXPERTS
    rows = max_tiles * MOE_TILE
    tot_g = jnp.sum(gran, axis=0)
    tiles = (tot_g + GRANULES_PER_TILE - 1) // GRANULES_PER_TILE
    tile_end = jnp.cumsum(tiles)
    seg_g = ((tile_end - tiles) * GRANULES_PER_TILE).astype(jnp.int32)
    n_valid = tile_end[-1]
    idx = jnp.minimum(jnp.arange(max_tiles, dtype=jnp.int32), n_valid - 1)
    tile_expert = jnp.sum(idx[:, None] >= tile_end[None, :], axis=1).astype(jnp.int32)
    buf_g = seg_g[None, :] + jnp.cumsum(gran, axis=0) - gran
    ctl = jnp.zeros((n_tiles, SUBLANES, LANES), jnp.int32)
    ctl = ctl.at[:, 0, :N_EXPERTS].set(gran).at[:, 1, :N_EXPERTS].set(loff).at[:, 2, :N_EXPERTS].set(buf_g)
    xs = _dispatch(x1, plan, ctl, seg_g, tot_g.astype(jnp.int32), tt, rows)
    ys = _ffn(xs, tile_expert, n_valid.reshape(1).astype(jnp.int32), wg, wu, wd, layer)
    return _combine(x1, route, ctl, ys, layer, tt, g, b)


def _moe_kernel(x_ref, r_ref, wg_ref, wu_ref, wd_ref, g_ref, b_ref, o_ref, xb_sc, acc_sc):
    e = pl.program_id(1)

    @pl.when(e == 0)
    def _():
        xb_sc[...] = x_ref[...].astype(BF16)
        acc_sc[...] = jnp.zeros_like(acc_sc)

    xb = xb_sc[...]
    hg = jnp.dot(xb, wg_ref[...].astype(BF16), preferred_element_type=F32)
    hu = jnp.dot(xb, wu_ref[...].astype(BF16), preferred_element_type=F32)
    r = r_ref[...]
    lane = lax.broadcasted_iota(jnp.int32, r.shape, 1)
    c = jnp.sum(jnp.where(lane == e, r, 0.0), axis=-1, keepdims=True)
    h = _silu(hg) * hu * c
    acc_sc[...] += jnp.dot(h.astype(BF16), wd_ref[...].astype(BF16), preferred_element_type=F32)

    @pl.when(e == pl.num_programs(1) - 1)
    def _():
        o_ref[...] = _layernorm(ALPHA * x_ref[...] + acc_sc[...], g_ref[...], b_ref[...])


def _moe(x1, route, wg, wu, wd, layer, tm, g, b):
    n = x1.shape[0]
    lay = lambda shape: pl.BlockSpec((None,) + shape, lambda i, e: (layer,) + (0,) * len(shape))
    return pl.pallas_call(
        _moe_kernel,
        grid=(n // tm, N_EXPERTS),
        in_specs=[pl.BlockSpec((tm, D_MODEL), lambda i, e: (i, 0)),
                  pl.BlockSpec((tm, LANES), lambda i, e: (i, 0)),
                  pl.BlockSpec((None, None, D_MODEL, D_EXPERT), lambda i, e: (layer, e, 0, 0)),
                  pl.BlockSpec((None, None, D_MODEL, D_EXPERT), lambda i, e: (layer, e, 0, 0)),
                  pl.BlockSpec((None, None, D_EXPERT, D_MODEL), lambda i, e: (layer, e, 0, 0)),
                  lay((1, D_MODEL)), lay((1, D_MODEL))],
        out_specs=pl.BlockSpec((tm, D_MODEL), lambda i, e: (i, 0)),
        out_shape=jax.ShapeDtypeStruct((n, D_MODEL), F32),
        scratch_shapes=[pltpu.VMEM((tm, D_MODEL), BF16), pltpu.VMEM((tm, D_MODEL), F32)],
        compiler_params=_cparams(2),
        name="moe",
    )(x1, route, wg, wu, wd, g, b)


def _block_diag(w):
    out = jnp.zeros((DEPTH, D_A, D_A), w.dtype)
    for gi in range(A_BLOCKS):
        out = out.at[:, gi * A_BLK:(gi + 1) * A_BLK, gi * A_BLK:(gi + 1) * A_BLK].set(w[:, gi])
    return out


def kernel(x_prompt, x_sample, state_rglru_h, state_conv, state_hgrn, state_ret, w_in, conv_w, conv_b, w_rgate, b_rgate, w_igate, b_igate, rglru_lambda, hgrn_lb_logits, hgrn_norm_g, w_out, ln1_g, ln1_b, router_w, router_b, exp_w_gate, exp_w_up, exp_w_down, ln2_g, ln2_b):
    batch, seq, _ = x_prompt.shape
    nb = x_sample.shape[0]

    w_in_bf = w_in.astype(BF16)
    w_out_bf = w_out.astype(BF16)
    wg, wu, wd = exp_w_gate, exp_w_up, exp_w_down
    wri = jnp.concatenate([_block_diag(w_rgate), _block_diag(w_igate)], axis=-1).astype(BF16)
    bri = jnp.concatenate([b_rgate.reshape(DEPTH, 1, D_A), b_igate.reshape(DEPTH, 1, D_A)], axis=-1)
    cb = conv_b.reshape(DEPTH, 1, D_A)
    lam = rglru_lambda.reshape(DEPTH, 1, D_A)
    ng = jnp.tile(hgrn_norm_g, (1, B_HEADS)).reshape(DEPTH, 1, D_B)
    g1, b1 = ln1_g.reshape(DEPTH, 1, D_MODEL), ln1_b.reshape(DEPTH, 1, D_MODEL)
    g2, b2 = ln2_g.reshape(DEPTH, 1, D_MODEL), ln2_b.reshape(DEPTH, 1, D_MODEL)
    rwt = router_w.T
    rb = router_b.reshape(N_EXPERTS, 1)
    conv0_t = jnp.transpose(state_conv, (0, 2, 1, 3))
    hg0_t = jnp.transpose(state_hgrn, (0, 2, 3, 4, 1))

    xp = x_prompt.reshape(batch * seq, D_MODEL)
    xs = x_sample.reshape(nb, D_MODEL)
    hs_p, convs_p, hgs_p, rts_p = [], [], [], []
    prev_s, prev_hg = [], []
    for l in range(DEPTH):
        y_p, h_p, conv_p, hg_p, rt_p = _mix_prompt(
            xp, w_in_bf, batch, seq, l, PROMPT_TILE, conv_w, cb, wri, bri, lam, hgrn_lb_logits, ng)
        x1_p, route_p, plan_p, gran_p, loff_p = _outproj(
            y_p, xp, w_out_bf, l, PROMPT_TILE, g1, b1, rwt, rb, plan=True, sub=2)
        xp = _moe_sparse(x1_p, route_p, plan_p, gran_p, loff_p, wg, wu, wd, l, PROMPT_TILE, g2, b2)
        hs_p.append(h_p.reshape(batch, D_A))
        convs_p.append(conv_p)
        hgs_p.append(hg_p)
        rts_p.append(rt_p)

        proj_s = _proj(xs, w_in, l, nb)
        last = l == DEPTH - 1
        y_s, *states_s = _mix_sample(
            proj_s, state_rglru_h, conv0_t, state_ret, l, conv_w, cb, wri, bri, lam,
            prev=prev_s if last else ())
        prev_s.append(states_s)
        y_s, hg_s = _hgrn_sample(proj_s, y_s, hg0_t, l, hgrn_lb_logits, ng[:, :, :LANES],
                                 prev=prev_hg if last else ())
        prev_hg.append(hg_s)
        x1_s, route_s = _outproj(y_s, xs, w_out, l, nb, g1, b1, rwt, rb)
        xs = _moe(x1_s, route_s, wg, wu, wd, l, nb, g2, b2)

    h_s, conv_s, rt_s = prev_s[-1]
    hg_s = jnp.transpose(prev_hg[-1], (0, 4, 1, 2, 3))
    return (xp.reshape(batch, seq, D_MODEL), xs.reshape(nb, 1, D_MODEL),
            jnp.stack(hs_p), h_s, jnp.stack(convs_p), jnp.transpose(conv_s, (0, 2, 1, 3)),
            jnp.stack(hgs_p), hg_s, jnp.stack(rts_p), rt_s)
```

```python
import functools

import numpy as np
import jax
import jax.numpy as jnp
from jax import lax
from jax.experimental import pallas as pl
from jax.experimental.pallas import tpu as pltpu

D_MODEL = 1024
DEPTH = 2
PAST_LEN = 16384
D_A = 256
A_BLOCKS = 4
A_BLK = 64
CONV_W = 4
RGLRU_C = 8.0
B_HEADS = 4
B_DK = 64
D_B = 256
C_HEADS = 4
C_DK = 128
D_C = 512
D_IN = 3584
B_CHUNK = 64
C_CHUNK = 128
ROPE_BASE = 10000.0
N_EXPERTS = 16
N_GROUPS = 4
EXP_PER_GROUP = 4
D_EXPERT = 512
LN_EPS = 1e-5
RMS_EPS = 1e-6
GN_EPS = 1e-6
F_TINY = 1e-30
ALPHA = (2 * DEPTH) ** 0.25

O_XA, O_GA, O_QB, O_FB, O_VB, O_GB, O_QC, O_KC, O_VC, O_GC = (
    0, 256, 512, 768, 1024, 1280, 1536, 2048, 2560, 3072)

V7X_VMEM_LIMIT_BYTES = 56 * 1024 * 1024
SUBLANES = 8
LANES = 128
PROMPT_TILE = 512
HGRN_SAFE_MIN_LOGDECAY = -60.0

BF16 = jnp.bfloat16
F32 = jnp.float32
_NT = (((1,), (1,)), ((), ()))
_TN = (((0,), (0,)), ((), ()))


def _cparams(n_axes):
    return pltpu.CompilerParams(
        dimension_semantics=("arbitrary",) * n_axes,
        vmem_limit_bytes=V7X_VMEM_LIMIT_BYTES)


def _dot(a, b):
    return jnp.dot(a.astype(BF16), b.astype(BF16), preferred_element_type=F32)


def _dot_g(a, b, dims):
    return lax.dot_general(a.astype(BF16), b.astype(BF16), dims, preferred_element_type=F32)


def _sigmoid(x):
    return 0.5 * jnp.tanh(0.5 * x) + 0.5


def _sqrt_nonneg(x):
    return jnp.where(x > 0.0, x * lax.rsqrt(x), 0.0)


def _silu(x):
    return x * _sigmoid(x)


def _gelu_tanh(x):
    c = np.float32(np.sqrt(2.0 / np.pi))
    return 0.5 * x * (1.0 + jnp.tanh(c * (x + np.float32(0.044715) * (x * x * x))))


def _log_sigmoid(x):
    return -(jnp.maximum(-x, 0.0) + jnp.log(1.0 + jnp.exp(-jnp.abs(x))))


def _layernorm(z, g, b):
    mu = jnp.mean(z, axis=-1, keepdims=True)
    zc = z - mu
    var = jnp.mean(zc * zc, axis=-1, keepdims=True)
    return zc * lax.rsqrt(var + LN_EPS) * g + b


def _hgrn_lower_bound(lbl, layer):
    rows = [lbl[j:j + 1, :] for j in range(DEPTH)]
    m = rows[0]
    for r in rows[1:]:
        m = jnp.maximum(m, r)
    ex = [jnp.exp(r - m) for r in rows]
    tot = ex[0]
    for e in ex[1:]:
        tot = tot + e
    lb = jnp.zeros_like(m)
    for j in range(1, layer + 1):
        lb = lb + ex[j] / tot
    return lb


def _rglru_gates(u, wri, bri, lam):
    gates = _dot(u, wri) + bri
    r = _sigmoid(gates[:, :D_A])
    i = _sigmoid(gates[:, D_A:])
    log_a = RGLRU_C * r * _log_sigmoid(lam)
    a = jnp.exp(log_a)
    bterm = _sqrt_nonneg(jnp.maximum(1.0 - a * a, 0.0)) * (i * u)
    return a, bterm


def _split(x):
    hi = x.astype(BF16)
    return hi, (x - hi.astype(F32)).astype(BF16)


def _proj_kernel(x_ref, w_ref, o_ref, wb_ref):
    xh, xl = _split(x_ref[...])
    for j in range(0, D_IN, 512):
        wh, wl = _split(w_ref[:, j:j + 512])
        wb_ref[:, j:j + 512] = wh
        o_ref[:, j:j + 512] = (jnp.dot(xh, wh, preferred_element_type=F32)
                               + jnp.dot(xl, wh, preferred_element_type=F32)
                               + jnp.dot(xh, wl, preferred_element_type=F32))


def _proj(x, w_in, layer):
    n = x.shape[0]
    return pl.pallas_call(
        _proj_kernel,
        grid=(1,),
        in_specs=[pl.BlockSpec((n, D_MODEL), lambda i: (0, 0)),
                  pl.BlockSpec((None, D_MODEL, D_IN), lambda i: (layer, 0, 0))],
        out_specs=[pl.BlockSpec((n, D_IN), lambda i: (0, 0)),
                   pl.BlockSpec((D_MODEL, D_IN), lambda i: (0, 0))],
        out_shape=[jax.ShapeDtypeStruct((n, D_IN), F32),
                   jax.ShapeDtypeStruct((D_MODEL, D_IN), BF16)],
        compiler_params=_cparams(1),
        name="proj",
    )(x, w_in)


def _retention_consts(chunk):
    lg = np.log1p(-np.exp2(-5.0 - np.arange(C_HEADS, dtype=np.float64)))
    idx = np.arange(chunk, dtype=np.float64)
    rel = idx[:, None] - idx[None, :]
    mask = rel >= 0
    dmat = np.where(mask[None], np.exp(np.where(mask, rel, 0.0)[None] * lg[:, None, None]), 0.0)
    qdec = np.exp((idx + 1.0)[None, :] * lg[:, None])
    kdec = np.exp((chunk - 1.0 - idx)[None, :] * lg[:, None])
    sdec = np.exp(chunk * lg)
    qdec_b = np.broadcast_to(qdec[:, :, None], (C_HEADS, chunk, C_DK))
    kdec_b = np.broadcast_to(kdec[:, :, None], (C_HEADS, chunk, C_DK))
    return (dmat.astype(np.float32), np.ascontiguousarray(qdec_b).astype(np.float32),
            np.ascontiguousarray(kdec_b).astype(np.float32), [float(v) for v in sdec])


def _rope_tables(positions):
    half = C_DK // 2
    inv = ROPE_BASE ** (-np.arange(half, dtype=np.float64) / half)
    ang = np.asarray(positions, dtype=np.float64)[:, None] * inv[None]
    cos = np.concatenate([np.cos(ang), np.cos(ang)], axis=-1)
    sin = np.concatenate([-np.sin(ang), np.sin(ang)], axis=-1)
    return cos.astype(np.float32), sin.astype(np.float32)


def _rope(x, cos, sin_signed):
    return x * cos + pltpu.roll(x, C_DK // 2, 1) * sin_signed


PROJ_SPLIT = 1536


def _mix_prompt_kernel(x0_ref, xn_ref, w_ref, cos_ref, sin_ref, cw_ref, cb_ref, wri_ref, bri_ref,
                       lam_ref, lbl_ref, ng_ref, dmat_ref, qdec_ref, kdec_ref,
                       y_ref, h_ref, conv_ref, hg_ref, rt_ref,
                       prev_sc, hprev_sc, st_sc, sret_sc, kb_sc, bb_sc, vb_sc, oi_sc, p_ref, next_ref, xb_sc, x0b_sc,
                       *, layer, tt, sdec):
    t = pl.program_id(1)
    nt = pl.num_programs(1)
    flat = pl.program_id(0) * nt + t

    def project(xb_ref, dst, c0, c1):
        xb = xb_ref[...]
        for j in range(c0, c1, 256):
            dst[:, j:j + 256] = jnp.dot(xb, w_ref[:, j:j + 256], preferred_element_type=F32)

    xb_sc[...] = xn_ref[...].astype(BF16)

    @pl.when(flat == 0)
    def _():
        x0b_sc[...] = x0_ref[...].astype(BF16)
        project(x0b_sc, p_ref, 0, D_IN)

    @pl.when(t == 0)
    def _():
        prev_sc[...] = jnp.zeros_like(prev_sc)
        hprev_sc[...] = jnp.zeros_like(hprev_sc)
        st_sc[...] = jnp.zeros_like(st_sc)
        sret_sc[...] = jnp.zeros_like(sret_sc)

    xa = p_ref[:, O_XA:O_XA + D_A]
    ga = p_ref[:, O_GA:O_GA + D_A]
    row = lax.broadcasted_iota(jnp.int32, (tt, D_A), 0)
    row8 = lax.broadcasted_iota(jnp.int32, (SUBLANES, D_A), 0)
    prev = prev_sc[...]

    def shifted(j):
        r = pltpu.roll(xa, j, 0)
        top = jnp.where(row8 < j, pltpu.roll(prev, j, 0), r[0:SUBLANES])
        return jnp.concatenate([top, r[SUBLANES:]], axis=0)

    u = cb_ref[...] + shifted(3) * cw_ref[0:1, :]
    u = u + shifted(2) * cw_ref[1:2, :]
    u = u + shifted(1) * cw_ref[2:3, :]
    u = u + xa * cw_ref[3:4, :]
    last8 = xa[tt - SUBLANES:tt]
    prev_sc[...] = last8

    a, bterm = _rglru_gates(u, wri_ref[...], bri_ref[...], lam_ref[...])
    project(xb_sc, next_ref, 0, PROJ_SPLIT)
    s = 1
    while s < SUBLANES:
        keep = (row % SUBLANES) >= s
        a_sh = jnp.where(keep, pltpu.roll(a, s, 0), 1.0)
        b_sh = jnp.where(keep, pltpu.roll(bterm, s, 0), 0.0)
        bterm = a * b_sh + bterm
        a = a * a_sh
        s *= 2
    carry = hprev_sc[...]
    groups = []
    for gi in range(tt // SUBLANES):
        rows = slice(gi * SUBLANES, (gi + 1) * SUBLANES)
        h_in = jnp.broadcast_to(carry[SUBLANES - 1:SUBLANES, :], (SUBLANES, D_A))
        carry = a[rows] * h_in + bterm[rows]
        groups.append(carry)
    h = jnp.concatenate(groups, axis=0)
    hlast8 = carry
    hprev_sc[...] = hlast8
    y_ref[:, 0:D_A] = h * _gelu_tanh(ga)

    lb = _hgrn_lower_bound(lbl_ref[...], layer)
    ng = ng_ref[...]
    cl = B_CHUNK
    crow = lax.broadcasted_iota(jnp.int32, (cl, D_B), 0)
    ccol = lax.broadcasted_iota(jnp.int32, (cl, D_B), 1)
    causal = (ccol % B_DK) <= crow
    br = lax.broadcasted_iota(jnp.int32, (D_B, D_B), 0)
    bc = lax.broadcasted_iota(jnp.int32, (D_B, D_B), 1)
    head_mask = (br // B_DK) == (bc // B_DK)
    seg = jnp.where(head_mask, 1.0, 0.0).astype(BF16)

    def seg_mean(x):
        hi = x.astype(BF16)
        lo = (x - hi.astype(F32)).astype(BF16)
        tot = (jnp.dot(hi, seg, preferred_element_type=F32)
               + jnp.dot(lo, seg, preferred_element_type=F32))
        return tot * (1.0 / B_DK)

    nck = tt // cl
    q = _silu(p_ref[:, O_QB:O_QB + D_B])
    fl = p_ref[:, O_FB:O_FB + D_B]
    v = p_ref[:, O_VB:O_VB + D_B]
    half_th = 0.5 * jnp.tanh(0.5 * fl)
    f = lb + (1.0 - lb) * (0.5 + half_th)
    b = jnp.log(jnp.maximum(f, F_TINY))
    k = (1.0 - lb) * (0.5 - half_th)
    trow = lax.broadcasted_iota(jnp.int32, (tt, D_B), 0) % cl
    sh = 1
    while sh < cl:
        b = b + jnp.where(trow >= sh, pltpu.roll(b, sh, 0), 0.0)
        sh *= 2

    def chunk_row(x, c, r):
        top = c * cl + r + 1
        return x[top - SUBLANES:top][SUBLANES - 1:SUBLANES]

    def spread(rows):
        return jnp.concatenate([jnp.broadcast_to(r, (cl, D_B)) for r in rows], axis=0)

    last_rows = [chunk_row(b, c, cl - 1) for c in range(nck)]
    mid_rows = [chunk_row(b, c, cl // 2 - 1) for c in range(nck)]
    b_last = spread(last_rows)
    b_mid = spread(mid_rows)
    qd = q * jnp.exp(b)
    kl = k * jnp.exp(b_last - b)
    lasts = jnp.concatenate(last_rows, axis=0)
    mids = jnp.concatenate(mid_rows, axis=0)
    safe = jnp.min(jnp.minimum(mids, lasts - mids)) >= HGRN_SAFE_MIN_LOGDECAY

    def next_state(st, c, sl):
        kv = jnp.where(head_mask, _dot_g(v[sl], kl[sl], _TN), 0.0)
        return st * jnp.exp(last_rows[c]) + kv

    @pl.when(safe)
    def _():
        qmid = q * jnp.exp(b - b_mid)
        kinv = k * jnp.exp(b_mid - b)
        st = st_sc[...]
        for c in range(nck):
            sl = slice(c * cl, (c + 1) * cl)
            k4 = jnp.where(head_mask, jnp.concatenate([kinv[sl]] * B_HEADS, axis=0), 0.0)
            sc = jnp.where(causal, _dot_g(qmid[sl], k4, _NT), 0.0)
            v4 = jnp.where(head_mask, jnp.concatenate([v[sl]] * B_HEADS, axis=0), 0.0)
            oi_sc[sl, :] = _dot(sc, v4) + _dot_g(qd[sl], st, _NT)
            st = next_state(st, c, sl)
        st_sc[...] = st

    @pl.when(jnp.logical_not(safe))
    def _():
        kb_sc[...] = k
        bb_sc[...] = b
        vb_sc[...] = v
        st = st_sc[...]
        for c in range(nck):
            sl = slice(c * cl, (c + 1) * cl)
            qc, bc_ = q[sl], b[sl]

            def pair(sidx, acc, c=c, qc=qc, bc_=bc_):
                src = pl.ds(c * cl + sidx, 1)
                e = jnp.exp(jnp.minimum(bc_ - bb_sc[src, :], 0.0)) * (qc * kb_sc[src, :])
                scr = jnp.dot(e.astype(BF16), seg, preferred_element_type=F32)
                return acc + jnp.where(crow >= sidx, scr, 0.0) * vb_sc[src, :]

            oi = lax.fori_loop(0, cl, pair, jnp.zeros((cl, D_B), F32))
            oi_sc[sl, :] = oi + _dot_g(qd[sl], st, _NT)
            st = next_state(st, c, sl)
        st_sc[...] = st

    o = oi_sc[...]
    o = o * lax.rsqrt(seg_mean(o * o) + RMS_EPS) * ng
    y_ref[:, D_A:D_A + D_B] = o * _silu(p_ref[:, O_GB:O_GB + D_B])

    rl = C_CHUNK
    per_head = (D_IN - PROJ_SPLIT) // C_HEADS
    cos = cos_ref[...]
    sin = sin_ref[...]
    for hh in range(C_HEADS):
        lo = hh * C_DK
        q = _rope(p_ref[:, O_QC + lo:O_QC + lo + C_DK], cos, sin)
        k = _rope(p_ref[:, O_KC + lo:O_KC + lo + C_DK], cos, sin) * (C_DK ** -0.5)
        v = p_ref[:, O_VC + lo:O_VC + lo + C_DK]
        s = sret_sc[hh]
        parts = []
        for c in range(tt // rl):
            sl = slice(c * rl, (c + 1) * rl)
            sc = _dot_g(q[sl], k[sl], _NT) * dmat_ref[hh]
            parts.append(_dot(sc, v[sl]) + _dot(q[sl] * qdec_ref[hh], s))
            s = sdec[hh] * s + _dot_g(k[sl] * kdec_ref[hh], v[sl], _TN)
        sret_sc[hh] = s
        project(xb_sc, next_ref, PROJ_SPLIT + hh * per_head, PROJ_SPLIT + (hh + 1) * per_head)
        o = jnp.concatenate(parts, axis=0)
        mu = jnp.mean(o, axis=-1, keepdims=True)
        oc = o - mu
        var = jnp.mean(oc * oc, axis=-1, keepdims=True)
        g = p_ref[:, O_GC + lo:O_GC + lo + C_DK]
        y_ref[:, D_A + D_B + lo:D_A + D_B + lo + C_DK] = oc * lax.rsqrt(var + GN_EPS) * _silu(g)

    p_ref[...] = next_ref[...]

    @pl.when(t == nt - 1)
    def _():
        h_ref[...] = pltpu.roll(hprev_sc[...], 1, 0)[0:1]
        conv_ref[...] = pltpu.roll(prev_sc[...], CONV_W - 1, 0)[0:CONV_W - 1]
        s_bd = st_sc[...].T
        for hh in range(B_HEADS):
            hg_ref[hh] = s_bd[hh * B_DK:(hh + 1) * B_DK, hh * B_DK:(hh + 1) * B_DK]
        rt_ref[...] = sret_sc[...]


def _mix_prompt(x, w_in_bf, batch, seq, layer, tt, cw, cb, wri, bri, lam, lbl, ng):
    nt = seq // tt
    last = batch * nt - 1
    dmat, qdec, kdec, sdec = _retention_consts(C_CHUNK)
    cos, sin = _rope_tables(np.arange(seq))
    full = lambda shape: pl.BlockSpec(shape, lambda b, t: (0,) * len(shape))
    lay = lambda shape: pl.BlockSpec((None,) + shape, lambda b, t: (layer,) + (0,) * len(shape))
    kern = functools.partial(_mix_prompt_kernel, layer=layer, tt=tt, sdec=sdec)
    return pl.pallas_call(
        kern,
        grid=(batch, nt),
        in_specs=[
            pl.BlockSpec((tt, D_MODEL), lambda b, t: (b * nt + t, 0)),
            pl.BlockSpec((tt, D_MODEL), lambda b, t: (jnp.minimum(b * nt + t + 1, last), 0)),
            full((D_MODEL, D_IN)),
            pl.BlockSpec((tt, C_DK), lambda b, t: (t, 0)),
            pl.BlockSpec((tt, C_DK), lambda b, t: (t, 0)),
            lay((CONV_W, D_A)), lay((1, D_A)), lay((D_A, 2 * D_A)), lay((1, 2 * D_A)),
            lay((1, D_A)), full((DEPTH, D_B)), lay((1, D_B)),
            full((C_HEADS, C_CHUNK, C_CHUNK)), full((C_HEADS, C_CHUNK, C_DK)),
            full((C_HEADS, C_CHUNK, C_DK)),
        ],
        out_specs=[
            pl.BlockSpec((tt, D_MODEL), lambda b, t: (b * nt + t, 0)),
            pl.BlockSpec((None, 1, D_A), lambda b, t: (b, 0, 0)),
            pl.BlockSpec((None, CONV_W - 1, D_A), lambda b, t: (b, 0, 0)),
            pl.BlockSpec((None, B_HEADS, B_DK, B_DK), lambda b, t: (b, 0, 0, 0)),
            pl.BlockSpec((None, C_HEADS, C_DK, C_DK), lambda b, t: (b, 0, 0, 0)),
        ],
        out_shape=[
            jax.ShapeDtypeStruct((batch * seq, D_MODEL), F32),
            jax.ShapeDtypeStruct((batch, 1, D_A), F32),
            jax.ShapeDtypeStruct((batch, CONV_W - 1, D_A), F32),
            jax.ShapeDtypeStruct((batch, B_HEADS, B_DK, B_DK), F32),
            jax.ShapeDtypeStruct((batch, C_HEADS, C_DK, C_DK), F32),
        ],
        scratch_shapes=[
            pltpu.VMEM((SUBLANES, D_A), F32), pltpu.VMEM((SUBLANES, D_A), F32),
            pltpu.VMEM((D_B, D_B), F32), pltpu.VMEM((C_HEADS, C_DK, C_DK), F32),
            pltpu.VMEM((tt, D_B), F32), pltpu.VMEM((tt, D_B), F32),
            pltpu.VMEM((tt, D_B), F32), pltpu.VMEM((tt, D_B), F32),
            pltpu.VMEM((tt, D_IN), F32), pltpu.VMEM((tt, D_IN), F32),
            pltpu.VMEM((tt, D_MODEL), BF16), pltpu.VMEM((tt, D_MODEL), BF16),
        ],
        compiler_params=_cparams(2),
        name="mix_prompt",
    )(x, x, w_in_bf, cos, sin, cw, cb, wri, bri, lam, lbl, ng, dmat, qdec, kdec)


def _column_matrix(x):
    pad = jnp.zeros((LANES - SUBLANES, LANES), F32)
    return jnp.concatenate([x, pad], axis=0).T


def _mix_sample_kernel(*refs, layer, gammas, n_prev):
    (p_ref, h0_ref, conv0_ref, rt0_ref, cos_ref, sin_ref,
     cw_ref, cb_ref, wri_ref, bri_ref, lam_ref) = refs[:11]
    prev = refs[11:11 + 3 * n_prev]
    y_ref, h_ref, conv_ref, rt_ref, o_sc = refs[11 + 3 * n_prev:]
    if n_prev:
        for j in range(n_prev):
            for dst, src in zip((h_ref, conv_ref, rt_ref), prev[3 * j:3 * j + 3]):
                dst[j] = src[...]
        h_ref, conv_ref, rt_ref = (r.at[layer] for r in (h_ref, conv_ref, rt_ref))
    tb = SUBLANES
    xa = p_ref[:, O_XA:O_XA + D_A]
    ga = p_ref[:, O_GA:O_GA + D_A]
    c0, c1, c2 = conv0_ref[0], conv0_ref[1], conv0_ref[2]
    u = cb_ref[...] + c0 * cw_ref[0:1, :]
    u = u + c1 * cw_ref[1:2, :]
    u = u + c2 * cw_ref[2:3, :]
    u = u + xa * cw_ref[3:4, :]
    conv_ref[0] = c1
    conv_ref[1] = c2
    conv_ref[2] = xa
    a, bterm = _rglru_gates(u, wri_ref[...], bri_ref[...], lam_ref[...])
    h = a * h0_ref[...] + bterm
    h_ref[...] = h
    y_ref[:, 0:D_A] = h * _gelu_tanh(ga)

    y_ref[:, D_A:D_A + D_B] = jnp.zeros((tb, D_B), F32)

    cos = cos_ref[...]
    sin = sin_ref[...]
    for head in range(C_HEADS):
        lo = head * C_DK
        gamma = gammas[head]
        q = _rope(p_ref[:, O_QC + lo:O_QC + lo + C_DK], cos, sin)
        k = _rope(p_ref[:, O_KC + lo:O_KC + lo + C_DK], cos, sin) * (C_DK ** -0.5)
        v = p_ref[:, O_VC + lo:O_VC + lo + C_DK]
        g = p_ref[:, O_GC + lo:O_GC + lo + C_DK]
        dots = jnp.sum(q * k, axis=-1, keepdims=True)
        q_cols = _column_matrix(q * gamma)
        k_cols = _column_matrix(k)
        for b in range(tb):
            s_old = rt0_ref[b, head]
            v_row = v[b:b + 1, :]
            rt_ref[b, head] = gamma * s_old + k_cols[:, b:b + 1] * v_row
            o_row = (jnp.sum(q_cols[:, b:b + 1] * s_old, axis=0, keepdims=True)
                     + dots[b:b + 1, :] * v_row)
            o_sc[b:b + 1, 0:C_DK] = o_row
        o = o_sc[:, 0:C_DK]
        mu = jnp.mean(o, axis=-1, keepdims=True)
        oc = o - mu
        var = jnp.mean(oc * oc, axis=-1, keepdims=True)
        y_ref[:, D_A + D_B + lo:D_A + D_B + lo + C_DK] = oc * lax.rsqrt(var + GN_EPS) * _silu(g)


def _mix_sample(proj, h0, conv0_t, rt0, layer, cw, cb, wri, bri, lam, prev=()):
    nb = proj.shape[0]
    tb = SUBLANES
    lg = np.log1p(-np.exp2(-5.0 - np.arange(C_HEADS, dtype=np.float64)))
    gammas = [float(np.exp(v)) for v in lg]
    cos, sin = _rope_tables([PAST_LEN])
    full = lambda shape: pl.BlockSpec(shape, lambda i: (0,) * len(shape))
    lay = lambda shape: pl.BlockSpec((None,) + shape, lambda i: (layer,) + (0,) * len(shape))
    state_shapes = [(tb, D_A), (CONV_W - 1, tb, D_A), (tb, C_HEADS, C_DK, C_DK)]
    state_maps = [lambda i: (i, 0), lambda i: (0, i, 0), lambda i: (i, 0, 0, 0)]
    full_shapes = [(nb, D_A), (CONV_W - 1, nb, D_A), (nb, C_HEADS, C_DK, C_DK)]
    state_specs = [pl.BlockSpec(s, m) for s, m in zip(state_shapes, state_maps)]
    if prev:
        stack = len(prev) + 1
        out_state_specs = [pl.BlockSpec((stack,) + s, lambda i, m=m: (0,) + m(i))
                           for s, m in zip(state_shapes, state_maps)]
        out_state_shapes = [jax.ShapeDtypeStruct((stack,) + s, F32) for s in full_shapes]
    else:
        out_state_specs = state_specs
        out_state_shapes = [jax.ShapeDtypeStruct(s, F32) for s in full_shapes]
    kern = functools.partial(_mix_sample_kernel, layer=layer, gammas=gammas, n_prev=len(prev))
    return pl.pallas_call(
        kern,
        grid=(nb // tb,),
        in_specs=[
            pl.BlockSpec((tb, D_IN), lambda i: (i, 0)),
            pl.BlockSpec((None, tb, D_A), lambda i: (layer, i, 0)),
            pl.BlockSpec((None, CONV_W - 1, tb, D_A), lambda i: (layer, 0, i, 0)),
            pl.BlockSpec((None, tb, C_HEADS, C_DK, C_DK), lambda i: (layer, i, 0, 0, 0)),
            full((1, C_DK)), full((1, C_DK)),
            lay((CONV_W, D_A)), lay((1, D_A)), lay((D_A, 2 * D_A)), lay((1, 2 * D_A)),
            lay((1, D_A)),
        ] + state_specs * len(prev),
        out_specs=[pl.BlockSpec((tb, D_MODEL), lambda i: (i, 0))] + out_state_specs,
        out_shape=[jax.ShapeDtypeStruct((nb, D_MODEL), F32)] + out_state_shapes,
        scratch_shapes=[pltpu.VMEM((tb, D_B), F32)],
        compiler_params=_cparams(1),
        name="mix_sample",
    )(proj, h0, conv0_t, rt0, cos, sin, cw, cb, wri, bri, lam,
      *[a for states in prev for a in states])


def _hgrn_sample_kernel(*refs, layer, n_prev):
    qb_ref, fb_ref, vb_ref, gb_ref, lbl_ref, ng_ref, s0_ref, y_in_ref = refs[:8]
    prev = refs[8:8 + n_prev]
    y_ref, s_ref = refs[8 + n_prev:]
    del y_in_ref
    for j in range(n_prev):
        s_ref[j] = prev[j][...]
    if n_prev:
        s_ref = s_ref.at[layer]
    lb = _hgrn_lower_bound(lbl_ref[...], layer)
    q = _silu(qb_ref[...])
    fl = fb_ref[...]
    v = vb_ref[...]
    half_th = 0.5 * jnp.tanh(0.5 * fl)
    f = lb + (1.0 - lb) * (0.5 + half_th)
    ef = jnp.exp(jnp.log(jnp.maximum(f, F_TINY)))
    k = (1.0 - lb) * (0.5 - half_th)
    qf_t, k_t, ef_t, v_t, qk_t = (x.T for x in (q * ef, k, ef, v, q * k))
    o_parts = []
    for hh in range(2):
        base = hh * B_DK
        v_h = v_t[base:base + B_DK]
        acc = jnp.zeros((B_DK, LANES), F32)
        for kk in range(B_DK):
            r = base + kk
            s_old = s0_ref[hh, kk]
            s_ref[hh, kk] = ef_t[r:r + 1] * s_old + k_t[r:r + 1] * v_h
            acc = acc + qf_t[r:r + 1] * s_old
        dots = jnp.sum(qk_t[base:base + B_DK], axis=0, keepdims=True)
        o = acc + dots * v_h
        o_parts.append(o * lax.rsqrt(jnp.mean(o * o, axis=0, keepdims=True) + RMS_EPS))
    o = jnp.concatenate(o_parts, axis=0).T
    y_ref[...] = o * ng_ref[...] * _silu(gb_ref[...])


def _hgrn_sample(proj, y, s0_t, layer, lbl, ng_pair, prev=()):
    nb = proj.shape[0]
    col = lambda off: pl.BlockSpec((nb, LANES), lambda j: (0, off // LANES + j))
    state_block = (2, B_DK, B_DK, nb)
    state_spec = pl.BlockSpec(state_block, lambda j: (j, 0, 0, 0))
    stack = len(prev) + 1
    if prev:
        out_state_spec = pl.BlockSpec((stack,) + state_block, lambda j: (0, j, 0, 0, 0))
        out_state_shape = jax.ShapeDtypeStruct((stack, B_HEADS, B_DK, B_DK, nb), F32)
    else:
        out_state_spec = state_spec
        out_state_shape = jax.ShapeDtypeStruct((B_HEADS, B_DK, B_DK, nb), F32)
    return pl.pallas_call(
        functools.partial(_hgrn_sample_kernel, layer=layer, n_prev=len(prev)),
        grid=(B_HEADS // 2,),
        in_specs=[col(O_QB), col(O_FB), col(O_VB), col(O_GB),
                  pl.BlockSpec((DEPTH, LANES), lambda j: (0, j)),
                  pl.BlockSpec((None, 1, LANES), lambda j: (layer, 0, 0)),
                  pl.BlockSpec((None,) + state_block, lambda j: (layer, j, 0, 0, 0)),
                  pl.BlockSpec(memory_space=pl.ANY)] + [state_spec] * len(prev),
        out_specs=[col(D_A), out_state_spec],
        out_shape=[jax.ShapeDtypeStruct(y.shape, F32), out_state_shape],
        input_output_aliases={7: 0},
        compiler_params=_cparams(1),
        name="hgrn_sample",
    )(proj, proj, proj, proj, lbl, ng_pair, s0_t, y, *prev)


def _route_rows(l):
    m = l[0]
    for x in l[1:]:
        m = jnp.maximum(m, x)
    ex = [jnp.exp(x - m) for x in l]
    tot = ex[0]
    for x in ex[1:]:
        tot = tot + x
    p = [x / tot for x in ex]
    scores = []
    for gi in range(N_GROUPS):
        a, b, c, d = p[4 * gi:4 * gi + 4]
        hi1, lo1 = jnp.maximum(a, b), jnp.minimum(a, b)
        hi2, lo2 = jnp.maximum(c, d), jnp.minimum(c, d)
        top1 = jnp.maximum(hi1, hi2)
        top2 = jnp.maximum(jnp.minimum(hi1, hi2), jnp.maximum(lo1, lo2))
        scores.append(top1 + top2)
    best = scores[0]
    gsel = jnp.zeros_like(best, dtype=jnp.int32)
    for gi in range(1, N_GROUPS):
        upd = scores[gi] > best
        gsel = jnp.where(upd, gi, gsel)
        best = jnp.where(upd, scores[gi], best)
    vals = []
    for j in range(EXP_PER_GROUP):
        v = p[j]
        for gi in range(1, N_GROUPS):
            v = jnp.where(gsel == gi, p[4 * gi + j], v)
        vals.append(v)
    sel = []
    for j in range(EXP_PER_GROUP):
        rank = jnp.zeros_like(gsel)
        for i in range(EXP_PER_GROUP):
            if i == j:
                continue
            ahead = (vals[i] > vals[j]) | ((vals[i] == vals[j]) & (i < j))
            rank = rank + jnp.where(ahead, 1, 0)
        sel.append(rank < 2)
    denom = jnp.zeros_like(best)
    for j in range(EXP_PER_GROUP):
        denom = denom + jnp.where(sel[j], vals[j], 0.0)
    gates = [jnp.where(sel[j], vals[j] / denom, 0.0) for j in range(EXP_PER_GROUP)]
    comb = [jnp.where(gsel == (e // EXP_PER_GROUP), gates[e % EXP_PER_GROUP], 0.0)
            for e in range(N_EXPERTS)]
    j1 = jnp.where(sel[0], 0, jnp.where(sel[1], 1, 2))
    j2 = jnp.where(sel[3], 3, jnp.where(sel[2], 2, 1))
    g1 = jnp.zeros_like(best)
    g2 = jnp.zeros_like(best)
    for j in range(EXP_PER_GROUP):
        g1 = jnp.where(j1 == j, gates[j], g1)
        g2 = jnp.where(j2 == j, gates[j], g2)
    return comb, gsel * EXP_PER_GROUP + j1, gsel * EXP_PER_GROUP + j2, g1, g2


MOE_TILE = 512
GRANULE = 16
GRANULES_PER_TILE = MOE_TILE // GRANULE
XS_WIDTH = D_MODEL + LANES
LARGE_PIECE = 8
ROUTE_SLOT1, ROUTE_SLOT2 = N_EXPERTS, N_EXPERTS + 1
PLAN_SLOT1, PLAN_SLOT2, PLAN_G1, PLAN_G2 = 0, 1, 2, 3


def _local_slots(tt):
    worst = 2 * tt + (N_EXPERTS - 1) * GRANULE
    return -(-worst // LANES) * LANES


def _outproj_kernel(*refs, plan, sub, st):
    if plan:
        (y_ref, x_ref, w_ref, g_ref, b_ref, rwt_ref, rb_ref, tri_ref, ltri_ref,
         x1_ref, route_ref, plan_ref, gran_ref, loff_ref, rt_sc, oh_sc) = refs
    else:
        y_ref, x_ref, w_ref, g_ref, b_ref, rwt_ref, rb_ref, x1_ref, route_ref, wb_ref, rt_sc = refs
    if plan:
        y = jnp.dot(y_ref[...].astype(BF16), w_ref[...], preferred_element_type=F32)
    else:
        yh, yl = _split(y_ref[...])
        wh, wl = _split(w_ref[...])
        wb_ref[...] = wh
        y = (jnp.dot(yh, wh, preferred_element_type=F32) + jnp.dot(yl, wh, preferred_element_type=F32)
             + jnp.dot(yh, wl, preferred_element_type=F32))
    x1 = _layernorm(ALPHA * x_ref[...] + y, g_ref[...], b_ref[...])
    x1_ref[...] = x1
    hi, lo = _split(x1)
    rhi, rlo = _split(rwt_ref[...])
    lg = (lax.dot_general(rhi, hi, _NT, preferred_element_type=F32)
          + lax.dot_general(rhi, lo, _NT, preferred_element_type=F32)
          + lax.dot_general(rlo, hi, _NT, preferred_element_type=F32)) + rb_ref[...]
    comb, e1, e2, g1, g2 = _route_rows([lg[e:e + 1, :] for e in range(N_EXPERTS)])
    rt_sc[...] = jnp.zeros_like(rt_sc)
    if not plan:
        for e in range(N_EXPERTS):
            rt_sc[e:e + 1, :] = comb[e]
    else:
        for e in range(N_EXPERTS):
            oh_sc[e:e + 1, :] = jnp.where((e1 == e) | (e2 == e), 1.0, 0.0)
        bases = []
        for s in range(sub):
            oh = oh_sc[:, s * st:(s + 1) * st]
            cum = jnp.dot(oh.astype(BF16), tri_ref[...], preferred_element_type=F32)
            count = jnp.sum(oh, axis=1, keepdims=True)
            gran = jnp.floor((count + (GRANULE - 1)) * (1.0 / GRANULE))
            gran_b = jnp.broadcast_to(gran, (N_EXPERTS, LANES))
            loff_b = jnp.dot(ltri_ref[...], gran_b.astype(BF16), preferred_element_type=F32)
            bases.append(loff_b[:, 0:1] * GRANULE - 1.0 + cum)
            gran_ref[s] = gran_b.astype(jnp.int32)
            loff_ref[s] = loff_b.astype(jnp.int32)
        slot_base = jnp.concatenate(bases, axis=1)
        s1 = jnp.zeros_like(g1)
        s2 = jnp.zeros_like(g1)
        for e in range(N_EXPERTS):
            row = slot_base[e:e + 1, :]
            s1 = jnp.where(e1 == e, row, s1)
            s2 = jnp.where(e2 == e, row, s2)
        for s in range(sub):
            cols = slice(s * st, (s + 1) * st)
            plan_ref[s] = jnp.zeros((SUBLANES, st), F32)
            plan_ref[s, PLAN_SLOT1:PLAN_SLOT1 + 1, :] = s1[:, cols]
            plan_ref[s, PLAN_SLOT2:PLAN_SLOT2 + 1, :] = s2[:, cols]
            plan_ref[s, PLAN_G1:PLAN_G1 + 1, :] = g1[:, cols]
            plan_ref[s, PLAN_G2:PLAN_G2 + 1, :] = g2[:, cols]
        rt_sc[ROUTE_SLOT1:ROUTE_SLOT1 + 1, :] = s1
        rt_sc[ROUTE_SLOT2:ROUTE_SLOT2 + 1, :] = s2
    route_ref[...] = rt_sc[...].T


def _outproj(y, x, w_out, layer, st, g, b, rwt, rb, plan=False, sub=1):
    n = x.shape[0]
    tm = st * sub
    full = lambda shape: pl.BlockSpec(shape, lambda i: (0,) * len(shape))
    lay = lambda shape: pl.BlockSpec((None,) + shape, lambda i: (layer,) + (0,) * len(shape))
    w_spec = full((D_MODEL, D_MODEL)) if plan else lay((D_MODEL, D_MODEL))
    in_specs = [pl.BlockSpec((tm, D_MODEL), lambda i: (i, 0)),
                pl.BlockSpec((tm, D_MODEL), lambda i: (i, 0)),
                w_spec, lay((1, D_MODEL)), lay((1, D_MODEL)),
                full((N_EXPERTS, D_MODEL)), full((N_EXPERTS, 1))]
    out_specs = [pl.BlockSpec((tm, D_MODEL), lambda i: (i, 0)),
                 pl.BlockSpec((tm, LANES), lambda i: (i, 0))]
    out_shape = [jax.ShapeDtypeStruct((n, D_MODEL), F32),
                 jax.ShapeDtypeStruct((n, LANES), F32)]
    scratch = [pltpu.VMEM((LANES, tm), F32)]
    args = [y, x, w_out, g, b, rwt, rb]
    if plan:
        tri = np.triu(np.ones((st, st), np.float32)).astype(jnp.bfloat16)
        ltri = np.tril(np.ones((N_EXPERTS, N_EXPERTS), np.float32), -1).astype(jnp.bfloat16)
        in_specs += [full((st, st)), full((N_EXPERTS, N_EXPERTS))]
        args += [tri, ltri]
        per_tile = pl.BlockSpec((sub, N_EXPERTS, LANES), lambda i: (i, 0, 0))
        out_specs += [pl.BlockSpec((sub, SUBLANES, st), lambda i: (i, 0, 0)), per_tile, per_tile]
        out_shape += [jax.ShapeDtypeStruct((n // st, SUBLANES, st), F32),
                      jax.ShapeDtypeStruct((n // st, N_EXPERTS, LANES), jnp.int32),
                      jax.ShapeDtypeStruct((n // st, N_EXPERTS, LANES), jnp.int32)]
        scratch += [pltpu.VMEM((N_EXPERTS, tm), F32)]
    else:
        assert sub == 1 and tm == n, "the sample trunk is one tile; it also emits the bf16 weight"
        out_specs.append(full((D_MODEL, D_MODEL)))
        out_shape.append(jax.ShapeDtypeStruct((D_MODEL, D_MODEL), BF16))
    return pl.pallas_call(
        functools.partial(_outproj_kernel, plan=plan, sub=sub, st=st),
        grid=(n // tm,),
        in_specs=in_specs,
        out_specs=out_specs,
        out_shape=out_shape,
        scratch_shapes=scratch,
        compiler_params=_cparams(1),
        name="outproj_plan" if plan else "outproj",
    )(*args)


def _granule_copies(src, dst, src_g, dst_g, n, sem, start, max_granules):
    def pieces(sizes, src_g, dst_g):
        for size in sizes:
            bit = n & size

            @pl.when(bit != 0)
            def _(size=size, src_g=src_g, dst_g=dst_g):
                s0 = pl.multiple_of(src_g * GRANULE, GRANULE)
                d0 = pl.multiple_of(dst_g * GRANULE, GRANULE)
                cp = pltpu.make_async_copy(src.at[pl.ds(s0, size * GRANULE)],
                                           dst.at[pl.ds(d0, size * GRANULE)], sem)
                cp.start() if start else cp.wait()

            src_g = src_g + bit
            dst_g = dst_g + bit

    sizes = [max_granules >> k for k in range(max_granules.bit_length())]
    large = [s for s in sizes if s >= LARGE_PIECE]
    small = [s for s in sizes if s < LARGE_PIECE]

    @pl.when(n >= LARGE_PIECE)
    def _():
        pieces(large, src_g, dst_g)

    done = n & sum(large)
    pieces(small, src_g + done, dst_g + done)


def _dispatch_kernel(segg_ref, totg_ref, ctl_ref, ctl_prev_ref, plan_ref, x_ref, xs_ref,
                     xl_sc, zx_sc, sem, *, tt, total_tiles):
    step = pl.program_id(0)
    cur = step % 2

    @pl.when(step == 0)
    def _():
        zx_sc[...] = jnp.zeros_like(zx_sc)

        def zero_tile(i):
            r0 = pl.multiple_of(i * MOE_TILE, MOE_TILE)
            return pltpu.make_async_copy(zx_sc, xs_ref.at[pl.ds(r0, MOE_TILE)], sem.at[2])

        def run(i, start):
            cp = zero_tile(i)
            cp.start() if start else cp.wait()

        for start in (True, False):
            for e in range(N_EXPERTS):
                end = segg_ref[e] + totg_ref[e]

                @pl.when(end % GRANULES_PER_TILE != 0)
                def _():
                    run(end // GRANULES_PER_TILE, start)

        last = N_EXPERTS - 1
        used = (segg_ref[last] + totg_ref[last] + GRANULES_PER_TILE - 1) // GRANULES_PER_TILE
        lax.fori_loop(used, total_tiles, lambda i, c: (run(i, True), c)[1], 0)
        lax.fori_loop(used, total_tiles, lambda i, c: (run(i, False), c)[1], 0)

    n_slots = xl_sc.shape[1]
    xl = xl_sc.at[cur]
    slot = lax.broadcasted_iota(jnp.int32, (n_slots, tt), 0)
    m1 = slot == plan_ref[PLAN_SLOT1:PLAN_SLOT1 + 1, :].astype(jnp.int32)
    m2 = slot == plan_ref[PLAN_SLOT2:PLAN_SLOT2 + 1, :].astype(jnp.int32)
    perm = jnp.where(m1 | m2, 1.0, 0.0).astype(BF16)
    xl[:, :D_MODEL] = jnp.dot(
        perm, x_ref[...].astype(BF16), preferred_element_type=F32).astype(BF16)
    gate = jnp.sum(jnp.where(m1, plan_ref[PLAN_G1:PLAN_G1 + 1, :], 0.0)
                   + jnp.where(m2, plan_ref[PLAN_G2:PLAN_G2 + 1, :], 0.0), axis=1, keepdims=True)
    g_hi = gate.astype(BF16).astype(F32)
    g_lo = (gate - g_hi).astype(BF16).astype(F32)
    lane = lax.broadcasted_iota(jnp.int32, (n_slots, LANES), 1)
    xl[:, D_MODEL:] = jnp.where(lane == 0, g_hi, jnp.where(lane == 1, g_lo, 0.0)).astype(BF16)

    def copies(ctl, half, start):
        for e in range(N_EXPERTS):
            n, src_g, dst_g = ctl[0, e], ctl[1, e], ctl[2, e]
            _granule_copies(xl_sc.at[half], xs_ref, src_g, dst_g, n, sem.at[half], start,
                            tt // GRANULE)

    copies(ctl_ref, cur, True)

    @pl.when(step > 0)
    def _():
        copies(ctl_prev_ref, 1 - cur, False)

    @pl.when(step == pl.num_programs(0) - 1)
    def _():
        copies(ctl_ref, cur, False)


def _dispatch(x1, plan, ctl, seg_g, tot_g, tt, rows):
    n = x1.shape[0]
    n_slots = _local_slots(tt)
    return pl.pallas_call(
        functools.partial(_dispatch_kernel, tt=tt, total_tiles=rows // MOE_TILE),
        grid_spec=pltpu.PrefetchScalarGridSpec(
            num_scalar_prefetch=2,
            grid=(n // tt,),
            in_specs=[pl.BlockSpec((None, SUBLANES, LANES), lambda i, s, c: (i, 0, 0),
                                   memory_space=pltpu.SMEM),
                      pl.BlockSpec((None, SUBLANES, LANES), lambda i, s, c: (jnp.maximum(i - 1, 0), 0, 0),
                                   memory_space=pltpu.SMEM),
                      pl.BlockSpec((None, SUBLANES, tt), lambda i, s, c: (i, 0, 0)),
                      pl.BlockSpec((tt, D_MODEL), lambda i, s, c: (i, 0))],
            out_specs=pl.BlockSpec(memory_space=pl.ANY),
            scratch_shapes=[pltpu.VMEM((2, n_slots, XS_WIDTH), BF16),
                            pltpu.VMEM((MOE_TILE, XS_WIDTH), BF16),
                            pltpu.SemaphoreType.DMA((3,))]),
        out_shape=jax.ShapeDtypeStruct((rows, XS_WIDTH), BF16),
        compiler_params=_cparams(1),
        name="moe_dispatch",
    )(seg_g, tot_g, ctl, ctl, plan, x1)


def _ffn_kernel(te_ref, nv_ref, xs_ref, wg_ref, wu_ref, wd_ref, ys_ref):
    used = pl.program_id(0) < nv_ref[0]

    @pl.when(used)
    def _():
        xb = xs_ref[:, :D_MODEL]
        gate_pair = xs_ref[:, D_MODEL:].astype(F32)
        gate = gate_pair[:, 0:1] + gate_pair[:, 1:2]
        hg = jnp.dot(xb, wg_ref[...].astype(BF16), preferred_element_type=F32)
        hu = jnp.dot(xb, wu_ref[...].astype(BF16), preferred_element_type=F32)
        h = (_silu(hg) * hu * gate).astype(BF16)
        ys_ref[...] = jnp.dot(h, wd_ref[...].astype(BF16), preferred_element_type=F32).astype(BF16)

    @pl.when(jnp.logical_not(used))
    def _():
        ys_ref[...] = jnp.zeros_like(ys_ref)


def _ffn(xs, tile_expert, n_valid, wg, wu, wd, layer):
    total_tiles = xs.shape[0] // MOE_TILE
    w_in_spec = pl.BlockSpec((None, None, D_MODEL, D_EXPERT), lambda i, te, nv: (layer, te[i], 0, 0))
    return pl.pallas_call(
        _ffn_kernel,
        grid_spec=pltpu.PrefetchScalarGridSpec(
            num_scalar_prefetch=2,
            grid=(total_tiles,),
            in_specs=[pl.BlockSpec((MOE_TILE, XS_WIDTH), lambda i, te, nv: (jnp.minimum(i, nv[0] - 1), 0)),
                      w_in_spec, w_in_spec,
                      pl.BlockSpec((None, None, D_EXPERT, D_MODEL),
                                   lambda i, te, nv: (layer, te[i], 0, 0))],
            out_specs=pl.BlockSpec((MOE_TILE, D_MODEL), lambda i, te, nv: (i, 0))),
        out_shape=jax.ShapeDtypeStruct((xs.shape[0], D_MODEL), BF16),
        compiler_params=_cparams(1),
        name="moe_ffn",
    )(tile_expert, n_valid, xs, wg, wu, wd)


def _combine_kernel(ctl_ref, ctl_next_ref, x_ref, route_ref, g_ref, b_ref, ys_ref, o_ref,
                    yl_sc, sem, *, tt):
    i = pl.program_id(0)
    cur = i % 2

    def copies(ctl, half, start):
        for e in range(N_EXPERTS):
            n, loc_g, buf_g = ctl[0, e], ctl[1, e], ctl[2, e]
            _granule_copies(ys_ref, yl_sc.at[half], buf_g, loc_g, n, sem.at[half], start,
                            tt // GRANULE)

    @pl.when(i == 0)
    def _():
        yl_sc[...] = jnp.zeros_like(yl_sc)
        copies(ctl_ref, 0, True)

    @pl.when(i + 1 < pl.num_programs(0))
    def _():
        copies(ctl_next_ref, 1 - cur, True)

    copies(ctl_ref, cur, False)

    n_slots = yl_sc.shape[1]
    route = route_ref[...]
    slot = lax.broadcasted_iota(jnp.int32, (tt, n_slots), 1)
    pick = ((slot == route[:, ROUTE_SLOT1:ROUTE_SLOT1 + 1].astype(jnp.int32))
            | (slot == route[:, ROUTE_SLOT2:ROUTE_SLOT2 + 1].astype(jnp.int32)))
    y = jnp.dot(jnp.where(pick, 1.0, 0.0).astype(BF16), yl_sc[cur], preferred_element_type=F32)
    o_ref[...] = _layernorm(ALPHA * x_ref[...] + y, g_ref[...], b_ref[...])


def _combine(x1, route, ctl, ys, layer, tt, g, b):
    n = x1.shape[0]
    last = n // tt - 1
    lay = lambda shape: pl.BlockSpec((None,) + shape, lambda i: (layer,) + (0,) * len(shape))
    return pl.pallas_call(
        functools.partial(_combine_kernel, tt=tt),
        grid=(n // tt,),
        in_specs=[pl.BlockSpec((None, SUBLANES, LANES), lambda i: (i, 0, 0), memory_space=pltpu.SMEM),
                  pl.BlockSpec((None, SUBLANES, LANES), lambda i: (jnp.minimum(i + 1, last), 0, 0),
                               memory_space=pltpu.SMEM),
                  pl.BlockSpec((tt, D_MODEL), lambda i: (i, 0)),
                  pl.BlockSpec((tt, LANES), lambda i: (i, 0)),
                  lay((1, D_MODEL)), lay((1, D_MODEL)),
                  pl.BlockSpec(memory_space=pl.ANY)],
        out_specs=pl.BlockSpec((tt, D_MODEL), lambda i: (i, 0)),
        scratch_shapes=[pltpu.VMEM((2, _local_slots(tt), D_MODEL), BF16),
                        pltpu.SemaphoreType.DMA((2,))],
        out_shape=jax.ShapeDtypeStruct((n, D_MODEL), F32),
        compiler_params=_cparams(1),
        name="moe_combine",
    )(ctl, ctl, x1, route, g, b, ys)


def _moe_sparse(x1, route, plan, gran, loff, wg, wu, wd, layer, tt, g, b):
    n = x1.shape[0]
    n_tiles = n // tt
    gran = gran[:, :, 0]
    loff = loff[:, :, 0]
    max_tiles = -(-(2 * n + (GRANULE - 1) * N_EXPERTS * n_tiles) // MOE_TILE) + N_EXPERTS
    rows = max_tiles * MOE_TILE
    tot_g = jnp.sum(gran, axis=0)
    tiles = (tot_g + GRANULES_PER_TILE - 1) // GRANULES_PER_TILE
    tile_end = jnp.cumsum(tiles)
    seg_g = ((tile_end - tiles) * GRANULES_PER_TILE).astype(jnp.int32)
    n_valid = tile_end[-1]
    idx = jnp.minimum(jnp.arange(max_tiles, dtype=jnp.int32), n_valid - 1)
    tile_expert = jnp.sum(idx[:, None] >= tile_end[None, :], axis=1).astype(jnp.int32)
    buf_g = seg_g[None, :] + jnp.cumsum(gran, axis=0) - gran
    ctl = jnp.zeros((n_tiles, SUBLANES, LANES), jnp.int32)
    ctl = ctl.at[:, 0, :N_EXPERTS].set(gran).at[:, 1, :N_EXPERTS].set(loff).at[:, 2, :N_EXPERTS].set(buf_g)
    xs = _dispatch(x1, plan, ctl, seg_g, tot_g.astype(jnp.int32), tt, rows)
    ys = _ffn(xs, tile_expert, n_valid.reshape(1).astype(jnp.int32), wg, wu, wd, layer)
    return _combine(x1, route, ctl, ys, layer, tt, g, b)


def _moe_kernel(x_ref, r_ref, wg_ref, wu_ref, wd_ref, g_ref, b_ref, o_ref, xb_sc, acc_sc):
    e = pl.program_id(1)

    @pl.when(e == 0)
    def _():
        xb_sc[...] = x_ref[...].astype(BF16)
        acc_sc[...] = jnp.zeros_like(acc_sc)

    xb = xb_sc[...]
    hg = jnp.dot(xb, wg_ref[...].astype(BF16), preferred_element_type=F32)
    hu = jnp.dot(xb, wu_ref[...].astype(BF16), preferred_element_type=F32)
    r = r_ref[...]
    lane = lax.broadcasted_iota(jnp.int32, r.shape, 1)
    c = jnp.sum(jnp.where(lane == e, r, 0.0), axis=-1, keepdims=True)
    h = _silu(hg) * hu * c
    acc_sc[...] += jnp.dot(h.astype(BF16), wd_ref[...].astype(BF16), preferred_element_type=F32)

    @pl.when(e == pl.num_programs(1) - 1)
    def _():
        o_ref[...] = _layernorm(ALPHA * x_ref[...] + acc_sc[...], g_ref[...], b_ref[...])


def _moe(x1, route, wg, wu, wd, layer, tm, g, b):
    n = x1.shape[0]
    lay = lambda shape: pl.BlockSpec((None,) + shape, lambda i, e: (layer,) + (0,) * len(shape))
    return pl.pallas_call(
        _moe_kernel,
        grid=(n // tm, N_EXPERTS),
        in_specs=[pl.BlockSpec((tm, D_MODEL), lambda i, e: (i, 0)),
                  pl.BlockSpec((tm, LANES), lambda i, e: (i, 0)),
                  pl.BlockSpec((None, None, D_MODEL, D_EXPERT), lambda i, e: (layer, e, 0, 0)),
                  pl.BlockSpec((None, None, D_MODEL, D_EXPERT), lambda i, e: (layer, e, 0, 0)),
                  pl.BlockSpec((None, None, D_EXPERT, D_MODEL), lambda i, e: (layer, e, 0, 0)),
                  lay((1, D_MODEL)), lay((1, D_MODEL))],
        out_specs=pl.BlockSpec((tm, D_MODEL), lambda i, e: (i, 0)),
        out_shape=jax.ShapeDtypeStruct((n, D_MODEL), F32),
        scratch_shapes=[pltpu.VMEM((tm, D_MODEL), BF16), pltpu.VMEM((tm, D_MODEL), F32)],
        compiler_params=_cparams(2),
        name="moe",
    )(x1, route, wg, wu, wd, g, b)


def _block_diag(w):
    out = jnp.zeros((DEPTH, D_A, D_A), w.dtype)
    for gi in range(A_BLOCKS):
        out = out.at[:, gi * A_BLK:(gi + 1) * A_BLK, gi * A_BLK:(gi + 1) * A_BLK].set(w[:, gi])
    return out


def kernel(x_prompt, x_sample, state_rglru_h, state_conv, state_hgrn, state_ret, w_in, conv_w, conv_b, w_rgate, b_rgate, w_igate, b_igate, rglru_lambda, hgrn_lb_logits, hgrn_norm_g, w_out, ln1_g, ln1_b, router_w, router_b, exp_w_gate, exp_w_up, exp_w_down, ln2_g, ln2_b):
    batch, seq, _ = x_prompt.shape
    nb = x_sample.shape[0]

    wg, wu, wd = exp_w_gate, exp_w_up, exp_w_down
    wri = jnp.concatenate([_block_diag(w_rgate), _block_diag(w_igate)], axis=-1).astype(BF16)
    bri = jnp.concatenate([b_rgate.reshape(DEPTH, 1, D_A), b_igate.reshape(DEPTH, 1, D_A)], axis=-1)
    cb = conv_b.reshape(DEPTH, 1, D_A)
    lam = rglru_lambda.reshape(DEPTH, 1, D_A)
    ng = jnp.tile(hgrn_norm_g, (1, B_HEADS)).reshape(DEPTH, 1, D_B)
    g1, b1 = ln1_g.reshape(DEPTH, 1, D_MODEL), ln1_b.reshape(DEPTH, 1, D_MODEL)
    g2, b2 = ln2_g.reshape(DEPTH, 1, D_MODEL), ln2_b.reshape(DEPTH, 1, D_MODEL)
    rwt = router_w.T
    rb = router_b.reshape(N_EXPERTS, 1)
    conv0_t = jnp.transpose(state_conv, (0, 2, 1, 3))
    hg0_t = jnp.transpose(state_hgrn, (0, 2, 3, 4, 1))

    xp = x_prompt.reshape(batch * seq, D_MODEL)
    xs = x_sample.reshape(nb, D_MODEL)
    hs_p, convs_p, hgs_p, rts_p = [], [], [], []
    prev_s, prev_hg = [], []
    for l in range(DEPTH):
        last = l == DEPTH - 1
        proj_s, w_in_bf = _proj(xs, w_in, l)
        y_s, *states_s = _mix_sample(
            proj_s, state_rglru_h, conv0_t, state_ret, l, conv_w, cb, wri, bri, lam,
            prev=prev_s if last else ())
        prev_s.append(states_s)
        y_s, hg_s = _hgrn_sample(proj_s, y_s, hg0_t, l, hgrn_lb_logits, ng[:, :, :LANES],
                                 prev=prev_hg if last else ())
        prev_hg.append(hg_s)
        x1_s, route_s, w_out_bf = _outproj(y_s, xs, w_out, l, nb, g1, b1, rwt, rb)
        xs = _moe(x1_s, route_s, wg, wu, wd, l, nb, g2, b2)

        y_p, h_p, conv_p, hg_p, rt_p = _mix_prompt(
            xp, w_in_bf, batch, seq, l, PROMPT_TILE, conv_w, cb, wri, bri, lam, hgrn_lb_logits, ng)
        x1_p, route_p, plan_p, gran_p, loff_p = _outproj(
            y_p, xp, w_out_bf, l, PROMPT_TILE, g1, b1, rwt, rb, plan=True, sub=2)
        xp = _moe_sparse(x1_p, route_p, plan_p, gran_p, loff_p, wg, wu, wd, l, PROMPT_TILE, g2, b2)
        hs_p.append(h_p.reshape(batch, D_A))
        convs_p.append(conv_p)
        hgs_p.append(hg_p)
        rts_p.append(rt_p)

    h_s, conv_s, rt_s = prev_s[-1]
    hg_s = jnp.transpose(prev_hg[-1], (0, 4, 1, 2, 3))
    return (xp.reshape(batch, seq, D_MODEL), xs.reshape(nb, 1, D_MODEL),
            jnp.stack(hs_p), h_s, jnp.stack(convs_p), jnp.transpose(conv_s, (0, 2, 1, 3)),
            jnp.stack(hgs_p), hg_s, jnp.stack(rts_p), rt_s)
```

```python
import functools

import numpy as np
import jax
import jax.numpy as jnp
from jax import lax
from jax.experimental import pallas as pl
from jax.experimental.pallas import tpu as pltpu

D_MODEL = 1024
DEPTH = 2
PAST_LEN = 16384
D_A = 256
A_BLOCKS = 4
A_BLK = 64
CONV_W = 4
RGLRU_C = 8.0
B_HEADS = 4
B_DK = 64
D_B = 256
C_HEADS = 4
C_DK = 128
D_C = 512
D_IN = 3584
B_CHUNK = 64
C_CHUNK = 128
ROPE_BASE = 10000.0
N_EXPERTS = 16
N_GROUPS = 4
EXP_PER_GROUP = 4
D_EXPERT = 512
LN_EPS = 1e-5
RMS_EPS = 1e-6
GN_EPS = 1e-6
F_TINY = 1e-30
ALPHA = (2 * DEPTH) ** 0.25

O_XA, O_GA, O_QB, O_FB, O_VB, O_GB, O_QC, O_KC, O_VC, O_GC = (
    0, 256, 512, 768, 1024, 1280, 1536, 2048, 2560, 3072)

V7X_VMEM_LIMIT_BYTES = 56 * 1024 * 1024
SUBLANES = 8
LANES = 128
PROMPT_TILE = 512
HGRN_SAFE_MIN_LOGDECAY = -60.0

BF16 = jnp.bfloat16
F32 = jnp.float32
_NT = (((1,), (1,)), ((), ()))
_TN = (((0,), (0,)), ((), ()))


def _cparams(n_axes):
    return pltpu.CompilerParams(
        dimension_semantics=("arbitrary",) * n_axes,
        vmem_limit_bytes=V7X_VMEM_LIMIT_BYTES)


def _dot(a, b):
    return jnp.dot(a.astype(BF16), b.astype(BF16), preferred_element_type=F32)


def _dot_g(a, b, dims):
    return lax.dot_general(a.astype(BF16), b.astype(BF16), dims, preferred_element_type=F32)


def _sigmoid(x):
    return 0.5 * jnp.tanh(0.5 * x) + 0.5


def _sqrt_nonneg(x):
    return jnp.where(x > 0.0, x * lax.rsqrt(x), 0.0)


def _silu(x):
    return x * _sigmoid(x)


def _gelu_tanh(x):
    c = np.float32(np.sqrt(2.0 / np.pi))
    return 0.5 * x * (1.0 + jnp.tanh(c * (x + np.float32(0.044715) * (x * x * x))))


def _log_sigmoid(x):
    return -(jnp.maximum(-x, 0.0) + jnp.log(1.0 + jnp.exp(-jnp.abs(x))))


def _layernorm(z, g, b):
    mu = jnp.mean(z, axis=-1, keepdims=True)
    zc = z - mu
    var = jnp.mean(zc * zc, axis=-1, keepdims=True)
    return zc * lax.rsqrt(var + LN_EPS) * g + b


def _hgrn_lower_bound(lbl, layer):
    rows = [lbl[j:j + 1, :] for j in range(DEPTH)]
    m = rows[0]
    for r in rows[1:]:
        m = jnp.maximum(m, r)
    ex = [jnp.exp(r - m) for r in rows]
    tot = ex[0]
    for e in ex[1:]:
        tot = tot + e
    lb = jnp.zeros_like(m)
    for j in range(1, layer + 1):
        lb = lb + ex[j] / tot
    return lb


def _rglru_gates(u, wri, bri, lam):
    gates = _dot(u, wri) + bri
    r = _sigmoid(gates[:, :D_A])
    i = _sigmoid(gates[:, D_A:])
    log_a = RGLRU_C * r * _log_sigmoid(lam)
    a = jnp.exp(log_a)
    bterm = _sqrt_nonneg(jnp.maximum(1.0 - a * a, 0.0)) * (i * u)
    return a, bterm


def _split(x):
    hi = x.astype(BF16)
    return hi, (x - hi.astype(F32)).astype(BF16)


def _proj_kernel(x_ref, w_ref, o_ref, wb_ref):
    xh, xl = _split(x_ref[...])
    for j in range(0, D_IN, 512):
        wh, wl = _split(w_ref[:, j:j + 512])
        wb_ref[:, j:j + 512] = wh
        o_ref[:, j:j + 512] = (jnp.dot(xh, wh, preferred_element_type=F32)
                               + jnp.dot(xl, wh, preferred_element_type=F32)
                               + jnp.dot(xh, wl, preferred_element_type=F32))


def _proj(x, w_in, layer):
    n = x.shape[0]
    return pl.pallas_call(
        _proj_kernel,
        grid=(1,),
        in_specs=[pl.BlockSpec((n, D_MODEL), lambda i: (0, 0)),
                  pl.BlockSpec((None, D_MODEL, D_IN), lambda i: (layer, 0, 0))],
        out_specs=[pl.BlockSpec((n, D_IN), lambda i: (0, 0)),
                   pl.BlockSpec((D_MODEL, D_IN), lambda i: (0, 0))],
        out_shape=[jax.ShapeDtypeStruct((n, D_IN), F32),
                   jax.ShapeDtypeStruct((D_MODEL, D_IN), BF16)],
        compiler_params=_cparams(1),
        name="proj",
    )(x, w_in)


def _retention_consts(chunk):
    lg = np.log1p(-np.exp2(-5.0 - np.arange(C_HEADS, dtype=np.float64)))
    idx = np.arange(chunk, dtype=np.float64)
    rel = idx[:, None] - idx[None, :]
    mask = rel >= 0
    dmat = np.where(mask[None], np.exp(np.where(mask, rel, 0.0)[None] * lg[:, None, None]), 0.0)
    qdec = np.exp((idx + 1.0)[None, :] * lg[:, None])
    kdec = np.exp((chunk - 1.0 - idx)[None, :] * lg[:, None])
    sdec = np.exp(chunk * lg)
    qdec_b = np.broadcast_to(qdec[:, :, None], (C_HEADS, chunk, C_DK))
    kdec_b = np.broadcast_to(kdec[:, :, None], (C_HEADS, chunk, C_DK))
    return (dmat.astype(np.float32), np.ascontiguousarray(qdec_b).astype(np.float32),
            np.ascontiguousarray(kdec_b).astype(np.float32), [float(v) for v in sdec])


def _rope_tables(positions):
    half = C_DK // 2
    inv = ROPE_BASE ** (-np.arange(half, dtype=np.float64) / half)
    ang = np.asarray(positions, dtype=np.float64)[:, None] * inv[None]
    cos = np.concatenate([np.cos(ang), np.cos(ang)], axis=-1)
    sin = np.concatenate([-np.sin(ang), np.sin(ang)], axis=-1)
    return cos.astype(np.float32), sin.astype(np.float32)


def _rope(x, cos, sin_signed):
    return x * cos + pltpu.roll(x, C_DK // 2, 1) * sin_signed


PROJ_SPLIT = 1536


def _mix_prompt_kernel(x0_ref, xn_ref, w_ref, cos_ref, sin_ref, cw_ref, cb_ref, wri_ref, bri_ref,
                       lam_ref, lbl_ref, ng_ref, dmat_ref, qdec_ref, kdec_ref,
                       y_ref, h_ref, conv_ref, hg_ref, rt_ref,
                       prev_sc, hprev_sc, st_sc, sret_sc, kb_sc, bb_sc, vb_sc, oi_sc, p_ref, next_ref, xb_sc, x0b_sc,
                       *, layer, tt, sdec):
    t = pl.program_id(1)
    nt = pl.num_programs(1)
    flat = pl.program_id(0) * nt + t

    def project(xb_ref, dst, c0, c1):
        xb = xb_ref[...]
        for j in range(c0, c1, 256):
            dst[:, j:j + 256] = jnp.dot(xb, w_ref[:, j:j + 256], preferred_element_type=F32)

    xb_sc[...] = xn_ref[...].astype(BF16)

    @pl.when(flat == 0)
    def _():
        x0b_sc[...] = x0_ref[...].astype(BF16)
        project(x0b_sc, p_ref, 0, D_IN)

    @pl.when(t == 0)
    def _():
        prev_sc[...] = jnp.zeros_like(prev_sc)
        hprev_sc[...] = jnp.zeros_like(hprev_sc)
        st_sc[...] = jnp.zeros_like(st_sc)
        sret_sc[...] = jnp.zeros_like(sret_sc)

    xa = p_ref[:, O_XA:O_XA + D_A]
    ga = p_ref[:, O_GA:O_GA + D_A]
    row = lax.broadcasted_iota(jnp.int32, (tt, D_A), 0)
    row8 = lax.broadcasted_iota(jnp.int32, (SUBLANES, D_A), 0)
    prev = prev_sc[...]

    def shifted(j):
        r = pltpu.roll(xa, j, 0)
        top = jnp.where(row8 < j, pltpu.roll(prev, j, 0), r[0:SUBLANES])
        return jnp.concatenate([top, r[SUBLANES:]], axis=0)

    u = cb_ref[...] + shifted(3) * cw_ref[0:1, :]
    u = u + shifted(2) * cw_ref[1:2, :]
    u = u + shifted(1) * cw_ref[2:3, :]
    u = u + xa * cw_ref[3:4, :]
    last8 = xa[tt - SUBLANES:tt]
    prev_sc[...] = last8

    a, bterm = _rglru_gates(u, wri_ref[...], bri_ref[...], lam_ref[...])
    project(xb_sc, next_ref, 0, PROJ_SPLIT)
    s = 1
    while s < SUBLANES:
        keep = (row % SUBLANES) >= s
        a_sh = jnp.where(keep, pltpu.roll(a, s, 0), 1.0)
        b_sh = jnp.where(keep, pltpu.roll(bterm, s, 0), 0.0)
        bterm = a * b_sh + bterm
        a = a * a_sh
        s *= 2
    carry = hprev_sc[...]
    groups = []
    for gi in range(tt // SUBLANES):
        rows = slice(gi * SUBLANES, (gi + 1) * SUBLANES)
        h_in = jnp.broadcast_to(carry[SUBLANES - 1:SUBLANES, :], (SUBLANES, D_A))
        carry = a[rows] * h_in + bterm[rows]
        groups.append(carry)
    h = jnp.concatenate(groups, axis=0)
    hlast8 = carry
    hprev_sc[...] = hlast8
    y_ref[:, 0:D_A] = h * _gelu_tanh(ga)

    lb = _hgrn_lower_bound(lbl_ref[...], layer)
    ng = ng_ref[...]
    cl = B_CHUNK
    crow = lax.broadcasted_iota(jnp.int32, (cl, D_B), 0)
    ccol = lax.broadcasted_iota(jnp.int32, (cl, D_B), 1)
    causal = (ccol % B_DK) <= crow
    br = lax.broadcasted_iota(jnp.int32, (D_B, D_B), 0)
    bc = lax.broadcasted_iota(jnp.int32, (D_B, D_B), 1)
    head_mask = (br // B_DK) == (bc // B_DK)
    seg = jnp.where(head_mask, 1.0, 0.0).astype(BF16)

    def seg_mean(x):
        hi = x.astype(BF16)
        lo = (x - hi.astype(F32)).astype(BF16)
        tot = (jnp.dot(hi, seg, preferred_element_type=F32)
               + jnp.dot(lo, seg, preferred_element_type=F32))
        return tot * (1.0 / B_DK)

    nck = tt // cl
    q = _silu(p_ref[:, O_QB:O_QB + D_B])
    fl = p_ref[:, O_FB:O_FB + D_B]
    v = p_ref[:, O_VB:O_VB + D_B]
    half_th = 0.5 * jnp.tanh(0.5 * fl)
    f = lb + (1.0 - lb) * (0.5 + half_th)
    b = jnp.log(jnp.maximum(f, F_TINY))
    k = (1.0 - lb) * (0.5 - half_th)
    trow = lax.broadcasted_iota(jnp.int32, (tt, D_B), 0) % cl
    sh = 1
    while sh < cl:
        b = b + jnp.where(trow >= sh, pltpu.roll(b, sh, 0), 0.0)
        sh *= 2

    def chunk_row(x, c, r):
        top = c * cl + r + 1
        return x[top - SUBLANES:top][SUBLANES - 1:SUBLANES]

    def spread(rows):
        return jnp.concatenate([jnp.broadcast_to(r, (cl, D_B)) for r in rows], axis=0)

    last_rows = [chunk_row(b, c, cl - 1) for c in range(nck)]
    mid_rows = [chunk_row(b, c, cl // 2 - 1) for c in range(nck)]
    b_last = spread(last_rows)
    b_mid = spread(mid_rows)
    qd = q * jnp.exp(b)
    kl = k * jnp.exp(b_last - b)
    lasts = jnp.concatenate(last_rows, axis=0)
    mids = jnp.concatenate(mid_rows, axis=0)
    safe = jnp.min(jnp.minimum(mids, lasts - mids)) >= HGRN_SAFE_MIN_LOGDECAY

    def next_state(st, c, sl):
        kv = jnp.where(head_mask, _dot_g(v[sl], kl[sl], _TN), 0.0)
        return st * jnp.exp(last_rows[c]) + kv

    @pl.when(safe)
    def _():
        qmid = q * jnp.exp(b - b_mid)
        kinv = k * jnp.exp(b_mid - b)
        st = st_sc[...]
        for c in range(nck):
            sl = slice(c * cl, (c + 1) * cl)
            k4 = jnp.where(head_mask, jnp.concatenate([kinv[sl]] * B_HEADS, axis=0), 0.0)
            sc = jnp.where(causal, _dot_g(qmid[sl], k4, _NT), 0.0)
            v4 = jnp.where(head_mask, jnp.concatenate([v[sl]] * B_HEADS, axis=0), 0.0)
            oi_sc[sl, :] = _dot(sc, v4) + _dot_g(qd[sl], st, _NT)
            st = next_state(st, c, sl)
        st_sc[...] = st

    @pl.when(jnp.logical_not(safe))
    def _():
        kb_sc[...] = k
        bb_sc[...] = b
        vb_sc[...] = v
        st = st_sc[...]
        for c in range(nck):
            sl = slice(c * cl, (c + 1) * cl)
            qc, bc_ = q[sl], b[sl]

            def pair(sidx, acc, c=c, qc=qc, bc_=bc_):
                src = pl.ds(c * cl + sidx, 1)
                e = jnp.exp(jnp.minimum(bc_ - bb_sc[src, :], 0.0)) * (qc * kb_sc[src, :])
                scr = jnp.dot(e.astype(BF16), seg, preferred_element_type=F32)
                return acc + jnp.where(crow >= sidx, scr, 0.0) * vb_sc[src, :]

            oi = lax.fori_loop(0, cl, pair, jnp.zeros((cl, D_B), F32))
            oi_sc[sl, :] = oi + _dot_g(qd[sl], st, _NT)
            st = next_state(st, c, sl)
        st_sc[...] = st

    o = oi_sc[...]
    o = o * lax.rsqrt(seg_mean(o * o) + RMS_EPS) * ng
    y_ref[:, D_A:D_A + D_B] = o * _silu(p_ref[:, O_GB:O_GB + D_B])

    rl = C_CHUNK
    per_head = (D_IN - PROJ_SPLIT) // C_HEADS
    cos = cos_ref[...]
    sin = sin_ref[...]
    for hh in range(C_HEADS):
        lo = hh * C_DK
        q = _rope(p_ref[:, O_QC + lo:O_QC + lo + C_DK], cos, sin)
        k = _rope(p_ref[:, O_KC + lo:O_KC + lo + C_DK], cos, sin) * (C_DK ** -0.5)
        v = p_ref[:, O_VC + lo:O_VC + lo + C_DK]
        s = sret_sc[hh]
        parts = []
        for c in range(tt // rl):
            sl = slice(c * rl, (c + 1) * rl)
            sc = _dot_g(q[sl], k[sl], _NT) * dmat_ref[hh]
            parts.append(_dot(sc, v[sl]) + _dot(q[sl] * qdec_ref[hh], s))
            s = sdec[hh] * s + _dot_g(k[sl] * kdec_ref[hh], v[sl], _TN)
        sret_sc[hh] = s
        project(xb_sc, next_ref, PROJ_SPLIT + hh * per_head, PROJ_SPLIT + (hh + 1) * per_head)
        o = jnp.concatenate(parts, axis=0)
        mu = jnp.mean(o, axis=-1, keepdims=True)
        oc = o - mu
        var = jnp.mean(oc * oc, axis=-1, keepdims=True)
        g = p_ref[:, O_GC + lo:O_GC + lo + C_DK]
        y_ref[:, D_A + D_B + lo:D_A + D_B + lo + C_DK] = oc * lax.rsqrt(var + GN_EPS) * _silu(g)

    p_ref[...] = next_ref[...]

    @pl.when(t == nt - 1)
    def _():
        h_ref[...] = pltpu.roll(hprev_sc[...], 1, 0)[0:1]
        conv_ref[...] = pltpu.roll(prev_sc[...], CONV_W - 1, 0)[0:CONV_W - 1]
        s_bd = st_sc[...].T
        for hh in range(B_HEADS):
            hg_ref[hh] = s_bd[hh * B_DK:(hh + 1) * B_DK, hh * B_DK:(hh + 1) * B_DK]
        rt_ref[...] = sret_sc[...]


def _mix_prompt(x, w_in_bf, batch, seq, layer, tt, cw, cb, wri, bri, lam, lbl, ng):
    nt = seq // tt
    last = batch * nt - 1
    dmat, qdec, kdec, sdec = _retention_consts(C_CHUNK)
    cos, sin = _rope_tables(np.arange(seq))
    full = lambda shape: pl.BlockSpec(shape, lambda b, t: (0,) * len(shape))
    lay = lambda shape: pl.BlockSpec((None,) + shape, lambda b, t: (layer,) + (0,) * len(shape))
    kern = functools.partial(_mix_prompt_kernel, layer=layer, tt=tt, sdec=sdec)
    return pl.pallas_call(
        kern,
        grid=(batch, nt),
        in_specs=[
            pl.BlockSpec((tt, D_MODEL), lambda b, t: (0, 0)),
            pl.BlockSpec((tt, D_MODEL), lambda b, t: (jnp.minimum(b * nt + t + 1, last), 0)),
            full((D_MODEL, D_IN)),
            pl.BlockSpec((tt, C_DK), lambda b, t: (t, 0)),
            pl.BlockSpec((tt, C_DK), lambda b, t: (t, 0)),
            lay((CONV_W, D_A)), lay((1, D_A)), lay((D_A, 2 * D_A)), lay((1, 2 * D_A)),
            lay((1, D_A)), full((DEPTH, D_B)), lay((1, D_B)),
            full((C_HEADS, C_CHUNK, C_CHUNK)), full((C_HEADS, C_CHUNK, C_DK)),
            full((C_HEADS, C_CHUNK, C_DK)),
        ],
        out_specs=[
            pl.BlockSpec((tt, D_MODEL), lambda b, t: (b * nt + t, 0)),
            pl.BlockSpec((None, 1, D_A), lambda b, t: (b, 0, 0)),
            pl.BlockSpec((None, CONV_W - 1, D_A), lambda b, t: (b, 0, 0)),
            pl.BlockSpec((None, B_HEADS, B_DK, B_DK), lambda b, t: (b, 0, 0, 0)),
            pl.BlockSpec((None, C_HEADS, C_DK, C_DK), lambda b, t: (b, 0, 0, 0)),
        ],
        out_shape=[
            jax.ShapeDtypeStruct((batch * seq, D_MODEL), F32),
            jax.ShapeDtypeStruct((batch, 1, D_A), F32),
            jax.ShapeDtypeStruct((batch, CONV_W - 1, D_A), F32),
            jax.ShapeDtypeStruct((batch, B_HEADS, B_DK, B_DK), F32),
            jax.ShapeDtypeStruct((batch, C_HEADS, C_DK, C_DK), F32),
        ],
        scratch_shapes=[
            pltpu.VMEM((SUBLANES, D_A), F32), pltpu.VMEM((SUBLANES, D_A), F32),
            pltpu.VMEM((D_B, D_B), F32), pltpu.VMEM((C_HEADS, C_DK, C_DK), F32),
            pltpu.VMEM((tt, D_B), F32), pltpu.VMEM((tt, D_B), F32),
            pltpu.VMEM((tt, D_B), F32), pltpu.VMEM((tt, D_B), F32),
            pltpu.VMEM((tt, D_IN), F32), pltpu.VMEM((tt, D_IN), F32),
            pltpu.VMEM((tt, D_MODEL), BF16), pltpu.VMEM((tt, D_MODEL), BF16),
        ],
        compiler_params=_cparams(2),
        name="mix_prompt",
    )(x, x, w_in_bf, cos, sin, cw, cb, wri, bri, lam, lbl, ng, dmat, qdec, kdec)


def _column_matrix(x):
    pad = jnp.zeros((LANES - SUBLANES, LANES), F32)
    return jnp.concatenate([x, pad], axis=0).T


def _mix_sample_kernel(*refs, layer, gammas, n_prev):
    (p_ref, h0_ref, conv0_ref, rt0_ref, cos_ref, sin_ref,
     cw_ref, cb_ref, wri_ref, bri_ref, lam_ref) = refs[:11]
    prev = refs[11:11 + 3 * n_prev]
    y_ref, h_ref, conv_ref, rt_ref, o_sc = refs[11 + 3 * n_prev:]
    if n_prev:
        for j in range(n_prev):
            for dst, src in zip((h_ref, conv_ref, rt_ref), prev[3 * j:3 * j + 3]):
                dst[j] = src[...]
        h_ref, conv_ref, rt_ref = (r.at[layer] for r in (h_ref, conv_ref, rt_ref))
    tb = SUBLANES
    xa = p_ref[:, O_XA:O_XA + D_A]
    ga = p_ref[:, O_GA:O_GA + D_A]
    c0, c1, c2 = conv0_ref[0], conv0_ref[1], conv0_ref[2]
    u = cb_ref[...] + c0 * cw_ref[0:1, :]
    u = u + c1 * cw_ref[1:2, :]
    u = u + c2 * cw_ref[2:3, :]
    u = u + xa * cw_ref[3:4, :]
    conv_ref[0] = c1
    conv_ref[1] = c2
    conv_ref[2] = xa
    a, bterm = _rglru_gates(u, wri_ref[...], bri_ref[...], lam_ref[...])
    h = a * h0_ref[...] + bterm
    h_ref[...] = h
    y_ref[:, 0:D_A] = h * _gelu_tanh(ga)

    y_ref[:, D_A:D_A + D_B] = jnp.zeros((tb, D_B), F32)

    cos = cos_ref[...]
    sin = sin_ref[...]
    for head in range(C_HEADS):
        lo = head * C_DK
        gamma = gammas[head]
        q = _rope(p_ref[:, O_QC + lo:O_QC + lo + C_DK], cos, sin)
        k = _rope(p_ref[:, O_KC + lo:O_KC + lo + C_DK], cos, sin) * (C_DK ** -0.5)
        v = p_ref[:, O_VC + lo:O_VC + lo + C_DK]
        g = p_ref[:, O_GC + lo:O_GC + lo + C_DK]
        dots = jnp.sum(q * k, axis=-1, keepdims=True)
        q_cols = _column_matrix(q * gamma)
        k_cols = _column_matrix(k)
        for b in range(tb):
            s_old = rt0_ref[b, head]
            v_row = v[b:b + 1, :]
            rt_ref[b, head] = gamma * s_old + k_cols[:, b:b + 1] * v_row
            o_row = (jnp.sum(q_cols[:, b:b + 1] * s_old, axis=0, keepdims=True)
                     + dots[b:b + 1, :] * v_row)
            o_sc[b:b + 1, 0:C_DK] = o_row
        o = o_sc[:, 0:C_DK]
        mu = jnp.mean(o, axis=-1, keepdims=True)
        oc = o - mu
        var = jnp.mean(oc * oc, axis=-1, keepdims=True)
        y_ref[:, D_A + D_B + lo:D_A + D_B + lo + C_DK] = oc * lax.rsqrt(var + GN_EPS) * _silu(g)


def _mix_sample(proj, h0, conv0_t, rt0, layer, cw, cb, wri, bri, lam, prev=()):
    nb = proj.shape[0]
    tb = SUBLANES
    lg = np.log1p(-np.exp2(-5.0 - np.arange(C_HEADS, dtype=np.float64)))
    gammas = [float(np.exp(v)) for v in lg]
    cos, sin = _rope_tables([PAST_LEN])
    full = lambda shape: pl.BlockSpec(shape, lambda i: (0,) * len(shape))
    lay = lambda shape: pl.BlockSpec((None,) + shape, lambda i: (layer,) + (0,) * len(shape))
    state_shapes = [(tb, D_A), (CONV_W - 1, tb, D_A), (tb, C_HEADS, C_DK, C_DK)]
    state_maps = [lambda i: (i, 0), lambda i: (0, i, 0), lambda i: (i, 0, 0, 0)]
    full_shapes = [(nb, D_A), (CONV_W - 1, nb, D_A), (nb, C_HEADS, C_DK, C_DK)]
    state_specs = [pl.BlockSpec(s, m) for s, m in zip(state_shapes, state_maps)]
    if prev:
        stack = len(prev) + 1
        out_state_specs = [pl.BlockSpec((stack,) + s, lambda i, m=m: (0,) + m(i))
                           for s, m in zip(state_shapes, state_maps)]
        out_state_shapes = [jax.ShapeDtypeStruct((stack,) + s, F32) for s in full_shapes]
    else:
        out_state_specs = state_specs
        out_state_shapes = [jax.ShapeDtypeStruct(s, F32) for s in full_shapes]
    kern = functools.partial(_mix_sample_kernel, layer=layer, gammas=gammas, n_prev=len(prev))
    return pl.pallas_call(
        kern,
        grid=(nb // tb,),
        in_specs=[
            pl.BlockSpec((tb, D_IN), lambda i: (i, 0)),
            pl.BlockSpec((None, tb, D_A), lambda i: (layer, i, 0)),
            pl.BlockSpec((None, CONV_W - 1, tb, D_A), lambda i: (layer, 0, i, 0)),
            pl.BlockSpec((None, tb, C_HEADS, C_DK, C_DK), lambda i: (layer, i, 0, 0, 0)),
            full((1, C_DK)), full((1, C_DK)),
            lay((CONV_W, D_A)), lay((1, D_A)), lay((D_A, 2 * D_A)), lay((1, 2 * D_A)),
            lay((1, D_A)),
        ] + state_specs * len(prev),
        out_specs=[pl.BlockSpec((tb, D_MODEL), lambda i: (i, 0))] + out_state_specs,
        out_shape=[jax.ShapeDtypeStruct((nb, D_MODEL), F32)] + out_state_shapes,
        scratch_shapes=[pltpu.VMEM((tb, D_B), F32)],
        compiler_params=_cparams(1),
        name="mix_sample",
    )(proj, h0, conv0_t, rt0, cos, sin, cw, cb, wri, bri, lam,
      *[a for states in prev for a in states])


def _hgrn_sample_kernel(*refs, layer, n_prev):
    qb_ref, fb_ref, vb_ref, gb_ref, lbl_ref, ng_ref, s0_ref, y_in_ref = refs[:8]
    prev = refs[8:8 + n_prev]
    y_ref, s_ref = refs[8 + n_prev:]
    del y_in_ref
    for j in range(n_prev):
        s_ref[j] = prev[j][...]
    if n_prev:
        s_ref = s_ref.at[layer]
    lb = _hgrn_lower_bound(lbl_ref[...], layer)
    q = _silu(qb_ref[...])
    fl = fb_ref[...]
    v = vb_ref[...]
    half_th = 0.5 * jnp.tanh(0.5 * fl)
    f = lb + (1.0 - lb) * (0.5 + half_th)
    ef = jnp.exp(jnp.log(jnp.maximum(f, F_TINY)))
    k = (1.0 - lb) * (0.5 - half_th)
    qf_t, k_t, ef_t, v_t, qk_t = (x.T for x in (q * ef, k, ef, v, q * k))
    o_parts = []
    for hh in range(2):
        base = hh * B_DK
        v_h = v_t[base:base + B_DK]
        acc = jnp.zeros((B_DK, LANES), F32)
        for kk in range(B_DK):
            r = base + kk
            s_old = s0_ref[hh, kk]
            s_ref[hh, kk] = ef_t[r:r + 1] * s_old + k_t[r:r + 1] * v_h
            acc = acc + qf_t[r:r + 1] * s_old
        dots = jnp.sum(qk_t[base:base + B_DK], axis=0, keepdims=True)
        o = acc + dots * v_h
        o_parts.append(o * lax.rsqrt(jnp.mean(o * o, axis=0, keepdims=True) + RMS_EPS))
    o = jnp.concatenate(o_parts, axis=0).T
    y_ref[...] = o * ng_ref[...] * _silu(gb_ref[...])


def _hgrn_sample(proj, y, s0_t, layer, lbl, ng_pair, prev=()):
    nb = proj.shape[0]
    col = lambda off: pl.BlockSpec((nb, LANES), lambda j: (0, off // LANES + j))
    state_block = (2, B_DK, B_DK, nb)
    state_spec = pl.BlockSpec(state_block, lambda j: (j, 0, 0, 0))
    stack = len(prev) + 1
    if prev:
        out_state_spec = pl.BlockSpec((stack,) + state_block, lambda j: (0, j, 0, 0, 0))
        out_state_shape = jax.ShapeDtypeStruct((stack, B_HEADS, B_DK, B_DK, nb), F32)
    else:
        out_state_spec = state_spec
        out_state_shape = jax.ShapeDtypeStruct((B_HEADS, B_DK, B_DK, nb), F32)
    return pl.pallas_call(
        functools.partial(_hgrn_sample_kernel, layer=layer, n_prev=len(prev)),
        grid=(B_HEADS // 2,),
        in_specs=[col(O_QB), col(O_FB), col(O_VB), col(O_GB),
                  pl.BlockSpec((DEPTH, LANES), lambda j: (0, j)),
                  pl.BlockSpec((None, 1, LANES), lambda j: (layer, 0, 0)),
                  pl.BlockSpec((None,) + state_block, lambda j: (layer, j, 0, 0, 0)),
                  pl.BlockSpec(memory_space=pl.ANY)] + [state_spec] * len(prev),
        out_specs=[col(D_A), out_state_spec],
        out_shape=[jax.ShapeDtypeStruct(y.shape, F32), out_state_shape],
        input_output_aliases={7: 0},
        compiler_params=_cparams(1),
        name="hgrn_sample",
    )(proj, proj, proj, proj, lbl, ng_pair, s0_t, y, *prev)


def _route_rows(l):
    m = l[0]
    for x in l[1:]:
        m = jnp.maximum(m, x)
    ex = [jnp.exp(x - m) for x in l]
    tot = ex[0]
    for x in ex[1:]:
        tot = tot + x
    p = [x / tot for x in ex]
    scores = []
    for gi in range(N_GROUPS):
        a, b, c, d = p[4 * gi:4 * gi + 4]
        hi1, lo1 = jnp.maximum(a, b), jnp.minimum(a, b)
        hi2, lo2 = jnp.maximum(c, d), jnp.minimum(c, d)
        top1 = jnp.maximum(hi1, hi2)
        top2 = jnp.maximum(jnp.minimum(hi1, hi2), jnp.maximum(lo1, lo2))
        scores.append(top1 + top2)
    best = scores[0]
    gsel = jnp.zeros_like(best, dtype=jnp.int32)
    for gi in range(1, N_GROUPS):
        upd = scores[gi] > best
        gsel = jnp.where(upd, gi, gsel)
        best = jnp.where(upd, scores[gi], best)
    vals = []
    for j in range(EXP_PER_GROUP):
        v = p[j]
        for gi in range(1, N_GROUPS):
            v = jnp.where(gsel == gi, p[4 * gi + j], v)
        vals.append(v)
    sel = []
    for j in range(EXP_PER_GROUP):
        rank = jnp.zeros_like(gsel)
        for i in range(EXP_PER_GROUP):
            if i == j:
                continue
            ahead = (vals[i] > vals[j]) | ((vals[i] == vals[j]) & (i < j))
            rank = rank + jnp.where(ahead, 1, 0)
        sel.append(rank < 2)
    denom = jnp.zeros_like(best)
    for j in range(EXP_PER_GROUP):
        denom = denom + jnp.where(sel[j], vals[j], 0.0)
    gates = [jnp.where(sel[j], vals[j] / denom, 0.0) for j in range(EXP_PER_GROUP)]
    comb = [jnp.where(gsel == (e // EXP_PER_GROUP), gates[e % EXP_PER_GROUP], 0.0)
            for e in range(N_EXPERTS)]
    j1 = jnp.where(sel[0], 0, jnp.where(sel[1], 1, 2))
    j2 = jnp.where(sel[3], 3, jnp.where(sel[2], 2, 1))
    g1 = jnp.zeros_like(best)
    g2 = jnp.zeros_like(best)
    for j in range(EXP_PER_GROUP):
        g1 = jnp.where(j1 == j, gates[j], g1)
        g2 = jnp.where(j2 == j, gates[j], g2)
    return comb, gsel * EXP_PER_GROUP + j1, gsel * EXP_PER_GROUP + j2, g1, g2


MOE_TILE = 512
GRANULE = 16
GRANULES_PER_TILE = MOE_TILE // GRANULE
XS_WIDTH = D_MODEL + LANES
LARGE_PIECE = 8
ROUTE_SLOT1, ROUTE_SLOT2 = N_EXPERTS, N_EXPERTS + 1
PLAN_SLOT1, PLAN_SLOT2, PLAN_G1, PLAN_G2 = 0, 1, 2, 3


def _local_slots(tt):
    worst = 2 * tt + (N_EXPERTS - 1) * GRANULE
    return -(-worst // LANES) * LANES


def _outproj_kernel(*refs, plan, sub, st):
    if plan:
        (y_ref, x_ref, w_ref, g_ref, b_ref, rwt_ref, rb_ref, tri_ref, ltri_ref,
         x1_ref, route_ref, plan_ref, gran_ref, loff_ref, rt_sc, oh_sc) = refs
    else:
        y_ref, x_ref, w_ref, g_ref, b_ref, rwt_ref, rb_ref, x1_ref, route_ref, wb_ref, rt_sc = refs
    if plan:
        y = jnp.dot(y_ref[...].astype(BF16), w_ref[...], preferred_element_type=F32)
    else:
        yh, yl = _split(y_ref[...])
        wh, wl = _split(w_ref[...])
        wb_ref[...] = wh
        y = (jnp.dot(yh, wh, preferred_element_type=F32) + jnp.dot(yl, wh, preferred_element_type=F32)
             + jnp.dot(yh, wl, preferred_element_type=F32))
    x1 = _layernorm(ALPHA * x_ref[...] + y, g_ref[...], b_ref[...])
    x1_ref[...] = x1
    hi, lo = _split(x1)
    rhi, rlo = _split(rwt_ref[...])
    lg = (lax.dot_general(rhi, hi, _NT, preferred_element_type=F32)
          + lax.dot_general(rhi, lo, _NT, preferred_element_type=F32)
          + lax.dot_general(rlo, hi, _NT, preferred_element_type=F32)) + rb_ref[...]
    comb, e1, e2, g1, g2 = _route_rows([lg[e:e + 1, :] for e in range(N_EXPERTS)])
    rt_sc[...] = jnp.zeros_like(rt_sc)
    if not plan:
        for e in range(N_EXPERTS):
            rt_sc[e:e + 1, :] = comb[e]
    else:
        for e in range(N_EXPERTS):
            oh_sc[e:e + 1, :] = jnp.where((e1 == e) | (e2 == e), 1.0, 0.0)
        bases = []
        for s in range(sub):
            oh = oh_sc[:, s * st:(s + 1) * st]
            cum = jnp.dot(oh.astype(BF16), tri_ref[...], preferred_element_type=F32)
            count = jnp.sum(oh, axis=1, keepdims=True)
            gran = jnp.floor((count + (GRANULE - 1)) * (1.0 / GRANULE))
            gran_b = jnp.broadcast_to(gran, (N_EXPERTS, LANES))
            loff_b = jnp.dot(ltri_ref[...], gran_b.astype(BF16), preferred_element_type=F32)
            bases.append(loff_b[:, 0:1] * GRANULE - 1.0 + cum)
            gran_ref[s] = gran_b.astype(jnp.int32)
            loff_ref[s] = loff_b.astype(jnp.int32)
        slot_base = jnp.concatenate(bases, axis=1)
        s1 = jnp.zeros_like(g1)
        s2 = jnp.zeros_like(g1)
        for e in range(N_EXPERTS):
            row = slot_base[e:e + 1, :]
            s1 = jnp.where(e1 == e, row, s1)
            s2 = jnp.where(e2 == e, row, s2)
        for s in range(sub):
            cols = slice(s * st, (s + 1) * st)
            plan_ref[s] = jnp.zeros((SUBLANES, st), F32)
            plan_ref[s, PLAN_SLOT1:PLAN_SLOT1 + 1, :] = s1[:, cols]
            plan_ref[s, PLAN_SLOT2:PLAN_SLOT2 + 1, :] = s2[:, cols]
            plan_ref[s, PLAN_G1:PLAN_G1 + 1, :] = g1[:, cols]
            plan_ref[s, PLAN_G2:PLAN_G2 + 1, :] = g2[:, cols]
        rt_sc[ROUTE_SLOT1:ROUTE_SLOT1 + 1, :] = s1
        rt_sc[ROUTE_SLOT2:ROUTE_SLOT2 + 1, :] = s2
    route_ref[...] = rt_sc[...].T


def _outproj(y, x, w_out, layer, st, g, b, rwt, rb, plan=False, sub=1):
    n = x.shape[0]
    tm = st * sub
    full = lambda shape: pl.BlockSpec(shape, lambda i: (0,) * len(shape))
    lay = lambda shape: pl.BlockSpec((None,) + shape, lambda i: (layer,) + (0,) * len(shape))
    w_spec = full((D_MODEL, D_MODEL)) if plan else lay((D_MODEL, D_MODEL))
    in_specs = [pl.BlockSpec((tm, D_MODEL), lambda i: (i, 0)),
                pl.BlockSpec((tm, D_MODEL), lambda i: (i, 0)),
                w_spec, lay((1, D_MODEL)), lay((1, D_MODEL)),
                full((N_EXPERTS, D_MODEL)), full((N_EXPERTS, 1))]
    out_specs = [pl.BlockSpec((tm, D_MODEL), lambda i: (i, 0)),
                 pl.BlockSpec((tm, LANES), lambda i: (i, 0))]
    out_shape = [jax.ShapeDtypeStruct((n, D_MODEL), F32),
                 jax.ShapeDtypeStruct((n, LANES), F32)]
    scratch = [pltpu.VMEM((LANES, tm), F32)]
    args = [y, x, w_out, g, b, rwt, rb]
    if plan:
        tri = np.triu(np.ones((st, st), np.float32)).astype(jnp.bfloat16)
        ltri = np.tril(np.ones((N_EXPERTS, N_EXPERTS), np.float32), -1).astype(jnp.bfloat16)
        in_specs += [full((st, st)), full((N_EXPERTS, N_EXPERTS))]
        args += [tri, ltri]
        per_tile = pl.BlockSpec((sub, N_EXPERTS, LANES), lambda i: (i, 0, 0))
        out_specs += [pl.BlockSpec((sub, SUBLANES, st), lambda i: (i, 0, 0)), per_tile, per_tile]
        out_shape += [jax.ShapeDtypeStruct((n // st, SUBLANES, st), F32),
                      jax.ShapeDtypeStruct((n // st, N_EXPERTS, LANES), jnp.int32),
                      jax.ShapeDtypeStruct((n // st, N_EXPERTS, LANES), jnp.int32)]
        scratch += [pltpu.VMEM((N_EXPERTS, tm), F32)]
    else:
        assert sub == 1 and tm == n, "the sample trunk is one tile; it also emits the bf16 weight"
        out_specs.append(full((D_MODEL, D_MODEL)))
        out_shape.append(jax.ShapeDtypeStruct((D_MODEL, D_MODEL), BF16))
    return pl.pallas_call(
        functools.partial(_outproj_kernel, plan=plan, sub=sub, st=st),
        grid=(n // tm,),
        in_specs=in_specs,
        out_specs=out_specs,
        out_shape=out_shape,
        scratch_shapes=scratch,
        compiler_params=_cparams(1),
        name="outproj_plan" if plan else "outproj",
    )(*args)


def _granule_copies(src, dst, src_g, dst_g, n, sem, start, max_granules):
    def pieces(sizes, src_g, dst_g):
        for size in sizes:
            bit = n & size

            @pl.when(bit != 0)
            def _(size=size, src_g=src_g, dst_g=dst_g):
                s0 = pl.multiple_of(src_g * GRANULE, GRANULE)
                d0 = pl.multiple_of(dst_g * GRANULE, GRANULE)
                cp = pltpu.make_async_copy(src.at[pl.ds(s0, size * GRANULE)],
                                           dst.at[pl.ds(d0, size * GRANULE)], sem)
                cp.start() if start else cp.wait()

            src_g = src_g + bit
            dst_g = dst_g + bit

    sizes = [max_granules >> k for k in range(max_granules.bit_length())]
    large = [s for s in sizes if s >= LARGE_PIECE]
    small = [s for s in sizes if s < LARGE_PIECE]

    @pl.when(n >= LARGE_PIECE)
    def _():
        pieces(large, src_g, dst_g)

    done = n & sum(large)
    pieces(small, src_g + done, dst_g + done)


def _dispatch_kernel(segg_ref, totg_ref, ctl_ref, ctl_prev_ref, plan_ref, x_ref, xs_ref,
                     xl_sc, zx_sc, sem, *, tt, total_tiles):
    step = pl.program_id(0)
    cur = step % 2

    @pl.when(step == 0)
    def _():
        zx_sc[...] = jnp.zeros_like(zx_sc)

        def zero_tile(i):
            r0 = pl.multiple_of(i * MOE_TILE, MOE_TILE)
            return pltpu.make_async_copy(zx_sc, xs_ref.at[pl.ds(r0, MOE_TILE)], sem.at[2])

        def run(i, start):
            cp = zero_tile(i)
            cp.start() if start else cp.wait()

        for start in (True, False):
            for e in range(N_EXPERTS):
                end = segg_ref[e] + totg_ref[e]

                @pl.when(end % GRANULES_PER_TILE != 0)
                def _():
                    run(end // GRANULES_PER_TILE, start)

        last = N_EXPERTS - 1
        used = (segg_ref[last] + totg_ref[last] + GRANULES_PER_TILE - 1) // GRANULES_PER_TILE
        lax.fori_loop(used, total_tiles, lambda i, c: (run(i, True), c)[1], 0)
        lax.fori_loop(used, total_tiles, lambda i, c: (run(i, False), c)[1], 0)

    n_slots = xl_sc.shape[1]
    xl = xl_sc.at[cur]
    slot = lax.broadcasted_iota(jnp.int32, (n_slots, tt), 0)
    m1 = slot == plan_ref[PLAN_SLOT1:PLAN_SLOT1 + 1, :].astype(jnp.int32)
    m2 = slot == plan_ref[PLAN_SLOT2:PLAN_SLOT2 + 1, :].astype(jnp.int32)
    perm = jnp.where(m1 | m2, 1.0, 0.0).astype(BF16)
    xl[:, :D_MODEL] = jnp.dot(
        perm, x_ref[...].astype(BF16), preferred_element_type=F32).astype(BF16)
    gate = jnp.sum(jnp.where(m1, plan_ref[PLAN_G1:PLAN_G1 + 1, :], 0.0)
                   + jnp.where(m2, plan_ref[PLAN_G2:PLAN_G2 + 1, :], 0.0), axis=1, keepdims=True)
    g_hi = gate.astype(BF16).astype(F32)
    g_lo = (gate - g_hi).astype(BF16).astype(F32)
    lane = lax.broadcasted_iota(jnp.int32, (n_slots, LANES), 1)
    xl[:, D_MODEL:] = jnp.where(lane == 0, g_hi, jnp.where(lane == 1, g_lo, 0.0)).astype(BF16)

    def copies(ctl, half, start):
        for e in range(N_EXPERTS):
            n, src_g, dst_g = ctl[0, e], ctl[1, e], ctl[2, e]
            _granule_copies(xl_sc.at[half], xs_ref, src_g, dst_g, n, sem.at[half], start,
                            tt // GRANULE)

    copies(ctl_ref, cur, True)

    @pl.when(step > 0)
    def _():
        copies(ctl_prev_ref, 1 - cur, False)

    @pl.when(step == pl.num_programs(0) - 1)
    def _():
        copies(ctl_ref, cur, False)


def _dispatch(x1, plan, ctl, seg_g, tot_g, tt, rows):
    n = x1.shape[0]
    n_slots = _local_slots(tt)
    return pl.pallas_call(
        functools.partial(_dispatch_kernel, tt=tt, total_tiles=rows // MOE_TILE),
        grid_spec=pltpu.PrefetchScalarGridSpec(
            num_scalar_prefetch=2,
            grid=(n // tt,),
            in_specs=[pl.BlockSpec((None, SUBLANES, LANES), lambda i, s, c: (i, 0, 0),
                                   memory_space=pltpu.SMEM),
                      pl.BlockSpec((None, SUBLANES, LANES), lambda i, s, c: (jnp.maximum(i - 1, 0), 0, 0),
                                   memory_space=pltpu.SMEM),
                      pl.BlockSpec((None, SUBLANES, tt), lambda i, s, c: (i, 0, 0)),
                      pl.BlockSpec((tt, D_MODEL), lambda i, s, c: (i, 0))],
            out_specs=pl.BlockSpec(memory_space=pl.ANY),
            scratch_shapes=[pltpu.VMEM((2, n_slots, XS_WIDTH), BF16),
                            pltpu.VMEM((MOE_TILE, XS_WIDTH), BF16),
                            pltpu.SemaphoreType.DMA((3,))]),
        out_shape=jax.ShapeDtypeStruct((rows, XS_WIDTH), BF16),
        compiler_params=_cparams(1),
        name="moe_dispatch",
    )(seg_g, tot_g, ctl, ctl, plan, x1)


def _ffn_kernel(te_ref, nv_ref, xs_ref, wg_ref, wu_ref, wd_ref, ys_ref):
    used = pl.program_id(0) < nv_ref[0]

    @pl.when(used)
    def _():
        xb = xs_ref[:, :D_MODEL]
        gate_pair = xs_ref[:, D_MODEL:].astype(F32)
        gate = gate_pair[:, 0:1] + gate_pair[:, 1:2]
        hg = jnp.dot(xb, wg_ref[...].astype(BF16), preferred_element_type=F32)
        hu = jnp.dot(xb, wu_ref[...].astype(BF16), preferred_element_type=F32)
        h = (_silu(hg) * hu * gate).astype(BF16)
        ys_ref[...] = jnp.dot(h, wd_ref[...].astype(BF16), preferred_element_type=F32).astype(BF16)

    @pl.when(jnp.logical_not(used))
    def _():
        ys_ref[...] = jnp.zeros_like(ys_ref)


def _ffn(xs, tile_expert, n_valid, wg, wu, wd, layer):
    total_tiles = xs.shape[0] // MOE_TILE
    w_in_spec = pl.BlockSpec((None, None, D_MODEL, D_EXPERT), lambda i, te, nv: (layer, te[i], 0, 0))
    return pl.pallas_call(
        _ffn_kernel,
        grid_spec=pltpu.PrefetchScalarGridSpec(
            num_scalar_prefetch=2,
            grid=(total_tiles,),
            in_specs=[pl.BlockSpec((MOE_TILE, XS_WIDTH), lambda i, te, nv: (jnp.minimum(i, nv[0] - 1), 0)),
                      w_in_spec, w_in_spec,
                      pl.BlockSpec((None, None, D_EXPERT, D_MODEL),
                                   lambda i, te, nv: (layer, te[i], 0, 0))],
            out_specs=pl.BlockSpec((MOE_TILE, D_MODEL), lambda i, te, nv: (i, 0))),
        out_shape=jax.ShapeDtypeStruct((xs.shape[0], D_MODEL), BF16),
        compiler_params=_cparams(1),
        name="moe_ffn",
    )(tile_expert, n_valid, xs, wg, wu, wd)


def _combine_kernel(ctl_ref, ctl_next_ref, x_ref, route_ref, g_ref, b_ref, ys_ref, o_ref,
                    yl_sc, sem, *, tt):
    i = pl.program_id(0)
    cur = i % 2

    def copies(ctl, half, start):
        for e in range(N_EXPERTS):
            n, loc_g, buf_g = ctl[0, e], ctl[1, e], ctl[2, e]
            _granule_copies(ys_ref, yl_sc.at[half], buf_g, loc_g, n, sem.at[half], start,
                            tt // GRANULE)

    @pl.when(i == 0)
    def _():
        yl_sc[...] = jnp.zeros_like(yl_sc)
        copies(ctl_ref, 0, True)

    @pl.when(i + 1 < pl.num_programs(0))
    def _():
        copies(ctl_next_ref, 1 - cur, True)

    copies(ctl_ref, cur, False)

    n_slots = yl_sc.shape[1]
    route = route_ref[...]
    slot = lax.broadcasted_iota(jnp.int32, (tt, n_slots), 1)
    pick = ((slot == route[:, ROUTE_SLOT1:ROUTE_SLOT1 + 1].astype(jnp.int32))
            | (slot == route[:, ROUTE_SLOT2:ROUTE_SLOT2 + 1].astype(jnp.int32)))
    y = jnp.dot(jnp.where(pick, 1.0, 0.0).astype(BF16), yl_sc[cur], preferred_element_type=F32)
    o_ref[...] = _layernorm(ALPHA * x_ref[...] + y, g_ref[...], b_ref[...])


def _combine(x1, route, ctl, ys, layer, tt, g, b):
    n = x1.shape[0]
    last = n // tt - 1
    lay = lambda shape: pl.BlockSpec((None,) + shape, lambda i: (layer,) + (0,) * len(shape))
    return pl.pallas_call(
        functools.partial(_combine_kernel, tt=tt),
        grid=(n // tt,),
        in_specs=[pl.BlockSpec((None, SUBLANES, LANES), lambda i: (i, 0, 0), memory_space=pltpu.SMEM),
                  pl.BlockSpec((None, SUBLANES, LANES), lambda i: (jnp.minimum(i + 1, last), 0, 0),
                               memory_space=pltpu.SMEM),
                  pl.BlockSpec((tt, D_MODEL), lambda i: (i, 0)),
                  pl.BlockSpec((tt, LANES), lambda i: (i, 0)),
                  lay((1, D_MODEL)), lay((1, D_MODEL)),
                  pl.BlockSpec(memory_space=pl.ANY)],
        out_specs=pl.BlockSpec((tt, D_MODEL), lambda i: (i, 0)),
        scratch_shapes=[pltpu.VMEM((2, _local_slots(tt), D_MODEL), BF16),
                        pltpu.SemaphoreType.DMA((2,))],
        out_shape=jax.ShapeDtypeStruct((n, D_MODEL), F32),
        compiler_params=_cparams(1),
        name="moe_combine",
    )(ctl, ctl, x1, route, g, b, ys)


def _moe_sparse(x1, route, plan, gran, loff, wg, wu, wd, layer, tt, g, b):
    n = x1.shape[0]
    n_tiles = n // tt
    gran = gran[:, :, 0]
    loff = loff[:, :, 0]
    max_tiles = -(-(2 * n + (GRANULE - 1) * N_EXPERTS * n_tiles) // MOE_TILE) + N_EXPERTS
    rows = max_tiles * MOE_TILE
    tot_g = jnp.sum(gran, axis=0)
    tiles = (tot_g + GRANULES_PER_TILE - 1) // GRANULES_PER_TILE
    tile_end = jnp.cumsum(tiles)
    seg_g = ((tile_end - tiles) * GRANULES_PER_TILE).astype(jnp.int32)
    n_valid = tile_end[-1]
    idx = jnp.minimum(jnp.arange(max_tiles, dtype=jnp.int32), n_valid - 1)
    tile_expert = jnp.sum(idx[:, None] >= tile_end[None, :], axis=1).astype(jnp.int32)
    buf_g = seg_g[None, :] + jnp.cumsum(gran, axis=0) - gran
    ctl = jnp.zeros((n_tiles, SUBLANES, LANES), jnp.int32)
    ctl = ctl.at[:, 0, :N_EXPERTS].set(gran).at[:, 1, :N_EXPERTS].set(loff).at[:, 2, :N_EXPERTS].set(buf_g)
    xs = _dispatch(x1, plan, ctl, seg_g, tot_g.astype(jnp.int32), tt, rows)
    ys = _ffn(xs, tile_expert, n_valid.reshape(1).astype(jnp.int32), wg, wu, wd, layer)
    return _combine(x1, route, ctl, ys, layer, tt, g, b)


def _moe_kernel(x_ref, r_ref, wg_ref, wu_ref, wd_ref, g_ref, b_ref, o_ref, xb_sc, acc_sc):
    e = pl.program_id(1)

    @pl.when(e == 0)
    def _():
        xb_sc[...] = x_ref[...].astype(BF16)
        acc_sc[...] = jnp.zeros_like(acc_sc)

    xb = xb_sc[...]
    hg = jnp.dot(xb, wg_ref[...].astype(BF16), preferred_element_type=F32)
    hu = jnp.dot(xb, wu_ref[...].astype(BF16), preferred_element_type=F32)
    r = r_ref[...]
    lane = lax.broadcasted_iota(jnp.int32, r.shape, 1)
    c = jnp.sum(jnp.where(lane == e, r, 0.0), axis=-1, keepdims=True)
    h = _silu(hg) * hu * c
    acc_sc[...] += jnp.dot(h.astype(BF16), wd_ref[...].astype(BF16), preferred_element_type=F32)

    @pl.when(e == pl.num_programs(1) - 1)
    def _():
        o_ref[...] = _layernorm(ALPHA * x_ref[...] + acc_sc[...], g_ref[...], b_ref[...])


def _moe(x1, route, wg, wu, wd, layer, tm, g, b):
    n = x1.shape[0]
    lay = lambda shape: pl.BlockSpec((None,) + shape, lambda i, e: (layer,) + (0,) * len(shape))
    return pl.pallas_call(
        _moe_kernel,
        grid=(n // tm, N_EXPERTS),
        in_specs=[pl.BlockSpec((tm, D_MODEL), lambda i, e: (i, 0)),
                  pl.BlockSpec((tm, LANES), lambda i, e: (i, 0)),
                  pl.BlockSpec((None, None, D_MODEL, D_EXPERT), lambda i, e: (layer, e, 0, 0)),
                  pl.BlockSpec((None, None, D_MODEL, D_EXPERT), lambda i, e: (layer, e, 0, 0)),
                  pl.BlockSpec((None, None, D_EXPERT, D_MODEL), lambda i, e: (layer, e, 0, 0)),
                  lay((1, D_MODEL)), lay((1, D_MODEL))],
        out_specs=pl.BlockSpec((tm, D_MODEL), lambda i, e: (i, 0)),
        out_shape=jax.ShapeDtypeStruct((n, D_MODEL), F32),
        scratch_shapes=[pltpu.VMEM((tm, D_MODEL), BF16), pltpu.VMEM((tm, D_MODEL), F32)],
        compiler_params=_cparams(2),
        name="moe",
    )(x1, route, wg, wu, wd, g, b)


def _block_diag(w):
    out = jnp.zeros((DEPTH, D_A, D_A), w.dtype)
    for gi in range(A_BLOCKS):
        out = out.at[:, gi * A_BLK:(gi + 1) * A_BLK, gi * A_BLK:(gi + 1) * A_BLK].set(w[:, gi])
    return out


def kernel(x_prompt, x_sample, state_rglru_h, state_conv, state_hgrn, state_ret, w_in, conv_w, conv_b, w_rgate, b_rgate, w_igate, b_igate, rglru_lambda, hgrn_lb_logits, hgrn_norm_g, w_out, ln1_g, ln1_b, router_w, router_b, exp_w_gate, exp_w_up, exp_w_down, ln2_g, ln2_b):
    batch, seq, _ = x_prompt.shape
    nb = x_sample.shape[0]

    wg, wu, wd = exp_w_gate, exp_w_up, exp_w_down
    wri = jnp.concatenate([_block_diag(w_rgate), _block_diag(w_igate)], axis=-1).astype(BF16)
    bri = jnp.concatenate([b_rgate.reshape(DEPTH, 1, D_A), b_igate.reshape(DEPTH, 1, D_A)], axis=-1)
    cb = conv_b.reshape(DEPTH, 1, D_A)
    lam = rglru_lambda.reshape(DEPTH, 1, D_A)
    ng = jnp.tile(hgrn_norm_g, (1, B_HEADS)).reshape(DEPTH, 1, D_B)
    g1, b1 = ln1_g.reshape(DEPTH, 1, D_MODEL), ln1_b.reshape(DEPTH, 1, D_MODEL)
    g2, b2 = ln2_g.reshape(DEPTH, 1, D_MODEL), ln2_b.reshape(DEPTH, 1, D_MODEL)
    rwt = router_w.T
    rb = router_b.reshape(N_EXPERTS, 1)
    conv0_t = jnp.transpose(state_conv, (0, 2, 1, 3))
    hg0_t = jnp.transpose(state_hgrn, (0, 2, 3, 4, 1))

    xp = x_prompt.reshape(batch * seq, D_MODEL)
    xs = x_sample.reshape(nb, D_MODEL)
    hs_p, convs_p, hgs_p, rts_p = [], [], [], []
    prev_s, prev_hg = [], []
    for l in range(DEPTH):
        last = l == DEPTH - 1
        proj_s, w_in_bf = _proj(xs, w_in, l)
        y_s, *states_s = _mix_sample(
            proj_s, state_rglru_h, conv0_t, state_ret, l, conv_w, cb, wri, bri, lam,
            prev=prev_s if last else ())
        prev_s.append(states_s)
        y_s, hg_s = _hgrn_sample(proj_s, y_s, hg0_t, l, hgrn_lb_logits, ng[:, :, :LANES],
                                 prev=prev_hg if last else ())
        prev_hg.append(hg_s)
        x1_s, route_s, w_out_bf = _outproj(y_s, xs, w_out, l, nb, g1, b1, rwt, rb)
        xs = _moe(x1_s, route_s, wg, wu, wd, l, nb, g2, b2)

        y_p, h_p, conv_p, hg_p, rt_p = _mix_prompt(
            xp, w_in_bf, batch, seq, l, PROMPT_TILE, conv_w, cb, wri, bri, lam, hgrn_lb_logits, ng)
        x1_p, route_p, plan_p, gran_p, loff_p = _outproj(
            y_p, xp, w_out_bf, l, PROMPT_TILE, g1, b1, rwt, rb, plan=True, sub=2)
        xp = _moe_sparse(x1_p, route_p, plan_p, gran_p, loff_p, wg, wu, wd, l, PROMPT_TILE, g2, b2)
        hs_p.append(h_p.reshape(batch, D_A))
        convs_p.append(conv_p)
        hgs_p.append(hg_p)
        rts_p.append(rt_p)

    h_s, conv_s, rt_s = prev_s[-1]
    hg_s = jnp.transpose(prev_hg[-1], (0, 4, 1, 2, 3))
    return (xp.reshape(batch, seq, D_MODEL), xs.reshape(nb, 1, D_MODEL),
            jnp.stack(hs_p), h_s, jnp.stack(convs_p), jnp.transpose(conv_s, (0, 2, 1, 3)),
            jnp.stack(hgs_p), hg_s, jnp.stack(rts_p), rt_s)
```

```python
import functools

import numpy as np
import jax
import jax.numpy as jnp
from jax import lax
from jax.experimental import pallas as pl
from jax.experimental.pallas import tpu as pltpu

D_MODEL = 1024
DEPTH = 2
PAST_LEN = 16384
D_A = 256
A_BLOCKS = 4
A_BLK = 64
CONV_W = 4
RGLRU_C = 8.0
B_HEADS = 4
B_DK = 64
D_B = 256
C_HEADS = 4
C_DK = 128
D_C = 512
D_IN = 3584
B_CHUNK = 64
C_CHUNK = 128
ROPE_BASE = 10000.0
N_EXPERTS = 16
N_GROUPS = 4
EXP_PER_GROUP = 4
D_EXPERT = 512
LN_EPS = 1e-5
RMS_EPS = 1e-6
GN_EPS = 1e-6
F_TINY = 1e-30
ALPHA = (2 * DEPTH) ** 0.25

O_XA, O_GA, O_QB, O_FB, O_VB, O_GB, O_QC, O_KC, O_VC, O_GC = (
    0, 256, 512, 768, 1024, 1280, 1536, 2048, 2560, 3072)

V7X_VMEM_LIMIT_BYTES = 56 * 1024 * 1024
SUBLANES = 8
LANES = 128
PROMPT_TILE = 512
HGRN_SAFE_MIN_LOGDECAY = -60.0

BF16 = jnp.bfloat16
F32 = jnp.float32
_NT = (((1,), (1,)), ((), ()))
_TN = (((0,), (0,)), ((), ()))


def _cparams(n_axes):
    return pltpu.CompilerParams(
        dimension_semantics=("arbitrary",) * n_axes,
        vmem_limit_bytes=V7X_VMEM_LIMIT_BYTES)


def _dot(a, b):
    return jnp.dot(a.astype(BF16), b.astype(BF16), preferred_element_type=F32)


def _dot_g(a, b, dims):
    return lax.dot_general(a.astype(BF16), b.astype(BF16), dims, preferred_element_type=F32)


def _sigmoid(x):
    return 0.5 * jnp.tanh(0.5 * x) + 0.5


def _sqrt_nonneg(x):
    return jnp.where(x > 0.0, x * lax.rsqrt(x), 0.0)


def _silu(x):
    return x * _sigmoid(x)


def _gelu_tanh(x):
    c = np.float32(np.sqrt(2.0 / np.pi))
    return 0.5 * x * (1.0 + jnp.tanh(c * (x + np.float32(0.044715) * (x * x * x))))


def _log_sigmoid(x):
    return -(jnp.maximum(-x, 0.0) + jnp.log(1.0 + jnp.exp(-jnp.abs(x))))


def _layernorm(z, g, b):
    mu = jnp.mean(z, axis=-1, keepdims=True)
    zc = z - mu
    var = jnp.mean(zc * zc, axis=-1, keepdims=True)
    return zc * lax.rsqrt(var + LN_EPS) * g + b


def _hgrn_lower_bound(lbl, layer):
    rows = [lbl[j:j + 1, :] for j in range(DEPTH)]
    m = rows[0]
    for r in rows[1:]:
        m = jnp.maximum(m, r)
    ex = [jnp.exp(r - m) for r in rows]
    tot = ex[0]
    for e in ex[1:]:
        tot = tot + e
    lb = jnp.zeros_like(m)
    for j in range(1, layer + 1):
        lb = lb + ex[j] / tot
    return lb


def _rglru_gates(u, wri, bri, lam):
    gates = _dot(u, wri) + bri
    r = _sigmoid(gates[:, :D_A])
    i = _sigmoid(gates[:, D_A:])
    log_a = RGLRU_C * r * _log_sigmoid(lam)
    a = jnp.exp(log_a)
    bterm = _sqrt_nonneg(jnp.maximum(1.0 - a * a, 0.0)) * (i * u)
    return a, bterm


def _split(x):
    hi = x.astype(BF16)
    return hi, (x - hi.astype(F32)).astype(BF16)


def _proj_kernel(x_ref, w_ref, o_ref, wb_ref):
    xh, xl = _split(x_ref[...])
    for j in range(0, D_IN, 512):
        wh, wl = _split(w_ref[:, j:j + 512])
        wb_ref[:, j:j + 512] = wh
        o_ref[:, j:j + 512] = (jnp.dot(xh, wh, preferred_element_type=F32)
                               + jnp.dot(xl, wh, preferred_element_type=F32)
                               + jnp.dot(xh, wl, preferred_element_type=F32))


def _proj(x, w_in, layer):
    n = x.shape[0]
    return pl.pallas_call(
        _proj_kernel,
        grid=(1,),
        in_specs=[pl.BlockSpec((n, D_MODEL), lambda i: (0, 0)),
                  pl.BlockSpec((None, D_MODEL, D_IN), lambda i: (layer, 0, 0))],
        out_specs=[pl.BlockSpec((n, D_IN), lambda i: (0, 0)),
                   pl.BlockSpec((D_MODEL, D_IN), lambda i: (0, 0))],
        out_shape=[jax.ShapeDtypeStruct((n, D_IN), F32),
                   jax.ShapeDtypeStruct((D_MODEL, D_IN), BF16)],
        compiler_params=_cparams(1),
        name="proj",
    )(x, w_in)


def _retention_consts(chunk):
    lg = np.log1p(-np.exp2(-5.0 - np.arange(C_HEADS, dtype=np.float64)))
    idx = np.arange(chunk, dtype=np.float64)
    rel = idx[:, None] - idx[None, :]
    mask = rel >= 0
    dmat = np.where(mask[None], np.exp(np.where(mask, rel, 0.0)[None] * lg[:, None, None]), 0.0)
    qdec = np.exp((idx + 1.0)[None, :] * lg[:, None])
    kdec = np.exp((chunk - 1.0 - idx)[None, :] * lg[:, None])
    sdec = np.exp(chunk * lg)
    qdec_b = np.broadcast_to(qdec[:, :, None], (C_HEADS, chunk, C_DK))
    kdec_b = np.broadcast_to(kdec[:, :, None], (C_HEADS, chunk, C_DK))
    return (dmat.astype(np.float32), np.ascontiguousarray(qdec_b).astype(np.float32),
            np.ascontiguousarray(kdec_b).astype(np.float32), [float(v) for v in sdec])


def _rope_tables(positions):
    half = C_DK // 2
    inv = ROPE_BASE ** (-np.arange(half, dtype=np.float64) / half)
    ang = np.asarray(positions, dtype=np.float64)[:, None] * inv[None]
    cos = np.concatenate([np.cos(ang), np.cos(ang)], axis=-1)
    sin = np.concatenate([-np.sin(ang), np.sin(ang)], axis=-1)
    return cos.astype(np.float32), sin.astype(np.float32)


def _rope(x, cos, sin_signed):
    return x * cos + pltpu.roll(x, C_DK // 2, 1) * sin_signed


PROJ_SPLIT = 1536


def _mix_prompt_kernel(x0_ref, xn_ref, w_ref, cos_ref, sin_ref, cw_ref, cb_ref, wri_ref, bri_ref,
                       lam_ref, lbl_ref, ng_ref, dmat_ref, qdec_ref, kdec_ref,
                       y_ref, h_ref, conv_ref, hg_ref, rt_ref,
                       prev_sc, hprev_sc, st_sc, sret_sc, kb_sc, bb_sc, vb_sc, oi_sc, p_ref, next_ref, xb_sc, x0b_sc,
                       *, layer, tt, sdec):
    t = pl.program_id(1)
    nt = pl.num_programs(1)
    flat = pl.program_id(0) * nt + t

    def project(xb_ref, dst, c0, c1):
        xb = xb_ref[...]
        for j in range(c0, c1, 256):
            dst[:, j:j + 256] = jnp.dot(xb, w_ref[:, j:j + 256], preferred_element_type=F32)

    xb_sc[...] = xn_ref[...].astype(BF16)

    @pl.when(flat == 0)
    def _():
        x0b_sc[...] = x0_ref[...].astype(BF16)
        project(x0b_sc, p_ref, 0, D_IN)

    @pl.when(t == 0)
    def _():
        prev_sc[...] = jnp.zeros_like(prev_sc)
        hprev_sc[...] = jnp.zeros_like(hprev_sc)
        st_sc[...] = jnp.zeros_like(st_sc)
        sret_sc[...] = jnp.zeros_like(sret_sc)

    xa = p_ref[:, O_XA:O_XA + D_A]
    ga = p_ref[:, O_GA:O_GA + D_A]
    row = lax.broadcasted_iota(jnp.int32, (tt, D_A), 0)
    row8 = lax.broadcasted_iota(jnp.int32, (SUBLANES, D_A), 0)
    prev = prev_sc[...]

    def shifted(j):
        r = pltpu.roll(xa, j, 0)
        top = jnp.where(row8 < j, pltpu.roll(prev, j, 0), r[0:SUBLANES])
        return jnp.concatenate([top, r[SUBLANES:]], axis=0)

    u = cb_ref[...] + shifted(3) * cw_ref[0:1, :]
    u = u + shifted(2) * cw_ref[1:2, :]
    u = u + shifted(1) * cw_ref[2:3, :]
    u = u + xa * cw_ref[3:4, :]
    last8 = xa[tt - SUBLANES:tt]
    prev_sc[...] = last8

    a, bterm = _rglru_gates(u, wri_ref[...], bri_ref[...], lam_ref[...])
    project(xb_sc, next_ref, 0, PROJ_SPLIT)
    s = 1
    while s < SUBLANES:
        keep = (row % SUBLANES) >= s
        a_sh = jnp.where(keep, pltpu.roll(a, s, 0), 1.0)
        b_sh = jnp.where(keep, pltpu.roll(bterm, s, 0), 0.0)
        bterm = a * b_sh + bterm
        a = a * a_sh
        s *= 2
    carry = hprev_sc[...]
    groups = []
    for gi in range(tt // SUBLANES):
        rows = slice(gi * SUBLANES, (gi + 1) * SUBLANES)
        h_in = jnp.broadcast_to(carry[SUBLANES - 1:SUBLANES, :], (SUBLANES, D_A))
        carry = a[rows] * h_in + bterm[rows]
        groups.append(carry)
    h = jnp.concatenate(groups, axis=0)
    hlast8 = carry
    hprev_sc[...] = hlast8
    y_ref[:, 0:D_A] = h * _gelu_tanh(ga)

    lb = _hgrn_lower_bound(lbl_ref[...], layer)
    ng = ng_ref[...]
    cl = B_CHUNK
    crow = lax.broadcasted_iota(jnp.int32, (cl, D_B), 0)
    ccol = lax.broadcasted_iota(jnp.int32, (cl, D_B), 1)
    causal = (ccol % B_DK) <= crow
    br = lax.broadcasted_iota(jnp.int32, (D_B, D_B), 0)
    bc = lax.broadcasted_iota(jnp.int32, (D_B, D_B), 1)
    head_mask = (br // B_DK) == (bc // B_DK)
    seg = jnp.where(head_mask, 1.0, 0.0).astype(BF16)

    def seg_mean(x):
        hi = x.astype(BF16)
        lo = (x - hi.astype(F32)).astype(BF16)
        tot = (jnp.dot(hi, seg, preferred_element_type=F32)
               + jnp.dot(lo, seg, preferred_element_type=F32))
        return tot * (1.0 / B_DK)

    nck = tt // cl
    q = _silu(p_ref[:, O_QB:O_QB + D_B])
    fl = p_ref[:, O_FB:O_FB + D_B]
    v = p_ref[:, O_VB:O_VB + D_B]
    half_th = 0.5 * jnp.tanh(0.5 * fl)
    f = lb + (1.0 - lb) * (0.5 + half_th)
    b = jnp.log(jnp.maximum(f, F_TINY))
    k = (1.0 - lb) * (0.5 - half_th)
    trow = lax.broadcasted_iota(jnp.int32, (tt, D_B), 0) % cl
    sh = 1
    while sh < cl:
        b = b + jnp.where(trow >= sh, pltpu.roll(b, sh, 0), 0.0)
        sh *= 2

    def chunk_row(x, c, r):
        top = c * cl + r + 1
        return x[top - SUBLANES:top][SUBLANES - 1:SUBLANES]

    def spread(rows):
        return jnp.concatenate([jnp.broadcast_to(r, (cl, D_B)) for r in rows], axis=0)

    last_rows = [chunk_row(b, c, cl - 1) for c in range(nck)]
    mid_rows = [chunk_row(b, c, cl // 2 - 1) for c in range(nck)]
    b_last = spread(last_rows)
    b_mid = spread(mid_rows)
    qd = q * jnp.exp(b)
    kl = k * jnp.exp(b_last - b)
    lasts = jnp.concatenate(last_rows, axis=0)
    mids = jnp.concatenate(mid_rows, axis=0)
    safe = jnp.min(jnp.minimum(mids, lasts - mids)) >= HGRN_SAFE_MIN_LOGDECAY

    def next_state(st, c, sl):
        kv = jnp.where(head_mask, _dot_g(v[sl], kl[sl], _TN), 0.0)
        return st * jnp.exp(last_rows[c]) + kv

    @pl.when(safe)
    def _():
        qmid = q * jnp.exp(b - b_mid)
        kinv = k * jnp.exp(b_mid - b)
        st = st_sc[...]
        for c in range(nck):
            sl = slice(c * cl, (c + 1) * cl)
            k4 = jnp.where(head_mask, jnp.concatenate([kinv[sl]] * B_HEADS, axis=0), 0.0)
            sc = jnp.where(causal, _dot_g(qmid[sl], k4, _NT), 0.0)
            v4 = jnp.where(head_mask, jnp.concatenate([v[sl]] * B_HEADS, axis=0), 0.0)
            oi_sc[sl, :] = _dot(sc, v4) + _dot_g(qd[sl], st, _NT)
            st = next_state(st, c, sl)
        st_sc[...] = st

    @pl.when(jnp.logical_not(safe))
    def _():
        kb_sc[...] = k
        bb_sc[...] = b
        vb_sc[...] = v
        st = st_sc[...]
        for c in range(nck):
            sl = slice(c * cl, (c + 1) * cl)
            qc, bc_ = q[sl], b[sl]

            def pair(sidx, acc, c=c, qc=qc, bc_=bc_):
                src = pl.ds(c * cl + sidx, 1)
                e = jnp.exp(jnp.minimum(bc_ - bb_sc[src, :], 0.0)) * (qc * kb_sc[src, :])
                scr = jnp.dot(e.astype(BF16), seg, preferred_element_type=F32)
                return acc + jnp.where(crow >= sidx, scr, 0.0) * vb_sc[src, :]

            oi = lax.fori_loop(0, cl, pair, jnp.zeros((cl, D_B), F32))
            oi_sc[sl, :] = oi + _dot_g(qd[sl], st, _NT)
            st = next_state(st, c, sl)
        st_sc[...] = st

    o = oi_sc[...]
    o = o * lax.rsqrt(seg_mean(o * o) + RMS_EPS) * ng
    y_ref[:, D_A:D_A + D_B] = o * _silu(p_ref[:, O_GB:O_GB + D_B])

    rl = C_CHUNK
    per_head = (D_IN - PROJ_SPLIT) // C_HEADS
    cos = cos_ref[...]
    sin = sin_ref[...]
    for hh in range(C_HEADS):
        lo = hh * C_DK
        q = _rope(p_ref[:, O_QC + lo:O_QC + lo + C_DK], cos, sin)
        k = _rope(p_ref[:, O_KC + lo:O_KC + lo + C_DK], cos, sin) * (C_DK ** -0.5)
        v = p_ref[:, O_VC + lo:O_VC + lo + C_DK]
        s = sret_sc[hh]
        parts = []
        for c in range(tt // rl):
            sl = slice(c * rl, (c + 1) * rl)
            sc = _dot_g(q[sl], k[sl], _NT) * dmat_ref[hh]
            parts.append(_dot(sc, v[sl]) + _dot(q[sl] * qdec_ref[hh], s))
            s = sdec[hh] * s + _dot_g(k[sl] * kdec_ref[hh], v[sl], _TN)
        sret_sc[hh] = s
        project(xb_sc, next_ref, PROJ_SPLIT + hh * per_head, PROJ_SPLIT + (hh + 1) * per_head)
        o = jnp.concatenate(parts, axis=0)
        mu = jnp.mean(o, axis=-1, keepdims=True)
        oc = o - mu
        var = jnp.mean(oc * oc, axis=-1, keepdims=True)
        g = p_ref[:, O_GC + lo:O_GC + lo + C_DK]
        y_ref[:, D_A + D_B + lo:D_A + D_B + lo + C_DK] = oc * lax.rsqrt(var + GN_EPS) * _silu(g)

    p_ref[...] = next_ref[...]

    @pl.when(t == nt - 1)
    def _():
        h_ref[...] = pltpu.roll(hprev_sc[...], 1, 0)[0:1]
        conv_ref[...] = pltpu.roll(prev_sc[...], CONV_W - 1, 0)[0:CONV_W - 1]
        s_bd = st_sc[...].T
        for hh in range(B_HEADS):
            hg_ref[hh] = s_bd[hh * B_DK:(hh + 1) * B_DK, hh * B_DK:(hh + 1) * B_DK]
        rt_ref[...] = sret_sc[...]


def _mix_prompt(x, w_in_bf, batch, seq, layer, tt, cw, cb, wri, bri, lam, lbl, ng):
    nt = seq // tt
    last = batch * nt - 1
    dmat, qdec, kdec, sdec = _retention_consts(C_CHUNK)
    cos, sin = _rope_tables(np.arange(seq))
    full = lambda shape: pl.BlockSpec(shape, lambda b, t: (0,) * len(shape))
    lay = lambda shape: pl.BlockSpec((None,) + shape, lambda b, t: (layer,) + (0,) * len(shape))
    kern = functools.partial(_mix_prompt_kernel, layer=layer, tt=tt, sdec=sdec)
    return pl.pallas_call(
        kern,
        grid=(batch, nt),
        in_specs=[
            pl.BlockSpec((tt, D_MODEL), lambda b, t: (b * nt + t, 0)),
            pl.BlockSpec((tt, D_MODEL), lambda b, t: (jnp.minimum(b * nt + t + 1, last), 0)),
            full((D_MODEL, D_IN)),
            pl.BlockSpec((tt, C_DK), lambda b, t: (t, 0)),
            pl.BlockSpec((tt, C_DK), lambda b, t: (t, 0)),
            lay((CONV_W, D_A)), lay((1, D_A)), lay((D_A, 2 * D_A)), lay((1, 2 * D_A)),
            lay((1, D_A)), full((DEPTH, D_B)), lay((1, D_B)),
            full((C_HEADS, C_CHUNK, C_CHUNK)), full((C_HEADS, C_CHUNK, C_DK)),
            full((C_HEADS, C_CHUNK, C_DK)),
        ],
        out_specs=[
            pl.BlockSpec((tt, D_MODEL), lambda b, t: (b * nt + t, 0)),
            pl.BlockSpec((None, 1, D_A), lambda b, t: (b, 0, 0)),
            pl.BlockSpec((None, CONV_W - 1, D_A), lambda b, t: (b, 0, 0)),
            pl.BlockSpec((None, B_HEADS, B_DK, B_DK), lambda b, t: (b, 0, 0, 0)),
            pl.BlockSpec((None, C_HEADS, C_DK, C_DK), lambda b, t: (b, 0, 0, 0)),
        ],
        out_shape=[
            jax.ShapeDtypeStruct((batch * seq, D_MODEL), F32),
            jax.ShapeDtypeStruct((batch, 1, D_A), F32),
            jax.ShapeDtypeStruct((batch, CONV_W - 1, D_A), F32),
            jax.ShapeDtypeStruct((batch, B_HEADS, B_DK, B_DK), F32),
            jax.ShapeDtypeStruct((batch, C_HEADS, C_DK, C_DK), F32),
        ],
        scratch_shapes=[
            pltpu.VMEM((SUBLANES, D_A), F32), pltpu.VMEM((SUBLANES, D_A), F32),
            pltpu.VMEM((D_B, D_B), F32), pltpu.VMEM((C_HEADS, C_DK, C_DK), F32),
            pltpu.VMEM((tt, D_B), F32), pltpu.VMEM((tt, D_B), F32),
            pltpu.VMEM((tt, D_B), F32), pltpu.VMEM((tt, D_B), F32),
            pltpu.VMEM((tt, D_IN), F32), pltpu.VMEM((tt, D_IN), F32),
            pltpu.VMEM((tt, D_MODEL), BF16), pltpu.VMEM((tt, D_MODEL), BF16),
        ],
        compiler_params=_cparams(2),
        name="mix_prompt",
    )(x, x, w_in_bf, cos, sin, cw, cb, wri, bri, lam, lbl, ng, dmat, qdec, kdec)


def _column_matrix(x):
    pad = jnp.zeros((LANES - SUBLANES, LANES), F32)
    return jnp.concatenate([x, pad], axis=0).T


def _mix_sample_kernel(*refs, layer, gammas, n_prev):
    (p_ref, h0_ref, conv0_ref, rt0_ref, cos_ref, sin_ref,
     cw_ref, cb_ref, wri_ref, bri_ref, lam_ref) = refs[:11]
    prev = refs[11:11 + 3 * n_prev]
    y_ref, h_ref, conv_ref, rt_ref, o_sc = refs[11 + 3 * n_prev:]
    if n_prev:
        for j in range(n_prev):
            for dst, src in zip((h_ref, conv_ref, rt_ref), prev[3 * j:3 * j + 3]):
                dst[j] = src[...]
        h_ref, conv_ref, rt_ref = (r.at[layer] for r in (h_ref, conv_ref, rt_ref))
    tb = SUBLANES
    xa = p_ref[:, O_XA:O_XA + D_A]
    ga = p_ref[:, O_GA:O_GA + D_A]
    c0, c1, c2 = conv0_ref[0], conv0_ref[1], conv0_ref[2]
    u = cb_ref[...] + c0 * cw_ref[0:1, :]
    u = u + c1 * cw_ref[1:2, :]
    u = u + c2 * cw_ref[2:3, :]
    u = u + xa * cw_ref[3:4, :]
    conv_ref[0] = c1
    conv_ref[1] = c2
    conv_ref[2] = xa
    a, bterm = _rglru_gates(u, wri_ref[...], bri_ref[...], lam_ref[...])
    h = a * h0_ref[...] + bterm
    h_ref[...] = h
    y_ref[:, 0:D_A] = h * _gelu_tanh(ga)

    y_ref[:, D_A:D_A + D_B] = jnp.zeros((tb, D_B), F32)

    cos = cos_ref[...]
    sin = sin_ref[...]
    for head in range(C_HEADS):
        lo = head * C_DK
        gamma = gammas[head]
        q = _rope(p_ref[:, O_QC + lo:O_QC + lo + C_DK], cos, sin)
        k = _rope(p_ref[:, O_KC + lo:O_KC + lo + C_DK], cos, sin) * (C_DK ** -0.5)
        v = p_ref[:, O_VC + lo:O_VC + lo + C_DK]
        g = p_ref[:, O_GC + lo:O_GC + lo + C_DK]
        dots = jnp.sum(q * k, axis=-1, keepdims=True)
        q_cols = _column_matrix(q * gamma)
        k_cols = _column_matrix(k)
        for b in range(tb):
            s_old = rt0_ref[b, head]
            v_row = v[b:b + 1, :]
            rt_ref[b, head] = gamma * s_old + k_cols[:, b:b + 1] * v_row
            o_row = (jnp.sum(q_cols[:, b:b + 1] * s_old, axis=0, keepdims=True)
                     + dots[b:b + 1, :] * v_row)
            o_sc[b:b + 1, 0:C_DK] = o_row
        o = o_sc[:, 0:C_DK]
        mu = jnp.mean(o, axis=-1, keepdims=True)
        oc = o - mu
        var = jnp.mean(oc * oc, axis=-1, keepdims=True)
        y_ref[:, D_A + D_B + lo:D_A + D_B + lo + C_DK] = oc * lax.rsqrt(var + GN_EPS) * _silu(g)


def _mix_sample(proj, h0, conv0_t, rt0, layer, cw, cb, wri, bri, lam, prev=()):
    nb = proj.shape[0]
    tb = SUBLANES
    lg = np.log1p(-np.exp2(-5.0 - np.arange(C_HEADS, dtype=np.float64)))
    gammas = [float(np.exp(v)) for v in lg]
    cos, sin = _rope_tables([PAST_LEN])
    full = lambda shape: pl.BlockSpec(shape, lambda i: (0,) * len(shape))
    lay = lambda shape: pl.BlockSpec((None,) + shape, lambda i: (layer,) + (0,) * len(shape))
    state_shapes = [(tb, D_A), (CONV_W - 1, tb, D_A), (tb, C_HEADS, C_DK, C_DK)]
    state_maps = [lambda i: (i, 0), lambda i: (0, i, 0), lambda i: (i, 0, 0, 0)]
    full_shapes = [(nb, D_A), (CONV_W - 1, nb, D_A), (nb, C_HEADS, C_DK, C_DK)]
    state_specs = [pl.BlockSpec(s, m) for s, m in zip(state_shapes, state_maps)]
    if prev:
        stack = len(prev) + 1
        out_state_specs = [pl.BlockSpec((stack,) + s, lambda i, m=m: (0,) + m(i))
                           for s, m in zip(state_shapes, state_maps)]
        out_state_shapes = [jax.ShapeDtypeStruct((stack,) + s, F32) for s in full_shapes]
    else:
        out_state_specs = state_specs
        out_state_shapes = [jax.ShapeDtypeStruct(s, F32) for s in full_shapes]
    kern = functools.partial(_mix_sample_kernel, layer=layer, gammas=gammas, n_prev=len(prev))
    return pl.pallas_call(
        kern,
        grid=(nb // tb,),
        in_specs=[
            pl.BlockSpec((tb, D_IN), lambda i: (i, 0)),
            pl.BlockSpec((None, tb, D_A), lambda i: (layer, i, 0)),
            pl.BlockSpec((None, CONV_W - 1, tb, D_A), lambda i: (layer, 0, i, 0)),
            pl.BlockSpec((None, tb, C_HEADS, C_DK, C_DK), lambda i: (layer, i, 0, 0, 0)),
            full((1, C_DK)), full((1, C_DK)),
            lay((CONV_W, D_A)), lay((1, D_A)), lay((D_A, 2 * D_A)), lay((1, 2 * D_A)),
            lay((1, D_A)),
        ] + state_specs * len(prev),
        out_specs=[pl.BlockSpec((tb, D_MODEL), lambda i: (i, 0))] + out_state_specs,
        out_shape=[jax.ShapeDtypeStruct((nb, D_MODEL), F32)] + out_state_shapes,
        scratch_shapes=[pltpu.VMEM((tb, D_B), F32)],
        compiler_params=_cparams(1),
        name="mix_sample",
    )(proj, h0, conv0_t, rt0, cos, sin, cw, cb, wri, bri, lam,
      *[a for states in prev for a in states])


def _hgrn_sample_kernel(*refs, layer, n_prev):
    qb_ref, fb_ref, vb_ref, gb_ref, lbl_ref, ng_ref, s0_ref, y_in_ref = refs[:8]
    prev = refs[8:8 + n_prev]
    y_ref, s_ref = refs[8 + n_prev:]
    del y_in_ref
    for j in range(n_prev):
        s_ref[j] = prev[j][...]
    if n_prev:
        s_ref = s_ref.at[layer]
    lb = _hgrn_lower_bound(lbl_ref[...], layer)
    q = _silu(qb_ref[...])
    fl = fb_ref[...]
    v = vb_ref[...]
    half_th = 0.5 * jnp.tanh(0.5 * fl)
    f = lb + (1.0 - lb) * (0.5 + half_th)
    ef = jnp.exp(jnp.log(jnp.maximum(f, F_TINY)))
    k = (1.0 - lb) * (0.5 - half_th)
    qf_t, k_t, ef_t, v_t, qk_t = (x.T for x in (q * ef, k, ef, v, q * k))
    o_parts = []
    for hh in range(2):
        base = hh * B_DK
        v_h = v_t[base:base + B_DK]
        acc = jnp.zeros((B_DK, LANES), F32)
        for kk in range(B_DK):
            r = base + kk
            s_old = s0_ref[hh, kk]
            s_ref[hh, kk] = ef_t[r:r + 1] * s_old + k_t[r:r + 1] * v_h
            acc = acc + qf_t[r:r + 1] * s_old
        dots = jnp.sum(qk_t[base:base + B_DK], axis=0, keepdims=True)
        o = acc + dots * v_h
        o_parts.append(o * lax.rsqrt(jnp.mean(o * o, axis=0, keepdims=True) + RMS_EPS))
    o = jnp.concatenate(o_parts, axis=0).T
    y_ref[...] = o * ng_ref[...] * _silu(gb_ref[...])


def _hgrn_sample(proj, y, s0_t, layer, lbl, ng_pair, prev=()):
    nb = proj.shape[0]
    col = lambda off: pl.BlockSpec((nb, LANES), lambda j: (0, off // LANES + j))
    state_block = (2, B_DK, B_DK, nb)
    state_spec = pl.BlockSpec(state_block, lambda j: (j, 0, 0, 0))
    stack = len(prev) + 1
    if prev:
        out_state_spec = pl.BlockSpec((stack,) + state_block, lambda j: (0, j, 0, 0, 0))
        out_state_shape = jax.ShapeDtypeStruct((stack, B_HEADS, B_DK, B_DK, nb), F32)
    else:
        out_state_spec = state_spec
        out_state_shape = jax.ShapeDtypeStruct((B_HEADS, B_DK, B_DK, nb), F32)
    return pl.pallas_call(
        functools.partial(_hgrn_sample_kernel, layer=layer, n_prev=len(prev)),
        grid=(B_HEADS // 2,),
        in_specs=[col(O_QB), col(O_FB), col(O_VB), col(O_GB),
                  pl.BlockSpec((DEPTH, LANES), lambda j: (0, j)),
                  pl.BlockSpec((None, 1, LANES), lambda j: (layer, 0, 0)),
                  pl.BlockSpec((None,) + state_block, lambda j: (layer, j, 0, 0, 0)),
                  pl.BlockSpec(memory_space=pl.ANY)] + [state_spec] * len(prev),
        out_specs=[col(D_A), out_state_spec],
        out_shape=[jax.ShapeDtypeStruct(y.shape, F32), out_state_shape],
        input_output_aliases={7: 0},
        compiler_params=_cparams(1),
        name="hgrn_sample",
    )(proj, proj, proj, proj, lbl, ng_pair, s0_t, y, *prev)


def _route_rows(l):
    m = l[0]
    for x in l[1:]:
        m = jnp.maximum(m, x)
    ex = [jnp.exp(x - m) for x in l]
    tot = ex[0]
    for x in ex[1:]:
        tot = tot + x
    p = [x / tot for x in ex]
    scores = []
    for gi in range(N_GROUPS):
        a, b, c, d = p[4 * gi:4 * gi + 4]
        hi1, lo1 = jnp.maximum(a, b), jnp.minimum(a, b)
        hi2, lo2 = jnp.maximum(c, d), jnp.minimum(c, d)
        top1 = jnp.maximum(hi1, hi2)
        top2 = jnp.maximum(jnp.minimum(hi1, hi2), jnp.maximum(lo1, lo2))
        scores.append(top1 + top2)
    best = scores[0]
    gsel = jnp.zeros_like(best, dtype=jnp.int32)
    for gi in range(1, N_GROUPS):
        upd = scores[gi] > best
        gsel = jnp.where(upd, gi, gsel)
        best = jnp.where(upd, scores[gi], best)
    vals = []
    for j in range(EXP_PER_GROUP):
        v = p[j]
        for gi in range(1, N_GROUPS):
            v = jnp.where(gsel == gi, p[4 * gi + j], v)
        vals.append(v)
    sel = []
    for j in range(EXP_PER_GROUP):
        rank = jnp.zeros_like(gsel)
        for i in range(EXP_PER_GROUP):
            if i == j:
                continue
            ahead = (vals[i] > vals[j]) | ((vals[i] == vals[j]) & (i < j))
            rank = rank + jnp.where(ahead, 1, 0)
        sel.append(rank < 2)
    denom = jnp.zeros_like(best)
    for j in range(EXP_PER_GROUP):
        denom = denom + jnp.where(sel[j], vals[j], 0.0)
    gates = [jnp.where(sel[j], vals[j] / denom, 0.0) for j in range(EXP_PER_GROUP)]
    comb = [jnp.where(gsel == (e // EXP_PER_GROUP), gates[e % EXP_PER_GROUP], 0.0)
            for e in range(N_EXPERTS)]
    j1 = jnp.where(sel[0], 0, jnp.where(sel[1], 1, 2))
    j2 = jnp.where(sel[3], 3, jnp.where(sel[2], 2, 1))
    g1 = jnp.zeros_like(best)
    g2 = jnp.zeros_like(best)
    for j in range(EXP_PER_GROUP):
        g1 = jnp.where(j1 == j, gates[j], g1)
        g2 = jnp.where(j2 == j, gates[j], g2)
    return comb, gsel * EXP_PER_GROUP + j1, gsel * EXP_PER_GROUP + j2, g1, g2


MOE_TILE = 512
GRANULE = 16
GRANULES_PER_TILE = MOE_TILE // GRANULE
XS_WIDTH = D_MODEL + LANES
LARGE_PIECE = 8
ROUTE_SLOT1, ROUTE_SLOT2 = N_EXPERTS, N_EXPERTS + 1
PLAN_SLOT1, PLAN_SLOT2, PLAN_G1, PLAN_G2 = 0, 1, 2, 3


def _local_slots(tt):
    worst = 2 * tt + (N_EXPERTS - 1) * GRANULE
    return -(-worst // LANES) * LANES


def _outproj_kernel(*refs, plan, sub, st):
    if plan:
        (y_ref, x_ref, w_ref, g_ref, b_ref, rwt_ref, rb_ref, tri_ref, ltri_ref,
         x1_ref, route_ref, plan_ref, gran_ref, loff_ref, rt_sc, oh_sc) = refs
    else:
        y_ref, x_ref, w_ref, g_ref, b_ref, rwt_ref, rb_ref, x1_ref, route_ref, wb_ref, rt_sc = refs
    if plan:
        y = jnp.dot(y_ref[...].astype(BF16), w_ref[...], preferred_element_type=F32)
    else:
        yh, yl = _split(y_ref[...])
        wh, wl = _split(w_ref[...])
        wb_ref[...] = wh
        y = (jnp.dot(yh, wh, preferred_element_type=F32) + jnp.dot(yl, wh, preferred_element_type=F32)
             + jnp.dot(yh, wl, preferred_element_type=F32))
    x1 = _layernorm(ALPHA * x_ref[...] + y, g_ref[...], b_ref[...])
    x1_ref[...] = x1
    hi, lo = _split(x1)
    rhi, rlo = _split(rwt_ref[...])
    lg = (lax.dot_general(rhi, hi, _NT, preferred_element_type=F32)
          + lax.dot_general(rhi, lo, _NT, preferred_element_type=F32)
          + lax.dot_general(rlo, hi, _NT, preferred_element_type=F32)) + rb_ref[...]
    comb, e1, e2, g1, g2 = _route_rows([lg[e:e + 1, :] for e in range(N_EXPERTS)])
    rt_sc[...] = jnp.zeros_like(rt_sc)
    if not plan:
        for e in range(N_EXPERTS):
            rt_sc[e:e + 1, :] = comb[e]
    else:
        for e in range(N_EXPERTS):
            oh_sc[e:e + 1, :] = jnp.where((e1 == e) | (e2 == e), 1.0, 0.0)
        bases = []
        for s in range(sub):
            oh = oh_sc[:, s * st:(s + 1) * st]
            cum = jnp.dot(oh.astype(BF16), tri_ref[...], preferred_element_type=F32)
            count = jnp.sum(oh, axis=1, keepdims=True)
            gran = jnp.floor((count + (GRANULE - 1)) * (1.0 / GRANULE))
            gran_b = jnp.broadcast_to(gran, (N_EXPERTS, LANES))
            loff_b = jnp.dot(ltri_ref[...], gran_b.astype(BF16), preferred_element_type=F32)
            bases.append(loff_b[:, 0:1] * GRANULE - 1.0 + cum)
            gran_ref[s] = gran_b.astype(jnp.int32)
            loff_ref[s] = loff_b.astype(jnp.int32)
        slot_base = jnp.concatenate(bases, axis=1)
        s1 = jnp.zeros_like(g1)
        s2 = jnp.zeros_like(g1)
        for e in range(N_EXPERTS):
            row = slot_base[e:e + 1, :]
            s1 = jnp.where(e1 == e, row, s1)
            s2 = jnp.where(e2 == e, row, s2)
        for s in range(sub):
            cols = slice(s * st, (s + 1) * st)
            plan_ref[s] = jnp.zeros((SUBLANES, st), F32)
            plan_ref[s, PLAN_SLOT1:PLAN_SLOT1 + 1, :] = s1[:, cols]
            plan_ref[s, PLAN_SLOT2:PLAN_SLOT2 + 1, :] = s2[:, cols]
            plan_ref[s, PLAN_G1:PLAN_G1 + 1, :] = g1[:, cols]
            plan_ref[s, PLAN_G2:PLAN_G2 + 1, :] = g2[:, cols]
        rt_sc[ROUTE_SLOT1:ROUTE_SLOT1 + 1, :] = s1
        rt_sc[ROUTE_SLOT2:ROUTE_SLOT2 + 1, :] = s2
    route_ref[...] = rt_sc[...].T


def _outproj(y, x, w_out, layer, st, g, b, rwt, rb, plan=False, sub=1):
    n = x.shape[0]
    tm = st * sub
    full = lambda shape: pl.BlockSpec(shape, lambda i: (0,) * len(shape))
    lay = lambda shape: pl.BlockSpec((None,) + shape, lambda i: (layer,) + (0,) * len(shape))
    w_spec = full((D_MODEL, D_MODEL)) if plan else lay((D_MODEL, D_MODEL))
    in_specs = [pl.BlockSpec((tm, D_MODEL), lambda i: (i, 0)),
                pl.BlockSpec((tm, D_MODEL), lambda i: (i, 0)),
                w_spec, lay((1, D_MODEL)), lay((1, D_MODEL)),
                full((N_EXPERTS, D_MODEL)), full((N_EXPERTS, 1))]
    out_specs = [pl.BlockSpec((tm, D_MODEL), lambda i: (i, 0)),
                 pl.BlockSpec((tm, LANES), lambda i: (i, 0))]
    out_shape = [jax.ShapeDtypeStruct((n, D_MODEL), F32),
                 jax.ShapeDtypeStruct((n, LANES), F32)]
    scratch = [pltpu.VMEM((LANES, tm), F32)]
    args = [y, x, w_out, g, b, rwt, rb]
    if plan:
        tri = np.triu(np.ones((st, st), np.float32)).astype(jnp.bfloat16)
        ltri = np.tril(np.ones((N_EXPERTS, N_EXPERTS), np.float32), -1).astype(jnp.bfloat16)
        in_specs += [full((st, st)), full((N_EXPERTS, N_EXPERTS))]
        args += [tri, ltri]
        per_tile = pl.BlockSpec((sub, N_EXPERTS, LANES), lambda i: (i, 0, 0))
        out_specs += [pl.BlockSpec((sub, SUBLANES, st), lambda i: (i, 0, 0)), per_tile, per_tile]
        out_shape += [jax.ShapeDtypeStruct((n // st, SUBLANES, st), F32),
                      jax.ShapeDtypeStruct((n // st, N_EXPERTS, LANES), jnp.int32),
                      jax.ShapeDtypeStruct((n // st, N_EXPERTS, LANES), jnp.int32)]
        scratch += [pltpu.VMEM((N_EXPERTS, tm), F32)]
    else:
        assert sub == 1 and tm == n, "the sample trunk is one tile; it also emits the bf16 weight"
        out_specs.append(full((D_MODEL, D_MODEL)))
        out_shape.append(jax.ShapeDtypeStruct((D_MODEL, D_MODEL), BF16))
    return pl.pallas_call(
        functools.partial(_outproj_kernel, plan=plan, sub=sub, st=st),
        grid=(n // tm,),
        in_specs=in_specs,
        out_specs=out_specs,
        out_shape=out_shape,
        scratch_shapes=scratch,
        compiler_params=_cparams(1),
        name="outproj_plan" if plan else "outproj",
    )(*args)


def _granule_copies(src, dst, src_g, dst_g, n, sem, start, max_granules):
    def pieces(sizes, src_g, dst_g):
        for size in sizes:
            bit = n & size

            @pl.when(bit != 0)
            def _(size=size, src_g=src_g, dst_g=dst_g):
                s0 = pl.multiple_of(src_g * GRANULE, GRANULE)
                d0 = pl.multiple_of(dst_g * GRANULE, GRANULE)
                cp = pltpu.make_async_copy(src.at[pl.ds(s0, size * GRANULE)],
                                           dst.at[pl.ds(d0, size * GRANULE)], sem)
                cp.start(priority=size.bit_length() % 2) if start else cp.wait()

            src_g = src_g + bit
            dst_g = dst_g + bit

    sizes = [max_granules >> k for k in range(max_granules.bit_length())]
    large = [s for s in sizes if s >= LARGE_PIECE]
    small = [s for s in sizes if s < LARGE_PIECE]

    @pl.when(n >= LARGE_PIECE)
    def _():
        pieces(large, src_g, dst_g)

    done = n & sum(large)
    pieces(small, src_g + done, dst_g + done)


def _dispatch_kernel(segg_ref, totg_ref, ctl_ref, ctl_prev_ref, plan_ref, x_ref, xs_ref,
                     xl_sc, zx_sc, sem, *, tt, total_tiles):
    step = pl.program_id(0)
    cur = step % 2

    @pl.when(step == 0)
    def _():
        zx_sc[...] = jnp.zeros_like(zx_sc)

        def zero_tile(i):
            r0 = pl.multiple_of(i * MOE_TILE, MOE_TILE)
            return pltpu.make_async_copy(zx_sc, xs_ref.at[pl.ds(r0, MOE_TILE)], sem.at[2])

        def run(i, start):
            cp = zero_tile(i)
            cp.start() if start else cp.wait()

        for start in (True, False):
            for e in range(N_EXPERTS):
                end = segg_ref[e] + totg_ref[e]

                @pl.when(end % GRANULES_PER_TILE != 0)
                def _():
                    run(end // GRANULES_PER_TILE, start)

        last = N_EXPERTS - 1
        used = (segg_ref[last] + totg_ref[last] + GRANULES_PER_TILE - 1) // GRANULES_PER_TILE
        lax.fori_loop(used, total_tiles, lambda i, c: (run(i, True), c)[1], 0)
        lax.fori_loop(used, total_tiles, lambda i, c: (run(i, False), c)[1], 0)

    n_slots = xl_sc.shape[1]
    xl = xl_sc.at[cur]
    slot = lax.broadcasted_iota(jnp.int32, (n_slots, tt), 0)
    m1 = slot == plan_ref[PLAN_SLOT1:PLAN_SLOT1 + 1, :].astype(jnp.int32)
    m2 = slot == plan_ref[PLAN_SLOT2:PLAN_SLOT2 + 1, :].astype(jnp.int32)
    perm = jnp.where(m1 | m2, 1.0, 0.0).astype(BF16)
    xl[:, :D_MODEL] = jnp.dot(
        perm, x_ref[...].astype(BF16), preferred_element_type=F32).astype(BF16)
    gate = jnp.sum(jnp.where(m1, plan_ref[PLAN_G1:PLAN_G1 + 1, :], 0.0)
                   + jnp.where(m2, plan_ref[PLAN_G2:PLAN_G2 + 1, :], 0.0), axis=1, keepdims=True)
    g_hi = gate.astype(BF16).astype(F32)
    g_lo = (gate - g_hi).astype(BF16).astype(F32)
    lane = lax.broadcasted_iota(jnp.int32, (n_slots, LANES), 1)
    xl[:, D_MODEL:] = jnp.where(lane == 0, g_hi, jnp.where(lane == 1, g_lo, 0.0)).astype(BF16)

    def copies(ctl, half, start):
        for e in range(N_EXPERTS):
            n, src_g, dst_g = ctl[0, e], ctl[1, e], ctl[2, e]
            _granule_copies(xl_sc.at[half], xs_ref, src_g, dst_g, n, sem.at[half], start,
                            tt // GRANULE)

    copies(ctl_ref, cur, True)

    @pl.when(step > 0)
    def _():
        copies(ctl_prev_ref, 1 - cur, False)

    @pl.when(step == pl.num_programs(0) - 1)
    def _():
        copies(ctl_ref, cur, False)


def _dispatch(x1, plan, ctl, seg_g, tot_g, tt, rows):
    n = x1.shape[0]
    n_slots = _local_slots(tt)
    return pl.pallas_call(
        functools.partial(_dispatch_kernel, tt=tt, total_tiles=rows // MOE_TILE),
        grid_spec=pltpu.PrefetchScalarGridSpec(
            num_scalar_prefetch=2,
            grid=(n // tt,),
            in_specs=[pl.BlockSpec((None, SUBLANES, LANES), lambda i, s, c: (i, 0, 0),
                                   memory_space=pltpu.SMEM),
                      pl.BlockSpec((None, SUBLANES, LANES), lambda i, s, c: (jnp.maximum(i - 1, 0), 0, 0),
                                   memory_space=pltpu.SMEM),
                      pl.BlockSpec((None, SUBLANES, tt), lambda i, s, c: (i, 0, 0)),
                      pl.BlockSpec((tt, D_MODEL), lambda i, s, c: (i, 0))],
            out_specs=pl.BlockSpec(memory_space=pl.ANY),
            scratch_shapes=[pltpu.VMEM((2, n_slots, XS_WIDTH), BF16),
                            pltpu.VMEM((MOE_TILE, XS_WIDTH), BF16),
                            pltpu.SemaphoreType.DMA((3,))]),
        out_shape=jax.ShapeDtypeStruct((rows, XS_WIDTH), BF16),
        compiler_params=_cparams(1),
        name="moe_dispatch",
    )(seg_g, tot_g, ctl, ctl, plan, x1)


def _ffn_kernel(te_ref, nv_ref, xs_ref, wg_ref, wu_ref, wd_ref, ys_ref):
    used = pl.program_id(0) < nv_ref[0]

    @pl.when(used)
    def _():
        xb = xs_ref[:, :D_MODEL]
        gate_pair = xs_ref[:, D_MODEL:].astype(F32)
        gate = gate_pair[:, 0:1] + gate_pair[:, 1:2]
        hg = jnp.dot(xb, wg_ref[...].astype(BF16), preferred_element_type=F32)
        hu = jnp.dot(xb, wu_ref[...].astype(BF16), preferred_element_type=F32)
        h = (_silu(hg) * hu * gate).astype(BF16)
        ys_ref[...] = jnp.dot(h, wd_ref[...].astype(BF16), preferred_element_type=F32).astype(BF16)

    @pl.when(jnp.logical_not(used))
    def _():
        ys_ref[...] = jnp.zeros_like(ys_ref)


def _ffn(xs, tile_expert, n_valid, wg, wu, wd, layer):
    total_tiles = xs.shape[0] // MOE_TILE
    w_in_spec = pl.BlockSpec((None, None, D_MODEL, D_EXPERT), lambda i, te, nv: (layer, te[i], 0, 0))
    return pl.pallas_call(
        _ffn_kernel,
        grid_spec=pltpu.PrefetchScalarGridSpec(
            num_scalar_prefetch=2,
            grid=(total_tiles,),
            in_specs=[pl.BlockSpec((MOE_TILE, XS_WIDTH), lambda i, te, nv: (jnp.minimum(i, nv[0] - 1), 0)),
                      w_in_spec, w_in_spec,
                      pl.BlockSpec((None, None, D_EXPERT, D_MODEL),
                                   lambda i, te, nv: (layer, te[i], 0, 0))],
            out_specs=pl.BlockSpec((MOE_TILE, D_MODEL), lambda i, te, nv: (i, 0))),
        out_shape=jax.ShapeDtypeStruct((xs.shape[0], D_MODEL), BF16),
        compiler_params=_cparams(1),
        name="moe_ffn",
    )(tile_expert, n_valid, xs, wg, wu, wd)


def _combine_kernel(ctl_ref, ctl_next_ref, x_ref, route_ref, g_ref, b_ref, ys_ref, o_ref,
                    yl_sc, sem, *, tt):
    i = pl.program_id(0)
    cur = i % 2

    def copies(ctl, half, start):
        for e in range(N_EXPERTS):
            n, loc_g, buf_g = ctl[0, e], ctl[1, e], ctl[2, e]
            _granule_copies(ys_ref, yl_sc.at[half], buf_g, loc_g, n, sem.at[half], start,
                            tt // GRANULE)

    @pl.when(i == 0)
    def _():
        yl_sc[...] = jnp.zeros_like(yl_sc)
        copies(ctl_ref, 0, True)

    @pl.when(i + 1 < pl.num_programs(0))
    def _():
        copies(ctl_next_ref, 1 - cur, True)

    copies(ctl_ref, cur, False)

    n_slots = yl_sc.shape[1]
    route = route_ref[...]
    slot = lax.broadcasted_iota(jnp.int32, (tt, n_slots), 1)
    pick = ((slot == route[:, ROUTE_SLOT1:ROUTE_SLOT1 + 1].astype(jnp.int32))
            | (slot == route[:, ROUTE_SLOT2:ROUTE_SLOT2 + 1].astype(jnp.int32)))
    y = jnp.dot(jnp.where(pick, 1.0, 0.0).astype(BF16), yl_sc[cur], preferred_element_type=F32)
    o_ref[...] = _layernorm(ALPHA * x_ref[...] + y, g_ref[...], b_ref[...])


def _combine(x1, route, ctl, ys, layer, tt, g, b):
    n = x1.shape[0]
    last = n // tt - 1
    lay = lambda shape: pl.BlockSpec((None,) + shape, lambda i: (layer,) + (0,) * len(shape))
    return pl.pallas_call(
        functools.partial(_combine_kernel, tt=tt),
        grid=(n // tt,),
        in_specs=[pl.BlockSpec((None, SUBLANES, LANES), lambda i: (i, 0, 0), memory_space=pltpu.SMEM),
                  pl.BlockSpec((None, SUBLANES, LANES), lambda i: (jnp.minimum(i + 1, last), 0, 0),
                               memory_space=pltpu.SMEM),
                  pl.BlockSpec((tt, D_MODEL), lambda i: (i, 0)),
                  pl.BlockSpec((tt, LANES), lambda i: (i, 0)),
                  lay((1, D_MODEL)), lay((1, D_MODEL)),
                  pl.BlockSpec(memory_space=pl.ANY)],
        out_specs=pl.BlockSpec((tt, D_MODEL), lambda i: (i, 0)),
        scratch_shapes=[pltpu.VMEM((2, _local_slots(tt), D_MODEL), BF16),
                        pltpu.SemaphoreType.DMA((2,))],
        out_shape=jax.ShapeDtypeStruct((n, D_MODEL), F32),
        compiler_params=_cparams(1),
        name="moe_combine",
    )(ctl, ctl, x1, route, g, b, ys)


def _moe_sparse(x1, route, plan, gran, loff, wg, wu, wd, layer, tt, g, b):
    n = x1.shape[0]
    n_tiles = n // tt
    gran = gran[:, :, 0]
    loff = loff[:, :, 0]
    max_tiles = -(-(2 * n + (GRANULE - 1) * N_EXPERTS * n_tiles) // MOE_TILE) + N_EXPERTS
    rows = max_tiles * MOE_TILE
    tot_g = jnp.sum(gran, axis=0)
    tiles = (tot_g + GRANULES_PER_TILE - 1) // GRANULES_PER_TILE
    tile_end = jnp.cumsum(tiles)
    seg_g = ((tile_end - tiles) * GRANULES_PER_TILE).astype(jnp.int32)
    n_valid = tile_end[-1]
    idx = jnp.minimum(jnp.arange(max_tiles, dtype=jnp.int32), n_valid - 1)
    tile_expert = jnp.sum(idx[:, None] >= tile_end[None, :], axis=1).astype(jnp.int32)
    buf_g = seg_g[None, :] + jnp.cumsum(gran, axis=0) - gran
    ctl = jnp.zeros((n_tiles, SUBLANES, LANES), jnp.int32)
    ctl = ctl.at[:, 0, :N_EXPERTS].set(gran).at[:, 1, :N_EXPERTS].set(loff).at[:, 2, :N_EXPERTS].set(buf_g)
    xs = _dispatch(x1, plan, ctl, seg_g, tot_g.astype(jnp.int32), tt, rows)
    ys = _ffn(xs, tile_expert, n_valid.reshape(1).astype(jnp.int32), wg, wu, wd, layer)
    return _combine(x1, route, ctl, ys, layer, tt, g, b)


def _moe_kernel(x_ref, r_ref, wg_ref, wu_ref, wd_ref, g_ref, b_ref, o_ref, xb_sc, acc_sc):
    e = pl.program_id(1)

    @pl.when(e == 0)
    def _():
        xb_sc[...] = x_ref[...].astype(BF16)
        acc_sc[...] = jnp.zeros_like(acc_sc)

    xb = xb_sc[...]
    hg = jnp.dot(xb, wg_ref[...].astype(BF16), preferred_element_type=F32)
    hu = jnp.dot(xb, wu_ref[...].astype(BF16), preferred_element_type=F32)
    r = r_ref[...]
    lane = lax.broadcasted_iota(jnp.int32, r.shape, 1)
    c = jnp.sum(jnp.where(lane == e, r, 0.0), axis=-1, keepdims=True)
    h = _silu(hg) * hu * c
    acc_sc[...] += jnp.dot(h.astype(BF16), wd_ref[...].astype(BF16), preferred_element_type=F32)

    @pl.when(e == pl.num_programs(1) - 1)
    def _():
        o_ref[...] = _layernorm(ALPHA * x_ref[...] + acc_sc[...], g_ref[...], b_ref[...])


def _moe(x1, route, wg, wu, wd, layer, tm, g, b):
    n = x1.shape[0]
    lay = lambda shape: pl.BlockSpec((None,) + shape, lambda i, e: (layer,) + (0,) * len(shape))
    return pl.pallas_call(
        _moe_kernel,
        grid=(n // tm, N_EXPERTS),
        in_specs=[pl.BlockSpec((tm, D_MODEL), lambda i, e: (i, 0)),
                  pl.BlockSpec((tm, LANES), lambda i, e: (i, 0)),
                  pl.BlockSpec((None, None, D_MODEL, D_EXPERT), lambda i, e: (layer, e, 0, 0)),
                  pl.BlockSpec((None, None, D_MODEL, D_EXPERT), lambda i, e: (layer, e, 0, 0)),
                  pl.BlockSpec((None, None, D_EXPERT, D_MODEL), lambda i, e: (layer, e, 0, 0)),
                  lay((1, D_MODEL)), lay((1, D_MODEL))],
        out_specs=pl.BlockSpec((tm, D_MODEL), lambda i, e: (i, 0)),
        out_shape=jax.ShapeDtypeStruct((n, D_MODEL), F32),
        scratch_shapes=[pltpu.VMEM((tm, D_MODEL), BF16), pltpu.VMEM((tm, D_MODEL), F32)],
        compiler_params=_cparams(2),
        name="moe",
    )(x1, route, wg, wu, wd, g, b)


def _block_diag(w):
    out = jnp.zeros((DEPTH, D_A, D_A), w.dtype)
    for gi in range(A_BLOCKS):
        out = out.at[:, gi * A_BLK:(gi + 1) * A_BLK, gi * A_BLK:(gi + 1) * A_BLK].set(w[:, gi])
    return out


def kernel(x_prompt, x_sample, state_rglru_h, state_conv, state_hgrn, state_ret, w_in, conv_w, conv_b, w_rgate, b_rgate, w_igate, b_igate, rglru_lambda, hgrn_lb_logits, hgrn_norm_g, w_out, ln1_g, ln1_b, router_w, router_b, exp_w_gate, exp_w_up, exp_w_down, ln2_g, ln2_b):
    batch, seq, _ = x_prompt.shape
    nb = x_sample.shape[0]

    wg, wu, wd = exp_w_gate, exp_w_up, exp_w_down
    wri = jnp.concatenate([_block_diag(w_rgate), _block_diag(w_igate)], axis=-1).astype(BF16)
    bri = jnp.concatenate([b_rgate.reshape(DEPTH, 1, D_A), b_igate.reshape(DEPTH, 1, D_A)], axis=-1)
    cb = conv_b.reshape(DEPTH, 1, D_A)
    lam = rglru_lambda.reshape(DEPTH, 1, D_A)
    ng = jnp.tile(hgrn_norm_g, (1, B_HEADS)).reshape(DEPTH, 1, D_B)
    g1, b1 = ln1_g.reshape(DEPTH, 1, D_MODEL), ln1_b.reshape(DEPTH, 1, D_MODEL)
    g2, b2 = ln2_g.reshape(DEPTH, 1, D_MODEL), ln2_b.reshape(DEPTH, 1, D_MODEL)
    rwt = router_w.T
    rb = router_b.reshape(N_EXPERTS, 1)
    conv0_t = jnp.transpose(state_conv, (0, 2, 1, 3))
    hg0_t = jnp.transpose(state_hgrn, (0, 2, 3, 4, 1))

    xp = x_prompt.reshape(batch * seq, D_MODEL)
    xs = x_sample.reshape(nb, D_MODEL)
    hs_p, convs_p, hgs_p, rts_p = [], [], [], []
    prev_s, prev_hg = [], []
    for l in range(DEPTH):
        last = l == DEPTH - 1
        proj_s, w_in_bf = _proj(xs, w_in, l)
        y_s, *states_s = _mix_sample(
            proj_s, state_rglru_h, conv0_t, state_ret, l, conv_w, cb, wri, bri, lam,
            prev=prev_s if last else ())
        prev_s.append(states_s)
        y_s, hg_s = _hgrn_sample(proj_s, y_s, hg0_t, l, hgrn_lb_logits, ng[:, :, :LANES],
                                 prev=prev_hg if last else ())
        prev_hg.append(hg_s)
        x1_s, route_s, w_out_bf = _outproj(y_s, xs, w_out, l, nb, g1, b1, rwt, rb)
        xs = _moe(x1_s, route_s, wg, wu, wd, l, nb, g2, b2)

        y_p, h_p, conv_p, hg_p, rt_p = _mix_prompt(
            xp, w_in_bf, batch, seq, l, PROMPT_TILE, conv_w, cb, wri, bri, lam, hgrn_lb_logits, ng)
        x1_p, route_p, plan_p, gran_p, loff_p = _outproj(
            y_p, xp, w_out_bf, l, PROMPT_TILE, g1, b1, rwt, rb, plan=True, sub=2)
        xp = _moe_sparse(x1_p, route_p, plan_p, gran_p, loff_p, wg, wu, wd, l, PROMPT_TILE, g2, b2)
        hs_p.append(h_p.reshape(batch, D_A))
        convs_p.append(conv_p)
        hgs_p.append(hg_p)
        rts_p.append(rt_p)

    h_s, conv_s, rt_s = prev_s[-1]
    hg_s = jnp.transpose(prev_hg[-1], (0, 4, 1, 2, 3))
    return (xp.reshape(batch, seq, D_MODEL), xs.reshape(nb, 1, D_MODEL),
            jnp.stack(hs_p), h_s, jnp.stack(convs_p), jnp.transpose(conv_s, (0, 2, 1, 3)),
            jnp.stack(hgs_p), hg_s, jnp.stack(rts_p), rt_s)
```
